```python
import jax
import jax.numpy as jnp
from jax import lax
import numpy as np

D_MODEL = 1024
BATCH = 8
SEQ = 4096
DEPTH = 2

GRID_W = 64
CTX_LEN = 256
EPS = 1e-6
LRU_WIDTH = D_MODEL // 2
LRU_BLOCKS = 8
LRU_BLOCK = LRU_WIDTH // LRU_BLOCKS
CONV_W = 4
LRU_C = 8.0
ATT_HEADS = 8
ATT_KV_HEADS = 2
HEAD_DIM = 64
WINDOW = 128
ATT_BLOCK = 128
ROPE_BASE = 10000.0
ATT_Q_WIDTH = ATT_HEADS * HEAD_DIM
ATT_KV_WIDTH = ATT_KV_HEADS * HEAD_DIM
EVEN_IN = 2 * LRU_WIDTH + ATT_Q_WIDTH + 2 * ATT_KV_WIDTH
EVEN_MIX = LRU_WIDTH + ATT_Q_WIDTH
GLA_HEADS = 4
GLA_DK = D_MODEL // 2 // GLA_HEADS
GLA_DV = D_MODEL // GLA_HEADS
GLA_LOWRANK = 16
GLA_TAU = 16.0
GLA_CHUNK = 64
GLA_K_WIDTH = GLA_HEADS * GLA_DK
GLA_V_WIDTH = GLA_HEADS * GLA_DV
ODD_IN = 2 * GLA_K_WIDTH + 2 * GLA_V_WIDTH + 2 * GLA_LOWRANK
ODD_MIX = GLA_V_WIDTH
N_EXPERTS = 16
N_GROUPS = 4
EXPERTS_PER_GROUP = N_EXPERTS // N_GROUPS
TOP_K = 2
D_EXPERT = 512

kernel_name = 'hybrid_rglru_swa_gla_moe_prefix_dit'


def rms_norm(x, g):
    x32 = x.astype(jnp.float32)
    y = x32 * lax.rsqrt(jnp.mean(x32 * x32, axis=-1, keepdims=True) + EPS)
    return y.astype(x.dtype) * g


def modulate(h, g, shift, scale):
    return rms_norm(h, g) * (1.0 + scale) + shift


def axial_rope_tables(n_tokens):
    rows = n_tokens // GRID_W
    row = jnp.repeat(jnp.arange(rows), GRID_W).astype(jnp.float32)
    col = jnp.tile(jnp.arange(GRID_W), rows).astype(jnp.float32)
    n_freq = HEAD_DIM // 4
    inv_freq = ROPE_BASE ** (-jnp.arange(n_freq, dtype=jnp.float32) / n_freq)
    ang_r = row[:, None] * inv_freq
    ang_c = col[:, None] * inv_freq
    return (jnp.cos(ang_r), jnp.sin(ang_r), jnp.cos(ang_c), jnp.sin(ang_c))


def _rotate(x, cos, sin):
    x1, x2 = jnp.split(x, 2, axis=-1)
    return jnp.concatenate([x1 * cos - x2 * sin, x2 * cos + x1 * sin], axis=-1)


def apply_axial_rope(x, tables):
    cr, sr, cc, sc = (t.astype(x.dtype)[:, None, :] for t in tables)
    xr, xc = jnp.split(x, 2, axis=-1)
    return jnp.concatenate([_rotate(xr, cr, sr), _rotate(xc, cc, sc)], axis=-1)


def centred_depthwise_conv(u, w, b):
    left = (CONV_W - 1) // 2
    y = lax.conv_general_dilated(u, w[:, None, :], window_strides=(1,),
                                 padding=[(left, CONV_W - 1 - left)],
                                 dimension_numbers=('NWC', 'WIO', 'NWC'),
                                 feature_group_count=u.shape[-1])
    return y + b


def block_diagonal(u, w, b):
    ub = u.reshape(u.shape[:-1] + (LRU_BLOCKS, LRU_BLOCK))
    return jnp.einsum('btnc,ncd->btnd', ub, w).reshape(u.shape) + b


def linear_recurrence(a, b, h0, reverse):
    if h0 is not None:
        edge = -1 if reverse else 0
        b = b.at[:, edge].add(a[:, edge] * h0)

    def combine(left, right):
        a_l, b_l = left
        a_r, b_r = right
        return a_l * a_r, a_r * b_l + b_r

    return lax.associative_scan(combine, (a, b), reverse=reverse, axis=1)[1]


def rglru_direction(u, w_r, b_r, w_i, b_i, lam, h0, reverse):
    r = jax.nn.sigmoid(block_diagonal(u, w_r, b_r).astype(jnp.float32))
    i = jax.nn.sigmoid(block_diagonal(u, w_i, b_i).astype(jnp.float32))
    log_a = -LRU_C * r * jax.nn.softplus(-lam.astype(jnp.float32))
    x_in = jnp.sqrt(-jnp.expm1(2.0 * log_a)) * (i * u.astype(jnp.float32))
    return linear_recurrence(jnp.exp(log_a), x_in, h0, reverse)


def rglru_group(u_lat, u_ctx, g_lat, g_ctx, conv_w, conv_b, w_r, b_r, w_i, b_i, lam, need_ctx_out):
    u_lat = centred_depthwise_conv(u_lat, conv_w, conv_b)
    u_ctx = centred_depthwise_conv(u_ctx, conv_w, conv_b)
    h_lat = 0.0
    h_ctx = 0.0
    for d, reverse in enumerate((False, True)):
        hc = rglru_direction(u_ctx, w_r[d], b_r[d], w_i[d], b_i[d], lam[d], None, reverse)
        h0 = hc[:, 0] if reverse else hc[:, -1]
        h_lat = h_lat + rglru_direction(u_lat, w_r[d], b_r[d], w_i[d], b_i[d], lam[d], h0, reverse)
        if need_ctx_out:
            h_ctx = h_ctx + hc
    y_lat = h_lat.astype(u_lat.dtype) * jax.nn.gelu(g_lat)
    y_ctx = h_ctx.astype(u_ctx.dtype) * jax.nn.gelu(g_ctx) if need_ctx_out else None
    return y_lat, y_ctx


def windowed_attention(q, k, v, k_ctx, v_ctx, sink):
    B, T, H, hd = q.shape
    KV = k.shape[2]
    G = H // KV
    nb = T // ATT_BLOCK
    scale = hd ** -0.5
    qb = q.reshape(B, nb, ATT_BLOCK, KV, G, hd)
    pad = ((0, 0), (ATT_BLOCK, ATT_BLOCK), (0, 0), (0, 0))
    kp = jnp.pad(k, pad).reshape(B, nb + 2, ATT_BLOCK, KV, hd)
    vp = jnp.pad(v, pad).reshape(B, nb + 2, ATT_BLOCK, KV, hd)

    def band(t):
        return jnp.concatenate([t[:, :-2], t[:, 1:-1], t[:, 2:]], axis=2)

    kw, vw = band(kp), band(vp)
    qi = jnp.arange(ATT_BLOCK)[:, None]
    ks = jnp.arange(3 * ATT_BLOCK)[None, :]
    kpos = (jnp.arange(nb) * ATT_BLOCK - ATT_BLOCK)[:, None, None] + ks[None]
    mask = (jnp.abs(ks - ATT_BLOCK - qi) <= WINDOW)[None] & (kpos >= 0) & (kpos < T)
    s_win = jnp.einsum('bnikgd,bnskd->bnkgis', qb, kw).astype(jnp.float32) * scale
    s_win = jnp.where(mask[None, :, None, None], s_win, -1e30)
    s_ctx = jnp.einsum('bnikgd,blkd->bnkgil', qb, k_ctx).astype(jnp.float32) * scale
    s_sink = sink.astype(jnp.float32).reshape(1, 1, KV, G, 1, 1)
    m = jnp.maximum(jnp.maximum(s_win.max(-1, keepdims=True), s_ctx.max(-1, keepdims=True)), s_sink)
    e_win = jnp.exp(s_win - m)
    e_ctx = jnp.exp(s_ctx - m)
    denom = e_win.sum(-1, keepdims=True) + e_ctx.sum(-1, keepdims=True) + jnp.exp(s_sink - m)
    o = (jnp.einsum('bnkgis,bnskd->bnkgid', e_win.astype(v.dtype), vw)
         + jnp.einsum('bnkgil,blkd->bnkgid', e_ctx.astype(v.dtype), v_ctx)) / denom.astype(v.dtype)
    return jnp.transpose(o, (0, 1, 4, 2, 3, 5)).reshape(B, T, H * hd)


def context_attention(q, k, v, sink):
    B, L, H, hd = q.shape
    KV = k.shape[2]
    G = H // KV
    qg = q.reshape(B, L, KV, G, hd)
    s = jnp.einsum('blkgd,bmkd->bkglm', qg, k).astype(jnp.float32) * hd ** -0.5
    s_sink = jnp.broadcast_to(sink.astype(jnp.float32).reshape(1, KV, G, 1, 1), s.shape[:-1] + (1,))
    p = jax.nn.softmax(jnp.concatenate([s, s_sink], axis=-1), axis=-1)[..., :L].astype(v.dtype)
    return jnp.einsum('bkglm,bmkd->blkgd', p, v).reshape(B, L, H * hd)


def rglru_swa_layer(n_lat, n_ctx, rope, w_in, w_out, conv_w, conv_b, rg_w, rg_b, ig_w, ig_b,
                    lam, q_g, k_g, sink, need_ctx_out):
    splits = [LRU_WIDTH, 2 * LRU_WIDTH, 2 * LRU_WIDTH + ATT_Q_WIDTH,
              2 * LRU_WIDTH + ATT_Q_WIDTH + ATT_KV_WIDTH]

    def project(n):
        B, L = n.shape[:2]
        u, gate, q, k, v = jnp.split(n @ w_in, splits, axis=-1)
        q = rms_norm(q.reshape(B, L, ATT_HEADS, HEAD_DIM), q_g)
        k = rms_norm(k.reshape(B, L, ATT_KV_HEADS, HEAD_DIM), k_g)
        return u, gate, q, k, v.reshape(B, L, ATT_KV_HEADS, HEAD_DIM)

    u_l, g_l, q_l, k_l, v_l = project(n_lat)
    u_c, g_c, q_c, k_c, v_c = project(n_ctx)
    q_l = apply_axial_rope(q_l, rope)
    k_l = apply_axial_rope(k_l, rope)
    a_lat, a_ctx = rglru_group(u_l, u_c, g_l, g_c, conv_w, conv_b, rg_w, rg_b, ig_w, ig_b, lam,
                               need_ctx_out)
    b_lat = windowed_attention(q_l, k_l, v_l, k_c, v_c, sink)
    y_lat = jnp.concatenate([a_lat, b_lat], axis=-1) @ w_out
    if not need_ctx_out:
        return y_lat, None
    b_ctx = context_attention(q_c, k_c, v_c, sink)
    y_ctx = jnp.concatenate([a_ctx, b_ctx], axis=-1) @ w_out
    return y_lat, y_ctx


def gla_chunked(q, k, v, log_a, s0, with_output):
    B, T, H, dk = k.shape
    dv = v.shape[-1]
    nc = T // GLA_CHUNK
    kc = k.astype(jnp.float32).reshape(B, nc, GLA_CHUNK, H, dk)
    vc = v.astype(jnp.float32).reshape(B, nc, GLA_CHUNK, H, dv)
    b = jnp.cumsum(log_a.reshape(B, nc, GLA_CHUNK, H, dk), axis=2)
    g = b[:, :, -1]
    k_end = kc * jnp.exp(g[:, :, None] - b)
    if s0 is None:
        s0 = jnp.zeros((B, H, dk, dv), jnp.float32)

    def chunk_major(t):
        return jnp.moveaxis(t, 1, 0)

    if not with_output:
        def carry_only(S, xs):
            k_n, v_n, g_n = xs
            return jnp.exp(g_n)[..., None] * S + jnp.einsum('bshd,bshv->bhdv', k_n, v_n), None

        s_final, _ = lax.scan(carry_only, s0, (chunk_major(k_end), chunk_major(vc), chunk_major(g)))
        return None, s_final
    q_dec = q.astype(jnp.float32).reshape(B, nc, GLA_CHUNK, H, dk) * jnp.exp(b)

    def step(S, xs):
        q_n, k_n, v_n, g_n = xs
        o_n = jnp.einsum('bthd,bhdv->bthv', q_n, S)
        S = jnp.exp(g_n)[..., None] * S + jnp.einsum('bshd,bshv->bhdv', k_n, v_n)
        return S, o_n

    s_final, o_inter = lax.scan(step, s0, (chunk_major(q_dec), chunk_major(k_end),
                                           chunk_major(vc), chunk_major(g)))
    k_inc = kc * jnp.exp(-b)
    intra = jnp.einsum('bnthd,bnshd->bnhts', q_dec, k_inc)
    intra = jnp.where(jnp.tril(jnp.ones((GLA_CHUNK, GLA_CHUNK), bool)), intra, 0.0)
    o = jnp.einsum('bnhts,bnshv->bnthv', intra, vc) + jnp.moveaxis(o_inter, 0, 1)
    return o.reshape(B, T, H, dv), s_final


def gla_direction(q, k, v, log_a, s0, reverse, with_output):
    if not reverse:
        return gla_chunked(q, k, v, log_a, s0, with_output)
    flip = lambda t: jnp.flip(t, axis=1)
    o, s = gla_chunked(flip(q), flip(k), flip(v), flip(log_a), s0, with_output)
    return (flip(o) if o is not None else None), s


def gla_layer(n_lat, n_ctx, w_in, w_out, gate_up, gate_b, out_norm_g, need_ctx_out):
    splits = [GLA_K_WIDTH, 2 * GLA_K_WIDTH, 2 * GLA_K_WIDTH + GLA_V_WIDTH,
              2 * GLA_K_WIDTH + 2 * GLA_V_WIDTH, 2 * GLA_K_WIDTH + 2 * GLA_V_WIDTH + GLA_LOWRANK]

    def project(n):
        B, L = n.shape[:2]
        q, k, v, r, a_f, a_b = jnp.split(n @ w_in, splits, axis=-1)
        q = q.reshape(B, L, GLA_HEADS, GLA_DK) * GLA_DK ** -0.5
        k = k.reshape(B, L, GLA_HEADS, GLA_DK)
        v = v.reshape(B, L, GLA_HEADS, GLA_DV)
        log_a = [(jax.nn.log_sigmoid((a_d @ gate_up[d] + gate_b[d]).astype(jnp.float32)) / GLA_TAU)
                 .reshape(B, L, GLA_HEADS, GLA_DK) for d, a_d in enumerate((a_f, a_b))]
        return q, k, v, r, log_a

    ql, kl, vl, rl, la_l = project(n_lat)
    qc, kc, vc, rc, la_c = project(n_ctx)
    o_lat = 0.0
    o_ctx = 0.0
    for d, reverse in enumerate((False, True)):
        oc, s_ctx = gla_direction(qc, kc, vc, la_c[d], None, reverse, need_ctx_out)
        ol, _ = gla_direction(ql, kl, vl, la_l[d], s_ctx, reverse, True)
        o_lat = o_lat + ol
        if need_ctx_out:
            o_ctx = o_ctx + oc

    def finish(o, r, n):
        B, L = n.shape[:2]
        o = rms_norm(o.astype(n.dtype), out_norm_g).reshape(B, L, GLA_V_WIDTH)
        return (o * jax.nn.silu(r)) @ w_out

    y_lat = finish(o_lat, rl, n_lat)
    y_ctx = finish(o_ctx, rc, n_ctx) if need_ctx_out else None
    return y_lat, y_ctx


def grouped_top2_moe(h, router_w, router_bias, w_gate, w_up, w_down):
    scores = jax.nn.sigmoid(h.astype(jnp.float32) @ router_w.astype(jnp.float32))
    sel = scores + router_bias.astype(jnp.float32)
    group_score = lax.top_k(sel.reshape(-1, N_GROUPS, EXPERTS_PER_GROUP), TOP_K)[0].sum(-1)
    best_group = jnp.argmax(group_score, axis=-1)
    in_group = (jnp.arange(N_EXPERTS) // EXPERTS_PER_GROUP)[None, :] == best_group[:, None]
    _, idx = lax.top_k(jnp.where(in_group, sel, -jnp.inf), TOP_K)
    w = jnp.take_along_axis(scores, idx, axis=-1)
    w = w / jnp.sum(w, axis=-1, keepdims=True)
    combine = jnp.sum(jax.nn.one_hot(idx, N_EXPERTS, dtype=jnp.float32) * w[..., None], axis=1)
    combine = combine.astype(h.dtype)
    out = jnp.zeros_like(h)
    for e in range(N_EXPERTS):
        y = (jax.nn.silu(h @ w_gate[e]) * (h @ w_up[e])) @ w_down[e]
        out = out + combine[:, e:e + 1] * y
    return out


def setup_inputs(seed: int = 0) -> dict:
    key = jax.random.key(seed)
    keys = iter(jax.random.split(key, 40))
    D = D_MODEL
    n_even = (DEPTH + 1) // 2
    n_odd = DEPTH // 2

    def nrm(shape, scale):
        return jax.random.normal(next(keys), shape, jnp.float32) * scale

    x = nrm((BATCH, SEQ, D), 1.0)
    c = nrm((BATCH, D), 1.0)
    ctx = nrm((BATCH, CTX_LEN, D), 1.0)
    c_ctx = nrm((D,), 1.0)
    router_w = nrm((D, N_EXPERTS), D ** -0.5)
    router_bias = nrm((N_EXPERTS,), 0.01)
    ada_w = nrm((DEPTH, D, 6 * D), 0.02)
    ada_b = nrm((DEPTH, 6 * D), 0.02)
    norm_mix_g = 1.0 + nrm((DEPTH, D), 0.02)
    norm_ffn_g = 1.0 + nrm((DEPTH, D), 0.02)
    moe_w_gate = nrm((DEPTH, N_EXPERTS, D, D_EXPERT), D ** -0.5)
    moe_w_up = nrm((DEPTH, N_EXPERTS, D, D_EXPERT), D ** -0.5)
    moe_w_down = nrm((DEPTH, N_EXPERTS, D_EXPERT, D), D_EXPERT ** -0.5)
    ev_w_in = nrm((n_even, D, EVEN_IN), D ** -0.5)
    ev_w_out = nrm((n_even, EVEN_MIX, D), EVEN_MIX ** -0.5)
    ev_conv_w = nrm((n_even, CONV_W, LRU_WIDTH), CONV_W ** -0.5)
    ev_conv_b = nrm((n_even, LRU_WIDTH), 0.02)
    ev_rg_w = nrm((n_even, 2, LRU_BLOCKS, LRU_BLOCK, LRU_BLOCK), LRU_BLOCK ** -0.5)
    ev_rg_b = nrm((n_even, 2, LRU_WIDTH), 0.02)
    ev_ig_w = nrm((n_even, 2, LRU_BLOCKS, LRU_BLOCK, LRU_BLOCK), LRU_BLOCK ** -0.5)
    ev_ig_b = nrm((n_even, 2, LRU_WIDTH), 0.02)
    a_pow = jax.random.uniform(next(keys), (n_even, 2, LRU_WIDTH), jnp.float32, 0.9, 0.999)
    a_base = a_pow ** (1.0 / LRU_C)
    ev_lambda = jnp.log(a_base) - jnp.log1p(-a_base)
    ev_q_norm_g = 1.0 + nrm((n_even, HEAD_DIM), 0.02)
    ev_k_norm_g = 1.0 + nrm((n_even, HEAD_DIM), 0.02)
    ev_sink = nrm((n_even, ATT_HEADS), 1.0)
    od_w_in = nrm((n_odd, D, ODD_IN), D ** -0.5)
    od_w_out = nrm((n_odd, ODD_MIX, D), ODD_MIX ** -0.5)
    od_gate_up = nrm((n_odd, 2, GLA_LOWRANK, GLA_K_WIDTH), GLA_LOWRANK ** -0.5)
    od_gate_b = nrm((n_odd, 2, GLA_K_WIDTH), 0.02)
    od_out_norm_g = 1.0 + nrm((n_odd, GLA_DV), 0.02)
    return {'x': x, 'c': c, 'ctx': ctx, 'c_ctx': c_ctx, 'router_w': router_w,
            'router_bias': router_bias, 'ada_w': ada_w, 'ada_b': ada_b,
            'norm_mix_g': norm_mix_g, 'norm_ffn_g': norm_ffn_g, 'moe_w_gate': moe_w_gate,
            'moe_w_up': moe_w_up, 'moe_w_down': moe_w_down, 'ev_w_in': ev_w_in,
            'ev_w_out': ev_w_out, 'ev_conv_w': ev_conv_w, 'ev_conv_b': ev_conv_b,
            'ev_rg_w': ev_rg_w, 'ev_rg_b': ev_rg_b, 'ev_ig_w': ev_ig_w, 'ev_ig_b': ev_ig_b,
            'ev_lambda': ev_lambda, 'ev_q_norm_g': ev_q_norm_g, 'ev_k_norm_g': ev_k_norm_g,
            'ev_sink': ev_sink, 'od_w_in': od_w_in, 'od_w_out': od_w_out,
            'od_gate_up': od_gate_up, 'od_gate_b': od_gate_b, 'od_out_norm_g': od_out_norm_g}


def reference(x, c, ctx, c_ctx, router_w, router_bias, ada_w, ada_b, norm_mix_g, norm_ffn_g,
              moe_w_gate, moe_w_up, moe_w_down, ev_w_in, ev_w_out, ev_conv_w, ev_conv_b,
              ev_rg_w, ev_rg_b, ev_ig_w, ev_ig_b, ev_lambda, ev_q_norm_g, ev_k_norm_g, ev_sink,
              od_w_in, od_w_out, od_gate_up, od_gate_b, od_out_norm_g):
    B, T, D = x.shape
    rope = axial_rope_tables(T)
    h_lat, h_ctx = x, ctx
    cond_lat = jax.nn.silu(c)[:, None, :]
    cond_ctx = jax.nn.silu(c_ctx)[None, None, :]
    for layer in range(DEPTH):
        last = layer == DEPTH - 1
        j = layer // 2
        mod_lat = jnp.split(cond_lat @ ada_w[layer] + ada_b[layer], 6, axis=-1)
        mod_ctx = jnp.split(cond_ctx @ ada_w[layer] + ada_b[layer], 6, axis=-1)
        n_lat = modulate(h_lat, norm_mix_g[layer], mod_lat[0], mod_lat[1])
        n_ctx = modulate(h_ctx, norm_mix_g[layer], mod_ctx[0], mod_ctx[1])
        if layer % 2 == 0:
            y_lat, y_ctx = rglru_swa_layer(n_lat, n_ctx, rope, ev_w_in[j], ev_w_out[j], ev_conv_w[j],
                                           ev_conv_b[j], ev_rg_w[j], ev_rg_b[j], ev_ig_w[j],
                                           ev_ig_b[j], ev_lambda[j], ev_q_norm_g[j],
                                           ev_k_norm_g[j], ev_sink[j], not last)
        else:
            y_lat, y_ctx = gla_layer(n_lat, n_ctx, od_w_in[j], od_w_out[j], od_gate_up[j],
                                     od_gate_b[j], od_out_norm_g[j], not last)
        h_lat = h_lat + mod_lat[2] * y_lat
        rows = modulate(h_lat, norm_ffn_g[layer], mod_lat[3], mod_lat[4]).reshape(-1, D)
        if not last:
            h_ctx = h_ctx + mod_ctx[2] * y_ctx
            f_ctx = modulate(h_ctx, norm_ffn_g[layer], mod_ctx[3], mod_ctx[4])
            rows = jnp.concatenate([rows, f_ctx.reshape(-1, D)], axis=0)
        ffn = grouped_top2_moe(rows, router_w, router_bias, moe_w_gate[layer], moe_w_up[layer],
                               moe_w_down[layer])
        h_lat = h_lat + mod_lat[5] * ffn[:B * T].reshape(B, T, D)
        if not last:
            h_ctx = h_ctx + mod_ctx[5] * ffn[B * T:].reshape(h_ctx.shape)
    return h_lat
```

```python
import functools

import jax
import jax.numpy as jnp
import numpy as np
from jax import lax
from jax.experimental import pallas as pl
from jax.experimental.pallas import tpu as pltpu

F32 = jnp.float32
BF16 = jnp.bfloat16
HIGHEST = lax.Precision.HIGHEST

D_MODEL = 1024
GRID_W = 64
EPS = 1e-6
LRU_WIDTH = 512
LRU_BLOCK = 64
CONV_W = 4
LRU_C = 8.0
ATT_HEADS = 8
ATT_KV_HEADS = 2
HEAD_DIM = 64
WINDOW = 128
ATT_BLOCK = 128
ROPE_BASE = 10000.0
ATT_Q_WIDTH = ATT_HEADS * HEAD_DIM
ATT_KV_WIDTH = ATT_KV_HEADS * HEAD_DIM
EVEN_IN = 2 * LRU_WIDTH + ATT_Q_WIDTH + 2 * ATT_KV_WIDTH
GLA_HEADS = 4
GLA_DK = 128
GLA_DV = 256
GLA_LOWRANK = 16
GLA_TAU = 16.0
GLA_CHUNK = 64
GLA_K_WIDTH = GLA_HEADS * GLA_DK
GLA_V_WIDTH = GLA_HEADS * GLA_DV
ODD_IN = 2 * GLA_K_WIDTH + 2 * GLA_V_WIDTH + 2 * GLA_LOWRANK
N_EXPERTS = 16
N_GROUPS = 4
EXPERTS_PER_GROUP = 4
D_EXPERT = 512
PAIRS_PER_GROUP = 6
N_CLASSES = N_GROUPS * PAIRS_PER_GROUP
PAIR_LO = (0, 0, 0, 1, 1, 2)
PAIR_HI = (1, 2, 3, 2, 3, 3)

LANES = 128
SUBLANES = 8
VMEM_LIMIT_BYTES = 56 * 1024 * 1024

ROW_TILE = 512
MOE_TILE = 256
GATHER_ROWS = 256
LRU_CHUNK = 128
LRU_HALF = 256
ODD_IN_PAD = ((ODD_IN + LANES - 1) // LANES) * LANES


def _params(*sem):
    return pltpu.CompilerParams(dimension_semantics=sem, vmem_limit_bytes=VMEM_LIMIT_BYTES)


def _modulated_norm(x, g, shift, scale):
    y = x * lax.rsqrt(jnp.mean(x * x, axis=-1, keepdims=True) + EPS)
    return (y * g) * (1.0 + scale) + shift


def _ada_kernel(c_ref, w_ref, b_ref, o_ref):
    c = c_ref[...]
    s = c * jax.nn.sigmoid(c)
    o_ref[...] = jnp.dot(s, w_ref[...], preferred_element_type=F32) + b_ref[...]


def _ada_call(cvec, ada_w, ada_b):
    depth, d, n6 = ada_w.shape
    r = cvec.shape[0]
    tn = n6 // 4
    return pl.pallas_call(
        _ada_kernel,
        grid=(depth, n6 // tn),
        in_specs=[pl.BlockSpec((r, d), lambda l, j: (0, 0)),
                  pl.BlockSpec((None, d, tn), lambda l, j: (l, 0, j)),
                  pl.BlockSpec((None, 1, tn), lambda l, j: (l, 0, j))],
        out_specs=pl.BlockSpec((None, r, tn), lambda l, j: (l, 0, j)),
        out_shape=jax.ShapeDtypeStruct((depth, r, n6), F32),
        compiler_params=_params("arbitrary", "arbitrary"),
        name="adaln",
    )(cvec, ada_w, ada_b.reshape(depth, 1, n6))


def _proj_kernel(x_ref, mod_ref, g_ref, w_ref, o_ref):
    n = _modulated_norm(x_ref[...], g_ref[...], mod_ref[0:1, :], mod_ref[1:2, :])
    o_ref[...] = jnp.dot(n.astype(BF16), w_ref[...], preferred_element_type=F32).astype(o_ref.dtype)


def _proj_res_kernel(h_ref, f_ref, pmod_ref, mod_ref, g_ref, w_ref, x_ref, o_ref):
    x = h_ref[...] + pmod_ref[5:6, :] * f_ref[...]
    x_ref[...] = x
    n = _modulated_norm(x, g_ref[...], mod_ref[0:1, :], mod_ref[1:2, :])
    o_ref[...] = jnp.dot(n.astype(BF16), w_ref[...], preferred_element_type=F32).astype(o_ref.dtype)


def _mod_spec(tiles_per_seq, n_batch):
    return pl.BlockSpec((None, 6, D_MODEL), lambda i: (jnp.minimum(i // tiles_per_seq, n_batch), 0, 0))


def _proj_call(x, mod, g, w, tiles_per_seq, n_batch):
    n, d = x.shape
    n_out = w.shape[1]
    return pl.pallas_call(
        _proj_kernel,
        grid=(n // ROW_TILE,),
        in_specs=[pl.BlockSpec((ROW_TILE, d), lambda i: (i, 0)),
                  _mod_spec(tiles_per_seq, n_batch),
                  pl.BlockSpec((1, d), lambda i: (0, 0)),
                  pl.BlockSpec((d, n_out), lambda i: (0, 0))],
        out_specs=pl.BlockSpec((ROW_TILE, n_out), lambda i: (i, 0)),
        out_shape=jax.ShapeDtypeStruct((n, n_out), F32),
        compiler_params=_params("arbitrary"),
        name="in_proj",
    )(x, mod, g, w)


def _proj_res_call(h, ffn, pmod, mod, g, w, tiles_per_seq, n_batch):
    n, d = h.shape
    n_out = w.shape[1]
    row = pl.BlockSpec((ROW_TILE, d), lambda i: (i, 0))
    return pl.pallas_call(
        _proj_res_kernel,
        grid=(n // ROW_TILE,),
        in_specs=[row, row, _mod_spec(tiles_per_seq, n_batch), _mod_spec(tiles_per_seq, n_batch),
                  pl.BlockSpec((1, d), lambda i: (0, 0)),
                  pl.BlockSpec((d, n_out), lambda i: (0, 0))],
        out_specs=[row, pl.BlockSpec((ROW_TILE, n_out), lambda i: (i, 0))],
        out_shape=[jax.ShapeDtypeStruct((n, d), F32), jax.ShapeDtypeStruct((n, n_out), F32)],
        compiler_params=_params("arbitrary"),
        name="in_proj_res",
    )(h, ffn, pmod, mod, g, w)


def _qkprep_kernel(q_ref, k_ref, v_ref, c_ref, s_ref, qg_ref, kg_ref, m_ref, qo_ref, kvo_ref):
    cos = c_ref[...]
    sin = s_ref[...]
    lane = lax.broadcasted_iota(jnp.int32, cos.shape, 1)
    first = (lane % 32) < 16
    low = lane < HEAD_DIM

    def norm_rope(x, g):
        ms = jnp.dot(x * x, m_ref[...], preferred_element_type=F32, precision=HIGHEST)
        xn = (x * lax.rsqrt(ms + EPS)) * g
        partner = jnp.where(first, pltpu.roll(xn, LANES - 16, 1), pltpu.roll(xn, 16, 1))
        return xn * cos + partner * sin

    for j in range(ATT_Q_WIDTH // LANES):
        qj = norm_rope(q_ref[:, j * LANES:(j + 1) * LANES], qg_ref[...])
        qo_ref[:, j * LANES:(j + 1) * LANES] = (qj * HEAD_DIM ** -0.5).astype(BF16)
    kr = norm_rope(k_ref[...], kg_ref[...])
    ks = pltpu.roll(kr, HEAD_DIM, 1)
    kvo_ref[:, 0:LANES] = jnp.where(low, kr, ks).astype(BF16)
    kvo_ref[:, LANES:2 * LANES] = jnp.where(low, ks, kr).astype(BF16)
    kvo_ref[:, 2 * LANES:3 * LANES] = v_ref[...].astype(BF16)


def _qkprep_call(proj, cos_t, sin_t, qg, kg, seg_mean, tiles_per_seq, n_lat_tiles):
    n = proj.shape[0]
    qcol = (2 * LRU_WIDTH) // ATT_Q_WIDTH
    kcol = (2 * LRU_WIDTH + ATT_Q_WIDTH) // LANES

    def pos(i):
        return (jnp.where(i < n_lat_tiles, i % tiles_per_seq, tiles_per_seq), 0)

    return pl.pallas_call(
        _qkprep_kernel,
        grid=(n // ROW_TILE,),
        in_specs=[pl.BlockSpec((ROW_TILE, ATT_Q_WIDTH), lambda i: (i, qcol)),
                  pl.BlockSpec((ROW_TILE, LANES), lambda i: (i, kcol)),
                  pl.BlockSpec((ROW_TILE, LANES), lambda i: (i, kcol + 1)),
                  pl.BlockSpec((ROW_TILE, LANES), pos),
                  pl.BlockSpec((ROW_TILE, LANES), pos),
                  pl.BlockSpec((1, LANES), lambda i: (0, 0)),
                  pl.BlockSpec((1, LANES), lambda i: (0, 0)),
                  pl.BlockSpec((LANES, LANES), lambda i: (0, 0))],
        out_specs=[pl.BlockSpec((ROW_TILE, ATT_Q_WIDTH), lambda i: (i, 0)),
                   pl.BlockSpec((ROW_TILE, 3 * LANES), lambda i: (i, 0))],
        out_shape=[jax.ShapeDtypeStruct((n, ATT_Q_WIDTH), BF16),
                   jax.ShapeDtypeStruct((n, 3 * LANES), BF16)],
        compiler_params=_params("arbitrary"),
        name="qk_prep",
    )(proj, proj, proj, cos_t, sin_t, qg, kg, seg_mean)


def _attend(q, kv, sink_ref, mask):
    rows = q.shape[0]
    lane = lax.broadcasted_iota(jnp.int32, (rows, LANES), 1)
    low = lane < HEAD_DIM
    zero = jnp.zeros((rows, LANES), BF16)
    vv = kv[:, 2 * LANES:3 * LANES]
    groups = []
    heads_per_kv = ATT_HEADS // ATT_KV_HEADS
    for kh in range(ATT_KV_HEADS):
        kk = kv[:, kh * LANES:(kh + 1) * LANES]
        parts = []
        sinks = []
        for j in range(heads_per_kv):
            h = kh * heads_per_kv + j
            qg = q[:, (h // 2) * LANES:(h // 2 + 1) * LANES]
            parts.append(jnp.where(low, qg, zero) if h % 2 == 0 else jnp.where(low, zero, qg))
            sinks.append(jnp.full((rows, 1), sink_ref[h], F32))
        lhs = jnp.concatenate(parts, axis=0)
        sink = jnp.concatenate(sinks, axis=0)
        s = lax.dot_general(lhs, kk, (((1,), (1,)), ((), ())), preferred_element_type=F32)
        if mask is not None:
            s = jnp.where(mask, s, -1e30)
        m = jnp.maximum(jnp.max(s, axis=-1, keepdims=True), sink)
        e = jnp.exp(s - m)
        denom = jnp.sum(e, axis=-1, keepdims=True) + jnp.exp(sink - m)
        o = jnp.dot(e.astype(BF16), vv, preferred_element_type=F32) / denom
        r = [o[j * rows:(j + 1) * rows] for j in range(heads_per_kv)]
        for p in range(heads_per_kv // 2):
            even, odd = r[2 * p], r[2 * p + 1]
            if kh == 0:
                groups.append(jnp.where(low, even, pltpu.roll(odd, HEAD_DIM, 1)))
            else:
                groups.append(jnp.where(low, pltpu.roll(even, HEAD_DIM, 1), odd))
    return jnp.concatenate(groups, axis=1)


def _attn_kernel(sink_ref, q_ref, kvp_ref, kvo_ref, kvn_ref, kvc_ref, o_ref, *, n_blocks, ctx_len):
    n = pl.program_id(1)
    kv = jnp.concatenate([kvp_ref[...], kvo_ref[...], kvn_ref[...], kvc_ref[...]], axis=0)
    blk = ATT_BLOCK
    stacked = (ATT_HEADS // ATT_KV_HEADS) * blk
    qi = lax.broadcasted_iota(jnp.int32, (stacked, 3 * blk + ctx_len), 0) % blk
    ks = lax.broadcasted_iota(jnp.int32, (stacked, 3 * blk + ctx_len), 1)
    in_win = jnp.abs(ks - blk - qi) <= WINDOW
    in_seq = ((ks >= blk) | (n > 0)) & ((ks < 2 * blk) | (n < n_blocks - 1))
    mask = (in_win & in_seq) | (ks >= 3 * blk)
    o_ref[...] = _attend(q_ref[...], kv, sink_ref, mask)


def _ctx_attn_kernel(sink_ref, q_ref, kvc_ref, o_ref):
    o_ref[...] = _attend(q_ref[...], kvc_ref[...], sink_ref, None)


def _attn_call(qrot, kvrot, sink, n_batch, seq, ctx_len):
    nb = seq // ATT_BLOCK
    ctx0 = (n_batch * seq) // ctx_len
    kvw = kvrot.shape[1]
    return pl.pallas_call(
        functools.partial(_attn_kernel, n_blocks=nb, ctx_len=ctx_len),
        grid=(n_batch, nb),
        in_specs=[pl.BlockSpec(memory_space=pltpu.SMEM),
                  pl.BlockSpec((ATT_BLOCK, ATT_Q_WIDTH), lambda b, n: (b * nb + n, 0)),
                  pl.BlockSpec((ATT_BLOCK, kvw), lambda b, n: (b * nb + jnp.maximum(n - 1, 0), 0)),
                  pl.BlockSpec((ATT_BLOCK, kvw), lambda b, n: (b * nb + n, 0)),
                  pl.BlockSpec((ATT_BLOCK, kvw), lambda b, n: (b * nb + jnp.minimum(n + 1, nb - 1), 0)),
                  pl.BlockSpec((ctx_len, kvw), lambda b, n: (ctx0 + b, 0))],
        out_specs=pl.BlockSpec((ATT_BLOCK, ATT_Q_WIDTH), lambda b, n: (b * nb + n, 0)),
        out_shape=jax.ShapeDtypeStruct((n_batch * seq, ATT_Q_WIDTH), F32),
        compiler_params=_params("arbitrary", "arbitrary"),
        name="window_attn",
    )(sink, qrot, kvrot, kvrot, kvrot, kvrot)


def _ctx_attn_call(qrot, kvrot, sink, n_batch, seq, ctx_len):
    ctx0 = (n_batch * seq) // ctx_len
    kvw = kvrot.shape[1]
    return pl.pallas_call(
        _ctx_attn_kernel,
        grid=(n_batch,),
        in_specs=[pl.BlockSpec(memory_space=pltpu.SMEM),
                  pl.BlockSpec((ctx_len, ATT_Q_WIDTH), lambda b: (ctx0 + b, 0)),
                  pl.BlockSpec((ctx_len, kvw), lambda b: (ctx0 + b, 0))],
        out_specs=pl.BlockSpec((ctx_len, ATT_Q_WIDTH), lambda b: (b, 0)),
        out_shape=jax.ShapeDtypeStruct((n_batch * ctx_len, ATT_Q_WIDTH), F32),
        compiler_params=_params("arbitrary"),
        name="ctx_attn",
    )(sink, qrot, kvrot)


def _scan_rows(a, x, carry, reverse):
    rows, width = a.shape
    sub = lax.broadcasted_iota(jnp.int32, (rows, width), 0) % SUBLANES
    for s in (1, 2, 4):
        shift = rows - s if reverse else s
        a_sh = pltpu.roll(a, shift, 0)
        x_sh = pltpu.roll(x, shift, 0)
        keep = (sub <= SUBLANES - 1 - s) if reverse else (sub >= s)
        x = x + a * jnp.where(keep, x_sh, 0.0)
        a = a * jnp.where(keep, a_sh, 1.0)
    n_groups = rows // SUBLANES
    out = [None] * n_groups
    order = range(n_groups - 1, -1, -1) if reverse else range(n_groups)
    for g in order:
        hg = x[g * SUBLANES:(g + 1) * SUBLANES] + a[g * SUBLANES:(g + 1) * SUBLANES] * carry
        carry = hg[0:1] if reverse else hg[SUBLANES - 1:SUBLANES]
        out[g] = hg
    return jnp.concatenate(out, axis=0), carry


def _rglru_kernel(ul_ref, gl_ref, uc_ref, gc_ref, cw_ref, cb_ref, wr_ref, wi_ref, br_ref, bi_ref, lam_ref,
                  yl_ref, yc_ref, upad_ref, conv_ref, *, seq, ctx_len):
    width = ul_ref.shape[1]
    tc = LRU_CHUNK
    neg_lam = -lam_ref[...]
    softplus = jnp.maximum(neg_lam, 0.0) + jnp.log1p(jnp.exp(-jnp.abs(neg_lam)))
    zeros = jnp.zeros((SUBLANES, width), F32)

    def gates(u, d):
        ub = u.astype(BF16)
        r = jax.nn.sigmoid(jnp.dot(ub, wr_ref[d], preferred_element_type=F32) + br_ref[d:d + 1, :])
        i = jax.nn.sigmoid(jnp.dot(ub, wi_ref[d], preferred_element_type=F32) + bi_ref[d:d + 1, :])
        log_a = (-LRU_C) * r * softplus[d:d + 1, :]
        a = jnp.exp(log_a)
        return a, jnp.sqrt(1.0 - a * a) * (i * u)

    def run(u_ref, g_ref, y_ref, n, h_fwd, h_bwd):
        n_chunks = n // tc
        upad_ref[0:SUBLANES, :] = zeros
        upad_ref[pl.ds(SUBLANES, n), :] = u_ref[...]
        upad_ref[pl.ds(SUBLANES + n, SUBLANES), :] = zeros

        def conv_body(c, carry):
            t0 = pl.multiple_of(c * tc, tc)
            win = upad_ref[pl.ds(t0, tc + 2 * SUBLANES), :]
            acc = cb_ref[...]
            for k in range(CONV_W):
                acc = acc + cw_ref[k:k + 1, :] * win[SUBLANES - 1 + k:SUBLANES - 1 + k + tc]
            conv_ref[pl.ds(t0, tc), :] = acc
            return carry

        lax.fori_loop(0, n_chunks, conv_body, 0)

        def fwd_body(c, carry):
            t0 = pl.multiple_of(c * tc, tc)
            a, x = gates(conv_ref[pl.ds(t0, tc), :], 0)
            h, carry = _scan_rows(a, x, carry, False)
            y_ref[pl.ds(t0, tc), :] = h
            return carry

        h_fwd = lax.fori_loop(0, n_chunks, fwd_body, h_fwd)

        def bwd_body(c, carry):
            t0 = pl.multiple_of((n_chunks - 1 - c) * tc, tc)
            a, x = gates(conv_ref[pl.ds(t0, tc), :], 1)
            h, carry = _scan_rows(a, x, carry, True)
            y_ref[pl.ds(t0, tc), :] = (y_ref[pl.ds(t0, tc), :] + h) * jax.nn.gelu(g_ref[pl.ds(t0, tc), :])
            return carry

        h_bwd = lax.fori_loop(0, n_chunks, bwd_body, h_bwd)
        return h_fwd, h_bwd

    h0 = jnp.zeros((1, width), F32)
    h_fwd, h_bwd = run(uc_ref, gc_ref, yc_ref, ctx_len, h0, h0)
    run(ul_ref, gl_ref, yl_ref, seq, h_fwd, h_bwd)


def _rglru_call(proj, conv_w, conv_b, wr, wi, br, bi, lam, n_batch, seq, ctx_len):
    ctx0 = (n_batch * seq) // ctx_len
    halves = LRU_WIDTH // LRU_HALF
    w = LRU_HALF
    vec2 = pl.BlockSpec((2, w), lambda b, c: (0, c))
    return pl.pallas_call(
        functools.partial(_rglru_kernel, seq=seq, ctx_len=ctx_len),
        grid=(n_batch, halves),
        in_specs=[pl.BlockSpec((seq, w), lambda b, c: (b, c)),
                  pl.BlockSpec((seq, w), lambda b, c: (b, halves + c)),
                  pl.BlockSpec((ctx_len, w), lambda b, c: (ctx0 + b, c)),
                  pl.BlockSpec((ctx_len, w), lambda b, c: (ctx0 + b, halves + c)),
                  pl.BlockSpec((CONV_W, w), lambda b, c: (0, c)),
                  pl.BlockSpec((1, w), lambda b, c: (0, c)),
                  pl.BlockSpec((None, 2, w, w), lambda b, c: (c, 0, 0, 0)),
                  pl.BlockSpec((None, 2, w, w), lambda b, c: (c, 0, 0, 0)),
                  vec2, vec2, vec2],
        out_specs=[pl.BlockSpec((seq, w), lambda b, c: (b, c)),
                   pl.BlockSpec((ctx_len, w), lambda b, c: (b, c))],
        out_shape=[jax.ShapeDtypeStruct((n_batch * seq, LRU_WIDTH), F32),
                   jax.ShapeDtypeStruct((n_batch * ctx_len, LRU_WIDTH), F32)],
        scratch_shapes=[pltpu.VMEM((seq + 2 * SUBLANES, w), F32), pltpu.VMEM((seq, w), F32)],
        compiler_params=_params("arbitrary", "arbitrary"),
        name="rglru",
    )(proj, proj, proj, proj, conv_w, conv_b, wr, wi, br, bi, lam)


def _gla_kernel(q_ref, k_ref, v_ref, r_ref, a_ref, kc_ref, vc_ref, ac_ref, gw_ref, gb_ref, ng_ref,
                o_ref, ob_ref, *, seq, ctx_len):
    ch = GLA_CHUNK
    row = lax.broadcasted_iota(jnp.int32, (ch, ch), 0)
    col = lax.broadcasted_iota(jnp.int32, (ch, ch), 1)
    keep = (row >= col, row <= col)
    tri = (keep[0].astype(F32), keep[1].astype(F32))
    scale = GLA_DK ** -0.5

    def decay(a_blk, d):
        pre = jnp.dot(a_blk, gw_ref[d], preferred_element_type=F32, precision=HIGHEST) + gb_ref[d:d + 1, :]
        log_a = (jnp.minimum(pre, 0.0) - jnp.log1p(jnp.exp(-jnp.abs(pre)))) / GLA_TAU
        b = jnp.dot(tri[d], log_a, preferred_element_type=F32, precision=HIGHEST)
        g = b[ch - 1:ch] if d == 0 else b[0:1]
        return b, g

    def update(state, k_blk, v_blk, b, g):
        k_end = k_blk * jnp.exp(g - b)
        aug = jnp.concatenate([k_end, jnp.broadcast_to(jnp.exp(g), (ch, GLA_DK))], axis=0)
        aug_t = aug.T
        k_t = aug_t[:, 0:ch].astype(BF16)
        decay_t = jnp.concatenate([aug_t[:, ch:2 * ch]] * (GLA_DV // ch), axis=1)
        return decay_t * state + jnp.dot(k_t, v_blk.astype(BF16), preferred_element_type=F32)

    def output(state, q_blk, k_blk, v_blk, b, d):
        q_dec = ((q_blk * scale) * jnp.exp(b)).astype(BF16)
        k_inc = (k_blk * jnp.exp(-b)).astype(BF16)
        intra = lax.dot_general(q_dec, k_inc, (((1,), (1,)), ((), ())), preferred_element_type=F32)
        intra = jnp.where(keep[d], intra, 0.0).astype(BF16)
        return (jnp.dot(intra, v_blk.astype(BF16), preferred_element_type=F32)
                + jnp.dot(q_dec, state.astype(BF16), preferred_element_type=F32))

    n_ctx = ctx_len // ch
    n_lat = seq // ch

    def ctx_body(c, states):
        new = []
        for d in range(2):
            t0 = pl.multiple_of((c if d == 0 else n_ctx - 1 - c) * ch, ch)
            b, g = decay(ac_ref[pl.ds(t0, ch), :], d)
            new.append(update(states[d], kc_ref[pl.ds(t0, ch), :], vc_ref[pl.ds(t0, ch), :], b, g))
        return tuple(new)

    zero = jnp.zeros((GLA_DK, GLA_DV), F32)
    states = lax.fori_loop(0, n_ctx, ctx_body, (zero, zero))

    def lat_body(c, states):
        new = []
        for d, dst in ((0, o_ref), (1, ob_ref)):
            t0 = pl.multiple_of((c if d == 0 else n_lat - 1 - c) * ch, ch)
            b, g = decay(a_ref[pl.ds(t0, ch), :], d)
            k_blk = k_ref[pl.ds(t0, ch), :]
            v_blk = v_ref[pl.ds(t0, ch), :]
            dst[pl.ds(t0, ch), :] = output(states[d], q_ref[pl.ds(t0, ch), :], k_blk, v_blk, b, d)
            new.append(update(states[d], k_blk, v_blk, b, g))
        return tuple(new)

    lax.fori_loop(0, n_lat, lat_body, states)

    tn = 4 * ch

    def finish_body(c, carry):
        t0 = pl.multiple_of(c * tn, tn)
        o = o_ref[pl.ds(t0, tn), :] + ob_ref[pl.ds(t0, tn), :]
        on = (o * lax.rsqrt(jnp.mean(o * o, axis=-1, keepdims=True) + EPS)) * ng_ref[...]
        r = r_ref[pl.ds(t0, tn), :]
        o_ref[pl.ds(t0, tn), :] = on * (r * jax.nn.sigmoid(r))
        return carry

    lax.fori_loop(0, seq // tn, finish_body, 0)


def _gla_call(proj, gate_w, gate_b, norm_g, n_batch, seq, ctx_len):
    ctx0 = (n_batch * seq) // ctx_len
    kcol = GLA_K_WIDTH // GLA_DK
    vcol = (2 * GLA_K_WIDTH) // GLA_DV
    rcol = (2 * GLA_K_WIDTH + GLA_V_WIDTH) // GLA_DV
    acol = (2 * GLA_K_WIDTH + 2 * GLA_V_WIDTH) // LANES
    return pl.pallas_call(
        functools.partial(_gla_kernel, seq=seq, ctx_len=ctx_len),
        grid=(n_batch, GLA_HEADS),
        in_specs=[pl.BlockSpec((seq, GLA_DK), lambda b, h: (b, h)),
                  pl.BlockSpec((seq, GLA_DK), lambda b, h: (b, kcol + h)),
                  pl.BlockSpec((seq, GLA_DV), lambda b, h: (b, vcol + h)),
                  pl.BlockSpec((seq, GLA_DV), lambda b, h: (b, rcol + h)),
                  pl.BlockSpec((seq, LANES), lambda b, h: (b, acol)),
                  pl.BlockSpec((ctx_len, GLA_DK), lambda b, h: (ctx0 + b, kcol + h)),
                  pl.BlockSpec((ctx_len, GLA_DV), lambda b, h: (ctx0 + b, vcol + h)),
                  pl.BlockSpec((ctx_len, LANES), lambda b, h: (ctx0 + b, acol)),
                  pl.BlockSpec((None, 2, LANES, GLA_DK), lambda b, h: (h, 0, 0, 0)),
                  pl.BlockSpec((None, 2, GLA_DK), lambda b, h: (h, 0, 0)),
                  pl.BlockSpec((1, GLA_DV), lambda b, h: (0, 0))],
        out_specs=pl.BlockSpec((seq, GLA_DV), lambda b, h: (b, h)),
        out_shape=jax.ShapeDtypeStruct((n_batch * seq, GLA_V_WIDTH), F32),
        scratch_shapes=[pltpu.VMEM((seq, GLA_DV), F32)],
        compiler_params=_params("arbitrary", "arbitrary"),
        name="gla",
    )(proj, proj, proj, proj, proj, proj, proj, proj, gate_w, gate_b, norm_g)


def _route(rows, rw_ref, rb_ref):
    n_rows = rows.shape[0]
    logits = lax.dot_general(rw_ref[...], rows, (((1,), (1,)), ((), ())),
                             preferred_element_type=F32, precision=HIGHEST)
    scores = jax.nn.sigmoid(logits)
    sel = scores + rb_ref[...]
    eid = lax.broadcasted_iota(jnp.int32, (N_EXPERTS, n_rows), 0)
    gid = eid // EXPERTS_PER_GROUP
    neg = -jnp.inf
    big = N_EXPERTS

    def top2(masked):
        m1 = jnp.max(masked, axis=0, keepdims=True)
        i1 = jnp.min(jnp.where(masked == m1, eid, big), axis=0, keepdims=True)
        rest = jnp.where(eid == i1, neg, masked)
        m2 = jnp.max(rest, axis=0, keepdims=True)
        i2 = jnp.min(jnp.where(rest == m2, eid, big), axis=0, keepdims=True)
        return m1, i1, m2, i2

    best = None
    best_g = None
    for g in range(N_GROUPS):
        m1, _, m2, _ = top2(jnp.where(gid == g, sel, neg))
        gs = m1 + m2
        if best is None:
            best, best_g = gs, jnp.zeros((1, n_rows), jnp.int32)
        else:
            better = gs > best
            best_g = jnp.where(better, g, best_g)
            best = jnp.where(better, gs, best)
    _, i1, _, i2 = top2(jnp.where(gid == best_g, sel, neg))
    s1 = jnp.sum(jnp.where(eid == i1, scores, 0.0), axis=0, keepdims=True)
    s2 = jnp.sum(jnp.where(eid == i2, scores, 0.0), axis=0, keepdims=True)
    total = s1 + s2
    swap = i2 < i1
    lo = jnp.where(swap, i2, i1)
    hi = jnp.where(swap, i1, i2)
    w_lo = jnp.where(swap, s2, s1) / total
    w_hi = jnp.where(swap, s1, s2) / total
    pad = jnp.zeros((4, n_rows), F32)
    return jnp.concatenate([lo.astype(F32), hi.astype(F32), w_lo, w_hi, pad], axis=0)


def _finish_rows(y, h_ref, mod_ref, g_ref, rw_ref, rb_ref, hn_ref, rows_ref, rt_ref):
    hn = h_ref[...] + mod_ref[2:3, :] * y
    hn_ref[...] = hn
    rows = _modulated_norm(hn, g_ref[...], mod_ref[3:4, :], mod_ref[4:5, :])
    rows_ref[...] = rows
    rt_ref[...] = _route(rows, rw_ref, rb_ref)


def _outproj_even_kernel(al_ref, ac_ref, bl_ref, bc_ref, w_ref, h_ref, mod_ref, g_ref, rw_ref, rb_ref,
                         hn_ref, rows_ref, rt_ref, *, n_lat_tiles):
    is_ctx = pl.program_id(0) >= n_lat_tiles
    a = jnp.where(is_ctx, ac_ref[...], al_ref[...]).astype(BF16)
    b = jnp.where(is_ctx, bc_ref[...], bl_ref[...]).astype(BF16)
    y = (jnp.dot(a, w_ref[0:LRU_WIDTH, :], preferred_element_type=F32)
         + jnp.dot(b, w_ref[LRU_WIDTH:LRU_WIDTH + ATT_Q_WIDTH, :], preferred_element_type=F32))
    _finish_rows(y, h_ref, mod_ref, g_ref, rw_ref, rb_ref, hn_ref, rows_ref, rt_ref)


def _outproj_odd_kernel(o_ref, w_ref, h_ref, mod_ref, g_ref, rw_ref, rb_ref, hn_ref, rows_ref, rt_ref):
    y = jnp.dot(o_ref[...].astype(BF16), w_ref[...], preferred_element_type=F32)
    _finish_rows(y, h_ref, mod_ref, g_ref, rw_ref, rb_ref, hn_ref, rows_ref, rt_ref)


def _outproj_specs(n_rows, tiles_per_seq, n_batch):
    d = D_MODEL
    row = pl.BlockSpec((ROW_TILE, d), lambda i: (i, 0))
    tail_in = [row, _mod_spec(tiles_per_seq, n_batch),
               pl.BlockSpec((1, d), lambda i: (0, 0)),
               pl.BlockSpec((N_EXPERTS, d), lambda i: (0, 0)),
               pl.BlockSpec((N_EXPERTS, 1), lambda i: (0, 0))]
    out_specs = [row, row, pl.BlockSpec((SUBLANES, ROW_TILE), lambda i: (0, i))]
    out_shape = [jax.ShapeDtypeStruct((n_rows, d), F32), jax.ShapeDtypeStruct((n_rows, d), F32),
                 jax.ShapeDtypeStruct((SUBLANES, n_rows), F32)]
    return tail_in, out_specs, out_shape


def _outproj_even_call(a_lat, a_ctx, b_lat, b_ctx, w, h, mod, g, rw_t, rb, tiles_per_seq, n_batch):
    n_rows = h.shape[0]
    n_lat_tiles = a_lat.shape[0] // ROW_TILE
    tail_in, out_specs, out_shape = _outproj_specs(n_rows, tiles_per_seq, n_batch)
    lat = pl.BlockSpec((ROW_TILE, LRU_WIDTH), lambda i: (jnp.minimum(i, n_lat_tiles - 1), 0))
    ctx = pl.BlockSpec((ROW_TILE, LRU_WIDTH), lambda i: (jnp.maximum(i - n_lat_tiles, 0), 0))
    return pl.pallas_call(
        functools.partial(_outproj_even_kernel, n_lat_tiles=n_lat_tiles),
        grid=(n_rows // ROW_TILE,),
        in_specs=[lat, ctx, lat, ctx, pl.BlockSpec(w.shape, lambda i: (0, 0))] + tail_in,
        out_specs=out_specs,
        out_shape=out_shape,
        compiler_params=_params("arbitrary"),
        name="out_proj_even",
    )(a_lat, a_ctx, b_lat, b_ctx, w, h, mod, g, rw_t, rb)


def _outproj_odd_call(o, w, h, mod, g, rw_t, rb, tiles_per_seq, n_batch):
    n_rows = o.shape[0]
    tail_in, out_specs, out_shape = _outproj_specs(n_rows, tiles_per_seq, n_batch)
    return pl.pallas_call(
        _outproj_odd_kernel,
        grid=(n_rows // ROW_TILE,),
        in_specs=[pl.BlockSpec((ROW_TILE, o.shape[1]), lambda i: (i, 0)),
                  pl.BlockSpec(w.shape, lambda i: (0, 0))] + tail_in,
        out_specs=out_specs,
        out_shape=out_shape,
        compiler_params=_params("arbitrary"),
        name="out_proj_odd",
    )(o, w, h, mod, g, rw_t, rb)


def _gather_kernel(idx_ref, src_ref, dst_ref, sem):
    base = pl.program_id(0) * GATHER_ROWS

    def row_copy(r):
        return pltpu.make_async_copy(src_ref.at[pl.ds(idx_ref[0, 0, r], 1)],
                                     dst_ref.at[pl.ds(base + r, 1)], sem)

    def start(r, carry):
        row_copy(r).start()
        return carry

    def wait(r, carry):
        row_copy(r).wait()
        return carry

    lax.fori_loop(0, GATHER_ROWS, start, 0, unroll=8)
    lax.fori_loop(0, GATHER_ROWS, wait, 0, unroll=8)


def _gather_call(src, idx):
    n_out = idx.shape[0]
    steps = n_out // GATHER_ROWS
    return pl.pallas_call(
        _gather_kernel,
        grid=(steps,),
        in_specs=[pl.BlockSpec((1, 1, GATHER_ROWS), lambda i: (i, 0, 0), memory_space=pltpu.SMEM),
                  pl.BlockSpec(memory_space=pl.ANY)],
        out_specs=pl.BlockSpec(memory_space=pl.ANY),
        out_shape=jax.ShapeDtypeStruct((n_out, src.shape[1]), src.dtype),
        scratch_shapes=[pltpu.SemaphoreType.DMA(())],
        compiler_params=_params("arbitrary"),
        name="row_gather",
    )(idx.reshape(steps, 1, GATHER_ROWS), src)


def _experts_kernel(elo_ref, ehi_ref, used_ref, x_ref, cw_ref, wg0_ref, wu0_ref, wd0_ref,
                    wg1_ref, wu1_ref, wd1_ref, y_ref):
    t = pl.program_id(0)

    @pl.when(t < used_ref[0])
    def _():
        x = x_ref[...].astype(BF16)
        cw = cw_ref[...]

        def expert(wg_ref, wu_ref, wd_ref):
            gate = jnp.dot(x, wg_ref[...], preferred_element_type=F32)
            up = jnp.dot(x, wu_ref[...], preferred_element_type=F32)
            act = ((gate * jax.nn.sigmoid(gate)) * up).astype(BF16)
            return jnp.dot(act, wd_ref[...], preferred_element_type=F32)

        y_ref[...] = (cw[:, 0:1] * expert(wg0_ref, wu0_ref, wd0_ref)
                      + cw[:, 1:2] * expert(wg1_ref, wu1_ref, wd1_ref))

    @pl.when(t >= used_ref[0])
    def _():
        y_ref[...] = jnp.zeros_like(y_ref)


def _experts_call(xs, cw, e_lo, e_hi, n_used, w_gate, w_up, w_down):
    p, d = xs.shape
    tiles = p // MOE_TILE

    def xmap(t, elo, ehi, used):
        return (jnp.minimum(t, used[0] - 1), 0)

    def lo(t, elo, ehi, used):
        return (elo[t], 0, 0)

    def hi(t, elo, ehi, used):
        return (ehi[t], 0, 0)

    up_spec = lambda m: pl.BlockSpec((None, d, D_EXPERT), m)
    down_spec = lambda m: pl.BlockSpec((None, D_EXPERT, d), m)
    grid_spec = pltpu.PrefetchScalarGridSpec(
        num_scalar_prefetch=3,
        grid=(tiles,),
        in_specs=[pl.BlockSpec((MOE_TILE, d), xmap),
                  pl.BlockSpec((MOE_TILE, 2), xmap),
                  up_spec(lo), up_spec(lo), down_spec(lo),
                  up_spec(hi), up_spec(hi), down_spec(hi)],
        out_specs=pl.BlockSpec((MOE_TILE, d), lambda t, elo, ehi, used: (t, 0)),
    )
    return pl.pallas_call(
        _experts_kernel,
        grid_spec=grid_spec,
        out_shape=jax.ShapeDtypeStruct((p, d), F32),
        compiler_params=_params("arbitrary"),
        name="experts",
    )(e_lo, e_hi, n_used, xs, cw, w_gate, w_up, w_down, w_gate, w_up, w_down)


def _moe(rows, route, w_gate, w_up, w_down):
    n = rows.shape[0]
    tm = MOE_TILE
    max_tiles = -(-n // tm) + N_CLASSES
    max_tiles = -(-(max_tiles * tm) // GATHER_ROWS) * GATHER_ROWS // tm
    p = max_tiles * tm
    lo = route[0].astype(jnp.int32)
    hi = route[1].astype(jnp.int32)
    w_lo, w_hi = route[2], route[3]
    lo_in = lo % EXPERTS_PER_GROUP
    hi_in = hi % EXPERTS_PER_GROUP
    pair = lo_in * 3 - (lo_in * (lo_in - 1)) // 2 + (hi_in - lo_in - 1)
    cls = (lo // EXPERTS_PER_GROUP) * PAIRS_PER_GROUP + pair
    onehot = (cls[:, None] == jnp.arange(N_CLASSES, dtype=jnp.int32)[None, :]).astype(jnp.int32)
    counts = jnp.sum(onehot, axis=0)
    rank = jnp.sum(jnp.cumsum(onehot, axis=0) * onehot, axis=1) - 1
    tiles_c = (counts + tm - 1) // tm
    tile_end = jnp.cumsum(tiles_c)
    tile_start = tile_end - tiles_c
    n_used = tile_end[-1]
    slot_of_token = (tile_start[cls] * tm + rank).astype(jnp.int32)
    tile_ids = jnp.arange(max_tiles, dtype=jnp.int32)
    tile_cls = jnp.minimum(jnp.sum((tile_end[None, :] <= tile_ids[:, None]).astype(jnp.int32), axis=1),
                           N_CLASSES - 1)
    src = jnp.zeros((p,), jnp.int32).at[slot_of_token].set(jnp.arange(n, dtype=jnp.int32))
    cw = jnp.zeros((p, 2), F32).at[slot_of_token].set(jnp.stack([w_lo, w_hi], axis=1))
    group0 = (tile_cls // PAIRS_PER_GROUP) * EXPERTS_PER_GROUP
    e_lo = group0 + jnp.asarray(PAIR_LO, jnp.int32)[tile_cls % PAIRS_PER_GROUP]
    e_hi = group0 + jnp.asarray(PAIR_HI, jnp.int32)[tile_cls % PAIRS_PER_GROUP]
    xs = _gather_call(rows, src)
    ys = _experts_call(xs, cw, e_lo.astype(jnp.int32), e_hi.astype(jnp.int32),
                       n_used.reshape(1).astype(jnp.int32), w_gate, w_up, w_down)
    n_pad = -(-n // GATHER_ROWS) * GATHER_ROWS
    back = jnp.zeros((n_pad,), jnp.int32).at[:n].set(slot_of_token)
    return _gather_call(ys, back)[:n]


def _final_kernel(h_ref, f_ref, mod_ref, o_ref):
    o_ref[...] = h_ref[...] + mod_ref[5:6, :] * f_ref[...]


def _final_call(h, ffn, mod, n_rows, tiles_per_seq, n_batch):
    d = h.shape[1]
    row = pl.BlockSpec((ROW_TILE, d), lambda i: (i, 0))
    return pl.pallas_call(
        _final_kernel,
        grid=(n_rows // ROW_TILE,),
        in_specs=[row, row, _mod_spec(tiles_per_seq, n_batch)],
        out_specs=row,
        out_shape=jax.ShapeDtypeStruct((n_rows, d), F32),
        compiler_params=_params("arbitrary"),
        name="final_residual",
    )(h, ffn, mod)


def _rope_tables(seq):
    rows = seq // GRID_W
    row = jnp.repeat(jnp.arange(rows), GRID_W).astype(F32)
    col = jnp.tile(jnp.arange(GRID_W), rows).astype(F32)
    n_freq = HEAD_DIM // 4
    inv_freq = ROPE_BASE ** (-jnp.arange(n_freq, dtype=F32) / n_freq)
    ang_r = row[:, None] * inv_freq
    ang_c = col[:, None] * inv_freq
    cos = jnp.concatenate([jnp.cos(ang_r), jnp.cos(ang_r), jnp.cos(ang_c), jnp.cos(ang_c)], axis=1)
    sin = jnp.concatenate([-jnp.sin(ang_r), jnp.sin(ang_r), -jnp.sin(ang_c), jnp.sin(ang_c)], axis=1)
    cos = jnp.concatenate([jnp.tile(cos, (1, 2)), jnp.ones((ROW_TILE, LANES), F32)], axis=0)
    sin = jnp.concatenate([jnp.tile(sin, (1, 2)), jnp.zeros((ROW_TILE, LANES), F32)], axis=0)
    return cos, sin


def _block_diag_halves(w):
    per_half = LRU_HALF // LRU_BLOCK
    halves = LRU_WIDTH // LRU_HALF
    out = jnp.zeros((halves, 2, LRU_HALF, LRU_HALF), F32)
    for c in range(halves):
        for j in range(per_half):
            s = slice(j * LRU_BLOCK, (j + 1) * LRU_BLOCK)
            out = out.at[c, :, s, s].set(w[:, c * per_half + j])
    return out.astype(BF16)


def _gla_gate_weights(gate_up, gate_b):
    w = jnp.zeros((GLA_HEADS, 2, LANES, GLA_DK), F32)
    for d in range(2):
        wd = gate_up[d].reshape(GLA_LOWRANK, GLA_HEADS, GLA_DK).transpose(1, 0, 2)
        w = w.at[:, d, d * GLA_LOWRANK:(d + 1) * GLA_LOWRANK, :].set(wd)
    b = gate_b.reshape(2, GLA_HEADS, GLA_DK).transpose(1, 0, 2)
    return w, b


def kernel(x, c, ctx, c_ctx, router_w, router_bias, ada_w, ada_b, norm_mix_g, norm_ffn_g, moe_w_gate, moe_w_up, moe_w_down, ev_w_in, ev_w_out, ev_conv_w, ev_conv_b, ev_rg_w, ev_rg_b, ev_ig_w, ev_ig_b, ev_lambda, ev_q_norm_g, ev_k_norm_g, ev_sink, od_w_in, od_w_out, od_gate_up, od_gate_b, od_out_norm_g):
    n_batch, seq, d = x.shape
    ctx_len = ctx.shape[1]
    n_lat = n_batch * seq
    n_ctx = n_batch * ctx_len
    assert ada_w.shape[0] == 2 and d == D_MODEL
    assert seq % ROW_TILE == 0 and n_ctx % ROW_TILE == 0 and seq % ctx_len == 0
    assert ctx_len % LRU_CHUNK == 0 and seq % (4 * GLA_CHUNK) == 0
    tiles_per_seq = seq // ROW_TILE
    n_lat_tiles = n_lat // ROW_TILE

    h = jnp.concatenate([x.reshape(n_lat, d), ctx.reshape(n_ctx, d)], axis=0)
    mod_rows = -(-(n_batch + 1) // SUBLANES) * SUBLANES
    cvec = jnp.zeros((mod_rows, d), F32).at[:n_batch].set(c).at[n_batch].set(c_ctx)
    mods = _ada_call(cvec, ada_w, ada_b).reshape(2, mod_rows, 6, d)
    mod_mix = [mods[l][:n_batch + 1, 0:6] for l in range(2)]
    rw_t = router_w.T
    rb = router_bias.reshape(N_EXPERTS, 1)

    w_in0 = ev_w_in[0].astype(BF16)
    w_out0 = ev_w_out[0].astype(BF16)
    proj0 = _proj_call(h, mod_mix[0], norm_mix_g[0:1], w_in0, tiles_per_seq, n_batch)
    cos_t, sin_t = _rope_tables(seq)
    seg = np.kron(np.eye(LANES // HEAD_DIM, dtype=np.float32), np.full((HEAD_DIM, HEAD_DIM), 1.0 / HEAD_DIM, np.float32))
    qrot, kvrot = _qkprep_call(proj0, cos_t, sin_t, jnp.tile(ev_q_norm_g[0], 2)[None, :],
                               jnp.tile(ev_k_norm_g[0], 2)[None, :], jnp.asarray(seg),
                               tiles_per_seq, n_lat_tiles)
    a_lat, a_ctx = _rglru_call(proj0, ev_conv_w[0], ev_conv_b[0:1], _block_diag_halves(ev_rg_w[0]),
                               _block_diag_halves(ev_ig_w[0]), ev_rg_b[0], ev_ig_b[0], ev_lambda[0],
                               n_batch, seq, ctx_len)
    b_lat = _attn_call(qrot, kvrot, ev_sink[0], n_batch, seq, ctx_len)
    b_ctx = _ctx_attn_call(qrot, kvrot, ev_sink[0], n_batch, seq, ctx_len)
    h, rows, route = _outproj_even_call(a_lat, a_ctx, b_lat, b_ctx, w_out0, h, mod_mix[0], norm_ffn_g[0:1],
                                        rw_t, rb, tiles_per_seq, n_batch)
    ffn = _moe(rows, route, moe_w_gate[0].astype(BF16), moe_w_up[0].astype(BF16), moe_w_down[0].astype(BF16))

    w_in1 = jnp.pad(od_w_in[0], ((0, 0), (0, ODD_IN_PAD - ODD_IN))).astype(BF16)
    h, proj1 = _proj_res_call(h, ffn, mod_mix[0], mod_mix[1], norm_mix_g[1:2], w_in1, tiles_per_seq, n_batch)
    gate_w, gate_b = _gla_gate_weights(od_gate_up[0], od_gate_b[0])
    o = _gla_call(proj1, gate_w, gate_b, od_out_norm_g[0:1], n_batch, seq, ctx_len)
    h_lat, rows, route = _outproj_odd_call(o, od_w_out[0].astype(BF16), h, mod_mix[1], norm_ffn_g[1:2],
                                           rw_t, rb, tiles_per_seq, n_batch)
    ffn = _moe(rows, route, moe_w_gate[1].astype(BF16), moe_w_up[1].astype(BF16), moe_w_down[1].astype(BF16))
    out = _final_call(h_lat, ffn, mod_mix[1], n_lat, tiles_per_seq, n_batch)
    return out.reshape(n_batch, seq, d)
```

```python
import functools

import jax
import jax.numpy as jnp
import numpy as np
from jax import lax
from jax.experimental import pallas as pl
from jax.experimental.pallas import tpu as pltpu

F32 = jnp.float32
BF16 = jnp.bfloat16
HIGHEST = lax.Precision.HIGHEST

D_MODEL = 1024
GRID_W = 64
EPS = 1e-6
LRU_WIDTH = 512
LRU_BLOCK = 64
CONV_W = 4
LRU_C = 8.0
ATT_HEADS = 8
ATT_KV_HEADS = 2
HEAD_DIM = 64
WINDOW = 128
ATT_BLOCK = 128
ROPE_BASE = 10000.0
ATT_Q_WIDTH = ATT_HEADS * HEAD_DIM
ATT_KV_WIDTH = ATT_KV_HEADS * HEAD_DIM
EVEN_IN = 2 * LRU_WIDTH + ATT_Q_WIDTH + 2 * ATT_KV_WIDTH
GLA_HEADS = 4
GLA_DK = 128
GLA_DV = 256
GLA_LOWRANK = 16
GLA_TAU = 16.0
GLA_CHUNK = 64
GLA_BLOCK = 256
GLA_K_WIDTH = GLA_HEADS * GLA_DK
GLA_V_WIDTH = GLA_HEADS * GLA_DV
ODD_IN = 2 * GLA_K_WIDTH + 2 * GLA_V_WIDTH + 2 * GLA_LOWRANK
N_EXPERTS = 16
N_GROUPS = 4
EXPERTS_PER_GROUP = 4
D_EXPERT = 512
PAIRS_PER_GROUP = 6
N_CLASSES = N_GROUPS * PAIRS_PER_GROUP
PAIR_LO = (0, 0, 0, 1, 1, 2)
PAIR_HI = (1, 2, 3, 2, 3, 3)

LANES = 128
SUBLANES = 8
VMEM_LIMIT_BYTES = 56 * 1024 * 1024

ROW_TILE = 512
MOE_TILE = 256
MOE_ROWS = 256
LRU_CHUNK = 128
LRU_HALF = 256
ODD_IN_PAD = ((ODD_IN + LANES - 1) // LANES) * LANES


def _params(*sem):
    return pltpu.CompilerParams(dimension_semantics=sem, vmem_limit_bytes=VMEM_LIMIT_BYTES)


def _modulated_norm(x, g, shift, scale):
    y = x * lax.rsqrt(jnp.mean(x * x, axis=-1, keepdims=True) + EPS)
    return (y * g) * (1.0 + scale) + shift


def _ada_kernel(c_ref, w_ref, b_ref, o_ref):
    c = c_ref[...]
    s = c * jax.nn.sigmoid(c)
    o_ref[...] = jnp.dot(s, w_ref[...], preferred_element_type=F32) + b_ref[...]


def _ada_call(cvec, ada_w, ada_b):
    depth, d, n6 = ada_w.shape
    r = cvec.shape[0]
    tn = n6 // 4
    return pl.pallas_call(
        _ada_kernel,
        grid=(depth, n6 // tn),
        in_specs=[pl.BlockSpec((r, d), lambda l, j: (0, 0)),
                  pl.BlockSpec((None, d, tn), lambda l, j: (l, 0, j)),
                  pl.BlockSpec((None, 1, tn), lambda l, j: (l, 0, j))],
        out_specs=pl.BlockSpec((None, r, tn), lambda l, j: (l, 0, j)),
        out_shape=jax.ShapeDtypeStruct((depth, r, n6), F32),
        compiler_params=_params("arbitrary", "arbitrary"),
        name="adaln",
    )(cvec, ada_w, ada_b.reshape(depth, 1, n6))


def _proj_kernel(x_ref, mod_ref, g_ref, w_ref, o_ref):
    n = _modulated_norm(x_ref[...], g_ref[...], mod_ref[0:1, :], mod_ref[1:2, :])
    o_ref[...] = jnp.dot(n.astype(BF16), w_ref[...], preferred_element_type=F32).astype(o_ref.dtype)


def _proj_res_kernel(h_ref, f_ref, pmod_ref, mod_ref, g_ref, w_ref, x_ref, o_ref):
    x = h_ref[...] + pmod_ref[5:6, :] * f_ref[...]
    x_ref[...] = x
    n = _modulated_norm(x, g_ref[...], mod_ref[0:1, :], mod_ref[1:2, :])
    o_ref[...] = jnp.dot(n.astype(BF16), w_ref[...], preferred_element_type=F32).astype(o_ref.dtype)


def _mod_spec(tiles_per_seq, n_batch):
    return pl.BlockSpec((None, 6, D_MODEL), lambda i: (jnp.minimum(i // tiles_per_seq, n_batch), 0, 0))


def _proj_call(x, mod, g, w, tiles_per_seq, n_batch):
    n, d = x.shape
    n_out = w.shape[1]
    return pl.pallas_call(
        _proj_kernel,
        grid=(n // ROW_TILE,),
        in_specs=[pl.BlockSpec((ROW_TILE, d), lambda i: (i, 0)),
                  _mod_spec(tiles_per_seq, n_batch),
                  pl.BlockSpec((1, d), lambda i: (0, 0)),
                  pl.BlockSpec((d, n_out), lambda i: (0, 0))],
        out_specs=pl.BlockSpec((ROW_TILE, n_out), lambda i: (i, 0)),
        out_shape=jax.ShapeDtypeStruct((n, n_out), F32),
        compiler_params=_params("arbitrary"),
        name="in_proj",
    )(x, mod, g, w)


def _proj_res_call(h, ffn, pmod, mod, g, w, tiles_per_seq, n_batch):
    n, d = h.shape
    n_out = w.shape[1]
    row = pl.BlockSpec((ROW_TILE, d), lambda i: (i, 0))
    return pl.pallas_call(
        _proj_res_kernel,
        grid=(n // ROW_TILE,),
        in_specs=[row, row, _mod_spec(tiles_per_seq, n_batch), _mod_spec(tiles_per_seq, n_batch),
                  pl.BlockSpec((1, d), lambda i: (0, 0)),
                  pl.BlockSpec((d, n_out), lambda i: (0, 0))],
        out_specs=[row, pl.BlockSpec((ROW_TILE, n_out), lambda i: (i, 0))],
        out_shape=[jax.ShapeDtypeStruct((n, d), F32), jax.ShapeDtypeStruct((n, n_out), F32)],
        compiler_params=_params("arbitrary"),
        name="in_proj_res",
    )(h, ffn, pmod, mod, g, w)


def _qkprep_kernel(q_ref, k_ref, v_ref, c_ref, s_ref, qg_ref, kg_ref, m_ref, qo_ref, kvo_ref):
    cos = c_ref[...]
    sin = s_ref[...]
    lane = lax.broadcasted_iota(jnp.int32, cos.shape, 1)
    first = (lane % 32) < 16
    low = lane < HEAD_DIM

    def norm_rope(x, g):
        ms = jnp.dot(x * x, m_ref[...], preferred_element_type=F32, precision=HIGHEST)
        xn = (x * lax.rsqrt(ms + EPS)) * g
        partner = jnp.where(first, pltpu.roll(xn, LANES - 16, 1), pltpu.roll(xn, 16, 1))
        return xn * cos + partner * sin

    for j in range(ATT_Q_WIDTH // LANES):
        qj = norm_rope(q_ref[:, j * LANES:(j + 1) * LANES], qg_ref[...])
        qo_ref[:, j * LANES:(j + 1) * LANES] = (qj * HEAD_DIM ** -0.5).astype(BF16)
    kr = norm_rope(k_ref[...], kg_ref[...])
    ks = pltpu.roll(kr, HEAD_DIM, 1)
    kvo_ref[:, 0:LANES] = jnp.where(low, kr, ks).astype(BF16)
    kvo_ref[:, LANES:2 * LANES] = jnp.where(low, ks, kr).astype(BF16)
    kvo_ref[:, 2 * LANES:3 * LANES] = v_ref[...].astype(BF16)


def _qkprep_call(proj, cos_t, sin_t, qg, kg, seg_mean, tiles_per_seq, n_lat_tiles):
    n = proj.shape[0]
    qcol = (2 * LRU_WIDTH) // ATT_Q_WIDTH
    kcol = (2 * LRU_WIDTH + ATT_Q_WIDTH) // LANES

    def pos(i):
        return (jnp.where(i < n_lat_tiles, i % tiles_per_seq, tiles_per_seq), 0)

    return pl.pallas_call(
        _qkprep_kernel,
        grid=(n // ROW_TILE,),
        in_specs=[pl.BlockSpec((ROW_TILE, ATT_Q_WIDTH), lambda i: (i, qcol)),
                  pl.BlockSpec((ROW_TILE, LANES), lambda i: (i, kcol)),
                  pl.BlockSpec((ROW_TILE, LANES), lambda i: (i, kcol + 1)),
                  pl.BlockSpec((ROW_TILE, LANES), pos),
                  pl.BlockSpec((ROW_TILE, LANES), pos),
                  pl.BlockSpec((1, LANES), lambda i: (0, 0)),
                  pl.BlockSpec((1, LANES), lambda i: (0, 0)),
                  pl.BlockSpec((LANES, LANES), lambda i: (0, 0))],
        out_specs=[pl.BlockSpec((ROW_TILE, ATT_Q_WIDTH), lambda i: (i, 0)),
                   pl.BlockSpec((ROW_TILE, 3 * LANES), lambda i: (i, 0))],
        out_shape=[jax.ShapeDtypeStruct((n, ATT_Q_WIDTH), BF16),
                   jax.ShapeDtypeStruct((n, 3 * LANES), BF16)],
        compiler_params=_params("arbitrary"),
        name="qk_prep",
    )(proj, proj, proj, cos_t, sin_t, qg, kg, seg_mean)


def _attend(q, kv, sink_ref, mask):
    rows = q.shape[0]
    lane = lax.broadcasted_iota(jnp.int32, (rows, LANES), 1)
    low = lane < HEAD_DIM
    zero = jnp.zeros((rows, LANES), BF16)
    vv = kv[:, 2 * LANES:3 * LANES]
    groups = []
    heads_per_kv = ATT_HEADS // ATT_KV_HEADS
    for kh in range(ATT_KV_HEADS):
        kk = kv[:, kh * LANES:(kh + 1) * LANES]
        parts = []
        sinks = []
        for j in range(heads_per_kv):
            h = kh * heads_per_kv + j
            qg = q[:, (h // 2) * LANES:(h // 2 + 1) * LANES]
            parts.append(jnp.where(low, qg, zero) if h % 2 == 0 else jnp.where(low, zero, qg))
            sinks.append(jnp.full((rows, 1), sink_ref[h], F32))
        lhs = jnp.concatenate(parts, axis=0)
        sink = jnp.concatenate(sinks, axis=0)
        s = lax.dot_general(lhs, kk, (((1,), (1,)), ((), ())), preferred_element_type=F32)
        if mask is not None:
            s = jnp.where(mask, s, -1e30)
        m = jnp.maximum(jnp.max(s, axis=-1, keepdims=True), sink)
        e = jnp.exp(s - m)
        denom = jnp.sum(e, axis=-1, keepdims=True) + jnp.exp(sink - m)
        o = jnp.dot(e.astype(BF16), vv, preferred_element_type=F32) / denom
        r = [o[j * rows:(j + 1) * rows] for j in range(heads_per_kv)]
        for p in range(heads_per_kv // 2):
            even, odd = r[2 * p], r[2 * p + 1]
            if kh == 0:
                groups.append(jnp.where(low, even, pltpu.roll(odd, HEAD_DIM, 1)))
            else:
                groups.append(jnp.where(low, pltpu.roll(even, HEAD_DIM, 1), odd))
    return jnp.concatenate(groups, axis=1)


def _attn_kernel(sink_ref, q_ref, kvp_ref, kvo_ref, kvn_ref, kvc_ref, o_ref, *, n_blocks, ctx_len):
    n = pl.program_id(1)
    kv = jnp.concatenate([kvp_ref[...], kvo_ref[...], kvn_ref[...], kvc_ref[...]], axis=0)
    blk = ATT_BLOCK
    stacked = (ATT_HEADS // ATT_KV_HEADS) * blk
    qi = lax.broadcasted_iota(jnp.int32, (stacked, 3 * blk + ctx_len), 0) % blk
    ks = lax.broadcasted_iota(jnp.int32, (stacked, 3 * blk + ctx_len), 1)
    in_win = jnp.abs(ks - blk - qi) <= WINDOW
    in_seq = ((ks >= blk) | (n > 0)) & ((ks < 2 * blk) | (n < n_blocks - 1))
    mask = (in_win & in_seq) | (ks >= 3 * blk)
    o_ref[...] = _attend(q_ref[...], kv, sink_ref, mask)


def _ctx_attn_kernel(sink_ref, q_ref, kvc_ref, o_ref):
    o_ref[...] = _attend(q_ref[...], kvc_ref[...], sink_ref, None)


def _attn_call(qrot, kvrot, sink, n_batch, seq, ctx_len):
    nb = seq // ATT_BLOCK
    ctx0 = (n_batch * seq) // ctx_len
    kvw = kvrot.shape[1]
    return pl.pallas_call(
        functools.partial(_attn_kernel, n_blocks=nb, ctx_len=ctx_len),
        grid=(n_batch, nb),
        in_specs=[pl.BlockSpec(memory_space=pltpu.SMEM),
                  pl.BlockSpec((ATT_BLOCK, ATT_Q_WIDTH), lambda b, n: (b * nb + n, 0)),
                  pl.BlockSpec((ATT_BLOCK, kvw), lambda b, n: (b * nb + jnp.maximum(n - 1, 0), 0)),
                  pl.BlockSpec((ATT_BLOCK, kvw), lambda b, n: (b * nb + n, 0)),
                  pl.BlockSpec((ATT_BLOCK, kvw), lambda b, n: (b * nb + jnp.minimum(n + 1, nb - 1), 0)),
                  pl.BlockSpec((ctx_len, kvw), lambda b, n: (ctx0 + b, 0))],
        out_specs=pl.BlockSpec((ATT_BLOCK, ATT_Q_WIDTH), lambda b, n: (b * nb + n, 0)),
        out_shape=jax.ShapeDtypeStruct((n_batch * seq, ATT_Q_WIDTH), F32),
        compiler_params=_params("arbitrary", "arbitrary"),
        name="window_attn",
    )(sink, qrot, kvrot, kvrot, kvrot, kvrot)


def _ctx_attn_call(qrot, kvrot, sink, n_batch, seq, ctx_len):
    ctx0 = (n_batch * seq) // ctx_len
    kvw = kvrot.shape[1]
    return pl.pallas_call(
        _ctx_attn_kernel,
        grid=(n_batch,),
        in_specs=[pl.BlockSpec(memory_space=pltpu.SMEM),
                  pl.BlockSpec((ctx_len, ATT_Q_WIDTH), lambda b: (ctx0 + b, 0)),
                  pl.BlockSpec((ctx_len, kvw), lambda b: (ctx0 + b, 0))],
        out_specs=pl.BlockSpec((ctx_len, ATT_Q_WIDTH), lambda b: (b, 0)),
        out_shape=jax.ShapeDtypeStruct((n_batch * ctx_len, ATT_Q_WIDTH), F32),
        compiler_params=_params("arbitrary"),
        name="ctx_attn",
    )(sink, qrot, kvrot)


def _scan_rows(a, x, carry, reverse):
    rows, width = a.shape
    sub = lax.broadcasted_iota(jnp.int32, (rows, width), 0) % SUBLANES
    for s in (1, 2, 4):
        shift = rows - s if reverse else s
        a_sh = pltpu.roll(a, shift, 0)
        x_sh = pltpu.roll(x, shift, 0)
        keep = (sub <= SUBLANES - 1 - s) if reverse else (sub >= s)
        x = x + a * jnp.where(keep, x_sh, 0.0)
        a = a * jnp.where(keep, a_sh, 1.0)
    n_groups = rows // SUBLANES
    out = [None] * n_groups
    order = range(n_groups - 1, -1, -1) if reverse else range(n_groups)
    for g in order:
        hg = x[g * SUBLANES:(g + 1) * SUBLANES] + a[g * SUBLANES:(g + 1) * SUBLANES] * carry
        carry = hg[0:1] if reverse else hg[SUBLANES - 1:SUBLANES]
        out[g] = hg
    return jnp.concatenate(out, axis=0), carry


def _rglru_kernel(ul_ref, gl_ref, uc_ref, gc_ref, cw_ref, cb_ref, wr_ref, wi_ref, br_ref, bi_ref, lam_ref,
                  yl_ref, yc_ref, upad_ref, conv_ref, *, seq, ctx_len):
    width = ul_ref.shape[1]
    tc = LRU_CHUNK
    neg_lam = -lam_ref[...]
    softplus = jnp.maximum(neg_lam, 0.0) + jnp.log1p(jnp.exp(-jnp.abs(neg_lam)))
    zeros = jnp.zeros((SUBLANES, width), F32)

    def gates(u, d):
        ub = u.astype(BF16)
        r = jax.nn.sigmoid(jnp.dot(ub, wr_ref[d], preferred_element_type=F32) + br_ref[d:d + 1, :])
        i = jax.nn.sigmoid(jnp.dot(ub, wi_ref[d], preferred_element_type=F32) + bi_ref[d:d + 1, :])
        log_a = (-LRU_C) * r * softplus[d:d + 1, :]
        a = jnp.exp(log_a)
        return a, jnp.sqrt(1.0 - a * a) * (i * u)

    def run(u_ref, g_ref, y_ref, n, h_fwd, h_bwd):
        n_chunks = n // tc
        upad_ref[0:SUBLANES, :] = zeros
        upad_ref[pl.ds(SUBLANES, n), :] = u_ref[...]
        upad_ref[pl.ds(SUBLANES + n, SUBLANES), :] = zeros

        def conv_body(c, carry):
            t0 = pl.multiple_of(c * tc, tc)
            win = upad_ref[pl.ds(t0, tc + 2 * SUBLANES), :]
            acc = cb_ref[...]
            for k in range(CONV_W):
                acc = acc + cw_ref[k:k + 1, :] * win[SUBLANES - 1 + k:SUBLANES - 1 + k + tc]
            conv_ref[pl.ds(t0, tc), :] = acc
            return carry

        lax.fori_loop(0, n_chunks, conv_body, 0)

        def fwd_body(c, carry):
            t0 = pl.multiple_of(c * tc, tc)
            a, x = gates(conv_ref[pl.ds(t0, tc), :], 0)
            h, carry = _scan_rows(a, x, carry, False)
            y_ref[pl.ds(t0, tc), :] = h
            return carry

        h_fwd = lax.fori_loop(0, n_chunks, fwd_body, h_fwd)

        def bwd_body(c, carry):
            t0 = pl.multiple_of((n_chunks - 1 - c) * tc, tc)
            a, x = gates(conv_ref[pl.ds(t0, tc), :], 1)
            h, carry = _scan_rows(a, x, carry, True)
            y_ref[pl.ds(t0, tc), :] = (y_ref[pl.ds(t0, tc), :] + h) * jax.nn.gelu(g_ref[pl.ds(t0, tc), :])
            return carry

        h_bwd = lax.fori_loop(0, n_chunks, bwd_body, h_bwd)
        return h_fwd, h_bwd

    h0 = jnp.zeros((1, width), F32)
    h_fwd, h_bwd = run(uc_ref, gc_ref, yc_ref, ctx_len, h0, h0)
    run(ul_ref, gl_ref, yl_ref, seq, h_fwd, h_bwd)


def _rglru_call(proj, conv_w, conv_b, wr, wi, br, bi, lam, n_batch, seq, ctx_len):
    ctx0 = (n_batch * seq) // ctx_len
    halves = LRU_WIDTH // LRU_HALF
    w = LRU_HALF
    vec2 = pl.BlockSpec((2, w), lambda b, c: (0, c))
    return pl.pallas_call(
        functools.partial(_rglru_kernel, seq=seq, ctx_len=ctx_len),
        grid=(n_batch, halves),
        in_specs=[pl.BlockSpec((seq, w), lambda b, c: (b, c)),
                  pl.BlockSpec((seq, w), lambda b, c: (b, halves + c)),
                  pl.BlockSpec((ctx_len, w), lambda b, c: (ctx0 + b, c)),
                  pl.BlockSpec((ctx_len, w), lambda b, c: (ctx0 + b, halves + c)),
                  pl.BlockSpec((CONV_W, w), lambda b, c: (0, c)),
                  pl.BlockSpec((1, w), lambda b, c: (0, c)),
                  pl.BlockSpec((None, 2, w, w), lambda b, c: (c, 0, 0, 0)),
                  pl.BlockSpec((None, 2, w, w), lambda b, c: (c, 0, 0, 0)),
                  vec2, vec2, vec2],
        out_specs=[pl.BlockSpec((seq, w), lambda b, c: (b, c)),
                   pl.BlockSpec((ctx_len, w), lambda b, c: (b, c))],
        out_shape=[jax.ShapeDtypeStruct((n_batch * seq, LRU_WIDTH), F32),
                   jax.ShapeDtypeStruct((n_batch * ctx_len, LRU_WIDTH), F32)],
        scratch_shapes=[pltpu.VMEM((seq + 2 * SUBLANES, w), F32), pltpu.VMEM((seq, w), F32)],
        compiler_params=_params("arbitrary", "arbitrary"),
        name="rglru",
    )(proj, proj, proj, proj, conv_w, conv_b, wr, wi, br, bi, lam)


def _segment_cumsum(x, reverse, seg):
    rows = x.shape[0]
    pos = lax.broadcasted_iota(jnp.int32, x.shape, 0) % seg
    s = 1
    while s < seg:
        if reverse:
            shifted, keep = pltpu.roll(x, rows - s, 0), pos <= seg - 1 - s
        else:
            shifted, keep = pltpu.roll(x, s, 0), pos >= s
        x = x + jnp.where(keep, shifted, 0.0)
        s *= 2
    return x


def _gla_kernel(q_ref, k_ref, v_ref, r_ref, a_ref, kc_ref, vc_ref, ac_ref, gw_ref, gb_ref, ng_ref,
                o_ref, ob_ref, qd_ref, ke_ref, eg_ref, st_ref, *, seq, ctx_len):
    ch = GLA_CHUNK
    blk = GLA_BLOCK
    per_blk = blk // ch
    row = lax.broadcasted_iota(jnp.int32, (blk, blk), 0)
    col = lax.broadcasted_iota(jnp.int32, (blk, blk), 1)
    same = (row // ch) == (col // ch)
    keep = (same & (row >= col), same & (row <= col))
    scale = GLA_DK ** -0.5
    nt = (((1,), (1,)), ((), ()))
    tn = (((0,), (0,)), ((), ()))

    def prepare(a_src, k_src, v_src, q_src, dst, n_rows, d):
        def body(i, carry):
            t0 = pl.multiple_of(i * blk, blk)
            pre = jnp.dot(a_src[pl.ds(t0, blk), :], gw_ref[d], preferred_element_type=F32,
                          precision=HIGHEST) + gb_ref[d:d + 1, :]
            log_a = (jnp.minimum(pre, 0.0) - jnp.log1p(jnp.exp(-jnp.abs(pre)))) / GLA_TAU
            prefix = _segment_cumsum(log_a, False, ch)
            suffix = _segment_cumsum(log_a, True, ch)
            b = prefix if d == 0 else suffix
            to_end = (suffix if d == 0 else prefix) - log_a
            total = prefix + suffix - log_a
            for j in range(per_blk):
                eg_ref[pl.ds(i * per_blk + j, 1), :] = jnp.exp(total[j * ch:j * ch + 1])
            k_blk = k_src[pl.ds(t0, blk), :]
            ke_ref[pl.ds(t0, blk), :] = (k_blk * jnp.exp(to_end)).astype(BF16)
            if q_src is not None:
                q_dec = ((q_src[pl.ds(t0, blk), :] * scale) * jnp.exp(b)).astype(BF16)
                qd_ref[pl.ds(t0, blk), :] = q_dec
                k_inc = (k_blk * jnp.exp(-b)).astype(BF16)
                s = lax.dot_general(q_dec, k_inc, nt, preferred_element_type=F32)
                p = jnp.where(keep[d], s, 0.0).astype(BF16)
                dst[pl.ds(t0, blk), :] = jnp.dot(p, v_src[pl.ds(t0, blk), :].astype(BF16),
                                                 preferred_element_type=F32)
            return carry

        lax.fori_loop(0, n_rows // blk, body, 0)

    def recur(v_src, dst, n_rows, d):
        n_chunks = n_rows // ch

        def body(c, carry):
            n = c if d == 0 else n_chunks - 1 - c
            t0 = pl.multiple_of(n * ch, ch)
            st = st_ref[...]
            if dst is not None:
                dst[pl.ds(t0, ch), :] = dst[pl.ds(t0, ch), :] + lax.dot_general(
                    qd_ref[pl.ds(t0, ch), :], st.astype(BF16), nt, preferred_element_type=F32)
            kv = lax.dot_general(v_src[pl.ds(t0, ch), :].astype(BF16), ke_ref[pl.ds(t0, ch), :], tn,
                                 preferred_element_type=F32)
            st_ref[...] = st * eg_ref[pl.ds(n, 1), :] + kv
            return carry

        lax.fori_loop(0, n_chunks, body, 0, unroll=2)

    for d, dst in ((0, o_ref), (1, ob_ref)):
        st_ref[...] = jnp.zeros(st_ref.shape, F32)
        prepare(ac_ref, kc_ref, vc_ref, None, None, ctx_len, d)
        recur(vc_ref, None, ctx_len, d)
        prepare(a_ref, k_ref, v_ref, q_ref, dst, seq, d)
        recur(v_ref, dst, seq, d)

    def finish_body(c, carry):
        t0 = pl.multiple_of(c * blk, blk)
        o = o_ref[pl.ds(t0, blk), :] + ob_ref[pl.ds(t0, blk), :]
        on = (o * lax.rsqrt(jnp.mean(o * o, axis=-1, keepdims=True) + EPS)) * ng_ref[...]
        r = r_ref[pl.ds(t0, blk), :]
        o_ref[pl.ds(t0, blk), :] = on * (r * jax.nn.sigmoid(r))
        return carry

    lax.fori_loop(0, seq // blk, finish_body, 0)


def _gla_call(proj, gate_w, gate_b, norm_g, n_batch, seq, ctx_len):
    ctx0 = (n_batch * seq) // ctx_len
    kcol = GLA_K_WIDTH // GLA_DK
    vcol = (2 * GLA_K_WIDTH) // GLA_DV
    rcol = (2 * GLA_K_WIDTH + GLA_V_WIDTH) // GLA_DV
    acol = (2 * GLA_K_WIDTH + 2 * GLA_V_WIDTH) // LANES
    return pl.pallas_call(
        functools.partial(_gla_kernel, seq=seq, ctx_len=ctx_len),
        grid=(n_batch, GLA_HEADS),
        in_specs=[pl.BlockSpec((seq, GLA_DK), lambda b, h: (b, h)),
                  pl.BlockSpec((seq, GLA_DK), lambda b, h: (b, kcol + h)),
                  pl.BlockSpec((seq, GLA_DV), lambda b, h: (b, vcol + h)),
                  pl.BlockSpec((seq, GLA_DV), lambda b, h: (b, rcol + h)),
                  pl.BlockSpec((seq, LANES), lambda b, h: (b, acol)),
                  pl.BlockSpec((ctx_len, GLA_DK), lambda b, h: (ctx0 + b, kcol + h)),
                  pl.BlockSpec((ctx_len, GLA_DV), lambda b, h: (ctx0 + b, vcol + h)),
                  pl.BlockSpec((ctx_len, LANES), lambda b, h: (ctx0 + b, acol)),
                  pl.BlockSpec((None, 2, LANES, GLA_DK), lambda b, h: (h, 0, 0, 0)),
                  pl.BlockSpec((None, 2, GLA_DK), lambda b, h: (h, 0, 0)),
                  pl.BlockSpec((1, GLA_DV), lambda b, h: (0, 0))],
        out_specs=pl.BlockSpec((seq, GLA_DV), lambda b, h: (b, h)),
        out_shape=jax.ShapeDtypeStruct((n_batch * seq, GLA_V_WIDTH), F32),
        scratch_shapes=[pltpu.VMEM((seq, GLA_DV), F32),
                        pltpu.VMEM((seq, GLA_DK), BF16),
                        pltpu.VMEM((seq, GLA_DK), BF16),
                        pltpu.VMEM((seq // GLA_CHUNK, GLA_DK), F32),
                        pltpu.VMEM((GLA_DV, GLA_DK), F32)],
        compiler_params=_params("arbitrary", "arbitrary"),
        name="gla",
    )(proj, proj, proj, proj, proj, proj, proj, proj, gate_w, gate_b, norm_g)


def _route(rows, rw_ref, rb_ref):
    n_rows = rows.shape[0]
    logits = lax.dot_general(rw_ref[...], rows, (((1,), (1,)), ((), ())),
                             preferred_element_type=F32, precision=HIGHEST)
    scores = jax.nn.sigmoid(logits)
    sel = scores + rb_ref[...]
    eid = lax.broadcasted_iota(jnp.int32, (N_EXPERTS, n_rows), 0)
    gid = eid // EXPERTS_PER_GROUP
    neg = -jnp.inf
    big = N_EXPERTS

    def top2(masked):
        m1 = jnp.max(masked, axis=0, keepdims=True)
        i1 = jnp.min(jnp.where(masked == m1, eid, big), axis=0, keepdims=True)
        rest = jnp.where(eid == i1, neg, masked)
        m2 = jnp.max(rest, axis=0, keepdims=True)
        i2 = jnp.min(jnp.where(rest == m2, eid, big), axis=0, keepdims=True)
        return m1, i1, m2, i2

    best = None
    best_g = None
    for g in range(N_GROUPS):
        m1, _, m2, _ = top2(jnp.where(gid == g, sel, neg))
        gs = m1 + m2
        if best is None:
            best, best_g = gs, jnp.zeros((1, n_rows), jnp.int32)
        else:
            better = gs > best
            best_g = jnp.where(better, g, best_g)
            best = jnp.where(better, gs, best)
    _, i1, _, i2 = top2(jnp.where(gid == best_g, sel, neg))
    s1 = jnp.sum(jnp.where(eid == i1, scores, 0.0), axis=0, keepdims=True)
    s2 = jnp.sum(jnp.where(eid == i2, scores, 0.0), axis=0, keepdims=True)
    total = s1 + s2
    swap = i2 < i1
    lo = jnp.where(swap, i2, i1)
    hi = jnp.where(swap, i1, i2)
    w_lo = jnp.where(swap, s2, s1) / total
    w_hi = jnp.where(swap, s1, s2) / total
    pad = jnp.zeros((4, n_rows), F32)
    return jnp.concatenate([lo.astype(F32), hi.astype(F32), w_lo, w_hi, pad], axis=0)


def _finish_rows(y, h_ref, mod_ref, g_ref, rw_ref, rb_ref, hn_ref, rows_ref, rt_ref, rc_ref):
    hn = h_ref[...] + mod_ref[2:3, :] * y
    hn_ref[...] = hn
    rows = _modulated_norm(hn, g_ref[...], mod_ref[3:4, :], mod_ref[4:5, :])
    rows_ref[...] = rows
    route = _route(rows, rw_ref, rb_ref)
    rt_ref[...] = route
    padded = jnp.concatenate([route, jnp.zeros((LANES - SUBLANES, route.shape[1]), F32)], axis=0)
    rc_ref[...] = padded.T


def _outproj_even_kernel(al_ref, ac_ref, bl_ref, bc_ref, w_ref, h_ref, mod_ref, g_ref, rw_ref, rb_ref,
                         hn_ref, rows_ref, rt_ref, rc_ref, *, n_lat_tiles):
    is_ctx = pl.program_id(0) >= n_lat_tiles
    a = jnp.where(is_ctx, ac_ref[...], al_ref[...]).astype(BF16)
    b = jnp.where(is_ctx, bc_ref[...], bl_ref[...]).astype(BF16)
    y = (jnp.dot(a, w_ref[0:LRU_WIDTH, :], preferred_element_type=F32)
         + jnp.dot(b, w_ref[LRU_WIDTH:LRU_WIDTH + ATT_Q_WIDTH, :], preferred_element_type=F32))
    _finish_rows(y, h_ref, mod_ref, g_ref, rw_ref, rb_ref, hn_ref, rows_ref, rt_ref, rc_ref)


def _outproj_odd_kernel(o_ref, w_ref, h_ref, mod_ref, g_ref, rw_ref, rb_ref, hn_ref, rows_ref, rt_ref, rc_ref):
    y = jnp.dot(o_ref[...].astype(BF16), w_ref[...], preferred_element_type=F32)
    _finish_rows(y, h_ref, mod_ref, g_ref, rw_ref, rb_ref, hn_ref, rows_ref, rt_ref, rc_ref)


def _outproj_specs(n_rows, tiles_per_seq, n_batch):
    d = D_MODEL
    row = pl.BlockSpec((ROW_TILE, d), lambda i: (i, 0))
    tail_in = [row, _mod_spec(tiles_per_seq, n_batch),
               pl.BlockSpec((1, d), lambda i: (0, 0)),
               pl.BlockSpec((N_EXPERTS, d), lambda i: (0, 0)),
               pl.BlockSpec((N_EXPERTS, 1), lambda i: (0, 0))]
    out_specs = [row, row, pl.BlockSpec((SUBLANES, ROW_TILE), lambda i: (0, i)),
                 pl.BlockSpec((ROW_TILE, LANES), lambda i: (i, 0))]
    out_shape = [jax.ShapeDtypeStruct((n_rows, d), F32), jax.ShapeDtypeStruct((n_rows, d), F32),
                 jax.ShapeDtypeStruct((SUBLANES, n_rows), F32), jax.ShapeDtypeStruct((n_rows, LANES), F32)]
    return tail_in, out_specs, out_shape


def _outproj_even_call(a_lat, a_ctx, b_lat, b_ctx, w, h, mod, g, rw_t, rb, tiles_per_seq, n_batch):
    n_rows = h.shape[0]
    n_lat_tiles = a_lat.shape[0] // ROW_TILE
    tail_in, out_specs, out_shape = _outproj_specs(n_rows, tiles_per_seq, n_batch)
    lat = pl.BlockSpec((ROW_TILE, LRU_WIDTH), lambda i: (jnp.minimum(i, n_lat_tiles - 1), 0))
    ctx = pl.BlockSpec((ROW_TILE, LRU_WIDTH), lambda i: (jnp.maximum(i - n_lat_tiles, 0), 0))
    return pl.pallas_call(
        functools.partial(_outproj_even_kernel, n_lat_tiles=n_lat_tiles),
        grid=(n_rows // ROW_TILE,),
        in_specs=[lat, ctx, lat, ctx, pl.BlockSpec(w.shape, lambda i: (0, 0))] + tail_in,
        out_specs=out_specs,
        out_shape=out_shape,
        compiler_params=_params("arbitrary"),
        name="out_proj_even",
    )(a_lat, a_ctx, b_lat, b_ctx, w, h, mod, g, rw_t, rb)


def _outproj_odd_call(o, w, h, mod, g, rw_t, rb, tiles_per_seq, n_batch):
    n_rows = o.shape[0]
    tail_in, out_specs, out_shape = _outproj_specs(n_rows, tiles_per_seq, n_batch)
    return pl.pallas_call(
        _outproj_odd_kernel,
        grid=(n_rows // ROW_TILE,),
        in_specs=[pl.BlockSpec((ROW_TILE, o.shape[1]), lambda i: (i, 0)),
                  pl.BlockSpec(w.shape, lambda i: (0, 0))] + tail_in,
        out_specs=out_specs,
        out_shape=out_shape,
        compiler_params=_params("arbitrary"),
        name="out_proj_odd",
    )(o, w, h, mod, g, rw_t, rb)


X_SLAB = D_MODEL // LANES
Y_SLAB = 2 * X_SLAB


def _for_rows(n, fn):
    def body(r, carry):
        fn(r)
        return carry

    lax.fori_loop(0, n, body, 0, unroll=8)


def _dispatch_kernel(slot_ref, rows_ref, xs_init_ref, xs_ref, slab_ref, sem, *, steps):
    del xs_init_ref
    i = pl.program_id(0)
    k = i % 2
    rt = MOE_ROWS

    def wait_all(buf):
        _for_rows(rt, lambda r: pltpu.make_async_copy(
            slab_ref.at[buf, pl.ds(0, X_SLAB)], xs_ref.at[pl.ds(0, X_SLAB)], sem.at[buf]).wait())

    @pl.when(i >= 2)
    def _():
        wait_all(k)

    for j in range(X_SLAB):
        slab_ref[k, pl.ds(j, rt, stride=X_SLAB), :] = rows_ref[:, j * LANES:(j + 1) * LANES]

    _for_rows(rt, lambda r: pltpu.make_async_copy(
        slab_ref.at[k, pl.ds(pl.multiple_of(r * X_SLAB, X_SLAB), X_SLAB)],
        xs_ref.at[pl.ds(pl.multiple_of(slot_ref[0, 0, r] * X_SLAB, X_SLAB), X_SLAB)],
        sem.at[k]).start())

    @pl.when(i == steps - 1)
    def _():
        wait_all(k)
        if steps >= 2:
            wait_all(1 - k)


def _dispatch_call(rows, slots, n_slots):
    n, d = rows.shape
    steps = n // MOE_ROWS
    xs_init = jnp.zeros((n_slots * X_SLAB, LANES), F32)
    return pl.pallas_call(
        functools.partial(_dispatch_kernel, steps=steps),
        grid=(steps,),
        in_specs=[pl.BlockSpec((1, 1, MOE_ROWS), lambda i: (i, 0, 0), memory_space=pltpu.SMEM),
                  pl.BlockSpec((MOE_ROWS, d), lambda i: (i, 0)),
                  pl.BlockSpec(memory_space=pl.ANY)],
        out_specs=pl.BlockSpec(memory_space=pl.ANY),
        out_shape=jax.ShapeDtypeStruct((n_slots * X_SLAB, LANES), F32),
        scratch_shapes=[pltpu.VMEM((2, MOE_ROWS * X_SLAB, LANES), F32), pltpu.SemaphoreType.DMA((2,))],
        input_output_aliases={2: 0},
        compiler_params=_params("arbitrary"),
        name="moe_dispatch",
    )(slots.reshape(steps, 1, MOE_ROWS), rows, xs_init)


def _experts_kernel(elo_ref, ehi_ref, used_ref, x_ref, wg0_ref, wu0_ref, wd0_ref,
                    wg1_ref, wu1_ref, wd1_ref, y_ref):
    t = pl.program_id(0)
    tm = MOE_TILE

    @pl.when(t < used_ref[0])
    def _():
        x = jnp.concatenate([x_ref[pl.ds(j, tm, stride=X_SLAB), :] for j in range(X_SLAB)],
                            axis=1).astype(BF16)
        for e, (wg_ref, wu_ref, wd_ref) in enumerate(((wg0_ref, wu0_ref, wd0_ref), (wg1_ref, wu1_ref, wd1_ref))):
            gate = jnp.dot(x, wg_ref[...], preferred_element_type=F32)
            up = jnp.dot(x, wu_ref[...], preferred_element_type=F32)
            act = ((gate * jax.nn.sigmoid(gate)) * up).astype(BF16)
            f = jnp.dot(act, wd_ref[...], preferred_element_type=F32)
            for j in range(X_SLAB):
                y_ref[pl.ds(e * X_SLAB + j, tm, stride=Y_SLAB), :] = f[:, j * LANES:(j + 1) * LANES]

    @pl.when(t >= used_ref[0])
    def _():
        y_ref[...] = jnp.zeros_like(y_ref)


def _experts_call(xs, e_lo, e_hi, n_used, w_gate, w_up, w_down):
    d = D_MODEL
    tiles = xs.shape[0] // (MOE_TILE * X_SLAB)

    def xmap(t, elo, ehi, used):
        return (jnp.minimum(t, used[0] - 1), 0)

    def lo(t, elo, ehi, used):
        return (elo[t], 0, 0)

    def hi(t, elo, ehi, used):
        return (ehi[t], 0, 0)

    up_spec = lambda m: pl.BlockSpec((None, d, D_EXPERT), m)
    down_spec = lambda m: pl.BlockSpec((None, D_EXPERT, d), m)
    grid_spec = pltpu.PrefetchScalarGridSpec(
        num_scalar_prefetch=3,
        grid=(tiles,),
        in_specs=[pl.BlockSpec((MOE_TILE * X_SLAB, LANES), xmap),
                  up_spec(lo), up_spec(lo), down_spec(lo),
                  up_spec(hi), up_spec(hi), down_spec(hi)],
        out_specs=pl.BlockSpec((MOE_TILE * Y_SLAB, LANES), lambda t, elo, ehi, used: (t, 0)),
    )
    return pl.pallas_call(
        _experts_kernel,
        grid_spec=grid_spec,
        out_shape=jax.ShapeDtypeStruct((tiles * MOE_TILE * Y_SLAB, LANES), F32),
        compiler_params=_params("arbitrary"),
        name="experts",
    )(e_lo, e_hi, n_used, xs, w_gate, w_up, w_down, w_gate, w_up, w_down)


def _collect_kernel(slot_ref, next_slot_ref, ys_ref, rc_ref, *rest, steps, residual):
    if residual:
        h_ref, mod_ref, o_ref, buf_ref, sem = rest
    else:
        o_ref, buf_ref, sem = rest
    i = pl.program_id(0)
    k = i % 2
    rt = MOE_ROWS

    def start_all(slots, buf):
        _for_rows(rt, lambda r: pltpu.make_async_copy(
            ys_ref.at[pl.ds(pl.multiple_of(slots[0, 0, r] * Y_SLAB, Y_SLAB), Y_SLAB)],
            buf_ref.at[buf, pl.ds(pl.multiple_of(r * Y_SLAB, Y_SLAB), Y_SLAB)],
            sem.at[buf]).start())

    @pl.when(i == 0)
    def _():
        start_all(slot_ref, 0)

    @pl.when(i + 1 < steps)
    def _():
        start_all(next_slot_ref, 1 - k)

    _for_rows(rt, lambda r: pltpu.make_async_copy(
        ys_ref.at[pl.ds(0, Y_SLAB)], buf_ref.at[k, pl.ds(0, Y_SLAB)], sem.at[k]).wait())

    rc = rc_ref[...]
    w_lo = rc[:, 2:3]
    w_hi = rc[:, 3:4]
    for j in range(X_SLAB):
        cols = slice(j * LANES, (j + 1) * LANES)
        f = (w_lo * buf_ref[k, pl.ds(j, rt, stride=Y_SLAB), :]
             + w_hi * buf_ref[k, pl.ds(X_SLAB + j, rt, stride=Y_SLAB), :])
        if residual:
            o_ref[:, cols] = h_ref[:, cols] + mod_ref[5:6, cols] * f
        else:
            o_ref[:, cols] = f


def _collect_call(ys, slots, route_cols, residual=None):
    n = slots.shape[0]
    d = D_MODEL
    steps = n // MOE_ROWS
    slots3 = slots.reshape(steps, 1, MOE_ROWS)
    row = pl.BlockSpec((MOE_ROWS, d), lambda i: (i, 0))
    in_specs = [pl.BlockSpec((1, 1, MOE_ROWS), lambda i: (i, 0, 0), memory_space=pltpu.SMEM),
                pl.BlockSpec((1, 1, MOE_ROWS), lambda i: (jnp.minimum(i + 1, steps - 1), 0, 0),
                             memory_space=pltpu.SMEM),
                pl.BlockSpec(memory_space=pl.ANY),
                pl.BlockSpec((MOE_ROWS, LANES), lambda i: (i, 0))]
    args = [slots3, slots3, ys, route_cols]
    if residual is not None:
        h, mod, tiles_per_seq, n_batch = residual
        in_specs += [row, _mod_spec(tiles_per_seq, n_batch)]
        args += [h, mod]
    return pl.pallas_call(
        functools.partial(_collect_kernel, steps=steps, residual=residual is not None),
        grid=(steps,),
        in_specs=in_specs,
        out_specs=row,
        out_shape=jax.ShapeDtypeStruct((n, d), F32),
        scratch_shapes=[pltpu.VMEM((2, MOE_ROWS * Y_SLAB, LANES), F32), pltpu.SemaphoreType.DMA((2,))],
        compiler_params=_params("arbitrary"),
        name="moe_collect",
    )(*args)


def _moe(rows, route, route_cols, w_gate, w_up, w_down, residual=None):
    n = rows.shape[0]
    tm = MOE_TILE
    assert n % MOE_ROWS == 0
    max_tiles = -(-n // tm) + N_CLASSES
    p = max_tiles * tm
    lo = route[0].astype(jnp.int32)
    hi = route[1].astype(jnp.int32)
    lo_in = lo % EXPERTS_PER_GROUP
    hi_in = hi % EXPERTS_PER_GROUP
    pair = lo_in * 3 - (lo_in * (lo_in - 1)) // 2 + (hi_in - lo_in - 1)
    cls = (lo // EXPERTS_PER_GROUP) * PAIRS_PER_GROUP + pair
    onehot = (cls[:, None] == jnp.arange(N_CLASSES, dtype=jnp.int32)[None, :]).astype(jnp.int32)
    counts = jnp.sum(onehot, axis=0)
    rank = jnp.sum(jnp.cumsum(onehot, axis=0) * onehot, axis=1) - 1
    tiles_c = (counts + tm - 1) // tm
    tile_end = jnp.cumsum(tiles_c)
    tile_start = tile_end - tiles_c
    n_used = tile_end[-1]
    slot_of_token = (tile_start[cls] * tm + rank).astype(jnp.int32)
    tile_ids = jnp.arange(max_tiles, dtype=jnp.int32)
    tile_cls = jnp.minimum(jnp.sum((tile_end[None, :] <= tile_ids[:, None]).astype(jnp.int32), axis=1),
                           N_CLASSES - 1)
    group0 = (tile_cls // PAIRS_PER_GROUP) * EXPERTS_PER_GROUP
    e_lo = group0 + jnp.asarray(PAIR_LO, jnp.int32)[tile_cls % PAIRS_PER_GROUP]
    e_hi = group0 + jnp.asarray(PAIR_HI, jnp.int32)[tile_cls % PAIRS_PER_GROUP]
    xs = _dispatch_call(rows, slot_of_token, p)
    ys = _experts_call(xs, e_lo.astype(jnp.int32), e_hi.astype(jnp.int32),
                       n_used.reshape(1).astype(jnp.int32), w_gate, w_up, w_down)
    return _collect_call(ys, slot_of_token, route_cols, residual)


def _rope_tables(seq):
    rows = seq // GRID_W
    row = jnp.repeat(jnp.arange(rows), GRID_W).astype(F32)
    col = jnp.tile(jnp.arange(GRID_W), rows).astype(F32)
    n_freq = HEAD_DIM // 4
    inv_freq = ROPE_BASE ** (-jnp.arange(n_freq, dtype=F32) / n_freq)
    ang_r = row[:, None] * inv_freq
    ang_c = col[:, None] * inv_freq
    cos = jnp.concatenate([jnp.cos(ang_r), jnp.cos(ang_r), jnp.cos(ang_c), jnp.cos(ang_c)], axis=1)
    sin = jnp.concatenate([-jnp.sin(ang_r), jnp.sin(ang_r), -jnp.sin(ang_c), jnp.sin(ang_c)], axis=1)
    cos = jnp.concatenate([jnp.tile(cos, (1, 2)), jnp.ones((ROW_TILE, LANES), F32)], axis=0)
    sin = jnp.concatenate([jnp.tile(sin, (1, 2)), jnp.zeros((ROW_TILE, LANES), F32)], axis=0)
    return cos, sin


def _block_diag_halves(w):
    per_half = LRU_HALF // LRU_BLOCK
    halves = LRU_WIDTH // LRU_HALF
    out = jnp.zeros((halves, 2, LRU_HALF, LRU_HALF), F32)
    for c in range(halves):
        for j in range(per_half):
            s = slice(j * LRU_BLOCK, (j + 1) * LRU_BLOCK)
            out = out.at[c, :, s, s].set(w[:, c * per_half + j])
    return out.astype(BF16)


def _gla_gate_weights(gate_up, gate_b):
    w = jnp.zeros((GLA_HEADS, 2, LANES, GLA_DK), F32)
    for d in range(2):
        wd = gate_up[d].reshape(GLA_LOWRANK, GLA_HEADS, GLA_DK).transpose(1, 0, 2)
        w = w.at[:, d, d * GLA_LOWRANK:(d + 1) * GLA_LOWRANK, :].set(wd)
    b = gate_b.reshape(2, GLA_HEADS, GLA_DK).transpose(1, 0, 2)
    return w, b


def kernel(x, c, ctx, c_ctx, router_w, router_bias, ada_w, ada_b, norm_mix_g, norm_ffn_g, moe_w_gate, moe_w_up, moe_w_down, ev_w_in, ev_w_out, ev_conv_w, ev_conv_b, ev_rg_w, ev_rg_b, ev_ig_w, ev_ig_b, ev_lambda, ev_q_norm_g, ev_k_norm_g, ev_sink, od_w_in, od_w_out, od_gate_up, od_gate_b, od_out_norm_g):
    n_batch, seq, d = x.shape
    ctx_len = ctx.shape[1]
    n_lat = n_batch * seq
    n_ctx = n_batch * ctx_len
    assert ada_w.shape[0] == 2 and d == D_MODEL
    assert seq % ROW_TILE == 0 and n_ctx % ROW_TILE == 0 and seq % ctx_len == 0
    assert ctx_len % LRU_CHUNK == 0 and seq % GLA_BLOCK == 0 and ctx_len % GLA_BLOCK == 0
    tiles_per_seq = seq // ROW_TILE
    n_lat_tiles = n_lat // ROW_TILE

    h = jnp.concatenate([x.reshape(n_lat, d), ctx.reshape(n_ctx, d)], axis=0)
    mod_rows = -(-(n_batch + 1) // SUBLANES) * SUBLANES
    cvec = jnp.zeros((mod_rows, d), F32).at[:n_batch].set(c).at[n_batch].set(c_ctx)
    mods = _ada_call(cvec, ada_w, ada_b).reshape(2, mod_rows, 6, d)
    mod_mix = [mods[l][:n_batch + 1, 0:6] for l in range(2)]
    rw_t = router_w.T
    rb = router_bias.reshape(N_EXPERTS, 1)

    w_in0 = ev_w_in[0].astype(BF16)
    w_out0 = ev_w_out[0].astype(BF16)
    proj0 = _proj_call(h, mod_mix[0], norm_mix_g[0:1], w_in0, tiles_per_seq, n_batch)
    cos_t, sin_t = _rope_tables(seq)
    seg = np.kron(np.eye(LANES // HEAD_DIM, dtype=np.float32), np.full((HEAD_DIM, HEAD_DIM), 1.0 / HEAD_DIM, np.float32))
    qrot, kvrot = _qkprep_call(proj0, cos_t, sin_t, jnp.tile(ev_q_norm_g[0], 2)[None, :],
                               jnp.tile(ev_k_norm_g[0], 2)[None, :], jnp.asarray(seg),
                               tiles_per_seq, n_lat_tiles)
    a_lat, a_ctx = _rglru_call(proj0, ev_conv_w[0], ev_conv_b[0:1], _block_diag_halves(ev_rg_w[0]),
                               _block_diag_halves(ev_ig_w[0]), ev_rg_b[0], ev_ig_b[0], ev_lambda[0],
                               n_batch, seq, ctx_len)
    b_lat = _attn_call(qrot, kvrot, ev_sink[0], n_batch, seq, ctx_len)
    b_ctx = _ctx_attn_call(qrot, kvrot, ev_sink[0], n_batch, seq, ctx_len)
    h, rows, route, route_cols = _outproj_even_call(a_lat, a_ctx, b_lat, b_ctx, w_out0, h, mod_mix[0],
                                                    norm_ffn_g[0:1], rw_t, rb, tiles_per_seq, n_batch)
    ffn = _moe(rows, route, route_cols, moe_w_gate[0].astype(BF16), moe_w_up[0].astype(BF16),
               moe_w_down[0].astype(BF16))

    w_in1 = jnp.pad(od_w_in[0], ((0, 0), (0, ODD_IN_PAD - ODD_IN))).astype(BF16)
    h, proj1 = _proj_res_call(h, ffn, mod_mix[0], mod_mix[1], norm_mix_g[1:2], w_in1, tiles_per_seq, n_batch)
    gate_w, gate_b = _gla_gate_weights(od_gate_up[0], od_gate_b[0])
    o = _gla_call(proj1, gate_w, gate_b, od_out_norm_g[0:1], n_batch, seq, ctx_len)
    h_lat, rows, route, route_cols = _outproj_odd_call(o, od_w_out[0].astype(BF16), h, mod_mix[1],
                                                       norm_ffn_g[1:2], rw_t, rb, tiles_per_seq, n_batch)
    out = _moe(rows, route, route_cols, moe_w_gate[1].astype(BF16), moe_w_up[1].astype(BF16),
               moe_w_down[1].astype(BF16), residual=(h_lat, mod_mix[1], seq // MOE_ROWS, n_batch))
    return out.reshape(n_batch, seq, d)
```

```python
import functools

import jax
import jax.numpy as jnp
import numpy as np
from jax import lax
from jax.experimental import pallas as pl
from jax.experimental.pallas import tpu as pltpu

F32 = jnp.float32
BF16 = jnp.bfloat16
HIGHEST = lax.Precision.HIGHEST
ACT_DTYPE = BF16

D_MODEL = 1024
GRID_W = 64
EPS = 1e-6
LRU_WIDTH = 512
LRU_BLOCK = 64
CONV_W = 4
LRU_C = 8.0
ATT_HEADS = 8
ATT_KV_HEADS = 2
HEAD_DIM = 64
WINDOW = 128
ATT_BLOCK = 128
ROPE_BASE = 10000.0
ATT_Q_WIDTH = ATT_HEADS * HEAD_DIM
ATT_KV_WIDTH = ATT_KV_HEADS * HEAD_DIM
EVEN_IN = 2 * LRU_WIDTH + ATT_Q_WIDTH + 2 * ATT_KV_WIDTH
GLA_HEADS = 4
GLA_DK = 128
GLA_DV = 256
GLA_LOWRANK = 16
GLA_TAU = 16.0
GLA_CHUNK = 64
GLA_BLOCK = 256
GLA_K_WIDTH = GLA_HEADS * GLA_DK
GLA_V_WIDTH = GLA_HEADS * GLA_DV
ODD_IN = 2 * GLA_K_WIDTH + 2 * GLA_V_WIDTH + 2 * GLA_LOWRANK
N_EXPERTS = 16
N_GROUPS = 4
EXPERTS_PER_GROUP = 4
D_EXPERT = 512
PAIRS_PER_GROUP = 6
N_CLASSES = N_GROUPS * PAIRS_PER_GROUP
PAIR_LO = (0, 0, 0, 1, 1, 2)
PAIR_HI = (1, 2, 3, 2, 3, 3)

LANES = 128
SUBLANES = 8
VMEM_LIMIT_BYTES = 56 * 1024 * 1024

ROW_TILE = 512
MOE_TILE = 256
MOE_ROWS = 256
LRU_CHUNK = 128
LRU_HALF = 256
ODD_IN_PAD = ((ODD_IN + LANES - 1) // LANES) * LANES


def _params(*sem):
    return pltpu.CompilerParams(dimension_semantics=sem, vmem_limit_bytes=VMEM_LIMIT_BYTES)


def _modulated_norm(x, g, shift, scale):
    y = x * lax.rsqrt(jnp.mean(x * x, axis=-1, keepdims=True) + EPS)
    return (y * g) * (1.0 + scale) + shift


def _ada_kernel(c_ref, w_ref, b_ref, o_ref):
    c = c_ref[...]
    s = c * jax.nn.sigmoid(c)
    o_ref[...] = jnp.dot(s, w_ref[...], preferred_element_type=F32) + b_ref[...]


def _ada_call(cvec, ada_w, ada_b):
    depth, d, n6 = ada_w.shape
    r = cvec.shape[0]
    tn = n6 // 4
    return pl.pallas_call(
        _ada_kernel,
        grid=(depth, n6 // tn),
        in_specs=[pl.BlockSpec((r, d), lambda l, j: (0, 0)),
                  pl.BlockSpec((None, d, tn), lambda l, j: (l, 0, j)),
                  pl.BlockSpec((None, 1, tn), lambda l, j: (l, 0, j))],
        out_specs=pl.BlockSpec((None, r, tn), lambda l, j: (l, 0, j)),
        out_shape=jax.ShapeDtypeStruct((depth, r, n6), F32),
        compiler_params=_params("arbitrary", "arbitrary"),
        name="adaln",
    )(cvec, ada_w, ada_b.reshape(depth, 1, n6))


def _two_source_specs(width, n_lat_tiles):
    lat = pl.BlockSpec((ROW_TILE, width), lambda i: (jnp.minimum(i, n_lat_tiles - 1), 0))
    ctx = pl.BlockSpec((ROW_TILE, width), lambda i: (jnp.maximum(i - n_lat_tiles, 0), 0))
    return lat, ctx


def _two_source(lat_ref, ctx_ref, n_lat_tiles):
    return jnp.where(pl.program_id(0) >= n_lat_tiles, ctx_ref[...], lat_ref[...])


def _proj_kernel(xl_ref, xc_ref, mod_ref, g_ref, w_ref, o_ref, *, n_lat_tiles):
    x = _two_source(xl_ref, xc_ref, n_lat_tiles)
    n = _modulated_norm(x, g_ref[...], mod_ref[0:1, :], mod_ref[1:2, :])
    o_ref[...] = jnp.dot(n.astype(BF16), w_ref[...], preferred_element_type=F32).astype(o_ref.dtype)


def _proj_res_kernel(h_ref, f_ref, pmod_ref, mod_ref, g_ref, w_ref, x_ref, o_ref):
    x = h_ref[...] + pmod_ref[5:6, :] * f_ref[...]
    x_ref[...] = x
    n = _modulated_norm(x, g_ref[...], mod_ref[0:1, :], mod_ref[1:2, :])
    o_ref[...] = jnp.dot(n.astype(BF16), w_ref[...], preferred_element_type=F32).astype(o_ref.dtype)


def _mod_spec(tiles_per_seq, n_batch):
    return pl.BlockSpec((None, 6, D_MODEL), lambda i: (jnp.minimum(i // tiles_per_seq, n_batch), 0, 0))


def _proj_call(x_lat, x_ctx, mod, g, w, tiles_per_seq, n_batch):
    d = x_lat.shape[1]
    n = x_lat.shape[0] + x_ctx.shape[0]
    n_out = w.shape[1]
    n_lat_tiles = x_lat.shape[0] // ROW_TILE
    lat, ctx = _two_source_specs(d, n_lat_tiles)
    return pl.pallas_call(
        functools.partial(_proj_kernel, n_lat_tiles=n_lat_tiles),
        grid=(n // ROW_TILE,),
        in_specs=[lat, ctx,
                  _mod_spec(tiles_per_seq, n_batch),
                  pl.BlockSpec((1, d), lambda i: (0, 0)),
                  pl.BlockSpec((d, n_out), lambda i: (0, 0))],
        out_specs=pl.BlockSpec((ROW_TILE, n_out), lambda i: (i, 0)),
        out_shape=jax.ShapeDtypeStruct((n, n_out), ACT_DTYPE),
        compiler_params=_params("arbitrary"),
        name="in_proj",
    )(x_lat, x_ctx, mod, g, w)


def _proj_res_call(h, ffn, pmod, mod, g, w, tiles_per_seq, n_batch):
    n, d = h.shape
    n_out = w.shape[1]
    row = pl.BlockSpec((ROW_TILE, d), lambda i: (i, 0))
    return pl.pallas_call(
        _proj_res_kernel,
        grid=(n // ROW_TILE,),
        in_specs=[row, row, _mod_spec(tiles_per_seq, n_batch), _mod_spec(tiles_per_seq, n_batch),
                  pl.BlockSpec((1, d), lambda i: (0, 0)),
                  pl.BlockSpec((d, n_out), lambda i: (0, 0))],
        out_specs=[row, pl.BlockSpec((ROW_TILE, n_out), lambda i: (i, 0))],
        out_shape=[jax.ShapeDtypeStruct((n, d), F32), jax.ShapeDtypeStruct((n, n_out), ACT_DTYPE)],
        compiler_params=_params("arbitrary"),
        name="in_proj_res",
    )(h, ffn, pmod, mod, g, w)


def _qkprep_kernel(q_ref, k_ref, v_ref, c_ref, s_ref, qg_ref, kg_ref, m_ref, qo_ref, kvo_ref):
    cos = c_ref[...]
    sin = s_ref[...]
    lane = lax.broadcasted_iota(jnp.int32, cos.shape, 1)
    first = (lane % 32) < 16
    low = lane < HEAD_DIM

    def norm_rope(x, g):
        ms = jnp.dot(x * x, m_ref[...], preferred_element_type=F32, precision=HIGHEST)
        xn = (x * lax.rsqrt(ms + EPS)) * g
        partner = jnp.where(first, pltpu.roll(xn, LANES - 16, 1), pltpu.roll(xn, 16, 1))
        return xn * cos + partner * sin

    for j in range(ATT_Q_WIDTH // LANES):
        qj = norm_rope(q_ref[:, j * LANES:(j + 1) * LANES].astype(F32), qg_ref[...])
        qo_ref[:, j * LANES:(j + 1) * LANES] = (qj * HEAD_DIM ** -0.5).astype(BF16)
    kr = norm_rope(k_ref[...].astype(F32), kg_ref[...])
    ks = pltpu.roll(kr, HEAD_DIM, 1)
    kvo_ref[:, 0:LANES] = jnp.where(low, kr, ks).astype(BF16)
    kvo_ref[:, LANES:2 * LANES] = jnp.where(low, ks, kr).astype(BF16)
    kvo_ref[:, 2 * LANES:3 * LANES] = v_ref[...].astype(BF16)


def _qkprep_call(proj, cos_t, sin_t, qg, kg, seg_mean, tiles_per_seq, n_lat_tiles):
    n = proj.shape[0]
    qcol = (2 * LRU_WIDTH) // ATT_Q_WIDTH
    kcol = (2 * LRU_WIDTH + ATT_Q_WIDTH) // LANES

    def pos(i):
        return (jnp.where(i < n_lat_tiles, i % tiles_per_seq, tiles_per_seq), 0)

    return pl.pallas_call(
        _qkprep_kernel,
        grid=(n // ROW_TILE,),
        in_specs=[pl.BlockSpec((ROW_TILE, ATT_Q_WIDTH), lambda i: (i, qcol)),
                  pl.BlockSpec((ROW_TILE, LANES), lambda i: (i, kcol)),
                  pl.BlockSpec((ROW_TILE, LANES), lambda i: (i, kcol + 1)),
                  pl.BlockSpec((ROW_TILE, LANES), pos),
                  pl.BlockSpec((ROW_TILE, LANES), pos),
                  pl.BlockSpec((1, LANES), lambda i: (0, 0)),
                  pl.BlockSpec((1, LANES), lambda i: (0, 0)),
                  pl.BlockSpec((LANES, LANES), lambda i: (0, 0))],
        out_specs=[pl.BlockSpec((ROW_TILE, ATT_Q_WIDTH), lambda i: (i, 0)),
                   pl.BlockSpec((ROW_TILE, 3 * LANES), lambda i: (i, 0))],
        out_shape=[jax.ShapeDtypeStruct((n, ATT_Q_WIDTH), BF16),
                   jax.ShapeDtypeStruct((n, 3 * LANES), BF16)],
        compiler_params=_params("arbitrary"),
        name="qk_prep",
    )(proj, proj, proj, cos_t, sin_t, qg, kg, seg_mean)


def _attend(q, kv, sink_ref, bias):
    rows = q.shape[0]
    lane = lax.broadcasted_iota(jnp.int32, (rows, LANES), 1)
    low = lane < HEAD_DIM
    zero = jnp.zeros((rows, LANES), BF16)
    vv = kv[:, 2 * LANES:3 * LANES]
    groups = []
    heads_per_kv = ATT_HEADS // ATT_KV_HEADS
    for kh in range(ATT_KV_HEADS):
        kk = kv[:, kh * LANES:(kh + 1) * LANES]
        parts = []
        sinks = []
        for j in range(heads_per_kv):
            h = kh * heads_per_kv + j
            qg = q[:, (h // 2) * LANES:(h // 2 + 1) * LANES]
            parts.append(jnp.where(low, qg, zero) if h % 2 == 0 else jnp.where(low, zero, qg))
            sinks.append(jnp.full((rows, 1), sink_ref[h], F32))
        lhs = jnp.concatenate(parts, axis=0)
        sink = jnp.concatenate(sinks, axis=0)
        s = lax.dot_general(lhs, kk, (((1,), (1,)), ((), ())), preferred_element_type=F32)
        if bias is not None:
            s = s + bias
        m = jnp.maximum(jnp.max(s, axis=-1, keepdims=True), sink)
        e = jnp.exp(s - m)
        denom = jnp.sum(e, axis=-1, keepdims=True) + jnp.exp(sink - m)
        o = jnp.dot(e.astype(BF16), vv, preferred_element_type=F32) / denom
        r = [o[j * rows:(j + 1) * rows] for j in range(heads_per_kv)]
        for p in range(heads_per_kv // 2):
            even, odd = r[2 * p], r[2 * p + 1]
            if kh == 0:
                groups.append(jnp.where(low, even, pltpu.roll(odd, HEAD_DIM, 1)))
            else:
                groups.append(jnp.where(low, pltpu.roll(even, HEAD_DIM, 1), odd))
    return jnp.concatenate(groups, axis=1)


def _attn_kernel(sink_ref, q_ref, kvp_ref, kvo_ref, kvn_ref, kvc_ref, bias_ref, o_ref):
    kv = jnp.concatenate([kvp_ref[...], kvo_ref[...], kvn_ref[...], kvc_ref[...]], axis=0)
    o_ref[...] = _attend(q_ref[...], kv, sink_ref, bias_ref[...]).astype(o_ref.dtype)


def _window_bias(ctx_len):
    blk = ATT_BLOCK
    stacked = (ATT_HEADS // ATT_KV_HEADS) * blk
    qi = lax.broadcasted_iota(jnp.int32, (4, stacked, 3 * blk + ctx_len), 1) % blk
    ks = lax.broadcasted_iota(jnp.int32, (4, stacked, 3 * blk + ctx_len), 2)
    var = lax.broadcasted_iota(jnp.int32, (4, stacked, 3 * blk + ctx_len), 0)
    in_win = jnp.abs(ks - blk - qi) <= WINDOW
    in_seq = ((ks >= blk) | (var % 2 == 1)) & ((ks < 2 * blk) | (var >= 2))
    keep = (in_win & in_seq) | (ks >= 3 * blk)
    return jnp.where(keep, 0.0, -1e30).astype(F32)


def _ctx_attn_kernel(sink_ref, q_ref, kvc_ref, o_ref):
    o_ref[...] = _attend(q_ref[...], kvc_ref[...], sink_ref, None).astype(o_ref.dtype)


def _attn_call(qrot, kvrot, sink, n_batch, seq, ctx_len):
    nb = seq // ATT_BLOCK
    ctx0 = (n_batch * seq) // ctx_len
    kvw = kvrot.shape[1]
    bias = _window_bias(ctx_len)

    def variant(b, n):
        return ((n > 0).astype(jnp.int32) + 2 * (n < nb - 1).astype(jnp.int32), 0, 0)

    return pl.pallas_call(
        _attn_kernel,
        grid=(n_batch, nb),
        in_specs=[pl.BlockSpec(memory_space=pltpu.SMEM),
                  pl.BlockSpec((ATT_BLOCK, ATT_Q_WIDTH), lambda b, n: (b * nb + n, 0)),
                  pl.BlockSpec((ATT_BLOCK, kvw), lambda b, n: (b * nb + jnp.maximum(n - 1, 0), 0)),
                  pl.BlockSpec((ATT_BLOCK, kvw), lambda b, n: (b * nb + n, 0)),
                  pl.BlockSpec((ATT_BLOCK, kvw), lambda b, n: (b * nb + jnp.minimum(n + 1, nb - 1), 0)),
                  pl.BlockSpec((ctx_len, kvw), lambda b, n: (ctx0 + b, 0)),
                  pl.BlockSpec((None,) + bias.shape[1:], variant)],
        out_specs=pl.BlockSpec((ATT_BLOCK, ATT_Q_WIDTH), lambda b, n: (b * nb + n, 0)),
        out_shape=jax.ShapeDtypeStruct((n_batch * seq, ATT_Q_WIDTH), ACT_DTYPE),
        compiler_params=_params("arbitrary", "arbitrary"),
        name="window_attn",
    )(sink, qrot, kvrot, kvrot, kvrot, kvrot, bias)


def _ctx_attn_call(qrot, kvrot, sink, n_batch, seq, ctx_len):
    ctx0 = (n_batch * seq) // ctx_len
    kvw = kvrot.shape[1]
    return pl.pallas_call(
        _ctx_attn_kernel,
        grid=(n_batch,),
        in_specs=[pl.BlockSpec(memory_space=pltpu.SMEM),
                  pl.BlockSpec((ctx_len, ATT_Q_WIDTH), lambda b: (ctx0 + b, 0)),
                  pl.BlockSpec((ctx_len, kvw), lambda b: (ctx0 + b, 0))],
        out_specs=pl.BlockSpec((ctx_len, ATT_Q_WIDTH), lambda b: (b, 0)),
        out_shape=jax.ShapeDtypeStruct((n_batch * ctx_len, ATT_Q_WIDTH), ACT_DTYPE),
        compiler_params=_params("arbitrary"),
        name="ctx_attn",
    )(sink, qrot, kvrot)


def _scan_rows(a, x, carry, reverse):
    rows, width = a.shape
    sub = lax.broadcasted_iota(jnp.int32, (rows, width), 0) % SUBLANES
    for s in (1, 2, 4):
        shift = rows - s if reverse else s
        a_sh = pltpu.roll(a, shift, 0)
        x_sh = pltpu.roll(x, shift, 0)
        keep = (sub <= SUBLANES - 1 - s) if reverse else (sub >= s)
        x = x + a * jnp.where(keep, x_sh, 0.0)
        a = a * jnp.where(keep, a_sh, 1.0)
    n_groups = rows // SUBLANES
    out = [None] * n_groups
    order = range(n_groups - 1, -1, -1) if reverse else range(n_groups)
    for g in order:
        hg = x[g * SUBLANES:(g + 1) * SUBLANES] + a[g * SUBLANES:(g + 1) * SUBLANES] * carry
        carry = hg[0:1] if reverse else hg[SUBLANES - 1:SUBLANES]
        out[g] = hg
    return jnp.concatenate(out, axis=0), carry


def _rglru_kernel(ul_ref, gl_ref, uc_ref, gc_ref, cw_ref, cb_ref, wr_ref, wi_ref, br_ref, bi_ref, lam_ref,
                  yl_ref, yc_ref, upad_ref, conv_ref, *, seq, ctx_len):
    width = ul_ref.shape[1]
    tc = LRU_CHUNK
    neg_lam = -lam_ref[...]
    softplus = jnp.maximum(neg_lam, 0.0) + jnp.log1p(jnp.exp(-jnp.abs(neg_lam)))
    zeros = jnp.zeros((SUBLANES, width), F32)

    def gates(u, d):
        ub = u.astype(BF16)
        r = jax.nn.sigmoid(jnp.dot(ub, wr_ref[d], preferred_element_type=F32) + br_ref[d:d + 1, :])
        i = jax.nn.sigmoid(jnp.dot(ub, wi_ref[d], preferred_element_type=F32) + bi_ref[d:d + 1, :])
        log_a = (-LRU_C) * r * softplus[d:d + 1, :]
        a = jnp.exp(log_a)
        y = 1.0 - a * a
        root = jnp.where(y > 0.0, y * lax.rsqrt(y), 0.0)
        return a, root * (i * u)

    def run(u_ref, g_ref, y_ref, n, h_fwd, h_bwd):
        n_chunks = n // tc
        upad_ref[0:SUBLANES, :] = zeros
        upad_ref[pl.ds(SUBLANES, n), :] = u_ref[...].astype(F32)
        upad_ref[pl.ds(SUBLANES + n, SUBLANES), :] = zeros

        def conv_body(c, carry):
            t0 = pl.multiple_of(c * tc, tc)
            win = upad_ref[pl.ds(t0, tc + 2 * SUBLANES), :]
            acc = cb_ref[...]
            for k in range(CONV_W):
                acc = acc + cw_ref[k:k + 1, :] * win[SUBLANES - 1 + k:SUBLANES - 1 + k + tc]
            conv_ref[pl.ds(t0, tc), :] = acc
            return carry

        lax.fori_loop(0, n_chunks, conv_body, 0)

        def fwd_body(c, carry):
            t0 = pl.multiple_of(c * tc, tc)
            a, x = gates(conv_ref[pl.ds(t0, tc), :], 0)
            h, carry = _scan_rows(a, x, carry, False)
            upad_ref[pl.ds(pl.multiple_of(t0 + SUBLANES, SUBLANES), tc), :] = h
            return carry

        h_fwd = lax.fori_loop(0, n_chunks, fwd_body, h_fwd)

        def bwd_body(c, carry):
            t0 = pl.multiple_of((n_chunks - 1 - c) * tc, tc)
            a, x = gates(conv_ref[pl.ds(t0, tc), :], 1)
            h, carry = _scan_rows(a, x, carry, True)
            h_both = upad_ref[pl.ds(pl.multiple_of(t0 + SUBLANES, SUBLANES), tc), :] + h
            y_ref[pl.ds(t0, tc), :] = (h_both * jax.nn.gelu(g_ref[pl.ds(t0, tc), :].astype(F32))).astype(y_ref.dtype)
            return carry

        h_bwd = lax.fori_loop(0, n_chunks, bwd_body, h_bwd)
        return h_fwd, h_bwd

    h0 = jnp.zeros((1, width), F32)
    h_fwd, h_bwd = run(uc_ref, gc_ref, yc_ref, ctx_len, h0, h0)
    run(ul_ref, gl_ref, yl_ref, seq, h_fwd, h_bwd)


def _rglru_call(proj, conv_w, conv_b, wr, wi, br, bi, lam, n_batch, seq, ctx_len):
    ctx0 = (n_batch * seq) // ctx_len
    halves = LRU_WIDTH // LRU_HALF
    w = LRU_HALF
    vec2 = pl.BlockSpec((2, w), lambda b, c: (0, c))
    return pl.pallas_call(
        functools.partial(_rglru_kernel, seq=seq, ctx_len=ctx_len),
        grid=(n_batch, halves),
        in_specs=[pl.BlockSpec((seq, w), lambda b, c: (b, c)),
                  pl.BlockSpec((seq, w), lambda b, c: (b, halves + c)),
                  pl.BlockSpec((ctx_len, w), lambda b, c: (ctx0 + b, c)),
                  pl.BlockSpec((ctx_len, w), lambda b, c: (ctx0 + b, halves + c)),
                  pl.BlockSpec((CONV_W, w), lambda b, c: (0, c)),
                  pl.BlockSpec((1, w), lambda b, c: (0, c)),
                  pl.BlockSpec((None, 2, w, w), lambda b, c: (c, 0, 0, 0)),
                  pl.BlockSpec((None, 2, w, w), lambda b, c: (c, 0, 0, 0)),
                  vec2, vec2, vec2],
        out_specs=[pl.BlockSpec((seq, w), lambda b, c: (b, c)),
                   pl.BlockSpec((ctx_len, w), lambda b, c: (b, c))],
        out_shape=[jax.ShapeDtypeStruct((n_batch * seq, LRU_WIDTH), ACT_DTYPE),
                   jax.ShapeDtypeStruct((n_batch * ctx_len, LRU_WIDTH), ACT_DTYPE)],
        scratch_shapes=[pltpu.VMEM((seq + 2 * SUBLANES, w), F32), pltpu.VMEM((seq, w), F32)],
        compiler_params=_params("arbitrary", "arbitrary"),
        name="rglru",
    )(proj, proj, proj, proj, conv_w, conv_b, wr, wi, br, bi, lam)


def _split3(x):
    hi = x.astype(BF16)
    r1 = x - hi.astype(F32)
    mid = r1.astype(BF16)
    lo = (r1 - mid.astype(F32)).astype(BF16)
    return hi, mid, lo


def _gla_kernel(q_ref, k_ref, v_ref, r_ref, a_ref, kc_ref, vc_ref, ac_ref, gw_ref, gb_ref, ng_ref,
                o_ref, of_ref, ob_ref, qd_ref, ki_ref, ke_ref, la_ref, eg_ref, st_ref, *, seq, ctx_len):
    ch = GLA_CHUNK
    blk = GLA_BLOCK
    per_blk = blk // ch
    row = lax.broadcasted_iota(jnp.int32, (blk, blk), 0)
    col = lax.broadcasted_iota(jnp.int32, (blk, blk), 1)
    same = (row // ch) == (col // ch)
    keep = (same & (row >= col), same & (row <= col))
    tri = (keep[0].astype(BF16),)
    scale = GLA_DK ** -0.5
    nt = (((1,), (1,)), ((), ()))
    tn = (((0,), (0,)), ((), ()))

    def mm(a, b):
        return jnp.dot(a, b, preferred_element_type=F32)

    def prepare(a_src, k_src, v_src, q_src, dst, n_rows, d):
        n_blk = n_rows // blk
        unroll = min(4, n_blk)

        def decay_body(i, carry):
            t0 = pl.multiple_of(i * blk, blk)
            both = mm(a_src[pl.ds(t0, blk), :], gw_ref[d])
            pre = (both[:, 0:GLA_DK] + both[:, GLA_DK:2 * GLA_DK]) + gb_ref[d:d + 1, :]
            la_ref[pl.ds(t0, blk), :] = (jnp.minimum(pre, 0.0) - jnp.log1p(jnp.exp(-jnp.abs(pre)))) / GLA_TAU
            return carry

        lax.fori_loop(0, n_blk, decay_body, 0, unroll=unroll)

        def scale_body(i, carry):
            t0 = pl.multiple_of(i * blk, blk)
            log_a = la_ref[pl.ds(t0, blk), :]
            sums = mm(tri[0], jnp.concatenate(_split3(log_a), axis=1))
            prefix = sums[:, 0:GLA_DK] + (sums[:, GLA_DK:2 * GLA_DK] + sums[:, 2 * GLA_DK:3 * GLA_DK])
            total = jnp.concatenate(
                [jnp.broadcast_to(prefix[(j + 1) * ch - 1:(j + 1) * ch], (ch, GLA_DK)) for j in range(per_blk)],
                axis=0)
            suffix = total - prefix + log_a
            b = prefix if d == 0 else suffix
            to_end = (suffix if d == 0 else prefix) - log_a
            for j in range(per_blk):
                eg_ref[pl.ds(i * per_blk + j, 1), :] = jnp.exp(total[j * ch:j * ch + 1])
            k_blk = k_src[pl.ds(t0, blk), :].astype(F32)
            ke_ref[pl.ds(t0, blk), :] = (k_blk * jnp.exp(to_end)).astype(BF16)
            if q_src is not None:
                q_blk = q_src[pl.ds(t0, blk), :].astype(F32)
                qd_ref[pl.ds(t0, blk), :] = ((q_blk * scale) * jnp.exp(b)).astype(BF16)
                ki_ref[pl.ds(t0, blk), :] = (k_blk * jnp.exp(-b)).astype(BF16)
            return carry

        lax.fori_loop(0, n_blk, scale_body, 0, unroll=unroll)
        if q_src is None:
            return

        def intra_body(i, carry):
            t0 = pl.multiple_of(i * blk, blk)
            s = lax.dot_general(qd_ref[pl.ds(t0, blk), :], ki_ref[pl.ds(t0, blk), :], nt,
                                preferred_element_type=F32)
            p = jnp.where(keep[d], s, 0.0).astype(BF16)
            dst[pl.ds(t0, blk), :] = mm(p, v_src[pl.ds(t0, blk), :].astype(BF16))
            return carry

        lax.fori_loop(0, n_blk, intra_body, 0, unroll=unroll)

    def recur(v_src, dst, n_rows, d):
        n_chunks = n_rows // ch

        def body(c, carry):
            n = c if d == 0 else n_chunks - 1 - c
            t0 = pl.multiple_of(n * ch, ch)
            st = st_ref[...]
            if dst is not None:
                dst[pl.ds(t0, ch), :] = dst[pl.ds(t0, ch), :] + lax.dot_general(
                    qd_ref[pl.ds(t0, ch), :], st.astype(BF16), nt, preferred_element_type=F32)
            kv = lax.dot_general(v_src[pl.ds(t0, ch), :].astype(BF16), ke_ref[pl.ds(t0, ch), :], tn,
                                 preferred_element_type=F32)
            st_ref[...] = st * eg_ref[pl.ds(n, 1), :] + kv
            return carry

        lax.fori_loop(0, n_chunks, body, 0, unroll=min(8, n_chunks))

    for d, dst in ((0, of_ref), (1, ob_ref)):
        st_ref[...] = jnp.zeros(st_ref.shape, F32)
        prepare(ac_ref, kc_ref, vc_ref, None, None, ctx_len, d)
        recur(vc_ref, None, ctx_len, d)
        prepare(a_ref, k_ref, v_ref, q_ref, dst, seq, d)
        recur(v_ref, dst, seq, d)

    def finish_body(c, carry):
        t0 = pl.multiple_of(c * blk, blk)
        o = of_ref[pl.ds(t0, blk), :] + ob_ref[pl.ds(t0, blk), :]
        on = (o * lax.rsqrt(jnp.mean(o * o, axis=-1, keepdims=True) + EPS)) * ng_ref[...]
        r = r_ref[pl.ds(t0, blk), :].astype(F32)
        o_ref[pl.ds(t0, blk), :] = (on * (r * jax.nn.sigmoid(r))).astype(o_ref.dtype)
        return carry

    lax.fori_loop(0, seq // blk, finish_body, 0)


def _gla_call(proj, gate_w, gate_b, norm_g, n_batch, seq, ctx_len):
    ctx0 = (n_batch * seq) // ctx_len
    kcol = GLA_K_WIDTH // GLA_DK
    vcol = (2 * GLA_K_WIDTH) // GLA_DV
    rcol = (2 * GLA_K_WIDTH + GLA_V_WIDTH) // GLA_DV
    acol = (2 * GLA_K_WIDTH + 2 * GLA_V_WIDTH) // LANES
    return pl.pallas_call(
        functools.partial(_gla_kernel, seq=seq, ctx_len=ctx_len),
        grid=(n_batch, GLA_HEADS),
        in_specs=[pl.BlockSpec((seq, GLA_DK), lambda b, h: (b, h)),
                  pl.BlockSpec((seq, GLA_DK), lambda b, h: (b, kcol + h)),
                  pl.BlockSpec((seq, GLA_DV), lambda b, h: (b, vcol + h)),
                  pl.BlockSpec((seq, GLA_DV), lambda b, h: (b, rcol + h)),
                  pl.BlockSpec((seq, LANES), lambda b, h: (b, acol)),
                  pl.BlockSpec((ctx_len, GLA_DK), lambda b, h: (ctx0 + b, kcol + h)),
                  pl.BlockSpec((ctx_len, GLA_DV), lambda b, h: (ctx0 + b, vcol + h)),
                  pl.BlockSpec((ctx_len, LANES), lambda b, h: (ctx0 + b, acol)),
                  pl.BlockSpec((None, 2, LANES, 2 * GLA_DK), lambda b, h: (h, 0, 0, 0)),
                  pl.BlockSpec((None, 2, GLA_DK), lambda b, h: (h, 0, 0)),
                  pl.BlockSpec((1, GLA_DV), lambda b, h: (0, 0))],
        out_specs=pl.BlockSpec((seq, GLA_DV), lambda b, h: (b, h)),
        out_shape=jax.ShapeDtypeStruct((n_batch * seq, GLA_V_WIDTH), ACT_DTYPE),
        scratch_shapes=[pltpu.VMEM((seq, GLA_DV), F32),
                        pltpu.VMEM((seq, GLA_DV), F32),
                        pltpu.VMEM((seq, GLA_DK), BF16),
                        pltpu.VMEM((seq, GLA_DK), BF16),
                        pltpu.VMEM((seq, GLA_DK), BF16),
                        pltpu.VMEM((seq, GLA_DK), F32),
                        pltpu.VMEM((seq // GLA_CHUNK, GLA_DK), F32),
                        pltpu.VMEM((GLA_DV, GLA_DK), F32)],
        compiler_params=_params("arbitrary", "arbitrary"),
        name="gla",
    )(proj, proj, proj, proj, proj, proj, proj, proj, gate_w, gate_b, norm_g)


def _route(rows, rw_ref, rb_ref):
    n_rows = rows.shape[0]
    logits = lax.dot_general(rw_ref[...], rows, (((1,), (1,)), ((), ())),
                             preferred_element_type=F32, precision=HIGHEST)
    scores = jax.nn.sigmoid(logits)
    sel = scores + rb_ref[...]
    eid = lax.broadcasted_iota(jnp.int32, (N_EXPERTS, n_rows), 0)
    gid = eid // EXPERTS_PER_GROUP
    neg = -jnp.inf
    big = N_EXPERTS

    def top2(masked):
        m1 = jnp.max(masked, axis=0, keepdims=True)
        i1 = jnp.min(jnp.where(masked == m1, eid, big), axis=0, keepdims=True)
        rest = jnp.where(eid == i1, neg, masked)
        m2 = jnp.max(rest, axis=0, keepdims=True)
        i2 = jnp.min(jnp.where(rest == m2, eid, big), axis=0, keepdims=True)
        return m1, i1, m2, i2

    best = None
    best_g = None
    for g in range(N_GROUPS):
        m1, _, m2, _ = top2(jnp.where(gid == g, sel, neg))
        gs = m1 + m2
        if best is None:
            best, best_g = gs, jnp.zeros((1, n_rows), jnp.int32)
        else:
            better = gs > best
            best_g = jnp.where(better, g, best_g)
            best = jnp.where(better, gs, best)
    _, i1, _, i2 = top2(jnp.where(gid == best_g, sel, neg))
    s1 = jnp.sum(jnp.where(eid == i1, scores, 0.0), axis=0, keepdims=True)
    s2 = jnp.sum(jnp.where(eid == i2, scores, 0.0), axis=0, keepdims=True)
    total = s1 + s2
    swap = i2 < i1
    lo = jnp.where(swap, i2, i1)
    hi = jnp.where(swap, i1, i2)
    w_lo = jnp.where(swap, s2, s1) / total
    w_hi = jnp.where(swap, s1, s2) / total
    pad = jnp.zeros((4, n_rows), F32)
    return jnp.concatenate([lo.astype(F32), hi.astype(F32), w_lo, w_hi, pad], axis=0)


def _finish_rows(y, h, mod_ref, g_ref, rw_ref, rb_ref, hn_ref, rows_ref, rt_ref, rc_ref):
    hn = h + mod_ref[2:3, :] * y
    hn_ref[...] = hn
    rows = _modulated_norm(hn, g_ref[...], mod_ref[3:4, :], mod_ref[4:5, :])
    rows_ref[...] = rows.astype(rows_ref.dtype)
    route = _route(rows, rw_ref, rb_ref)
    rt_ref[...] = route
    padded = jnp.concatenate([route, jnp.zeros((LANES - SUBLANES, route.shape[1]), F32)], axis=0)
    rc_ref[...] = padded.T


def _outproj_even_kernel(al_ref, ac_ref, bl_ref, bc_ref, w_ref, hl_ref, hc_ref, mod_ref, g_ref, rw_ref, rb_ref,
                         hn_ref, rows_ref, rt_ref, rc_ref, *, n_lat_tiles):
    a = _two_source(al_ref, ac_ref, n_lat_tiles).astype(BF16)
    b = _two_source(bl_ref, bc_ref, n_lat_tiles).astype(BF16)
    y = (jnp.dot(a, w_ref[0:LRU_WIDTH, :], preferred_element_type=F32)
         + jnp.dot(b, w_ref[LRU_WIDTH:LRU_WIDTH + ATT_Q_WIDTH, :], preferred_element_type=F32))
    h = _two_source(hl_ref, hc_ref, n_lat_tiles)
    _finish_rows(y, h, mod_ref, g_ref, rw_ref, rb_ref, hn_ref, rows_ref, rt_ref, rc_ref)


def _outproj_odd_kernel(o_ref, w_ref, h_ref, mod_ref, g_ref, rw_ref, rb_ref, hn_ref, rows_ref, rt_ref, rc_ref):
    y = jnp.dot(o_ref[...].astype(BF16), w_ref[...], preferred_element_type=F32)
    _finish_rows(y, h_ref[...], mod_ref, g_ref, rw_ref, rb_ref, hn_ref, rows_ref, rt_ref, rc_ref)


def _outproj_specs(n_rows, tiles_per_seq, n_batch):
    d = D_MODEL
    row = pl.BlockSpec((ROW_TILE, d), lambda i: (i, 0))
    tail_in = [_mod_spec(tiles_per_seq, n_batch),
               pl.BlockSpec((1, d), lambda i: (0, 0)),
               pl.BlockSpec((N_EXPERTS, d), lambda i: (0, 0)),
               pl.BlockSpec((N_EXPERTS, 1), lambda i: (0, 0))]
    out_specs = [row, row, pl.BlockSpec((SUBLANES, ROW_TILE), lambda i: (0, i)),
                 pl.BlockSpec((ROW_TILE, LANES), lambda i: (i, 0))]
    out_shape = [jax.ShapeDtypeStruct((n_rows, d), F32), jax.ShapeDtypeStruct((n_rows, d), ACT_DTYPE),
                 jax.ShapeDtypeStruct((SUBLANES, n_rows), F32), jax.ShapeDtypeStruct((n_rows, LANES), F32)]
    return tail_in, out_specs, out_shape


def _outproj_even_call(a_lat, a_ctx, b_lat, b_ctx, w, h_lat, h_ctx, mod, g, rw_t, rb, tiles_per_seq, n_batch):
    n_rows = h_lat.shape[0] + h_ctx.shape[0]
    n_lat_tiles = a_lat.shape[0] // ROW_TILE
    tail_in, out_specs, out_shape = _outproj_specs(n_rows, tiles_per_seq, n_batch)
    lat, ctx = _two_source_specs(LRU_WIDTH, n_lat_tiles)
    h_specs = list(_two_source_specs(D_MODEL, n_lat_tiles))
    return pl.pallas_call(
        functools.partial(_outproj_even_kernel, n_lat_tiles=n_lat_tiles),
        grid=(n_rows // ROW_TILE,),
        in_specs=[lat, ctx, lat, ctx, pl.BlockSpec(w.shape, lambda i: (0, 0))] + h_specs + tail_in,
        out_specs=out_specs,
        out_shape=out_shape,
        compiler_params=_params("arbitrary"),
        name="out_proj_even",
    )(a_lat, a_ctx, b_lat, b_ctx, w, h_lat, h_ctx, mod, g, rw_t, rb)


def _outproj_odd_call(o, w, h, mod, g, rw_t, rb, tiles_per_seq, n_batch):
    n_rows = o.shape[0]
    tail_in, out_specs, out_shape = _outproj_specs(n_rows, tiles_per_seq, n_batch)
    return pl.pallas_call(
        _outproj_odd_kernel,
        grid=(n_rows // ROW_TILE,),
        in_specs=[pl.BlockSpec((ROW_TILE, o.shape[1]), lambda i: (i, 0)),
                  pl.BlockSpec(w.shape, lambda i: (0, 0)),
                  pl.BlockSpec((ROW_TILE, D_MODEL), lambda i: (i, 0))] + tail_in,
        out_specs=out_specs,
        out_shape=out_shape,
        compiler_params=_params("arbitrary"),
        name="out_proj_odd",
    )(o, w, h, mod, g, rw_t, rb)


X_SLAB = D_MODEL // LANES
Y_SLAB = 2 * X_SLAB


def _for_rows(n, fn):
    def body(r, carry):
        fn(r)
        return carry

    lax.fori_loop(0, n, body, 0, unroll=8)


def _dispatch_kernel(slot_ref, rows_ref, xs_init_ref, xs_ref, slab_ref, sem, *, steps):
    del xs_init_ref
    i = pl.program_id(0)
    k = i % 2
    rt = MOE_ROWS

    def wait_all(buf):
        _for_rows(rt, lambda r: pltpu.make_async_copy(
            slab_ref.at[buf, pl.ds(0, X_SLAB)], xs_ref.at[pl.ds(0, X_SLAB)], sem.at[buf]).wait())

    @pl.when(i >= 2)
    def _():
        wait_all(k)

    for j in range(X_SLAB):
        slab_ref[k, pl.ds(j, rt, stride=X_SLAB), :] = rows_ref[:, j * LANES:(j + 1) * LANES].astype(F32)

    _for_rows(rt, lambda r: pltpu.make_async_copy(
        slab_ref.at[k, pl.ds(pl.multiple_of(r * X_SLAB, X_SLAB), X_SLAB)],
        xs_ref.at[pl.ds(pl.multiple_of(slot_ref[0, 0, r] * X_SLAB, X_SLAB), X_SLAB)],
        sem.at[k]).start())

    @pl.when(i == steps - 1)
    def _():
        wait_all(k)
        if steps >= 2:
            wait_all(1 - k)


def _dispatch_call(rows, slots, n_slots):
    n, d = rows.shape
    steps = n // MOE_ROWS
    xs_init = jnp.zeros((n_slots * X_SLAB, LANES), F32)
    return pl.pallas_call(
        functools.partial(_dispatch_kernel, steps=steps),
        grid=(steps,),
        in_specs=[pl.BlockSpec((1, 1, MOE_ROWS), lambda i: (i, 0, 0), memory_space=pltpu.SMEM),
                  pl.BlockSpec((MOE_ROWS, d), lambda i: (i, 0)),
                  pl.BlockSpec(memory_space=pl.ANY)],
        out_specs=pl.BlockSpec(memory_space=pl.ANY),
        out_shape=jax.ShapeDtypeStruct((n_slots * X_SLAB, LANES), F32),
        scratch_shapes=[pltpu.VMEM((2, MOE_ROWS * X_SLAB, LANES), F32), pltpu.SemaphoreType.DMA((2,))],
        input_output_aliases={2: 0},
        compiler_params=_params("arbitrary"),
        name="moe_dispatch",
    )(slots.reshape(steps, 1, MOE_ROWS), rows, xs_init)


def _experts_kernel(elo_ref, ehi_ref, used_ref, x_ref, wg0_ref, wu0_ref, wd0_ref,
                    wg1_ref, wu1_ref, wd1_ref, y_ref):
    t = pl.program_id(0)
    tm = MOE_TILE

    @pl.when(t < used_ref[0])
    def _():
        x = jnp.concatenate([x_ref[pl.ds(j, tm, stride=X_SLAB), :] for j in range(X_SLAB)],
                            axis=1).astype(BF16)
        for e, (wg_ref, wu_ref, wd_ref) in enumerate(((wg0_ref, wu0_ref, wd0_ref), (wg1_ref, wu1_ref, wd1_ref))):
            gate = jnp.dot(x, wg_ref[...], preferred_element_type=F32)
            up = jnp.dot(x, wu_ref[...], preferred_element_type=F32)
            act = ((gate * jax.nn.sigmoid(gate)) * up).astype(BF16)
            f = jnp.dot(act, wd_ref[...], preferred_element_type=F32)
            for j in range(X_SLAB):
                y_ref[pl.ds(e * X_SLAB + j, tm, stride=Y_SLAB), :] = f[:, j * LANES:(j + 1) * LANES]

    @pl.when(t >= used_ref[0])
    def _():
        y_ref[...] = jnp.zeros_like(y_ref)


def _experts_call(xs, e_lo, e_hi, n_used, w_gate, w_up, w_down):
    d = D_MODEL
    tiles = xs.shape[0] // (MOE_TILE * X_SLAB)

    def xmap(t, elo, ehi, used):
        return (jnp.maximum(jnp.minimum(t, used[0] - 1), 0), 0)

    def lo(t, elo, ehi, used):
        return (elo[t], 0, 0)

    def hi(t, elo, ehi, used):
        return (ehi[t], 0, 0)

    up_spec = lambda m: pl.BlockSpec((None, d, D_EXPERT), m)
    down_spec = lambda m: pl.BlockSpec((None, D_EXPERT, d), m)
    grid_spec = pltpu.PrefetchScalarGridSpec(
        num_scalar_prefetch=3,
        grid=(tiles,),
        in_specs=[pl.BlockSpec((MOE_TILE * X_SLAB, LANES), xmap),
                  up_spec(lo), up_spec(lo), down_spec(lo),
                  up_spec(hi), up_spec(hi), down_spec(hi)],
        out_specs=pl.BlockSpec((MOE_TILE * Y_SLAB, LANES), lambda t, elo, ehi, used: (t, 0)),
    )
    return pl.pallas_call(
        _experts_kernel,
        grid_spec=grid_spec,
        out_shape=jax.ShapeDtypeStruct((tiles * MOE_TILE * Y_SLAB, LANES), F32),
        compiler_params=_params("arbitrary"),
        name="experts",
    )(e_lo, e_hi, n_used, xs, w_gate, w_up, w_down, w_gate, w_up, w_down)


def _collect_kernel(slot_ref, next_slot_ref, ys_ref, rc_ref, *rest, steps, residual):
    if residual:
        h_ref, mod_ref, o_ref, buf_ref, sem = rest
    else:
        o_ref, buf_ref, sem = rest
    i = pl.program_id(0)
    k = i % 2
    rt = MOE_ROWS

    def start_all(slots, buf):
        _for_rows(rt, lambda r: pltpu.make_async_copy(
            ys_ref.at[pl.ds(pl.multiple_of(slots[0, 0, r] * Y_SLAB, Y_SLAB), Y_SLAB)],
            buf_ref.at[buf, pl.ds(pl.multiple_of(r * Y_SLAB, Y_SLAB), Y_SLAB)],
            sem.at[buf]).start())

    @pl.when(i == 0)
    def _():
        start_all(slot_ref, 0)

    @pl.when(i + 1 < steps)
    def _():
        start_all(next_slot_ref, 1 - k)

    _for_rows(rt, lambda r: pltpu.make_async_copy(
        ys_ref.at[pl.ds(0, Y_SLAB)], buf_ref.at[k, pl.ds(0, Y_SLAB)], sem.at[k]).wait())

    rc = rc_ref[...]
    w_lo = rc[:, 2:3]
    w_hi = rc[:, 3:4]
    for j in range(X_SLAB):
        cols = slice(j * LANES, (j + 1) * LANES)
        f = (w_lo * buf_ref[k, pl.ds(j, rt, stride=Y_SLAB), :]
             + w_hi * buf_ref[k, pl.ds(X_SLAB + j, rt, stride=Y_SLAB), :])
        if residual:
            o_ref[:, cols] = h_ref[:, cols] + mod_ref[5:6, cols] * f
        else:
            o_ref[:, cols] = f


def _collect_call(ys, slots, route_cols, residual=None):
    n = slots.shape[0]
    d = D_MODEL
    steps = n // MOE_ROWS
    slots3 = slots.reshape(steps, 1, MOE_ROWS)
    row = pl.BlockSpec((MOE_ROWS, d), lambda i: (i, 0))
    in_specs = [pl.BlockSpec((1, 1, MOE_ROWS), lambda i: (i, 0, 0), memory_space=pltpu.SMEM),
                pl.BlockSpec((1, 1, MOE_ROWS), lambda i: (jnp.minimum(i + 1, steps - 1), 0, 0),
                             memory_space=pltpu.SMEM),
                pl.BlockSpec(memory_space=pl.ANY),
                pl.BlockSpec((MOE_ROWS, LANES), lambda i: (i, 0))]
    args = [slots3, slots3, ys, route_cols]
    if residual is not None:
        h, mod, tiles_per_seq, n_batch = residual
        in_specs += [row, _mod_spec(tiles_per_seq, n_batch)]
        args += [h, mod]
    return pl.pallas_call(
        functools.partial(_collect_kernel, steps=steps, residual=residual is not None),
        grid=(steps,),
        in_specs=in_specs,
        out_specs=row,
        out_shape=jax.ShapeDtypeStruct((n, d), F32),
        scratch_shapes=[pltpu.VMEM((2, MOE_ROWS * Y_SLAB, LANES), F32), pltpu.SemaphoreType.DMA((2,))],
        compiler_params=_params("arbitrary"),
        name="moe_collect",
    )(*args)


def _moe(rows, route, route_cols, w_gate, w_up, w_down, residual=None):
    n = rows.shape[0]
    tm = MOE_TILE
    assert n % MOE_ROWS == 0
    max_tiles = -(-n // tm) + N_CLASSES
    p = max_tiles * tm
    lo = route[0].astype(jnp.int32)
    hi = route[1].astype(jnp.int32)
    lo_in = lo % EXPERTS_PER_GROUP
    hi_in = hi % EXPERTS_PER_GROUP
    pair = lo_in * 3 - (lo_in * (lo_in - 1)) // 2 + (hi_in - lo_in - 1)
    cls = (lo // EXPERTS_PER_GROUP) * PAIRS_PER_GROUP + pair
    onehot = (cls[:, None] == jnp.arange(N_CLASSES, dtype=jnp.int32)[None, :]).astype(jnp.int32)
    counts = jnp.sum(onehot, axis=0)
    rank = jnp.sum(jnp.cumsum(onehot, axis=0) * onehot, axis=1) - 1
    tiles_c = (counts + tm - 1) // tm
    tile_end = jnp.cumsum(tiles_c)
    tile_start = tile_end - tiles_c
    n_used = tile_end[-1]
    slot_of_token = (tile_start[cls] * tm + rank).astype(jnp.int32)
    tile_ids = jnp.arange(max_tiles, dtype=jnp.int32)
    tile_cls = jnp.minimum(jnp.sum((tile_end[None, :] <= tile_ids[:, None]).astype(jnp.int32), axis=1),
                           N_CLASSES - 1)
    group0 = (tile_cls // PAIRS_PER_GROUP) * EXPERTS_PER_GROUP
    e_lo = group0 + jnp.asarray(PAIR_LO, jnp.int32)[tile_cls % PAIRS_PER_GROUP]
    e_hi = group0 + jnp.asarray(PAIR_HI, jnp.int32)[tile_cls % PAIRS_PER_GROUP]
    xs = _dispatch_call(rows, slot_of_token, p)
    ys = _experts_call(xs, e_lo.astype(jnp.int32), e_hi.astype(jnp.int32),
                       n_used.reshape(1).astype(jnp.int32), w_gate, w_up, w_down)
    return _collect_call(ys, slot_of_token, route_cols, residual)


def _rope_tables(seq):
    rows = seq // GRID_W
    row = jnp.repeat(jnp.arange(rows), GRID_W).astype(F32)
    col = jnp.tile(jnp.arange(GRID_W), rows).astype(F32)
    n_freq = HEAD_DIM // 4
    inv_freq = ROPE_BASE ** (-jnp.arange(n_freq, dtype=F32) / n_freq)
    ang_r = row[:, None] * inv_freq
    ang_c = col[:, None] * inv_freq
    cos = jnp.concatenate([jnp.cos(ang_r), jnp.cos(ang_r), jnp.cos(ang_c), jnp.cos(ang_c)], axis=1)
    sin = jnp.concatenate([-jnp.sin(ang_r), jnp.sin(ang_r), -jnp.sin(ang_c), jnp.sin(ang_c)], axis=1)
    cos = jnp.concatenate([jnp.tile(cos, (1, 2)), jnp.ones((ROW_TILE, LANES), F32)], axis=0)
    sin = jnp.concatenate([jnp.tile(sin, (1, 2)), jnp.zeros((ROW_TILE, LANES), F32)], axis=0)
    return cos, sin


def _block_diag_halves(w):
    per_half = LRU_HALF // LRU_BLOCK
    halves = LRU_WIDTH // LRU_HALF
    out = jnp.zeros((halves, 2, LRU_HALF, LRU_HALF), F32)
    for c in range(halves):
        for j in range(per_half):
            s = slice(j * LRU_BLOCK, (j + 1) * LRU_BLOCK)
            out = out.at[c, :, s, s].set(w[:, c * per_half + j])
    return out.astype(BF16)


def _gla_gate_weights(gate_up, gate_b):
    w = jnp.zeros((GLA_HEADS, 2, LANES, GLA_DK), F32)
    for d in range(2):
        wd = gate_up[d].reshape(GLA_LOWRANK, GLA_HEADS, GLA_DK).transpose(1, 0, 2)
        w = w.at[:, d, d * GLA_LOWRANK:(d + 1) * GLA_LOWRANK, :].set(wd)
    b = gate_b.reshape(2, GLA_HEADS, GLA_DK).transpose(1, 0, 2)
    w_hi = w.astype(BF16)
    w_lo = (w - w_hi.astype(F32)).astype(BF16)
    return jnp.concatenate([w_hi, w_lo], axis=-1), b


def kernel(x, c, ctx, c_ctx, router_w, router_bias, ada_w, ada_b, norm_mix_g, norm_ffn_g, moe_w_gate, moe_w_up, moe_w_down, ev_w_in, ev_w_out, ev_conv_w, ev_conv_b, ev_rg_w, ev_rg_b, ev_ig_w, ev_ig_b, ev_lambda, ev_q_norm_g, ev_k_norm_g, ev_sink, od_w_in, od_w_out, od_gate_up, od_gate_b, od_out_norm_g):
    n_batch, seq, d = x.shape
    ctx_len = ctx.shape[1]
    n_lat = n_batch * seq
    n_ctx = n_batch * ctx_len
    assert ada_w.shape[0] == 2 and d == D_MODEL
    assert seq % ROW_TILE == 0 and n_ctx % ROW_TILE == 0 and seq % ctx_len == 0
    assert ctx_len % LRU_CHUNK == 0 and seq % GLA_BLOCK == 0 and ctx_len % GLA_BLOCK == 0
    tiles_per_seq = seq // ROW_TILE
    n_lat_tiles = n_lat // ROW_TILE

    x_lat = x.reshape(n_lat, d)
    x_ctx = ctx.reshape(n_ctx, d)
    mod_rows = -(-(n_batch + 1) // SUBLANES) * SUBLANES
    cvec = jnp.zeros((mod_rows, d), F32).at[:n_batch].set(c).at[n_batch].set(c_ctx)
    mods = _ada_call(cvec, ada_w, ada_b).reshape(2, mod_rows, 6, d)
    mod_mix = [mods[l][:n_batch + 1, 0:6] for l in range(2)]
    rw_t = router_w.T
    rb = router_bias.reshape(N_EXPERTS, 1)

    w_in0 = ev_w_in[0].astype(BF16)
    w_out0 = ev_w_out[0].astype(BF16)
    proj0 = _proj_call(x_lat, x_ctx, mod_mix[0], norm_mix_g[0:1], w_in0, tiles_per_seq, n_batch)
    cos_t, sin_t = _rope_tables(seq)
    seg = np.kron(np.eye(LANES // HEAD_DIM, dtype=np.float32), np.full((HEAD_DIM, HEAD_DIM), 1.0 / HEAD_DIM, np.float32))
    qrot, kvrot = _qkprep_call(proj0, cos_t, sin_t, jnp.tile(ev_q_norm_g[0], 2)[None, :],
                               jnp.tile(ev_k_norm_g[0], 2)[None, :], jnp.asarray(seg),
                               tiles_per_seq, n_lat_tiles)
    a_lat, a_ctx = _rglru_call(proj0, ev_conv_w[0], ev_conv_b[0:1], _block_diag_halves(ev_rg_w[0]),
                               _block_diag_halves(ev_ig_w[0]), ev_rg_b[0], ev_ig_b[0], ev_lambda[0],
                               n_batch, seq, ctx_len)
    b_lat = _attn_call(qrot, kvrot, ev_sink[0], n_batch, seq, ctx_len)
    b_ctx = _ctx_attn_call(qrot, kvrot, ev_sink[0], n_batch, seq, ctx_len)
    h, rows, route, route_cols = _outproj_even_call(a_lat, a_ctx, b_lat, b_ctx, w_out0, x_lat, x_ctx, mod_mix[0],
                                                    norm_ffn_g[0:1], rw_t, rb, tiles_per_seq, n_batch)
    ffn = _moe(rows, route, route_cols, moe_w_gate[0].astype(BF16), moe_w_up[0].astype(BF16),
               moe_w_down[0].astype(BF16))

    w_in1 = jnp.pad(od_w_in[0], ((0, 0), (0, ODD_IN_PAD - ODD_IN))).astype(BF16)
    h, proj1 = _proj_res_call(h, ffn, mod_mix[0], mod_mix[1], norm_mix_g[1:2], w_in1, tiles_per_seq, n_batch)
    gate_w, gate_b = _gla_gate_weights(od_gate_up[0], od_gate_b[0])
    o = _gla_call(proj1, gate_w, gate_b, od_out_norm_g[0:1], n_batch, seq, ctx_len)
    h_lat, rows, route, route_cols = _outproj_odd_call(o, od_w_out[0].astype(BF16), h, mod_mix[1],
                                                       norm_ffn_g[1:2], rw_t, rb, tiles_per_seq, n_batch)
    out = _moe(rows, route, route_cols, moe_w_gate[1].astype(BF16), moe_w_up[1].astype(BF16),
               moe_w_down[1].astype(BF16), residual=(h_lat, mod_mix[1], seq // MOE_ROWS, n_batch))
    return out.reshape(n_batch, seq, d)
```

```python
import functools

import jax
import jax.numpy as jnp
import numpy as np
from jax import lax
from jax.experimental import pallas as pl
from jax.experimental.pallas import tpu as pltpu

F32 = jnp.float32
BF16 = jnp.bfloat16
HIGHEST = lax.Precision.HIGHEST
ACT_DTYPE = BF16

D_MODEL = 1024
GRID_W = 64
EPS = 1e-6
LRU_WIDTH = 512
LRU_BLOCK = 64
CONV_W = 4
LRU_C = 8.0
ATT_HEADS = 8
ATT_KV_HEADS = 2
HEAD_DIM = 64
WINDOW = 128
ATT_BLOCK = 128
ROPE_BASE = 10000.0
ATT_Q_WIDTH = ATT_HEADS * HEAD_DIM
ATT_KV_WIDTH = ATT_KV_HEADS * HEAD_DIM
EVEN_IN = 2 * LRU_WIDTH + ATT_Q_WIDTH + 2 * ATT_KV_WIDTH
GLA_HEADS = 4
GLA_DK = 128
GLA_DV = 256
GLA_LOWRANK = 16
GLA_TAU = 16.0
GLA_CHUNK = 64
GLA_BLOCK = 256
GLA_K_WIDTH = GLA_HEADS * GLA_DK
GLA_V_WIDTH = GLA_HEADS * GLA_DV
ODD_IN = 2 * GLA_K_WIDTH + 2 * GLA_V_WIDTH + 2 * GLA_LOWRANK
N_EXPERTS = 16
N_GROUPS = 4
EXPERTS_PER_GROUP = 4
D_EXPERT = 512
PAIRS_PER_GROUP = 6
N_CLASSES = N_GROUPS * PAIRS_PER_GROUP
PAIR_LO = (0, 0, 0, 1, 1, 2)
PAIR_HI = (1, 2, 3, 2, 3, 3)

LANES = 128
SUBLANES = 8
VMEM_LIMIT_BYTES = 56 * 1024 * 1024

ROW_TILE = 512
MOE_TILE = 256
MOE_ROWS = 256
LRU_CHUNK = 128
LRU_HALF = 256
ODD_IN_PAD = ((ODD_IN + LANES - 1) // LANES) * LANES


def _params(*sem):
    return pltpu.CompilerParams(dimension_semantics=sem, vmem_limit_bytes=VMEM_LIMIT_BYTES)


def _modulated_norm(x, g, shift, scale):
    y = x * lax.rsqrt(jnp.mean(x * x, axis=-1, keepdims=True) + EPS)
    return (y * g) * (1.0 + scale) + shift


def _ada_kernel(c_ref, w_ref, b_ref, o_ref):
    c = c_ref[...]
    s = c * jax.nn.sigmoid(c)
    o_ref[...] = jnp.dot(s, w_ref[...], preferred_element_type=F32) + b_ref[...]


def _ada_call(cvec, ada_w, ada_b):
    depth, d, n6 = ada_w.shape
    r = cvec.shape[0]
    tn = n6 // 4
    return pl.pallas_call(
        _ada_kernel,
        grid=(depth, n6 // tn),
        in_specs=[pl.BlockSpec((r, d), lambda l, j: (0, 0)),
                  pl.BlockSpec((None, d, tn), lambda l, j: (l, 0, j)),
                  pl.BlockSpec((None, 1, tn), lambda l, j: (l, 0, j))],
        out_specs=pl.BlockSpec((None, r, tn), lambda l, j: (l, 0, j)),
        out_shape=jax.ShapeDtypeStruct((depth, r, n6), F32),
        compiler_params=_params("arbitrary", "arbitrary"),
        name="adaln",
    )(cvec, ada_w, ada_b.reshape(depth, 1, n6))


def _two_source_specs(width, n_lat_tiles):
    lat = pl.BlockSpec((ROW_TILE, width), lambda i: (jnp.minimum(i, n_lat_tiles - 1), 0))
    ctx = pl.BlockSpec((ROW_TILE, width), lambda i: (jnp.maximum(i - n_lat_tiles, 0), 0))
    return lat, ctx


def _two_source(lat_ref, ctx_ref, n_lat_tiles):
    return jnp.where(pl.program_id(0) >= n_lat_tiles, ctx_ref[...], lat_ref[...])


def _proj_kernel(xl_ref, xc_ref, mod_ref, g_ref, w_ref, o_ref, *, n_lat_tiles):
    x = _two_source(xl_ref, xc_ref, n_lat_tiles)
    n = _modulated_norm(x, g_ref[...], mod_ref[0:1, :], mod_ref[1:2, :])
    o_ref[...] = jnp.dot(n.astype(BF16), w_ref[...], preferred_element_type=F32).astype(o_ref.dtype)


def _proj_res_kernel(h_ref, f_ref, pmod_ref, mod_ref, g_ref, w_ref, x_ref, o_ref):
    x = h_ref[...] + pmod_ref[5:6, :] * f_ref[...]
    x_ref[...] = x
    n = _modulated_norm(x, g_ref[...], mod_ref[0:1, :], mod_ref[1:2, :])
    o_ref[...] = jnp.dot(n.astype(BF16), w_ref[...], preferred_element_type=F32).astype(o_ref.dtype)


def _mod_spec(tiles_per_seq, n_batch):
    return pl.BlockSpec((None, 6, D_MODEL), lambda i: (jnp.minimum(i // tiles_per_seq, n_batch), 0, 0))


def _proj_call(x_lat, x_ctx, mod, g, w, tiles_per_seq, n_batch):
    d = x_lat.shape[1]
    n = x_lat.shape[0] + x_ctx.shape[0]
    n_out = w.shape[1]
    n_lat_tiles = x_lat.shape[0] // ROW_TILE
    lat, ctx = _two_source_specs(d, n_lat_tiles)
    return pl.pallas_call(
        functools.partial(_proj_kernel, n_lat_tiles=n_lat_tiles),
        grid=(n // ROW_TILE,),
        in_specs=[lat, ctx,
                  _mod_spec(tiles_per_seq, n_batch),
                  pl.BlockSpec((1, d), lambda i: (0, 0)),
                  pl.BlockSpec((d, n_out), lambda i: (0, 0))],
        out_specs=pl.BlockSpec((ROW_TILE, n_out), lambda i: (i, 0)),
        out_shape=jax.ShapeDtypeStruct((n, n_out), ACT_DTYPE),
        compiler_params=_params("arbitrary"),
        name="in_proj",
    )(x_lat, x_ctx, mod, g, w)


def _proj_res_call(h, ffn, pmod, mod, g, w, tiles_per_seq, n_batch):
    n, d = h.shape
    n_out = w.shape[1]
    row = pl.BlockSpec((ROW_TILE, d), lambda i: (i, 0))
    return pl.pallas_call(
        _proj_res_kernel,
        grid=(n // ROW_TILE,),
        in_specs=[row, row, _mod_spec(tiles_per_seq, n_batch), _mod_spec(tiles_per_seq, n_batch),
                  pl.BlockSpec((1, d), lambda i: (0, 0)),
                  pl.BlockSpec((d, n_out), lambda i: (0, 0))],
        out_specs=[row, pl.BlockSpec((ROW_TILE, n_out), lambda i: (i, 0))],
        out_shape=[jax.ShapeDtypeStruct((n, d), F32), jax.ShapeDtypeStruct((n, n_out), ACT_DTYPE)],
        compiler_params=_params("arbitrary"),
        name="in_proj_res",
    )(h, ffn, pmod, mod, g, w)


def _qkprep_kernel(q_ref, k_ref, v_ref, c_ref, s_ref, qg_ref, kg_ref, m_ref, qo_ref, kvo_ref):
    cos = c_ref[...]
    sin = s_ref[...]
    lane = lax.broadcasted_iota(jnp.int32, cos.shape, 1)
    first = (lane % 32) < 16
    low = lane < HEAD_DIM

    def norm_rope(x, g):
        ms = jnp.dot(x * x, m_ref[...], preferred_element_type=F32, precision=HIGHEST)
        xn = (x * lax.rsqrt(ms + EPS)) * g
        partner = jnp.where(first, pltpu.roll(xn, LANES - 16, 1), pltpu.roll(xn, 16, 1))
        return xn * cos + partner * sin

    for j in range(ATT_Q_WIDTH // LANES):
        qj = norm_rope(q_ref[:, j * LANES:(j + 1) * LANES].astype(F32), qg_ref[...])
        qo_ref[:, j * LANES:(j + 1) * LANES] = (qj * HEAD_DIM ** -0.5).astype(BF16)
    kr = norm_rope(k_ref[...].astype(F32), kg_ref[...])
    ks = pltpu.roll(kr, HEAD_DIM, 1)
    kvo_ref[:, 0:LANES] = jnp.where(low, kr, ks).astype(BF16)
    kvo_ref[:, LANES:2 * LANES] = jnp.where(low, ks, kr).astype(BF16)
    kvo_ref[:, 2 * LANES:3 * LANES] = v_ref[...].astype(BF16)


def _qkprep_call(proj, cos_t, sin_t, qg, kg, seg_mean, tiles_per_seq, n_lat_tiles):
    n = proj.shape[0]
    qcol = (2 * LRU_WIDTH) // ATT_Q_WIDTH
    kcol = (2 * LRU_WIDTH + ATT_Q_WIDTH) // LANES

    def pos(i):
        return (jnp.where(i < n_lat_tiles, i % tiles_per_seq, tiles_per_seq), 0)

    return pl.pallas_call(
        _qkprep_kernel,
        grid=(n // ROW_TILE,),
        in_specs=[pl.BlockSpec((ROW_TILE, ATT_Q_WIDTH), lambda i: (i, qcol)),
                  pl.BlockSpec((ROW_TILE, LANES), lambda i: (i, kcol)),
                  pl.BlockSpec((ROW_TILE, LANES), lambda i: (i, kcol + 1)),
                  pl.BlockSpec((ROW_TILE, LANES), pos),
                  pl.BlockSpec((ROW_TILE, LANES), pos),
                  pl.BlockSpec((1, LANES), lambda i: (0, 0)),
                  pl.BlockSpec((1, LANES), lambda i: (0, 0)),
                  pl.BlockSpec((LANES, LANES), lambda i: (0, 0))],
        out_specs=[pl.BlockSpec((ROW_TILE, ATT_Q_WIDTH), lambda i: (i, 0)),
                   pl.BlockSpec((ROW_TILE, 3 * LANES), lambda i: (i, 0))],
        out_shape=[jax.ShapeDtypeStruct((n, ATT_Q_WIDTH), BF16),
                   jax.ShapeDtypeStruct((n, 3 * LANES), BF16)],
        compiler_params=_params("arbitrary"),
        name="qk_prep",
    )(proj, proj, proj, cos_t, sin_t, qg, kg, seg_mean)


def _attend(q, kv, sink_ref, bias):
    rows = q.shape[0]
    lane = lax.broadcasted_iota(jnp.int32, (rows, LANES), 1)
    low = lane < HEAD_DIM
    zero = jnp.zeros((rows, LANES), BF16)
    vv = kv[:, 2 * LANES:3 * LANES]
    groups = []
    heads_per_kv = ATT_HEADS // ATT_KV_HEADS
    for kh in range(ATT_KV_HEADS):
        kk = kv[:, kh * LANES:(kh + 1) * LANES]
        parts = []
        sinks = []
        for j in range(heads_per_kv):
            h = kh * heads_per_kv + j
            qg = q[:, (h // 2) * LANES:(h // 2 + 1) * LANES]
            parts.append(jnp.where(low, qg, zero) if h % 2 == 0 else jnp.where(low, zero, qg))
            sinks.append(jnp.full((rows, 1), sink_ref[h], F32))
        lhs = jnp.concatenate(parts, axis=0)
        sink = jnp.concatenate(sinks, axis=0)
        s = lax.dot_general(lhs, kk, (((1,), (1,)), ((), ())), preferred_element_type=F32)
        if bias is not None:
            s = s + bias
        m = jnp.maximum(jnp.max(s, axis=-1, keepdims=True), sink)
        e = jnp.exp(s - m)
        denom = jnp.sum(e, axis=-1, keepdims=True) + jnp.exp(sink - m)
        o = jnp.dot(e.astype(BF16), vv, preferred_element_type=F32) / denom
        r = [o[j * rows:(j + 1) * rows] for j in range(heads_per_kv)]
        for p in range(heads_per_kv // 2):
            even, odd = r[2 * p], r[2 * p + 1]
            if kh == 0:
                groups.append(jnp.where(low, even, pltpu.roll(odd, HEAD_DIM, 1)))
            else:
                groups.append(jnp.where(low, pltpu.roll(even, HEAD_DIM, 1), odd))
    return jnp.concatenate(groups, axis=1)


def _attn_kernel(sink_ref, q_ref, kvp_ref, kvo_ref, kvn_ref, kvc_ref, bias0_ref, bias1_ref, o_ref):
    blk = ATT_BLOCK
    own = kvo_ref[...]
    ctx = kvc_ref[...]
    kv0 = jnp.concatenate([kvp_ref[...], own, ctx], axis=0)
    kv1 = jnp.concatenate([own, kvn_ref[...], ctx], axis=0)
    o_ref[0:blk, :] = _attend(q_ref[0:blk, :], kv0, sink_ref, bias0_ref[...]).astype(o_ref.dtype)
    o_ref[blk:2 * blk, :] = _attend(q_ref[blk:2 * blk, :], kv1, sink_ref, bias1_ref[...]).astype(o_ref.dtype)


def _window_bias(ctx_len):
    blk = ATT_BLOCK
    stacked = (ATT_HEADS // ATT_KV_HEADS) * blk
    qi = lax.broadcasted_iota(jnp.int32, (4, stacked, 3 * blk + ctx_len), 1) % blk
    ks = lax.broadcasted_iota(jnp.int32, (4, stacked, 3 * blk + ctx_len), 2)
    var = lax.broadcasted_iota(jnp.int32, (4, stacked, 3 * blk + ctx_len), 0)
    in_win = jnp.abs(ks - blk - qi) <= WINDOW
    in_seq = ((ks >= blk) | (var % 2 == 1)) & ((ks < 2 * blk) | (var >= 2))
    keep = (in_win & in_seq) | (ks >= 3 * blk)
    return jnp.where(keep, 0.0, -1e30).astype(F32)


def _ctx_attn_kernel(sink_ref, q_ref, kvc_ref, o_ref):
    o_ref[...] = _attend(q_ref[...], kvc_ref[...], sink_ref, None).astype(o_ref.dtype)


def _attn_call(qrot, kvrot, sink, n_batch, seq, ctx_len):
    nb = seq // ATT_BLOCK
    ctx0 = (n_batch * seq) // ctx_len
    kvw = kvrot.shape[1]
    bias = _window_bias(ctx_len)

    assert nb % 2 == 0
    pairs = nb // 2

    def variant0(b, m):
        return ((m > 0).astype(jnp.int32) + 2, 0, 0)

    def variant1(b, m):
        return (1 + 2 * (m < pairs - 1).astype(jnp.int32), 0, 0)

    blk = ATT_BLOCK
    return pl.pallas_call(
        _attn_kernel,
        grid=(n_batch, pairs),
        in_specs=[pl.BlockSpec(memory_space=pltpu.SMEM),
                  pl.BlockSpec((2 * blk, ATT_Q_WIDTH), lambda b, m: (b * pairs + m, 0)),
                  pl.BlockSpec((blk, kvw), lambda b, m: (b * nb + jnp.maximum(2 * m - 1, 0), 0)),
                  pl.BlockSpec((2 * blk, kvw), lambda b, m: (b * pairs + m, 0)),
                  pl.BlockSpec((blk, kvw), lambda b, m: (b * nb + jnp.minimum(2 * m + 2, nb - 1), 0)),
                  pl.BlockSpec((ctx_len, kvw), lambda b, m: (ctx0 + b, 0)),
                  pl.BlockSpec((None,) + bias.shape[1:], variant0),
                  pl.BlockSpec((None,) + bias.shape[1:], variant1)],
        out_specs=pl.BlockSpec((2 * blk, ATT_Q_WIDTH), lambda b, m: (b * pairs + m, 0)),
        out_shape=jax.ShapeDtypeStruct((n_batch * seq, ATT_Q_WIDTH), ACT_DTYPE),
        compiler_params=_params("arbitrary", "arbitrary"),
        name="window_attn",
    )(sink, qrot, kvrot, kvrot, kvrot, kvrot, bias, bias)


def _ctx_attn_call(qrot, kvrot, sink, n_batch, seq, ctx_len):
    ctx0 = (n_batch * seq) // ctx_len
    kvw = kvrot.shape[1]
    return pl.pallas_call(
        _ctx_attn_kernel,
        grid=(n_batch,),
        in_specs=[pl.BlockSpec(memory_space=pltpu.SMEM),
                  pl.BlockSpec((ctx_len, ATT_Q_WIDTH), lambda b: (ctx0 + b, 0)),
                  pl.BlockSpec((ctx_len, kvw), lambda b: (ctx0 + b, 0))],
        out_specs=pl.BlockSpec((ctx_len, ATT_Q_WIDTH), lambda b: (b, 0)),
        out_shape=jax.ShapeDtypeStruct((n_batch * ctx_len, ATT_Q_WIDTH), ACT_DTYPE),
        compiler_params=_params("arbitrary"),
        name="ctx_attn",
    )(sink, qrot, kvrot)


def _scan_rows(a, x, carry, reverse):
    rows, width = a.shape
    sub = lax.broadcasted_iota(jnp.int32, (rows, width), 0) % SUBLANES
    for s in (1, 2, 4):
        shift = rows - s if reverse else s
        a_sh = pltpu.roll(a, shift, 0)
        x_sh = pltpu.roll(x, shift, 0)
        keep = (sub <= SUBLANES - 1 - s) if reverse else (sub >= s)
        x = x + a * jnp.where(keep, x_sh, 0.0)
        a = a * jnp.where(keep, a_sh, 1.0)
    n_groups = rows // SUBLANES
    out = [None] * n_groups
    order = range(n_groups - 1, -1, -1) if reverse else range(n_groups)
    for g in order:
        hg = x[g * SUBLANES:(g + 1) * SUBLANES] + a[g * SUBLANES:(g + 1) * SUBLANES] * carry
        carry = hg[0:1] if reverse else hg[SUBLANES - 1:SUBLANES]
        out[g] = hg
    return jnp.concatenate(out, axis=0), carry


def _rglru_kernel(ul_ref, gl_ref, uc_ref, gc_ref, cw_ref, cb_ref, wr_ref, wi_ref, br_ref, bi_ref, lam_ref,
                  yl_ref, yc_ref, upad_ref, conv_ref, *, seq, ctx_len):
    width = ul_ref.shape[1]
    tc = LRU_CHUNK
    neg_lam = -lam_ref[...]
    softplus = jnp.maximum(neg_lam, 0.0) + jnp.log1p(jnp.exp(-jnp.abs(neg_lam)))
    zeros = jnp.zeros((SUBLANES, width), F32)

    def gates(u, d):
        ub = u.astype(BF16)
        r = jax.nn.sigmoid(jnp.dot(ub, wr_ref[d], preferred_element_type=F32) + br_ref[d:d + 1, :])
        i = jax.nn.sigmoid(jnp.dot(ub, wi_ref[d], preferred_element_type=F32) + bi_ref[d:d + 1, :])
        log_a = (-LRU_C) * r * softplus[d:d + 1, :]
        a = jnp.exp(log_a)
        y = 1.0 - a * a
        root = jnp.where(y > 0.0, y * lax.rsqrt(y), 0.0)
        return a, root * (i * u)

    def run(u_ref, g_ref, y_ref, n, h_fwd, h_bwd):
        n_chunks = n // tc
        upad_ref[0:SUBLANES, :] = zeros
        upad_ref[pl.ds(SUBLANES, n), :] = u_ref[...].astype(F32)
        upad_ref[pl.ds(SUBLANES + n, SUBLANES), :] = zeros

        def conv_body(c, carry):
            t0 = pl.multiple_of(c * tc, tc)
            win = upad_ref[pl.ds(t0, tc + 2 * SUBLANES), :]
            acc = cb_ref[...]
            for k in range(CONV_W):
                acc = acc + cw_ref[k:k + 1, :] * win[SUBLANES - 1 + k:SUBLANES - 1 + k + tc]
            conv_ref[pl.ds(t0, tc), :] = acc
            return carry

        lax.fori_loop(0, n_chunks, conv_body, 0)

        def fwd_body(c, carry):
            t0 = pl.multiple_of(c * tc, tc)
            a, x = gates(conv_ref[pl.ds(t0, tc), :], 0)
            h, carry = _scan_rows(a, x, carry, False)
            upad_ref[pl.ds(pl.multiple_of(t0 + SUBLANES, SUBLANES), tc), :] = h
            return carry

        h_fwd = lax.fori_loop(0, n_chunks, fwd_body, h_fwd)

        def bwd_body(c, carry):
            t0 = pl.multiple_of((n_chunks - 1 - c) * tc, tc)
            a, x = gates(conv_ref[pl.ds(t0, tc), :], 1)
            h, carry = _scan_rows(a, x, carry, True)
            h_both = upad_ref[pl.ds(pl.multiple_of(t0 + SUBLANES, SUBLANES), tc), :] + h
            y_ref[pl.ds(t0, tc), :] = (h_both * jax.nn.gelu(g_ref[pl.ds(t0, tc), :].astype(F32))).astype(y_ref.dtype)
            return carry

        h_bwd = lax.fori_loop(0, n_chunks, bwd_body, h_bwd)
        return h_fwd, h_bwd

    h0 = jnp.zeros((1, width), F32)
    h_fwd, h_bwd = run(uc_ref, gc_ref, yc_ref, ctx_len, h0, h0)
    run(ul_ref, gl_ref, yl_ref, seq, h_fwd, h_bwd)


def _rglru_call(proj, conv_w, conv_b, wr, wi, br, bi, lam, n_batch, seq, ctx_len):
    ctx0 = (n_batch * seq) // ctx_len
    halves = LRU_WIDTH // LRU_HALF
    w = LRU_HALF
    vec2 = pl.BlockSpec((2, w), lambda b, c: (0, c))
    return pl.pallas_call(
        functools.partial(_rglru_kernel, seq=seq, ctx_len=ctx_len),
        grid=(n_batch, halves),
        in_specs=[pl.BlockSpec((seq, w), lambda b, c: (b, c)),
                  pl.BlockSpec((seq, w), lambda b, c: (b, halves + c)),
                  pl.BlockSpec((ctx_len, w), lambda b, c: (ctx0 + b, c)),
                  pl.BlockSpec((ctx_len, w), lambda b, c: (ctx0 + b, halves + c)),
                  pl.BlockSpec((CONV_W, w), lambda b, c: (0, c)),
                  pl.BlockSpec((1, w), lambda b, c: (0, c)),
                  pl.BlockSpec((None, 2, w, w), lambda b, c: (c, 0, 0, 0)),
                  pl.BlockSpec((None, 2, w, w), lambda b, c: (c, 0, 0, 0)),
                  vec2, vec2, vec2],
        out_specs=[pl.BlockSpec((seq, w), lambda b, c: (b, c)),
                   pl.BlockSpec((ctx_len, w), lambda b, c: (b, c))],
        out_shape=[jax.ShapeDtypeStruct((n_batch * seq, LRU_WIDTH), ACT_DTYPE),
                   jax.ShapeDtypeStruct((n_batch * ctx_len, LRU_WIDTH), ACT_DTYPE)],
        scratch_shapes=[pltpu.VMEM((seq + 2 * SUBLANES, w), F32), pltpu.VMEM((seq, w), F32)],
        compiler_params=_params("arbitrary", "arbitrary"),
        name="rglru",
    )(proj, proj, proj, proj, conv_w, conv_b, wr, wi, br, bi, lam)


def _split3(x):
    hi = x.astype(BF16)
    r1 = x - hi.astype(F32)
    mid = r1.astype(BF16)
    lo = (r1 - mid.astype(F32)).astype(BF16)
    return hi, mid, lo


def _gla_kernel(q_ref, k_ref, v_ref, r_ref, a_ref, kc_ref, vc_ref, ac_ref, gw_ref, gb_ref, ng_ref,
                o_ref, of_ref, ob_ref, qd_ref, ki_ref, ke_ref, la_ref, eg_ref, st_ref, *, seq, ctx_len):
    ch = GLA_CHUNK
    blk = GLA_BLOCK
    per_blk = blk // ch
    row = lax.broadcasted_iota(jnp.int32, (blk, blk), 0)
    col = lax.broadcasted_iota(jnp.int32, (blk, blk), 1)
    same = (row // ch) == (col // ch)
    keep = (same & (row >= col), same & (row <= col))
    tri = (keep[0].astype(BF16),)
    scale = GLA_DK ** -0.5
    nt = (((1,), (1,)), ((), ()))
    tn = (((0,), (0,)), ((), ()))

    def mm(a, b):
        return jnp.dot(a, b, preferred_element_type=F32)

    def prepare(a_src, k_src, v_src, q_src, dst, n_rows, d):
        n_blk = n_rows // blk
        unroll = min(4, n_blk)

        def decay_body(i, carry):
            t0 = pl.multiple_of(i * blk, blk)
            both = mm(a_src[pl.ds(t0, blk), :], gw_ref[d])
            pre = (both[:, 0:GLA_DK] + both[:, GLA_DK:2 * GLA_DK]) + gb_ref[d:d + 1, :]
            la_ref[pl.ds(t0, blk), :] = (jnp.minimum(pre, 0.0) - jnp.log1p(jnp.exp(-jnp.abs(pre)))) / GLA_TAU
            return carry

        lax.fori_loop(0, n_blk, decay_body, 0, unroll=unroll)

        def scale_body(i, carry):
            t0 = pl.multiple_of(i * blk, blk)
            log_a = la_ref[pl.ds(t0, blk), :]
            sums = mm(tri[0], jnp.concatenate(_split3(log_a), axis=1))
            prefix = sums[:, 0:GLA_DK] + (sums[:, GLA_DK:2 * GLA_DK] + sums[:, 2 * GLA_DK:3 * GLA_DK])
            total = jnp.concatenate(
                [jnp.broadcast_to(prefix[(j + 1) * ch - 1:(j + 1) * ch], (ch, GLA_DK)) for j in range(per_blk)],
                axis=0)
            suffix = total - prefix + log_a
            b = prefix if d == 0 else suffix
            to_end = (suffix if d == 0 else prefix) - log_a
            for j in range(per_blk):
                eg_ref[pl.ds(i * per_blk + j, 1), :] = jnp.exp(total[j * ch:j * ch + 1])
            k_blk = k_src[pl.ds(t0, blk), :].astype(F32)
            ke_ref[pl.ds(t0, blk), :] = (k_blk * jnp.exp(to_end)).astype(BF16)
            if q_src is not None:
                q_blk = q_src[pl.ds(t0, blk), :].astype(F32)
                qd_ref[pl.ds(t0, blk), :] = ((q_blk * scale) * jnp.exp(b)).astype(BF16)
                ki_ref[pl.ds(t0, blk), :] = (k_blk * jnp.exp(-b)).astype(BF16)
            return carry

        lax.fori_loop(0, n_blk, scale_body, 0, unroll=unroll)
        if q_src is None:
            return

        def intra_body(i, carry):
            t0 = pl.multiple_of(i * blk, blk)
            s = lax.dot_general(qd_ref[pl.ds(t0, blk), :], ki_ref[pl.ds(t0, blk), :], nt,
                                preferred_element_type=F32)
            p = jnp.where(keep[d], s, 0.0).astype(BF16)
            dst[pl.ds(t0, blk), :] = mm(p, v_src[pl.ds(t0, blk), :].astype(BF16))
            return carry

        lax.fori_loop(0, n_blk, intra_body, 0, unroll=unroll)

    def recur(v_src, dst, n_rows, d):
        n_chunks = n_rows // ch

        def body(c, carry):
            n = c if d == 0 else n_chunks - 1 - c
            t0 = pl.multiple_of(n * ch, ch)
            st = st_ref[...]
            if dst is not None:
                dst[pl.ds(t0, ch), :] = dst[pl.ds(t0, ch), :] + lax.dot_general(
                    qd_ref[pl.ds(t0, ch), :], st.astype(BF16), nt, preferred_element_type=F32)
            kv = lax.dot_general(v_src[pl.ds(t0, ch), :].astype(BF16), ke_ref[pl.ds(t0, ch), :], tn,
                                 preferred_element_type=F32)
            st_ref[...] = st * eg_ref[pl.ds(n, 1), :] + kv
            return carry

        lax.fori_loop(0, n_chunks, body, 0, unroll=min(8, n_chunks))

    for d, dst in ((0, of_ref), (1, ob_ref)):
        st_ref[...] = jnp.zeros(st_ref.shape, F32)
        prepare(ac_ref, kc_ref, vc_ref, None, None, ctx_len, d)
        recur(vc_ref, None, ctx_len, d)
        prepare(a_ref, k_ref, v_ref, q_ref, dst, seq, d)
        recur(v_ref, dst, seq, d)

    def finish_body(c, carry):
        t0 = pl.multiple_of(c * blk, blk)
        o = of_ref[pl.ds(t0, blk), :] + ob_ref[pl.ds(t0, blk), :]
        on = (o * lax.rsqrt(jnp.mean(o * o, axis=-1, keepdims=True) + EPS)) * ng_ref[...]
        r = r_ref[pl.ds(t0, blk), :].astype(F32)
        o_ref[pl.ds(t0, blk), :] = (on * (r * jax.nn.sigmoid(r))).astype(o_ref.dtype)
        return carry

    lax.fori_loop(0, seq // blk, finish_body, 0)


def _gla_call(proj, gate_w, gate_b, norm_g, n_batch, seq, ctx_len):
    ctx0 = (n_batch * seq) // ctx_len
    kcol = GLA_K_WIDTH // GLA_DK
    vcol = (2 * GLA_K_WIDTH) // GLA_DV
    rcol = (2 * GLA_K_WIDTH + GLA_V_WIDTH) // GLA_DV
    acol = (2 * GLA_K_WIDTH + 2 * GLA_V_WIDTH) // LANES
    return pl.pallas_call(
        functools.partial(_gla_kernel, seq=seq, ctx_len=ctx_len),
        grid=(n_batch, GLA_HEADS),
        in_specs=[pl.BlockSpec((seq, GLA_DK), lambda b, h: (b, h)),
                  pl.BlockSpec((seq, GLA_DK), lambda b, h: (b, kcol + h)),
                  pl.BlockSpec((seq, GLA_DV), lambda b, h: (b, vcol + h)),
                  pl.BlockSpec((seq, GLA_DV), lambda b, h: (b, rcol + h)),
                  pl.BlockSpec((seq, LANES), lambda b, h: (b, acol)),
                  pl.BlockSpec((ctx_len, GLA_DK), lambda b, h: (ctx0 + b, kcol + h)),
                  pl.BlockSpec((ctx_len, GLA_DV), lambda b, h: (ctx0 + b, vcol + h)),
                  pl.BlockSpec((ctx_len, LANES), lambda b, h: (ctx0 + b, acol)),
                  pl.BlockSpec((None, 2, LANES, 2 * GLA_DK), lambda b, h: (h, 0, 0, 0)),
                  pl.BlockSpec((None, 2, GLA_DK), lambda b, h: (h, 0, 0)),
                  pl.BlockSpec((1, GLA_DV), lambda b, h: (0, 0))],
        out_specs=pl.BlockSpec((seq, GLA_DV), lambda b, h: (b, h)),
        out_shape=jax.ShapeDtypeStruct((n_batch * seq, GLA_V_WIDTH), ACT_DTYPE),
        scratch_shapes=[pltpu.VMEM((seq, GLA_DV), F32),
                        pltpu.VMEM((seq, GLA_DV), F32),
                        pltpu.VMEM((seq, GLA_DK), BF16),
                        pltpu.VMEM((seq, GLA_DK), BF16),
                        pltpu.VMEM((seq, GLA_DK), BF16),
                        pltpu.VMEM((seq, GLA_DK), F32),
                        pltpu.VMEM((seq // GLA_CHUNK, GLA_DK), F32),
                        pltpu.VMEM((GLA_DV, GLA_DK), F32)],
        compiler_params=_params("arbitrary", "arbitrary"),
        name="gla",
    )(proj, proj, proj, proj, proj, proj, proj, proj, gate_w, gate_b, norm_g)


def _route(rows, rows_bf16, rw_ref, rb_ref):
    n_rows = rows.shape[0]
    rows_lo = (rows - rows_bf16.astype(F32)).astype(BF16)
    both = jnp.dot(rows_bf16, rw_ref[...], preferred_element_type=F32)
    logits_rows = (both[:, 0:LANES] + both[:, LANES:2 * LANES]
                   + jnp.dot(rows_lo, rw_ref[:, 0:LANES], preferred_element_type=F32))
    logits = logits_rows.T[0:N_EXPERTS, :]
    scores = jax.nn.sigmoid(logits)
    sel = scores + rb_ref[...]
    eid = lax.broadcasted_iota(jnp.int32, (N_EXPERTS, n_rows), 0)
    gid = eid // EXPERTS_PER_GROUP
    neg = -jnp.inf
    big = N_EXPERTS

    def top2(masked):
        m1 = jnp.max(masked, axis=0, keepdims=True)
        i1 = jnp.min(jnp.where(masked == m1, eid, big), axis=0, keepdims=True)
        rest = jnp.where(eid == i1, neg, masked)
        m2 = jnp.max(rest, axis=0, keepdims=True)
        i2 = jnp.min(jnp.where(rest == m2, eid, big), axis=0, keepdims=True)
        return m1, i1, m2, i2

    best = None
    best_g = None
    for g in range(N_GROUPS):
        m1, _, m2, _ = top2(jnp.where(gid == g, sel, neg))
        gs = m1 + m2
        if best is None:
            best, best_g = gs, jnp.zeros((1, n_rows), jnp.int32)
        else:
            better = gs > best
            best_g = jnp.where(better, g, best_g)
            best = jnp.where(better, gs, best)
    _, i1, _, i2 = top2(jnp.where(gid == best_g, sel, neg))
    s1 = jnp.sum(jnp.where(eid == i1, scores, 0.0), axis=0, keepdims=True)
    s2 = jnp.sum(jnp.where(eid == i2, scores, 0.0), axis=0, keepdims=True)
    total = s1 + s2
    swap = i2 < i1
    lo = jnp.where(swap, i2, i1)
    hi = jnp.where(swap, i1, i2)
    w_lo = jnp.where(swap, s2, s1) / total
    w_hi = jnp.where(swap, s1, s2) / total
    pad = jnp.zeros((4, n_rows), F32)
    return jnp.concatenate([lo.astype(F32), hi.astype(F32), w_lo, w_hi, pad], axis=0)


def _finish_rows(y, h, mod_ref, g_ref, rw_ref, rb_ref, hn_ref, rows_ref, rt_ref, rc_ref):
    hn = h + mod_ref[2:3, :] * y
    hn_ref[...] = hn
    rows = _modulated_norm(hn, g_ref[...], mod_ref[3:4, :], mod_ref[4:5, :])
    rows_bf16 = rows.astype(BF16)
    rows_ref[...] = rows_bf16
    route = _route(rows, rows_bf16, rw_ref, rb_ref)
    rt_ref[...] = route
    padded = jnp.concatenate([route, jnp.zeros((LANES - SUBLANES, route.shape[1]), F32)], axis=0)
    rc_ref[...] = padded.T


def _outproj_even_kernel(al_ref, ac_ref, bl_ref, bc_ref, w_ref, hl_ref, hc_ref, mod_ref, g_ref, rw_ref, rb_ref,
                         hn_ref, rows_ref, rt_ref, rc_ref, *, n_lat_tiles):
    a = _two_source(al_ref, ac_ref, n_lat_tiles).astype(BF16)
    b = _two_source(bl_ref, bc_ref, n_lat_tiles).astype(BF16)
    y = (jnp.dot(a, w_ref[0:LRU_WIDTH, :], preferred_element_type=F32)
         + jnp.dot(b, w_ref[LRU_WIDTH:LRU_WIDTH + ATT_Q_WIDTH, :], preferred_element_type=F32))
    h = _two_source(hl_ref, hc_ref, n_lat_tiles)
    _finish_rows(y, h, mod_ref, g_ref, rw_ref, rb_ref, hn_ref, rows_ref, rt_ref, rc_ref)


def _outproj_odd_kernel(o_ref, w_ref, h_ref, mod_ref, g_ref, rw_ref, rb_ref, hn_ref, rows_ref, rt_ref, rc_ref):
    y = jnp.dot(o_ref[...].astype(BF16), w_ref[...], preferred_element_type=F32)
    _finish_rows(y, h_ref[...], mod_ref, g_ref, rw_ref, rb_ref, hn_ref, rows_ref, rt_ref, rc_ref)


def _outproj_specs(n_rows, tiles_per_seq, n_batch):
    d = D_MODEL
    row = pl.BlockSpec((ROW_TILE, d), lambda i: (i, 0))
    tail_in = [_mod_spec(tiles_per_seq, n_batch),
               pl.BlockSpec((1, d), lambda i: (0, 0)),
               pl.BlockSpec((d, 2 * LANES), lambda i: (0, 0)),
               pl.BlockSpec((N_EXPERTS, 1), lambda i: (0, 0))]
    out_specs = [row, row, pl.BlockSpec((SUBLANES, ROW_TILE), lambda i: (0, i)),
                 pl.BlockSpec((ROW_TILE, LANES), lambda i: (i, 0))]
    out_shape = [jax.ShapeDtypeStruct((n_rows, d), F32), jax.ShapeDtypeStruct((n_rows, d), ACT_DTYPE),
                 jax.ShapeDtypeStruct((SUBLANES, n_rows), F32), jax.ShapeDtypeStruct((n_rows, LANES), F32)]
    return tail_in, out_specs, out_shape


def _outproj_even_call(a_lat, a_ctx, b_lat, b_ctx, w, h_lat, h_ctx, mod, g, rw_t, rb, tiles_per_seq, n_batch):
    n_rows = h_lat.shape[0] + h_ctx.shape[0]
    n_lat_tiles = a_lat.shape[0] // ROW_TILE
    tail_in, out_specs, out_shape = _outproj_specs(n_rows, tiles_per_seq, n_batch)
    lat, ctx = _two_source_specs(LRU_WIDTH, n_lat_tiles)
    h_specs = list(_two_source_specs(D_MODEL, n_lat_tiles))
    return pl.pallas_call(
        functools.partial(_outproj_even_kernel, n_lat_tiles=n_lat_tiles),
        grid=(n_rows // ROW_TILE,),
        in_specs=[lat, ctx, lat, ctx, pl.BlockSpec(w.shape, lambda i: (0, 0))] + h_specs + tail_in,
        out_specs=out_specs,
        out_shape=out_shape,
        compiler_params=_params("arbitrary"),
        name="out_proj_even",
    )(a_lat, a_ctx, b_lat, b_ctx, w, h_lat, h_ctx, mod, g, rw_t, rb)


def _outproj_odd_call(o, w, h, mod, g, rw_t, rb, tiles_per_seq, n_batch):
    n_rows = o.shape[0]
    tail_in, out_specs, out_shape = _outproj_specs(n_rows, tiles_per_seq, n_batch)
    return pl.pallas_call(
        _outproj_odd_kernel,
        grid=(n_rows // ROW_TILE,),
        in_specs=[pl.BlockSpec((ROW_TILE, o.shape[1]), lambda i: (i, 0)),
                  pl.BlockSpec(w.shape, lambda i: (0, 0)),
                  pl.BlockSpec((ROW_TILE, D_MODEL), lambda i: (i, 0))] + tail_in,
        out_specs=out_specs,
        out_shape=out_shape,
        compiler_params=_params("arbitrary"),
        name="out_proj_odd",
    )(o, w, h, mod, g, rw_t, rb)


X_SLAB = D_MODEL // LANES
Y_SLAB = 2 * X_SLAB


def _for_rows(n_rows, fn):
    def body(g, carry):
        for u in range(SUBLANES):
            fn(g, u)
        return carry

    lax.fori_loop(0, n_rows // SUBLANES, body, 0)


def _dispatch_kernel(tail_ref, slot_ref, rows_ref, xs_ref, slab_ref, zero_ref, sem, zero_sem, *, steps, n_tiles):
    i = pl.program_id(0)
    k = i % 2
    rt = MOE_ROWS
    tile_rows = MOE_TILE * X_SLAB

    def slab_copy(buf, g, u, slot):
        return pltpu.make_async_copy(
            slab_ref.at[buf, g, :, pl.ds(u, 1), :],
            xs_ref.at[pl.ds(pl.multiple_of(slot * X_SLAB, X_SLAB), X_SLAB)], sem.at[buf])

    def wait_all(buf):
        _for_rows(rt, lambda g, u: slab_copy(buf, 0, 0, 0).wait())

    @pl.when(i == 0)
    def _():
        zero_ref[...] = jnp.zeros(zero_ref.shape, F32)

        def zero_tile(t):
            start = pl.multiple_of(t * tile_rows, tile_rows)
            fill = pltpu.make_async_copy(zero_ref, xs_ref.at[pl.ds(start, tile_rows)], zero_sem)
            fill.start()
            fill.wait()

        n_used = tail_ref[1, N_CLASSES - 1]
        for c in range(N_CLASSES):
            pl.when(tail_ref[0, c] > 0)(functools.partial(zero_tile, tail_ref[1, c] - 1))
            pl.when(n_used + c < n_tiles)(functools.partial(zero_tile, n_used + c))

    @pl.when(i >= 2)
    def _():
        wait_all(k)

    for j in range(X_SLAB):
        lanes = rows_ref[:, j * LANES:(j + 1) * LANES].astype(F32)
        slab_ref[k, :, j, :, :] = lanes.reshape(rt // SUBLANES, SUBLANES, LANES)

    _for_rows(rt, lambda g, u: slab_copy(k, g, u, slot_ref[0, 0, g * SUBLANES + u]).start())

    @pl.when(i == steps - 1)
    def _():
        wait_all(k)
        if steps >= 2:
            wait_all(1 - k)


def _dispatch_call(rows, slots, tails, n_slots):
    n, d = rows.shape
    steps = n // MOE_ROWS
    grid_spec = pltpu.PrefetchScalarGridSpec(
        num_scalar_prefetch=1,
        grid=(steps,),
        in_specs=[pl.BlockSpec((1, 1, MOE_ROWS), lambda i, tails: (i, 0, 0), memory_space=pltpu.SMEM),
                  pl.BlockSpec((MOE_ROWS, d), lambda i, tails: (i, 0))],
        out_specs=pl.BlockSpec(memory_space=pl.ANY),
        scratch_shapes=[pltpu.VMEM((2, MOE_ROWS // SUBLANES, X_SLAB, SUBLANES, LANES), F32),
                        pltpu.VMEM((MOE_TILE * X_SLAB, 1, LANES), F32),
                        pltpu.SemaphoreType.DMA((2,)), pltpu.SemaphoreType.DMA(())],
    )
    xs = pl.pallas_call(
        functools.partial(_dispatch_kernel, steps=steps, n_tiles=n_slots // MOE_TILE),
        grid_spec=grid_spec,
        out_shape=jax.ShapeDtypeStruct((n_slots * X_SLAB, 1, LANES), F32),
        compiler_params=_params("arbitrary"),
        name="moe_dispatch",
    )(tails, slots.reshape(steps, 1, MOE_ROWS), rows)
    return xs.reshape(n_slots * X_SLAB, LANES)


def _experts_kernel(elo_ref, ehi_ref, used_ref, x_ref, wg0_ref, wu0_ref, wd0_ref,
                    wg1_ref, wu1_ref, wd1_ref, y_ref, cg0_ref, cu0_ref, cd0_ref, cg1_ref, cu1_ref, cd1_ref):
    t = pl.program_id(0)
    tm = MOE_TILE
    prev = jnp.maximum(t - 1, 0)
    sides = ((elo_ref, (wg0_ref, wu0_ref, wd0_ref), (cg0_ref, cu0_ref, cd0_ref)),
             (ehi_ref, (wg1_ref, wu1_ref, wd1_ref), (cg1_ref, cu1_ref, cd1_ref)))

    @pl.when(t < used_ref[0])
    def _():
        for ids, weights, cached in sides:
            @pl.when((t == 0) | (ids[t] != ids[prev]))
            def _():
                for w_ref, c_ref in zip(weights, cached):
                    c_ref[...] = w_ref[...].astype(BF16)

        x = jnp.concatenate([x_ref[pl.ds(j, tm, stride=X_SLAB), :] for j in range(X_SLAB)],
                            axis=1).astype(BF16)
        for e, (_, _, (cg_ref, cu_ref, cd_ref)) in enumerate(sides):
            gate = jnp.dot(x, cg_ref[...], preferred_element_type=F32)
            up = jnp.dot(x, cu_ref[...], preferred_element_type=F32)
            act = ((gate * jax.nn.sigmoid(gate)) * up).astype(BF16)
            f = jnp.dot(act, cd_ref[...], preferred_element_type=F32)
            for j in range(X_SLAB):
                y_ref[pl.ds(e * X_SLAB + j, tm, stride=Y_SLAB), :] = f[:, j * LANES:(j + 1) * LANES]

    @pl.when(t >= used_ref[0])
    def _():
        y_ref[...] = jnp.zeros_like(y_ref)


def _experts_call(xs, e_lo, e_hi, n_used, w_gate, w_up, w_down, layer):
    d = D_MODEL
    tiles = xs.shape[0] // (MOE_TILE * X_SLAB)

    def xmap(t, elo, ehi, used):
        return (jnp.maximum(jnp.minimum(t, used[0] - 1), 0), 0)

    def lo(t, elo, ehi, used):
        return (layer, elo[t], 0, 0)

    def hi(t, elo, ehi, used):
        return (layer, ehi[t], 0, 0)

    up_spec = lambda m: pl.BlockSpec((None, None, d, D_EXPERT), m)
    down_spec = lambda m: pl.BlockSpec((None, None, D_EXPERT, d), m)
    cached = [pltpu.VMEM((d, D_EXPERT), BF16), pltpu.VMEM((d, D_EXPERT), BF16), pltpu.VMEM((D_EXPERT, d), BF16)]
    grid_spec = pltpu.PrefetchScalarGridSpec(
        num_scalar_prefetch=3,
        grid=(tiles,),
        in_specs=[pl.BlockSpec((MOE_TILE * X_SLAB, LANES), xmap),
                  up_spec(lo), up_spec(lo), down_spec(lo),
                  up_spec(hi), up_spec(hi), down_spec(hi)],
        out_specs=pl.BlockSpec((MOE_TILE * Y_SLAB, LANES), lambda t, elo, ehi, used: (t, 0)),
        scratch_shapes=cached + cached,
    )
    return pl.pallas_call(
        _experts_kernel,
        grid_spec=grid_spec,
        out_shape=jax.ShapeDtypeStruct((tiles * MOE_TILE * Y_SLAB, LANES), F32),
        compiler_params=_params("arbitrary"),
        name="experts",
    )(e_lo, e_hi, n_used, xs, w_gate, w_up, w_down, w_gate, w_up, w_down)


def _collect_kernel(slot_ref, next_slot_ref, ys_ref, rc_ref, *rest, steps, residual):
    if residual:
        h_ref, mod_ref, o_ref, buf_ref, sem = rest
    else:
        o_ref, buf_ref, sem = rest
    i = pl.program_id(0)
    k = i % 2
    rt = MOE_ROWS

    def slab_copy(buf, g, u, slot):
        return pltpu.make_async_copy(
            ys_ref.at[pl.ds(pl.multiple_of(slot * Y_SLAB, Y_SLAB), Y_SLAB)],
            buf_ref.at[buf, g, :, pl.ds(u, 1), :], sem.at[buf])

    def start_all(slots, buf):
        _for_rows(rt, lambda g, u: slab_copy(buf, g, u, slots[0, 0, g * SUBLANES + u]).start())

    @pl.when(i == 0)
    def _():
        start_all(slot_ref, 0)

    @pl.when(i + 1 < steps)
    def _():
        start_all(next_slot_ref, 1 - k)

    _for_rows(rt, lambda g, u: slab_copy(k, 0, 0, 0).wait())

    rc = rc_ref[...]
    w_lo = rc[:, 2:3]
    w_hi = rc[:, 3:4]
    for j in range(X_SLAB):
        cols = slice(j * LANES, (j + 1) * LANES)
        f = (w_lo * buf_ref[k, :, j, :, :].reshape(rt, LANES)
             + w_hi * buf_ref[k, :, X_SLAB + j, :, :].reshape(rt, LANES))
        if residual:
            o_ref[:, cols] = h_ref[:, cols] + mod_ref[5:6, cols] * f
        else:
            o_ref[:, cols] = f


def _collect_call(ys, slots, route_cols, residual=None):
    n = slots.shape[0]
    d = D_MODEL
    steps = n // MOE_ROWS
    slots3 = slots.reshape(steps, 1, MOE_ROWS)
    row = pl.BlockSpec((MOE_ROWS, d), lambda i: (i, 0))
    in_specs = [pl.BlockSpec((1, 1, MOE_ROWS), lambda i: (i, 0, 0), memory_space=pltpu.SMEM),
                pl.BlockSpec((1, 1, MOE_ROWS), lambda i: (jnp.minimum(i + 1, steps - 1), 0, 0),
                             memory_space=pltpu.SMEM),
                pl.BlockSpec(memory_space=pl.ANY),
                pl.BlockSpec((MOE_ROWS, LANES), lambda i: (i, 0))]
    args = [slots3, slots3, ys.reshape(ys.shape[0], 1, LANES), route_cols]
    if residual is not None:
        h, mod, tiles_per_seq, n_batch = residual
        in_specs += [row, _mod_spec(tiles_per_seq, n_batch)]
        args += [h, mod]
    return pl.pallas_call(
        functools.partial(_collect_kernel, steps=steps, residual=residual is not None),
        grid=(steps,),
        in_specs=in_specs,
        out_specs=row,
        out_shape=jax.ShapeDtypeStruct((n, d), F32),
        scratch_shapes=[pltpu.VMEM((2, MOE_ROWS // SUBLANES, Y_SLAB, SUBLANES, LANES), F32),
                        pltpu.SemaphoreType.DMA((2,))],
        compiler_params=_params("arbitrary"),
        name="moe_collect",
    )(*args)


def _moe(rows, route, route_cols, w_gate, w_up, w_down, layer, residual=None):
    n = rows.shape[0]
    tm = MOE_TILE
    assert n % MOE_ROWS == 0
    max_tiles = -(-n // tm) + N_CLASSES
    p = max_tiles * tm
    lo = route[0].astype(jnp.int32)
    hi = route[1].astype(jnp.int32)
    lo_in = lo % EXPERTS_PER_GROUP
    hi_in = hi % EXPERTS_PER_GROUP
    pair = lo_in * 3 - (lo_in * (lo_in - 1)) // 2 + (hi_in - lo_in - 1)
    cls = (lo // EXPERTS_PER_GROUP) * PAIRS_PER_GROUP + pair
    onehot = (cls[:, None] == jnp.arange(N_CLASSES, dtype=jnp.int32)[None, :]).astype(jnp.int32)
    counts = jnp.sum(onehot, axis=0)
    rank = jnp.sum(jnp.cumsum(onehot, axis=0) * onehot, axis=1) - 1
    tiles_c = (counts + tm - 1) // tm
    tile_end = jnp.cumsum(tiles_c)
    tile_start = tile_end - tiles_c
    n_used = tile_end[-1]
    slot_of_token = (tile_start[cls] * tm + rank).astype(jnp.int32)
    tile_ids = jnp.arange(max_tiles, dtype=jnp.int32)
    tile_cls = jnp.minimum(jnp.sum((tile_end[None, :] <= tile_ids[:, None]).astype(jnp.int32), axis=1),
                           N_CLASSES - 1)
    group0 = (tile_cls // PAIRS_PER_GROUP) * EXPERTS_PER_GROUP
    e_lo = group0 + jnp.asarray(PAIR_LO, jnp.int32)[tile_cls % PAIRS_PER_GROUP]
    e_hi = group0 + jnp.asarray(PAIR_HI, jnp.int32)[tile_cls % PAIRS_PER_GROUP]
    tails = jnp.stack([tiles_c, tile_end]).astype(jnp.int32)
    xs = _dispatch_call(rows, slot_of_token, tails, p)
    ys = _experts_call(xs, e_lo.astype(jnp.int32), e_hi.astype(jnp.int32),
                       n_used.reshape(1).astype(jnp.int32), w_gate, w_up, w_down, layer)
    return _collect_call(ys, slot_of_token, route_cols, residual)


def _rope_tables(seq):
    rows = seq // GRID_W
    row = jnp.repeat(jnp.arange(rows), GRID_W).astype(F32)
    col = jnp.tile(jnp.arange(GRID_W), rows).astype(F32)
    n_freq = HEAD_DIM // 4
    inv_freq = ROPE_BASE ** (-jnp.arange(n_freq, dtype=F32) / n_freq)
    ang_r = row[:, None] * inv_freq
    ang_c = col[:, None] * inv_freq
    cos = jnp.concatenate([jnp.cos(ang_r), jnp.cos(ang_r), jnp.cos(ang_c), jnp.cos(ang_c)], axis=1)
    sin = jnp.concatenate([-jnp.sin(ang_r), jnp.sin(ang_r), -jnp.sin(ang_c), jnp.sin(ang_c)], axis=1)
    cos = jnp.concatenate([jnp.tile(cos, (1, 2)), jnp.ones((ROW_TILE, LANES), F32)], axis=0)
    sin = jnp.concatenate([jnp.tile(sin, (1, 2)), jnp.zeros((ROW_TILE, LANES), F32)], axis=0)
    return cos, sin


def _block_diag_halves(w):
    per_half = LRU_HALF // LRU_BLOCK
    halves = LRU_WIDTH // LRU_HALF
    out = jnp.zeros((halves, 2, LRU_HALF, LRU_HALF), F32)
    for c in range(halves):
        for j in range(per_half):
            s = slice(j * LRU_BLOCK, (j + 1) * LRU_BLOCK)
            out = out.at[c, :, s, s].set(w[:, c * per_half + j])
    return out.astype(BF16)


def _gla_gate_weights(gate_up, gate_b):
    w = jnp.zeros((GLA_HEADS, 2, LANES, GLA_DK), F32)
    for d in range(2):
        wd = gate_up[d].reshape(GLA_LOWRANK, GLA_HEADS, GLA_DK).transpose(1, 0, 2)
        w = w.at[:, d, d * GLA_LOWRANK:(d + 1) * GLA_LOWRANK, :].set(wd)
    b = gate_b.reshape(2, GLA_HEADS, GLA_DK).transpose(1, 0, 2)
    w_hi = w.astype(BF16)
    w_lo = (w - w_hi.astype(F32)).astype(BF16)
    return jnp.concatenate([w_hi, w_lo], axis=-1), b


def kernel(x, c, ctx, c_ctx, router_w, router_bias, ada_w, ada_b, norm_mix_g, norm_ffn_g, moe_w_gate, moe_w_up, moe_w_down, ev_w_in, ev_w_out, ev_conv_w, ev_conv_b, ev_rg_w, ev_rg_b, ev_ig_w, ev_ig_b, ev_lambda, ev_q_norm_g, ev_k_norm_g, ev_sink, od_w_in, od_w_out, od_gate_up, od_gate_b, od_out_norm_g):
    n_batch, seq, d = x.shape
    ctx_len = ctx.shape[1]
    n_lat = n_batch * seq
    n_ctx = n_batch * ctx_len
    assert ada_w.shape[0] == 2 and d == D_MODEL
    assert seq % ROW_TILE == 0 and n_ctx % ROW_TILE == 0 and seq % ctx_len == 0
    assert ctx_len % LRU_CHUNK == 0 and seq % GLA_BLOCK == 0 and ctx_len % GLA_BLOCK == 0
    tiles_per_seq = seq // ROW_TILE
    n_lat_tiles = n_lat // ROW_TILE

    x_lat = x.reshape(n_lat, d)
    x_ctx = ctx.reshape(n_ctx, d)
    mod_rows = -(-(n_batch + 1) // SUBLANES) * SUBLANES
    cvec = jnp.zeros((mod_rows, d), F32).at[:n_batch].set(c).at[n_batch].set(c_ctx)
    mods = _ada_call(cvec, ada_w, ada_b).reshape(2, mod_rows, 6, d)
    mod_mix = [mods[l][:n_batch + 1, 0:6] for l in range(2)]
    rw_pad = jnp.pad(router_w, ((0, 0), (0, LANES - N_EXPERTS)))
    rw_hi = rw_pad.astype(BF16)
    rw_t = jnp.concatenate([rw_hi, (rw_pad - rw_hi.astype(F32)).astype(BF16)], axis=1)
    rb = router_bias.reshape(N_EXPERTS, 1)

    w_in0 = ev_w_in[0].astype(BF16)
    w_out0 = ev_w_out[0].astype(BF16)
    proj0 = _proj_call(x_lat, x_ctx, mod_mix[0], norm_mix_g[0:1], w_in0, tiles_per_seq, n_batch)
    cos_t, sin_t = _rope_tables(seq)
    seg = np.kron(np.eye(LANES // HEAD_DIM, dtype=np.float32), np.full((HEAD_DIM, HEAD_DIM), 1.0 / HEAD_DIM, np.float32))
    qrot, kvrot = _qkprep_call(proj0, cos_t, sin_t, jnp.tile(ev_q_norm_g[0], 2)[None, :],
                               jnp.tile(ev_k_norm_g[0], 2)[None, :], jnp.asarray(seg),
                               tiles_per_seq, n_lat_tiles)
    a_lat, a_ctx = _rglru_call(proj0, ev_conv_w[0], ev_conv_b[0:1], _block_diag_halves(ev_rg_w[0]),
                               _block_diag_halves(ev_ig_w[0]), ev_rg_b[0], ev_ig_b[0], ev_lambda[0],
                               n_batch, seq, ctx_len)
    b_lat = _attn_call(qrot, kvrot, ev_sink[0], n_batch, seq, ctx_len)
    b_ctx = _ctx_attn_call(qrot, kvrot, ev_sink[0], n_batch, seq, ctx_len)
    h, rows, route, route_cols = _outproj_even_call(a_lat, a_ctx, b_lat, b_ctx, w_out0, x_lat, x_ctx, mod_mix[0],
                                                    norm_ffn_g[0:1], rw_t, rb, tiles_per_seq, n_batch)
    ffn = _moe(rows, route, route_cols, moe_w_gate, moe_w_up, moe_w_down, 0)

    w_in1 = jnp.pad(od_w_in[0], ((0, 0), (0, ODD_IN_PAD - ODD_IN))).astype(BF16)
    h, proj1 = _proj_res_call(h, ffn, mod_mix[0], mod_mix[1], norm_mix_g[1:2], w_in1, tiles_per_seq, n_batch)
    gate_w, gate_b = _gla_gate_weights(od_gate_up[0], od_gate_b[0])
    o = _gla_call(proj1, gate_w, gate_b, od_out_norm_g[0:1], n_batch, seq, ctx_len)
    h_lat, rows, route, route_cols = _outproj_odd_call(o, od_w_out[0].astype(BF16), h, mod_mix[1],
                                                       norm_ffn_g[1:2], rw_t, rb, tiles_per_seq, n_batch)
    out = _moe(rows, route, route_cols, moe_w_gate, moe_w_up, moe_w_down, 1,
               residual=(h_lat, mod_mix[1], seq // MOE_ROWS, n_batch))
    return out.reshape(n_batch, seq, d)
```

```python
import functools

import jax
import jax.numpy as jnp
import numpy as np
from jax import lax
from jax.experimental import pallas as pl
from jax.experimental.pallas import tpu as pltpu

F32 = jnp.float32
BF16 = jnp.bfloat16
HIGHEST = lax.Precision.HIGHEST
ACT_DTYPE = BF16

D_MODEL = 1024
GRID_W = 64
EPS = 1e-6
LRU_WIDTH = 512
LRU_BLOCK = 64
CONV_W = 4
LRU_C = 8.0
ATT_HEADS = 8
ATT_KV_HEADS = 2
HEAD_DIM = 64
WINDOW = 128
ATT_BLOCK = 128
ROPE_BASE = 10000.0
ATT_Q_WIDTH = ATT_HEADS * HEAD_DIM
ATT_KV_WIDTH = ATT_KV_HEADS * HEAD_DIM
EVEN_IN = 2 * LRU_WIDTH + ATT_Q_WIDTH + 2 * ATT_KV_WIDTH
GLA_HEADS = 4
GLA_DK = 128
GLA_DV = 256
GLA_LOWRANK = 16
GLA_TAU = 16.0
GLA_CHUNK = 64
GLA_BLOCK = 256
GLA_K_WIDTH = GLA_HEADS * GLA_DK
GLA_V_WIDTH = GLA_HEADS * GLA_DV
ODD_IN = 2 * GLA_K_WIDTH + 2 * GLA_V_WIDTH + 2 * GLA_LOWRANK
N_EXPERTS = 16
N_GROUPS = 4
EXPERTS_PER_GROUP = 4
D_EXPERT = 512
PAIRS_PER_GROUP = 6
N_CLASSES = N_GROUPS * PAIRS_PER_GROUP
PAIR_LO = (0, 0, 0, 1, 1, 2)
PAIR_HI = (1, 2, 3, 2, 3, 3)

LANES = 128
SUBLANES = 8
VMEM_LIMIT_BYTES = 56 * 1024 * 1024

ROW_TILE = 512
MOE_TILE = 256
MOE_ROWS = 256
LRU_CHUNK = 128
LRU_HALF = 256
ODD_IN_PAD = ((ODD_IN + LANES - 1) // LANES) * LANES


def _params(*sem):
    return pltpu.CompilerParams(dimension_semantics=sem, vmem_limit_bytes=VMEM_LIMIT_BYTES)


def _modulated_norm(x, g, shift, scale):
    y = x * lax.rsqrt(jnp.mean(x * x, axis=-1, keepdims=True) + EPS)
    return (y * g) * (1.0 + scale) + shift


def _ada_kernel(c_ref, w_ref, b_ref, o_ref):
    c = c_ref[...]
    s = c * jax.nn.sigmoid(c)
    o_ref[...] = jnp.dot(s, w_ref[...], preferred_element_type=F32) + b_ref[...]


def _ada_call(cvec, ada_w, ada_b):
    depth, d, n6 = ada_w.shape
    r = cvec.shape[0]
    tn = n6 // 4
    return pl.pallas_call(
        _ada_kernel,
        grid=(depth, n6 // tn),
        in_specs=[pl.BlockSpec((r, d), lambda l, j: (0, 0)),
                  pl.BlockSpec((None, d, tn), lambda l, j: (l, 0, j)),
                  pl.BlockSpec((None, 1, tn), lambda l, j: (l, 0, j))],
        out_specs=pl.BlockSpec((None, r, tn), lambda l, j: (l, 0, j)),
        out_shape=jax.ShapeDtypeStruct((depth, r, n6), F32),
        compiler_params=_params("arbitrary", "arbitrary"),
        name="adaln",
    )(cvec, ada_w, ada_b.reshape(depth, 1, n6))


def _two_source_specs(width, n_lat_tiles):
    lat = pl.BlockSpec((ROW_TILE, width), lambda i: (jnp.minimum(i, n_lat_tiles - 1), 0))
    ctx = pl.BlockSpec((ROW_TILE, width), lambda i: (jnp.maximum(i - n_lat_tiles, 0), 0))
    return lat, ctx


def _two_source(lat_ref, ctx_ref, n_lat_tiles):
    return jnp.where(pl.program_id(0) >= n_lat_tiles, ctx_ref[...], lat_ref[...])


def _proj_kernel(xl_ref, xc_ref, mod_ref, g_ref, w_ref, o_ref, *, n_lat_tiles):
    x = _two_source(xl_ref, xc_ref, n_lat_tiles)
    n = _modulated_norm(x, g_ref[...], mod_ref[0:1, :], mod_ref[1:2, :])
    o_ref[...] = jnp.dot(n.astype(BF16), w_ref[...], preferred_element_type=F32).astype(o_ref.dtype)


def _proj_res_kernel(h_ref, f_ref, pmod_ref, mod_ref, g_ref, w_ref, x_ref, o_ref):
    x = h_ref[...] + pmod_ref[5:6, :] * f_ref[...]
    x_ref[...] = x
    n = _modulated_norm(x, g_ref[...], mod_ref[0:1, :], mod_ref[1:2, :])
    o_ref[...] = jnp.dot(n.astype(BF16), w_ref[...], preferred_element_type=F32).astype(o_ref.dtype)


def _mod_spec(tiles_per_seq, n_batch):
    return pl.BlockSpec((None, 6, D_MODEL), lambda i: (jnp.minimum(i // tiles_per_seq, n_batch), 0, 0))


def _proj_call(x_lat, x_ctx, mod, g, w, tiles_per_seq, n_batch):
    d = x_lat.shape[1]
    n = x_lat.shape[0] + x_ctx.shape[0]
    n_out = w.shape[1]
    n_lat_tiles = x_lat.shape[0] // ROW_TILE
    lat, ctx = _two_source_specs(d, n_lat_tiles)
    return pl.pallas_call(
        functools.partial(_proj_kernel, n_lat_tiles=n_lat_tiles),
        grid=(n // ROW_TILE,),
        in_specs=[lat, ctx,
                  _mod_spec(tiles_per_seq, n_batch),
                  pl.BlockSpec((1, d), lambda i: (0, 0)),
                  pl.BlockSpec((d, n_out), lambda i: (0, 0))],
        out_specs=pl.BlockSpec((ROW_TILE, n_out), lambda i: (i, 0)),
        out_shape=jax.ShapeDtypeStruct((n, n_out), ACT_DTYPE),
        compiler_params=_params("arbitrary"),
        name="in_proj",
    )(x_lat, x_ctx, mod, g, w)


def _proj_res_call(h, ffn, pmod, mod, g, w, tiles_per_seq, n_batch):
    n, d = h.shape
    n_out = w.shape[1]
    row = pl.BlockSpec((ROW_TILE, d), lambda i: (i, 0))
    return pl.pallas_call(
        _proj_res_kernel,
        grid=(n // ROW_TILE,),
        in_specs=[row, row, _mod_spec(tiles_per_seq, n_batch), _mod_spec(tiles_per_seq, n_batch),
                  pl.BlockSpec((1, d), lambda i: (0, 0)),
                  pl.BlockSpec((d, n_out), lambda i: (0, 0))],
        out_specs=[row, pl.BlockSpec((ROW_TILE, n_out), lambda i: (i, 0))],
        out_shape=[jax.ShapeDtypeStruct((n, d), F32), jax.ShapeDtypeStruct((n, n_out), ACT_DTYPE)],
        compiler_params=_params("arbitrary"),
        name="in_proj_res",
    )(h, ffn, pmod, mod, g, w)


def _qkprep_kernel(q_ref, k_ref, v_ref, c_ref, s_ref, qg_ref, kg_ref, m_ref, qo_ref, kvo_ref):
    cos = c_ref[...]
    sin = s_ref[...]
    lane = lax.broadcasted_iota(jnp.int32, cos.shape, 1)
    first = (lane % 32) < 16
    low = lane < HEAD_DIM

    def norm_rope(x, g):
        ms = jnp.dot(x * x, m_ref[...], preferred_element_type=F32, precision=HIGHEST)
        xn = (x * lax.rsqrt(ms + EPS)) * g
        partner = jnp.where(first, pltpu.roll(xn, LANES - 16, 1), pltpu.roll(xn, 16, 1))
        return xn * cos + partner * sin

    for j in range(ATT_Q_WIDTH // LANES):
        qj = norm_rope(q_ref[:, j * LANES:(j + 1) * LANES].astype(F32), qg_ref[...])
        qo_ref[:, j * LANES:(j + 1) * LANES] = (qj * HEAD_DIM ** -0.5).astype(BF16)
    kr = norm_rope(k_ref[...].astype(F32), kg_ref[...])
    ks = pltpu.roll(kr, HEAD_DIM, 1)
    kvo_ref[:, 0:LANES] = jnp.where(low, kr, ks).astype(BF16)
    kvo_ref[:, LANES:2 * LANES] = jnp.where(low, ks, kr).astype(BF16)
    kvo_ref[:, 2 * LANES:3 * LANES] = v_ref[...].astype(BF16)


def _qkprep_call(proj, cos_t, sin_t, qg, kg, seg_mean, tiles_per_seq, n_lat_tiles):
    n = proj.shape[0]
    qcol = (2 * LRU_WIDTH) // ATT_Q_WIDTH
    kcol = (2 * LRU_WIDTH + ATT_Q_WIDTH) // LANES

    def pos(i):
        return (jnp.where(i < n_lat_tiles, i % tiles_per_seq, tiles_per_seq), 0)

    return pl.pallas_call(
        _qkprep_kernel,
        grid=(n // ROW_TILE,),
        in_specs=[pl.BlockSpec((ROW_TILE, ATT_Q_WIDTH), lambda i: (i, qcol)),
                  pl.BlockSpec((ROW_TILE, LANES), lambda i: (i, kcol)),
                  pl.BlockSpec((ROW_TILE, LANES), lambda i: (i, kcol + 1)),
                  pl.BlockSpec((ROW_TILE, LANES), pos),
                  pl.BlockSpec((ROW_TILE, LANES), pos),
                  pl.BlockSpec((1, LANES), lambda i: (0, 0)),
                  pl.BlockSpec((1, LANES), lambda i: (0, 0)),
                  pl.BlockSpec((LANES, LANES), lambda i: (0, 0))],
        out_specs=[pl.BlockSpec((ROW_TILE, ATT_Q_WIDTH), lambda i: (i, 0)),
                   pl.BlockSpec((ROW_TILE, 3 * LANES), lambda i: (i, 0))],
        out_shape=[jax.ShapeDtypeStruct((n, ATT_Q_WIDTH), BF16),
                   jax.ShapeDtypeStruct((n, 3 * LANES), BF16)],
        compiler_params=_params("arbitrary"),
        name="qk_prep",
    )(proj, proj, proj, cos_t, sin_t, qg, kg, seg_mean)


def _attend(q, kv, sink_ref, bias):
    rows = q.shape[0]
    lane = lax.broadcasted_iota(jnp.int32, (rows, LANES), 1)
    low = lane < HEAD_DIM
    zero = jnp.zeros((rows, LANES), BF16)
    vv = kv[:, 2 * LANES:3 * LANES]
    groups = []
    heads_per_kv = ATT_HEADS // ATT_KV_HEADS
    for kh in range(ATT_KV_HEADS):
        kk = kv[:, kh * LANES:(kh + 1) * LANES]
        parts = []
        sinks = []
        for j in range(heads_per_kv):
            h = kh * heads_per_kv + j
            qg = q[:, (h // 2) * LANES:(h // 2 + 1) * LANES]
            parts.append(jnp.where(low, qg, zero) if h % 2 == 0 else jnp.where(low, zero, qg))
            sinks.append(jnp.full((rows, 1), sink_ref[h], F32))
        lhs = jnp.concatenate(parts, axis=0)
        sink = jnp.concatenate(sinks, axis=0)
        s = lax.dot_general(lhs, kk, (((1,), (1,)), ((), ())), preferred_element_type=F32)
        if bias is not None:
            s = s + bias
        m = jnp.maximum(jnp.max(s, axis=-1, keepdims=True), sink)
        e = jnp.exp(s - m)
        denom = jnp.sum(e, axis=-1, keepdims=True) + jnp.exp(sink - m)
        o = jnp.dot(e.astype(BF16), vv, preferred_element_type=F32) / denom
        r = [o[j * rows:(j + 1) * rows] for j in range(heads_per_kv)]
        for p in range(heads_per_kv // 2):
            even, odd = r[2 * p], r[2 * p + 1]
            if kh == 0:
                groups.append(jnp.where(low, even, pltpu.roll(odd, HEAD_DIM, 1)))
            else:
                groups.append(jnp.where(low, pltpu.roll(even, HEAD_DIM, 1), odd))
    return jnp.concatenate(groups, axis=1)


def _attn_kernel(sink_ref, q_ref, kvp_ref, kvo_ref, kvn_ref, kvc_ref, bias0_ref, bias1_ref, o_ref):
    blk = ATT_BLOCK
    own = kvo_ref[...]
    ctx = kvc_ref[...]
    kv0 = jnp.concatenate([kvp_ref[...], own, ctx], axis=0)
    kv1 = jnp.concatenate([own, kvn_ref[...], ctx], axis=0)
    o_ref[0:blk, :] = _attend(q_ref[0:blk, :], kv0, sink_ref, bias0_ref[...]).astype(o_ref.dtype)
    o_ref[blk:2 * blk, :] = _attend(q_ref[blk:2 * blk, :], kv1, sink_ref, bias1_ref[...]).astype(o_ref.dtype)


def _window_bias(ctx_len):
    blk = ATT_BLOCK
    stacked = (ATT_HEADS // ATT_KV_HEADS) * blk
    qi = lax.broadcasted_iota(jnp.int32, (4, stacked, 3 * blk + ctx_len), 1) % blk
    ks = lax.broadcasted_iota(jnp.int32, (4, stacked, 3 * blk + ctx_len), 2)
    var = lax.broadcasted_iota(jnp.int32, (4, stacked, 3 * blk + ctx_len), 0)
    in_win = jnp.abs(ks - blk - qi) <= WINDOW
    in_seq = ((ks >= blk) | (var % 2 == 1)) & ((ks < 2 * blk) | (var >= 2))
    keep = (in_win & in_seq) | (ks >= 3 * blk)
    return jnp.where(keep, 0.0, -1e30).astype(F32)


def _ctx_attn_kernel(sink_ref, q_ref, kvc_ref, o_ref):
    o_ref[...] = _attend(q_ref[...], kvc_ref[...], sink_ref, None).astype(o_ref.dtype)


def _attn_call(qrot, kvrot, sink, n_batch, seq, ctx_len):
    nb = seq // ATT_BLOCK
    ctx0 = (n_batch * seq) // ctx_len
    kvw = kvrot.shape[1]
    bias = _window_bias(ctx_len)

    assert nb % 2 == 0
    pairs = nb // 2

    def variant0(b, m):
        return ((m > 0).astype(jnp.int32) + 2, 0, 0)

    def variant1(b, m):
        return (1 + 2 * (m < pairs - 1).astype(jnp.int32), 0, 0)

    blk = ATT_BLOCK
    return pl.pallas_call(
        _attn_kernel,
        grid=(n_batch, pairs),
        in_specs=[pl.BlockSpec(memory_space=pltpu.SMEM),
                  pl.BlockSpec((2 * blk, ATT_Q_WIDTH), lambda b, m: (b * pairs + m, 0)),
                  pl.BlockSpec((blk, kvw), lambda b, m: (b * nb + jnp.maximum(2 * m - 1, 0), 0)),
                  pl.BlockSpec((2 * blk, kvw), lambda b, m: (b * pairs + m, 0)),
                  pl.BlockSpec((blk, kvw), lambda b, m: (b * nb + jnp.minimum(2 * m + 2, nb - 1), 0)),
                  pl.BlockSpec((ctx_len, kvw), lambda b, m: (ctx0 + b, 0)),
                  pl.BlockSpec((None,) + bias.shape[1:], variant0),
                  pl.BlockSpec((None,) + bias.shape[1:], variant1)],
        out_specs=pl.BlockSpec((2 * blk, ATT_Q_WIDTH), lambda b, m: (b * pairs + m, 0)),
        out_shape=jax.ShapeDtypeStruct((n_batch * seq, ATT_Q_WIDTH), ACT_DTYPE),
        compiler_params=_params("arbitrary", "arbitrary"),
        name="window_attn",
    )(sink, qrot, kvrot, kvrot, kvrot, kvrot, bias, bias)


def _ctx_attn_call(qrot, kvrot, sink, n_batch, seq, ctx_len):
    ctx0 = (n_batch * seq) // ctx_len
    kvw = kvrot.shape[1]
    return pl.pallas_call(
        _ctx_attn_kernel,
        grid=(n_batch,),
        in_specs=[pl.BlockSpec(memory_space=pltpu.SMEM),
                  pl.BlockSpec((ctx_len, ATT_Q_WIDTH), lambda b: (ctx0 + b, 0)),
                  pl.BlockSpec((ctx_len, kvw), lambda b: (ctx0 + b, 0))],
        out_specs=pl.BlockSpec((ctx_len, ATT_Q_WIDTH), lambda b: (b, 0)),
        out_shape=jax.ShapeDtypeStruct((n_batch * ctx_len, ATT_Q_WIDTH), ACT_DTYPE),
        compiler_params=_params("arbitrary"),
        name="ctx_attn",
    )(sink, qrot, kvrot)


def _scan_rows(a, x, carry, reverse):
    rows, width = a.shape
    sub = lax.broadcasted_iota(jnp.int32, (rows, width), 0) % SUBLANES
    for s in (1, 2, 4):
        shift = rows - s if reverse else s
        a_sh = pltpu.roll(a, shift, 0)
        x_sh = pltpu.roll(x, shift, 0)
        keep = (sub <= SUBLANES - 1 - s) if reverse else (sub >= s)
        x = x + a * jnp.where(keep, x_sh, 0.0)
        a = a * jnp.where(keep, a_sh, 1.0)
    n_groups = rows // SUBLANES
    out = [None] * n_groups
    order = range(n_groups - 1, -1, -1) if reverse else range(n_groups)
    for g in order:
        hg = x[g * SUBLANES:(g + 1) * SUBLANES] + a[g * SUBLANES:(g + 1) * SUBLANES] * carry
        carry = hg[0:1] if reverse else hg[SUBLANES - 1:SUBLANES]
        out[g] = hg
    return jnp.concatenate(out, axis=0), carry


def _rglru_kernel(ul_ref, gl_ref, uc_ref, gc_ref, cw_ref, cb_ref, wr_ref, wi_ref, br_ref, bi_ref, lam_ref,
                  yl_ref, yc_ref, upad_ref, conv_ref, *, seq, ctx_len):
    width = ul_ref.shape[1]
    tc = LRU_CHUNK
    neg_lam = -lam_ref[...]
    softplus = jnp.maximum(neg_lam, 0.0) + jnp.log1p(jnp.exp(-jnp.abs(neg_lam)))
    zeros = jnp.zeros((SUBLANES, width), F32)

    def gates(u, d):
        ub = u.astype(BF16)
        r = jax.nn.sigmoid(jnp.dot(ub, wr_ref[d], preferred_element_type=F32) + br_ref[d:d + 1, :])
        i = jax.nn.sigmoid(jnp.dot(ub, wi_ref[d], preferred_element_type=F32) + bi_ref[d:d + 1, :])
        log_a = (-LRU_C) * r * softplus[d:d + 1, :]
        a = jnp.exp(log_a)
        y = 1.0 - a * a
        root = jnp.where(y > 0.0, y * lax.rsqrt(y), 0.0)
        return a, root * (i * u)

    def run(u_ref, g_ref, y_ref, n, h_fwd, h_bwd):
        n_chunks = n // tc
        upad_ref[0:SUBLANES, :] = zeros
        upad_ref[pl.ds(SUBLANES, n), :] = u_ref[...].astype(F32)
        upad_ref[pl.ds(SUBLANES + n, SUBLANES), :] = zeros

        def conv_body(c, carry):
            t0 = pl.multiple_of(c * tc, tc)
            win = upad_ref[pl.ds(t0, tc + 2 * SUBLANES), :]
            acc = cb_ref[...]
            for k in range(CONV_W):
                acc = acc + cw_ref[k:k + 1, :] * win[SUBLANES - 1 + k:SUBLANES - 1 + k + tc]
            conv_ref[pl.ds(t0, tc), :] = acc
            return carry

        lax.fori_loop(0, n_chunks, conv_body, 0)

        def fwd_body(c, carry):
            t0 = pl.multiple_of(c * tc, tc)
            a, x = gates(conv_ref[pl.ds(t0, tc), :], 0)
            h, carry = _scan_rows(a, x, carry, False)
            upad_ref[pl.ds(pl.multiple_of(t0 + SUBLANES, SUBLANES), tc), :] = h
            return carry

        h_fwd = lax.fori_loop(0, n_chunks, fwd_body, h_fwd, unroll=min(4, n_chunks))

        def bwd_body(c, carry):
            t0 = pl.multiple_of((n_chunks - 1 - c) * tc, tc)
            a, x = gates(conv_ref[pl.ds(t0, tc), :], 1)
            h, carry = _scan_rows(a, x, carry, True)
            h_both = upad_ref[pl.ds(pl.multiple_of(t0 + SUBLANES, SUBLANES), tc), :] + h
            y_ref[pl.ds(t0, tc), :] = (h_both * jax.nn.gelu(g_ref[pl.ds(t0, tc), :].astype(F32))).astype(y_ref.dtype)
            return carry

        h_bwd = lax.fori_loop(0, n_chunks, bwd_body, h_bwd, unroll=min(4, n_chunks))
        return h_fwd, h_bwd

    h0 = jnp.zeros((1, width), F32)
    h_fwd, h_bwd = run(uc_ref, gc_ref, yc_ref, ctx_len, h0, h0)
    run(ul_ref, gl_ref, yl_ref, seq, h_fwd, h_bwd)


def _rglru_call(proj, conv_w, conv_b, wr, wi, br, bi, lam, n_batch, seq, ctx_len):
    ctx0 = (n_batch * seq) // ctx_len
    halves = LRU_WIDTH // LRU_HALF
    w = LRU_HALF
    vec2 = pl.BlockSpec((2, w), lambda b, c: (0, c))
    return pl.pallas_call(
        functools.partial(_rglru_kernel, seq=seq, ctx_len=ctx_len),
        grid=(n_batch, halves),
        in_specs=[pl.BlockSpec((seq, w), lambda b, c: (b, c)),
                  pl.BlockSpec((seq, w), lambda b, c: (b, halves + c)),
                  pl.BlockSpec((ctx_len, w), lambda b, c: (ctx0 + b, c)),
                  pl.BlockSpec((ctx_len, w), lambda b, c: (ctx0 + b, halves + c)),
                  pl.BlockSpec((CONV_W, w), lambda b, c: (0, c)),
                  pl.BlockSpec((1, w), lambda b, c: (0, c)),
                  pl.BlockSpec((None, 2, w, w), lambda b, c: (c, 0, 0, 0)),
                  pl.BlockSpec((None, 2, w, w), lambda b, c: (c, 0, 0, 0)),
                  vec2, vec2, vec2],
        out_specs=[pl.BlockSpec((seq, w), lambda b, c: (b, c)),
                   pl.BlockSpec((ctx_len, w), lambda b, c: (b, c))],
        out_shape=[jax.ShapeDtypeStruct((n_batch * seq, LRU_WIDTH), ACT_DTYPE),
                   jax.ShapeDtypeStruct((n_batch * ctx_len, LRU_WIDTH), ACT_DTYPE)],
        scratch_shapes=[pltpu.VMEM((seq + 2 * SUBLANES, w), F32), pltpu.VMEM((seq, w), F32)],
        compiler_params=_params("arbitrary", "arbitrary"),
        name="rglru",
    )(proj, proj, proj, proj, conv_w, conv_b, wr, wi, br, bi, lam)


def _split2(x):
    hi = x.astype(BF16)
    return hi, (x - hi.astype(F32)).astype(BF16)


def _gla_kernel(q_ref, k_ref, v_ref, r_ref, a_ref, kc_ref, vc_ref, ac_ref, gw_ref, gb_ref, ng_ref,
                o_ref, of_ref, ob_ref, qd_ref, ki_ref, ke_ref, la_ref, eg_ref, st_ref, *, seq, ctx_len):
    ch = GLA_CHUNK
    blk = GLA_BLOCK
    per_blk = blk // ch
    row = lax.broadcasted_iota(jnp.int32, (blk, blk), 0)
    col = lax.broadcasted_iota(jnp.int32, (blk, blk), 1)
    same = (row // ch) == (col // ch)
    keep = (same & (row >= col), same & (row <= col))
    tri = (keep[0].astype(BF16),)
    scale = GLA_DK ** -0.5
    nt = (((1,), (1,)), ((), ()))
    tn = (((0,), (0,)), ((), ()))

    def mm(a, b):
        return jnp.dot(a, b, preferred_element_type=F32)

    def prepare(a_src, k_src, v_src, q_src, dst, n_rows, d):
        n_blk = n_rows // blk
        unroll = min(4, n_blk)

        def decay_body(i, carry):
            t0 = pl.multiple_of(i * blk, blk)
            both = mm(a_src[pl.ds(t0, blk), :], gw_ref[d])
            pre = (both[:, 0:GLA_DK] + both[:, GLA_DK:2 * GLA_DK]) + gb_ref[d:d + 1, :]
            la_ref[pl.ds(t0, blk), :] = (jnp.minimum(pre, 0.0) - jnp.log(1.0 + jnp.exp(-jnp.abs(pre)))) / GLA_TAU
            return carry

        lax.fori_loop(0, n_blk, decay_body, 0, unroll=unroll)

        def scale_body(i, carry):
            t0 = pl.multiple_of(i * blk, blk)
            log_a = la_ref[pl.ds(t0, blk), :]
            sums = mm(tri[0], jnp.concatenate(_split2(log_a), axis=1))
            prefix = sums[:, 0:GLA_DK] + sums[:, GLA_DK:2 * GLA_DK]
            total = jnp.concatenate(
                [jnp.broadcast_to(prefix[(j + 1) * ch - 1:(j + 1) * ch], (ch, GLA_DK)) for j in range(per_blk)],
                axis=0)
            suffix = total - prefix + log_a
            b = prefix if d == 0 else suffix
            to_end = (suffix if d == 0 else prefix) - log_a
            for j in range(per_blk):
                eg_ref[pl.ds(i * per_blk + j, 1), :] = jnp.exp(total[j * ch:j * ch + 1])
            k_blk = k_src[pl.ds(t0, blk), :].astype(F32)
            ke_ref[pl.ds(t0, blk), :] = (k_blk * jnp.exp(to_end)).astype(BF16)
            if q_src is not None:
                q_blk = q_src[pl.ds(t0, blk), :].astype(F32)
                qd_ref[pl.ds(t0, blk), :] = ((q_blk * scale) * jnp.exp(b)).astype(BF16)
                ki_ref[pl.ds(t0, blk), :] = (k_blk * jnp.exp(-b)).astype(BF16)
            return carry

        lax.fori_loop(0, n_blk, scale_body, 0, unroll=unroll)
        if q_src is None:
            return

        def intra_body(i, carry):
            t0 = pl.multiple_of(i * blk, blk)
            s = lax.dot_general(qd_ref[pl.ds(t0, blk), :], ki_ref[pl.ds(t0, blk), :], nt,
                                preferred_element_type=F32)
            p = jnp.where(keep[d], s, 0.0).astype(BF16)
            dst[pl.ds(t0, blk), :] = mm(p, v_src[pl.ds(t0, blk), :].astype(BF16))
            return carry

        lax.fori_loop(0, n_blk, intra_body, 0, unroll=min(8, n_blk))

    def recur(v_src, dst, n_rows, d):
        n_chunks = n_rows // ch

        def body(c, carry):
            n = c if d == 0 else n_chunks - 1 - c
            t0 = pl.multiple_of(n * ch, ch)
            st = st_ref[...]
            if dst is not None:
                dst[pl.ds(t0, ch), :] = dst[pl.ds(t0, ch), :] + lax.dot_general(
                    qd_ref[pl.ds(t0, ch), :], st.astype(BF16), nt, preferred_element_type=F32)
            kv = lax.dot_general(v_src[pl.ds(t0, ch), :].astype(BF16), ke_ref[pl.ds(t0, ch), :], tn,
                                 preferred_element_type=F32)
            st_ref[...] = st * eg_ref[pl.ds(n, 1), :] + kv
            return carry

        lax.fori_loop(0, n_chunks, body, 0, unroll=min(16, n_chunks))

    for d, dst in ((0, of_ref), (1, ob_ref)):
        st_ref[...] = jnp.zeros(st_ref.shape, F32)
        prepare(ac_ref, kc_ref, vc_ref, None, None, ctx_len, d)
        recur(vc_ref, None, ctx_len, d)
        prepare(a_ref, k_ref, v_ref, q_ref, dst, seq, d)
        recur(v_ref, dst, seq, d)

    def finish_body(c, carry):
        t0 = pl.multiple_of(c * blk, blk)
        o = of_ref[pl.ds(t0, blk), :] + ob_ref[pl.ds(t0, blk), :]
        on = (o * lax.rsqrt(jnp.mean(o * o, axis=-1, keepdims=True) + EPS)) * ng_ref[...]
        r = r_ref[pl.ds(t0, blk), :].astype(F32)
        o_ref[pl.ds(t0, blk), :] = (on * (r * jax.nn.sigmoid(r))).astype(o_ref.dtype)
        return carry

    lax.fori_loop(0, seq // blk, finish_body, 0)


def _gla_call(proj, gate_w, gate_b, norm_g, n_batch, seq, ctx_len):
    ctx0 = (n_batch * seq) // ctx_len
    kcol = GLA_K_WIDTH // GLA_DK
    vcol = (2 * GLA_K_WIDTH) // GLA_DV
    rcol = (2 * GLA_K_WIDTH + GLA_V_WIDTH) // GLA_DV
    acol = (2 * GLA_K_WIDTH + 2 * GLA_V_WIDTH) // LANES
    return pl.pallas_call(
        functools.partial(_gla_kernel, seq=seq, ctx_len=ctx_len),
        grid=(n_batch, GLA_HEADS),
        in_specs=[pl.BlockSpec((seq, GLA_DK), lambda b, h: (b, h)),
                  pl.BlockSpec((seq, GLA_DK), lambda b, h: (b, kcol + h)),
                  pl.BlockSpec((seq, GLA_DV), lambda b, h: (b, vcol + h)),
                  pl.BlockSpec((seq, GLA_DV), lambda b, h: (b, rcol + h)),
                  pl.BlockSpec((seq, LANES), lambda b, h: (b, acol)),
                  pl.BlockSpec((ctx_len, GLA_DK), lambda b, h: (ctx0 + b, kcol + h)),
                  pl.BlockSpec((ctx_len, GLA_DV), lambda b, h: (ctx0 + b, vcol + h)),
                  pl.BlockSpec((ctx_len, LANES), lambda b, h: (ctx0 + b, acol)),
                  pl.BlockSpec((None, 2, LANES, 2 * GLA_DK), lambda b, h: (h, 0, 0, 0)),
                  pl.BlockSpec((None, 2, GLA_DK), lambda b, h: (h, 0, 0)),
                  pl.BlockSpec((1, GLA_DV), lambda b, h: (0, 0))],
        out_specs=pl.BlockSpec((seq, GLA_DV), lambda b, h: (b, h)),
        out_shape=jax.ShapeDtypeStruct((n_batch * seq, GLA_V_WIDTH), ACT_DTYPE),
        scratch_shapes=[pltpu.VMEM((seq, GLA_DV), F32),
                        pltpu.VMEM((seq, GLA_DV), F32),
                        pltpu.VMEM((seq, GLA_DK), BF16),
                        pltpu.VMEM((seq, GLA_DK), BF16),
                        pltpu.VMEM((seq, GLA_DK), BF16),
                        pltpu.VMEM((seq, GLA_DK), F32),
                        pltpu.VMEM((seq // GLA_CHUNK, GLA_DK), F32),
                        pltpu.VMEM((GLA_DV, GLA_DK), F32)],
        compiler_params=_params("arbitrary", "arbitrary"),
        name="gla",
    )(proj, proj, proj, proj, proj, proj, proj, proj, gate_w, gate_b, norm_g)


def _route(rows, rows_bf16, rw_ref, rb_ref):
    n_rows = rows.shape[0]
    rows_lo = (rows - rows_bf16.astype(F32)).astype(BF16)
    both = jnp.dot(rows_bf16, rw_ref[...], preferred_element_type=F32)
    logits_rows = (both[:, 0:LANES] + both[:, LANES:2 * LANES]
                   + jnp.dot(rows_lo, rw_ref[:, 0:LANES], preferred_element_type=F32))
    logits = logits_rows.T[0:N_EXPERTS, :]
    scores = jax.nn.sigmoid(logits)
    sel = scores + rb_ref[...]
    eid = lax.broadcasted_iota(jnp.int32, (N_EXPERTS, n_rows), 0)
    gid = eid // EXPERTS_PER_GROUP
    neg = -jnp.inf
    big = N_EXPERTS

    def top2(masked):
        m1 = jnp.max(masked, axis=0, keepdims=True)
        i1 = jnp.min(jnp.where(masked == m1, eid, big), axis=0, keepdims=True)
        rest = jnp.where(eid == i1, neg, masked)
        m2 = jnp.max(rest, axis=0, keepdims=True)
        i2 = jnp.min(jnp.where(rest == m2, eid, big), axis=0, keepdims=True)
        return m1, i1, m2, i2

    best = None
    best_g = None
    for g in range(N_GROUPS):
        m1, _, m2, _ = top2(jnp.where(gid == g, sel, neg))
        gs = m1 + m2
        if best is None:
            best, best_g = gs, jnp.zeros((1, n_rows), jnp.int32)
        else:
            better = gs > best
            best_g = jnp.where(better, g, best_g)
            best = jnp.where(better, gs, best)
    _, i1, _, i2 = top2(jnp.where(gid == best_g, sel, neg))
    s1 = jnp.sum(jnp.where(eid == i1, scores, 0.0), axis=0, keepdims=True)
    s2 = jnp.sum(jnp.where(eid == i2, scores, 0.0), axis=0, keepdims=True)
    total = s1 + s2
    swap = i2 < i1
    lo = jnp.where(swap, i2, i1)
    hi = jnp.where(swap, i1, i2)
    w_lo = jnp.where(swap, s2, s1) / total
    w_hi = jnp.where(swap, s1, s2) / total
    pad = jnp.zeros((4, n_rows), F32)
    return jnp.concatenate([lo.astype(F32), hi.astype(F32), w_lo, w_hi, pad], axis=0)


def _finish_rows(y, h, mod_ref, g_ref, rw_ref, rb_ref, hn_ref, rows_ref, rt_ref, rc_ref):
    hn = h + mod_ref[2:3, :] * y
    hn_ref[...] = hn
    rows = _modulated_norm(hn, g_ref[...], mod_ref[3:4, :], mod_ref[4:5, :])
    rows_bf16 = rows.astype(BF16)
    rows_ref[...] = rows_bf16
    route = _route(rows, rows_bf16, rw_ref, rb_ref)
    rt_ref[...] = route
    padded = jnp.concatenate([route, jnp.zeros((LANES - SUBLANES, route.shape[1]), F32)], axis=0)
    rc_ref[...] = padded.T


def _outproj_even_kernel(al_ref, ac_ref, bl_ref, bc_ref, w_ref, hl_ref, hc_ref, mod_ref, g_ref, rw_ref, rb_ref,
                         hn_ref, rows_ref, rt_ref, rc_ref, *, n_lat_tiles):
    a = _two_source(al_ref, ac_ref, n_lat_tiles).astype(BF16)
    b = _two_source(bl_ref, bc_ref, n_lat_tiles).astype(BF16)
    y = (jnp.dot(a, w_ref[0:LRU_WIDTH, :], preferred_element_type=F32)
         + jnp.dot(b, w_ref[LRU_WIDTH:LRU_WIDTH + ATT_Q_WIDTH, :], preferred_element_type=F32))
    h = _two_source(hl_ref, hc_ref, n_lat_tiles)
    _finish_rows(y, h, mod_ref, g_ref, rw_ref, rb_ref, hn_ref, rows_ref, rt_ref, rc_ref)


def _outproj_odd_kernel(o_ref, w_ref, h_ref, mod_ref, g_ref, rw_ref, rb_ref, hn_ref, rows_ref, rt_ref, rc_ref):
    y = jnp.dot(o_ref[...].astype(BF16), w_ref[...], preferred_element_type=F32)
    _finish_rows(y, h_ref[...], mod_ref, g_ref, rw_ref, rb_ref, hn_ref, rows_ref, rt_ref, rc_ref)


def _outproj_specs(n_rows, tiles_per_seq, n_batch):
    d = D_MODEL
    row = pl.BlockSpec((ROW_TILE, d), lambda i: (i, 0))
    tail_in = [_mod_spec(tiles_per_seq, n_batch),
               pl.BlockSpec((1, d), lambda i: (0, 0)),
               pl.BlockSpec((d, 2 * LANES), lambda i: (0, 0)),
               pl.BlockSpec((N_EXPERTS, 1), lambda i: (0, 0))]
    out_specs = [row, row, pl.BlockSpec((SUBLANES, ROW_TILE), lambda i: (0, i)),
                 pl.BlockSpec((ROW_TILE, LANES), lambda i: (i, 0))]
    out_shape = [jax.ShapeDtypeStruct((n_rows, d), F32), jax.ShapeDtypeStruct((n_rows, d), ACT_DTYPE),
                 jax.ShapeDtypeStruct((SUBLANES, n_rows), F32), jax.ShapeDtypeStruct((n_rows, LANES), F32)]
    return tail_in, out_specs, out_shape


def _outproj_even_call(a_lat, a_ctx, b_lat, b_ctx, w, h_lat, h_ctx, mod, g, rw_t, rb, tiles_per_seq, n_batch):
    n_rows = h_lat.shape[0] + h_ctx.shape[0]
    n_lat_tiles = a_lat.shape[0] // ROW_TILE
    tail_in, out_specs, out_shape = _outproj_specs(n_rows, tiles_per_seq, n_batch)
    lat, ctx = _two_source_specs(LRU_WIDTH, n_lat_tiles)
    h_specs = list(_two_source_specs(D_MODEL, n_lat_tiles))
    return pl.pallas_call(
        functools.partial(_outproj_even_kernel, n_lat_tiles=n_lat_tiles),
        grid=(n_rows // ROW_TILE,),
        in_specs=[lat, ctx, lat, ctx, pl.BlockSpec(w.shape, lambda i: (0, 0))] + h_specs + tail_in,
        out_specs=out_specs,
        out_shape=out_shape,
        compiler_params=_params("arbitrary"),
        name="out_proj_even",
    )(a_lat, a_ctx, b_lat, b_ctx, w, h_lat, h_ctx, mod, g, rw_t, rb)


def _outproj_odd_call(o, w, h, mod, g, rw_t, rb, tiles_per_seq, n_batch):
    n_rows = o.shape[0]
    tail_in, out_specs, out_shape = _outproj_specs(n_rows, tiles_per_seq, n_batch)
    return pl.pallas_call(
        _outproj_odd_kernel,
        grid=(n_rows // ROW_TILE,),
        in_specs=[pl.BlockSpec((ROW_TILE, o.shape[1]), lambda i: (i, 0)),
                  pl.BlockSpec(w.shape, lambda i: (0, 0)),
                  pl.BlockSpec((ROW_TILE, D_MODEL), lambda i: (i, 0))] + tail_in,
        out_specs=out_specs,
        out_shape=out_shape,
        compiler_params=_params("arbitrary"),
        name="out_proj_odd",
    )(o, w, h, mod, g, rw_t, rb)


X_SLAB = D_MODEL // LANES
Y_SLAB = 2 * X_SLAB


def _for_rows(n_rows, fn):
    def body(g, carry):
        for u in range(SUBLANES):
            fn(g, u)
        return carry

    lax.fori_loop(0, n_rows // SUBLANES, body, 0)


def _dispatch_kernel(tail_ref, slot_ref, rows_ref, xs_ref, slab_ref, zero_ref, sem, zero_sem, *, steps, n_tiles):
    i = pl.program_id(0)
    k = i % 2
    rt = MOE_ROWS
    tile_rows = MOE_TILE * X_SLAB

    def slab_copy(buf, g, u, slot):
        return pltpu.make_async_copy(
            slab_ref.at[buf, g, :, pl.ds(u, 1), :],
            xs_ref.at[pl.ds(pl.multiple_of(slot * X_SLAB, X_SLAB), X_SLAB)], sem.at[buf])

    def wait_all(buf):
        _for_rows(rt, lambda g, u: slab_copy(buf, 0, 0, 0).wait())

    @pl.when(i == 0)
    def _():
        zero_ref[...] = jnp.zeros(zero_ref.shape, F32)

        def fill(t):
            start = pl.multiple_of(t * tile_rows, tile_rows)
            return pltpu.make_async_copy(zero_ref, xs_ref.at[pl.ds(start, tile_rows)], zero_sem)

        n_used = tail_ref[1, N_CLASSES - 1]
        for phase in ("start", "wait"):
            for c in range(N_CLASSES):
                for cond, t in ((tail_ref[0, c] > 0, tail_ref[1, c] - 1), (n_used + c < n_tiles, n_used + c)):
                    @pl.when(cond)
                    def _():
                        getattr(fill(t), phase)()

    @pl.when(i >= 2)
    def _():
        wait_all(k)

    for j in range(X_SLAB):
        lanes = rows_ref[:, j * LANES:(j + 1) * LANES].astype(F32)
        slab_ref[k, :, j, :, :] = lanes.reshape(rt // SUBLANES, SUBLANES, LANES)

    _for_rows(rt, lambda g, u: slab_copy(k, g, u, slot_ref[0, 0, g * SUBLANES + u]).start())

    @pl.when(i == steps - 1)
    def _():
        wait_all(k)
        if steps >= 2:
            wait_all(1 - k)


def _dispatch_call(rows, slots, tails, n_slots):
    n, d = rows.shape
    steps = n // MOE_ROWS
    grid_spec = pltpu.PrefetchScalarGridSpec(
        num_scalar_prefetch=1,
        grid=(steps,),
        in_specs=[pl.BlockSpec((1, 1, MOE_ROWS), lambda i, tails: (i, 0, 0), memory_space=pltpu.SMEM),
                  pl.BlockSpec((MOE_ROWS, d), lambda i, tails: (i, 0))],
        out_specs=pl.BlockSpec(memory_space=pl.ANY),
        scratch_shapes=[pltpu.VMEM((2, MOE_ROWS // SUBLANES, X_SLAB, SUBLANES, LANES), F32),
                        pltpu.VMEM((MOE_TILE * X_SLAB, 1, LANES), F32),
                        pltpu.SemaphoreType.DMA((2,)), pltpu.SemaphoreType.DMA(())],
    )
    xs = pl.pallas_call(
        functools.partial(_dispatch_kernel, steps=steps, n_tiles=n_slots // MOE_TILE),
        grid_spec=grid_spec,
        out_shape=jax.ShapeDtypeStruct((n_slots * X_SLAB, 1, LANES), F32),
        compiler_params=_params("arbitrary"),
        name="moe_dispatch",
    )(tails, slots.reshape(steps, 1, MOE_ROWS), rows)
    return xs.reshape(n_slots * X_SLAB, LANES)


def _experts_kernel(elo_ref, ehi_ref, used_ref, x_ref, wg0_ref, wu0_ref, wd0_ref,
                    wg1_ref, wu1_ref, wd1_ref, y_ref, cg0_ref, cu0_ref, cd0_ref, cg1_ref, cu1_ref, cd1_ref):
    t = pl.program_id(0)
    tm = MOE_TILE
    prev = jnp.maximum(t - 1, 0)
    sides = ((elo_ref, (wg0_ref, wu0_ref, wd0_ref), (cg0_ref, cu0_ref, cd0_ref)),
             (ehi_ref, (wg1_ref, wu1_ref, wd1_ref), (cg1_ref, cu1_ref, cd1_ref)))

    @pl.when(t < used_ref[0])
    def _():
        for ids, weights, cached in sides:
            @pl.when((t == 0) | (ids[t] != ids[prev]))
            def _():
                for w_ref, c_ref in zip(weights, cached):
                    c_ref[...] = w_ref[...].astype(BF16)

        x = jnp.concatenate([x_ref[pl.ds(j, tm, stride=X_SLAB), :] for j in range(X_SLAB)],
                            axis=1).astype(BF16)
        for e, (_, _, (cg_ref, cu_ref, cd_ref)) in enumerate(sides):
            gate = jnp.dot(x, cg_ref[...], preferred_element_type=F32)
            up = jnp.dot(x, cu_ref[...], preferred_element_type=F32)
            act = ((gate * jax.nn.sigmoid(gate)) * up).astype(BF16)
            f = jnp.dot(act, cd_ref[...], preferred_element_type=F32)
            for j in range(X_SLAB):
                y_ref[pl.ds(e * X_SLAB + j, tm, stride=Y_SLAB), :] = f[:, j * LANES:(j + 1) * LANES]

    @pl.when(t >= used_ref[0])
    def _():
        y_ref[...] = jnp.zeros_like(y_ref)


def _experts_call(xs, e_lo, e_hi, n_used, w_gate, w_up, w_down, layer):
    d = D_MODEL
    tiles = xs.shape[0] // (MOE_TILE * X_SLAB)

    def xmap(t, elo, ehi, used):
        return (jnp.maximum(jnp.minimum(t, used[0] - 1), 0), 0)

    def lo(t, elo, ehi, used):
        return (layer, elo[t], 0, 0)

    def hi(t, elo, ehi, used):
        return (layer, ehi[t], 0, 0)

    up_spec = lambda m: pl.BlockSpec((None, None, d, D_EXPERT), m)
    down_spec = lambda m: pl.BlockSpec((None, None, D_EXPERT, d), m)
    cached = [pltpu.VMEM((d, D_EXPERT), BF16), pltpu.VMEM((d, D_EXPERT), BF16), pltpu.VMEM((D_EXPERT, d), BF16)]
    grid_spec = pltpu.PrefetchScalarGridSpec(
        num_scalar_prefetch=3,
        grid=(tiles,),
        in_specs=[pl.BlockSpec((MOE_TILE * X_SLAB, LANES), xmap),
                  up_spec(lo), up_spec(lo), down_spec(lo),
                  up_spec(hi), up_spec(hi), down_spec(hi)],
        out_specs=pl.BlockSpec((MOE_TILE * Y_SLAB, LANES), lambda t, elo, ehi, used: (t, 0)),
        scratch_shapes=cached + cached,
    )
    return pl.pallas_call(
        _experts_kernel,
        grid_spec=grid_spec,
        out_shape=jax.ShapeDtypeStruct((tiles * MOE_TILE * Y_SLAB, LANES), F32),
        compiler_params=_params("arbitrary"),
        name="experts",
    )(e_lo, e_hi, n_used, xs, w_gate, w_up, w_down, w_gate, w_up, w_down)


def _collect_kernel(slot_ref, next_slot_ref, ys_ref, rc_ref, *rest, steps, residual):
    if residual:
        h_ref, mod_ref, o_ref, buf_ref, sem = rest
    else:
        o_ref, buf_ref, sem = rest
    i = pl.program_id(0)
    k = i % 2
    rt = MOE_ROWS

    def slab_copy(buf, g, u, slot):
        return pltpu.make_async_copy(
            ys_ref.at[pl.ds(pl.multiple_of(slot * Y_SLAB, Y_SLAB), Y_SLAB)],
            buf_ref.at[buf, g, :, pl.ds(u, 1), :], sem.at[buf])

    def start_all(slots, buf):
        _for_rows(rt, lambda g, u: slab_copy(buf, g, u, slots[0, 0, g * SUBLANES + u]).start())

    @pl.when(i == 0)
    def _():
        start_all(slot_ref, 0)

    @pl.when(i + 1 < steps)
    def _():
        start_all(next_slot_ref, 1 - k)

    _for_rows(rt, lambda g, u: slab_copy(k, 0, 0, 0).wait())

    rc = rc_ref[...]
    w_lo = rc[:, 2:3]
    w_hi = rc[:, 3:4]
    for j in range(X_SLAB):
        cols = slice(j * LANES, (j + 1) * LANES)
        f = (w_lo * buf_ref[k, :, j, :, :].reshape(rt, LANES)
             + w_hi * buf_ref[k, :, X_SLAB + j, :, :].reshape(rt, LANES))
        if residual:
            o_ref[:, cols] = h_ref[:, cols] + mod_ref[5:6, cols] * f
        else:
            o_ref[:, cols] = f


def _collect_call(ys, slots, route_cols, residual=None):
    n = slots.shape[0]
    d = D_MODEL
    steps = n // MOE_ROWS
    slots3 = slots.reshape(steps, 1, MOE_ROWS)
    row = pl.BlockSpec((MOE_ROWS, d), lambda i: (i, 0))
    in_specs = [pl.BlockSpec((1, 1, MOE_ROWS), lambda i: (i, 0, 0), memory_space=pltpu.SMEM),
                pl.BlockSpec((1, 1, MOE_ROWS), lambda i: (jnp.minimum(i + 1, steps - 1), 0, 0),
                             memory_space=pltpu.SMEM),
                pl.BlockSpec(memory_space=pl.ANY),
                pl.BlockSpec((MOE_ROWS, LANES), lambda i: (i, 0))]
    args = [slots3, slots3, ys.reshape(ys.shape[0], 1, LANES), route_cols]
    if residual is not None:
        h, mod, tiles_per_seq, n_batch = residual
        in_specs += [row, _mod_spec(tiles_per_seq, n_batch)]
        args += [h, mod]
    return pl.pallas_call(
        functools.partial(_collect_kernel, steps=steps, residual=residual is not None),
        grid=(steps,),
        in_specs=in_specs,
        out_specs=row,
        out_shape=jax.ShapeDtypeStruct((n, d), F32),
        scratch_shapes=[pltpu.VMEM((2, MOE_ROWS // SUBLANES, Y_SLAB, SUBLANES, LANES), F32),
                        pltpu.SemaphoreType.DMA((2,))],
        compiler_params=_params("arbitrary"),
        name="moe_collect",
    )(*args)


def _moe(rows, route, route_cols, w_gate, w_up, w_down, layer, residual=None):
    n = rows.shape[0]
    tm = MOE_TILE
    assert n % MOE_ROWS == 0
    max_tiles = -(-n // tm) + N_CLASSES
    p = max_tiles * tm
    lo = route[0].astype(jnp.int32)
    hi = route[1].astype(jnp.int32)
    lo_in = lo % EXPERTS_PER_GROUP
    hi_in = hi % EXPERTS_PER_GROUP
    pair = lo_in * 3 - (lo_in * (lo_in - 1)) // 2 + (hi_in - lo_in - 1)
    cls = (lo // EXPERTS_PER_GROUP) * PAIRS_PER_GROUP + pair
    onehot = (cls[:, None] == jnp.arange(N_CLASSES, dtype=jnp.int32)[None, :]).astype(jnp.int32)
    counts = jnp.sum(onehot, axis=0)
    rank = jnp.sum(jnp.cumsum(onehot, axis=0) * onehot, axis=1) - 1
    tiles_c = (counts + tm - 1) // tm
    tile_end = jnp.cumsum(tiles_c)
    tile_start = tile_end - tiles_c
    n_used = tile_end[-1]
    slot_of_token = (tile_start[cls] * tm + rank).astype(jnp.int32)
    tile_ids = jnp.arange(max_tiles, dtype=jnp.int32)
    tile_cls = jnp.minimum(jnp.sum((tile_end[None, :] <= tile_ids[:, None]).astype(jnp.int32), axis=1),
                           N_CLASSES - 1)
    group0 = (tile_cls // PAIRS_PER_GROUP) * EXPERTS_PER_GROUP
    e_lo = group0 + jnp.asarray(PAIR_LO, jnp.int32)[tile_cls % PAIRS_PER_GROUP]
    e_hi = group0 + jnp.asarray(PAIR_HI, jnp.int32)[tile_cls % PAIRS_PER_GROUP]
    tails = jnp.stack([tiles_c, tile_end]).astype(jnp.int32)
    xs = _dispatch_call(rows, slot_of_token, tails, p)
    ys = _experts_call(xs, e_lo.astype(jnp.int32), e_hi.astype(jnp.int32),
                       n_used.reshape(1).astype(jnp.int32), w_gate, w_up, w_down, layer)
    return _collect_call(ys, slot_of_token, route_cols, residual)


def _rope_tables(seq):
    rows = seq // GRID_W
    row = jnp.repeat(jnp.arange(rows), GRID_W).astype(F32)
    col = jnp.tile(jnp.arange(GRID_W), rows).astype(F32)
    n_freq = HEAD_DIM // 4
    inv_freq = ROPE_BASE ** (-jnp.arange(n_freq, dtype=F32) / n_freq)
    ang_r = row[:, None] * inv_freq
    ang_c = col[:, None] * inv_freq
    cos = jnp.concatenate([jnp.cos(ang_r), jnp.cos(ang_r), jnp.cos(ang_c), jnp.cos(ang_c)], axis=1)
    sin = jnp.concatenate([-jnp.sin(ang_r), jnp.sin(ang_r), -jnp.sin(ang_c), jnp.sin(ang_c)], axis=1)
    cos = jnp.concatenate([jnp.tile(cos, (1, 2)), jnp.ones((ROW_TILE, LANES), F32)], axis=0)
    sin = jnp.concatenate([jnp.tile(sin, (1, 2)), jnp.zeros((ROW_TILE, LANES), F32)], axis=0)
    return cos, sin


def _block_diag_halves(w):
    per_half = LRU_HALF // LRU_BLOCK
    halves = LRU_WIDTH // LRU_HALF
    out = jnp.zeros((halves, 2, LRU_HALF, LRU_HALF), F32)
    for c in range(halves):
        for j in range(per_half):
            s = slice(j * LRU_BLOCK, (j + 1) * LRU_BLOCK)
            out = out.at[c, :, s, s].set(w[:, c * per_half + j])
    return out.astype(BF16)


def _gla_gate_weights(gate_up, gate_b):
    w = jnp.zeros((GLA_HEADS, 2, LANES, GLA_DK), F32)
    for d in range(2):
        wd = gate_up[d].reshape(GLA_LOWRANK, GLA_HEADS, GLA_DK).transpose(1, 0, 2)
        w = w.at[:, d, d * GLA_LOWRANK:(d + 1) * GLA_LOWRANK, :].set(wd)
    b = gate_b.reshape(2, GLA_HEADS, GLA_DK).transpose(1, 0, 2)
    w_hi = w.astype(BF16)
    w_lo = (w - w_hi.astype(F32)).astype(BF16)
    return jnp.concatenate([w_hi, w_lo], axis=-1), b


def kernel(x, c, ctx, c_ctx, router_w, router_bias, ada_w, ada_b, norm_mix_g, norm_ffn_g, moe_w_gate, moe_w_up, moe_w_down, ev_w_in, ev_w_out, ev_conv_w, ev_conv_b, ev_rg_w, ev_rg_b, ev_ig_w, ev_ig_b, ev_lambda, ev_q_norm_g, ev_k_norm_g, ev_sink, od_w_in, od_w_out, od_gate_up, od_gate_b, od_out_norm_g):
    n_batch, seq, d = x.shape
    ctx_len = ctx.shape[1]
    n_lat = n_batch * seq
    n_ctx = n_batch * ctx_len
    assert ada_w.shape[0] == 2 and d == D_MODEL
    assert seq % ROW_TILE == 0 and n_ctx % ROW_TILE == 0 and seq % ctx_len == 0
    assert ctx_len % LRU_CHUNK == 0 and seq % GLA_BLOCK == 0 and ctx_len % GLA_BLOCK == 0
    tiles_per_seq = seq // ROW_TILE
    n_lat_tiles = n_lat // ROW_TILE

    x_lat = x.reshape(n_lat, d)
    x_ctx = ctx.reshape(n_ctx, d)
    mod_rows = -(-(n_batch + 1) // SUBLANES) * SUBLANES
    cvec = jnp.zeros((mod_rows, d), F32).at[:n_batch].set(c).at[n_batch].set(c_ctx)
    mods = _ada_call(cvec, ada_w, ada_b).reshape(2, mod_rows, 6, d)
    mod_mix = [mods[l][:n_batch + 1, 0:6] for l in range(2)]
    rw_pad = jnp.pad(router_w, ((0, 0), (0, LANES - N_EXPERTS)))
    rw_hi = rw_pad.astype(BF16)
    rw_t = jnp.concatenate([rw_hi, (rw_pad - rw_hi.astype(F32)).astype(BF16)], axis=1)
    rb = router_bias.reshape(N_EXPERTS, 1)

    w_in0 = ev_w_in[0].astype(BF16)
    w_out0 = ev_w_out[0].astype(BF16)
    proj0 = _proj_call(x_lat, x_ctx, mod_mix[0], norm_mix_g[0:1], w_in0, tiles_per_seq, n_batch)
    cos_t, sin_t = _rope_tables(seq)
    seg = np.kron(np.eye(LANES // HEAD_DIM, dtype=np.float32), np.full((HEAD_DIM, HEAD_DIM), 1.0 / HEAD_DIM, np.float32))
    qrot, kvrot = _qkprep_call(proj0, cos_t, sin_t, jnp.tile(ev_q_norm_g[0], 2)[None, :],
                               jnp.tile(ev_k_norm_g[0], 2)[None, :], jnp.asarray(seg),
                               tiles_per_seq, n_lat_tiles)
    a_lat, a_ctx = _rglru_call(proj0, ev_conv_w[0], ev_conv_b[0:1], _block_diag_halves(ev_rg_w[0]),
                               _block_diag_halves(ev_ig_w[0]), ev_rg_b[0], ev_ig_b[0], ev_lambda[0],
                               n_batch, seq, ctx_len)
    b_lat = _attn_call(qrot, kvrot, ev_sink[0], n_batch, seq, ctx_len)
    b_ctx = _ctx_attn_call(qrot, kvrot, ev_sink[0], n_batch, seq, ctx_len)
    h, rows, route, route_cols = _outproj_even_call(a_lat, a_ctx, b_lat, b_ctx, w_out0, x_lat, x_ctx, mod_mix[0],
                                                    norm_ffn_g[0:1], rw_t, rb, tiles_per_seq, n_batch)
    ffn = _moe(rows, route, route_cols, moe_w_gate, moe_w_up, moe_w_down, 0)

    w_in1 = jnp.pad(od_w_in[0], ((0, 0), (0, ODD_IN_PAD - ODD_IN))).astype(BF16)
    h, proj1 = _proj_res_call(h, ffn, mod_mix[0], mod_mix[1], norm_mix_g[1:2], w_in1, tiles_per_seq, n_batch)
    gate_w, gate_b = _gla_gate_weights(od_gate_up[0], od_gate_b[0])
    o = _gla_call(proj1, gate_w, gate_b, od_out_norm_g[0:1], n_batch, seq, ctx_len)
    h_lat, rows, route, route_cols = _outproj_odd_call(o, od_w_out[0].astype(BF16), h, mod_mix[1],
                                                       norm_ffn_g[1:2], rw_t, rb, tiles_per_seq, n_batch)
    out = _moe(rows, route, route_cols, moe_w_gate, moe_w_up, moe_w_down, 1,
               residual=(h_lat, mod_mix[1], seq // MOE_ROWS, n_batch))
    return out.reshape(n_batch, seq, d)
```

```python
import functools

import jax
import jax.numpy as jnp
import numpy as np
from jax import lax
from jax.experimental import pallas as pl
from jax.experimental.pallas import tpu as pltpu

F32 = jnp.float32
BF16 = jnp.bfloat16
HIGHEST = lax.Precision.HIGHEST
ACT_DTYPE = BF16

D_MODEL = 1024
GRID_W = 64
EPS = 1e-6
LRU_WIDTH = 512
LRU_BLOCK = 64
CONV_W = 4
LRU_C = 8.0
ATT_HEADS = 8
ATT_KV_HEADS = 2
HEAD_DIM = 64
WINDOW = 128
ATT_BLOCK = 128
ROPE_BASE = 10000.0
LOG2E = 1.4426950408889634
ATT_Q_WIDTH = ATT_HEADS * HEAD_DIM
ATT_KV_WIDTH = ATT_KV_HEADS * HEAD_DIM
EVEN_IN = 2 * LRU_WIDTH + ATT_Q_WIDTH + 2 * ATT_KV_WIDTH
GLA_HEADS = 4
GLA_DK = 128
GLA_DV = 256
GLA_LOWRANK = 16
GLA_TAU = 16.0
GLA_CHUNK = 64
GLA_BLOCK = 256
GLA_K_WIDTH = GLA_HEADS * GLA_DK
GLA_V_WIDTH = GLA_HEADS * GLA_DV
ODD_IN = 2 * GLA_K_WIDTH + 2 * GLA_V_WIDTH + 2 * GLA_LOWRANK
N_EXPERTS = 16
N_GROUPS = 4
EXPERTS_PER_GROUP = 4
D_EXPERT = 512
PAIRS_PER_GROUP = 6
N_CLASSES = N_GROUPS * PAIRS_PER_GROUP
PAIR_LO = (0, 0, 0, 1, 1, 2)
PAIR_HI = (1, 2, 3, 2, 3, 3)

LANES = 128
SUBLANES = 8
VMEM_LIMIT_BYTES = 56 * 1024 * 1024

ROW_TILE = 512
MOE_TILE = 256
MOE_ROWS = 256
LRU_CHUNK = 128
LRU_HALF = 256
ODD_IN_PAD = ((ODD_IN + LANES - 1) // LANES) * LANES


def _params(*sem):
    return pltpu.CompilerParams(dimension_semantics=sem, vmem_limit_bytes=VMEM_LIMIT_BYTES)


def _modulated_norm(x, g, shift, scale):
    y = x * lax.rsqrt(jnp.mean(x * x, axis=-1, keepdims=True) + EPS)
    return (y * g) * (1.0 + scale) + shift


def _ada_kernel(c_ref, w_ref, b_ref, o_ref):
    c = c_ref[...]
    s = c * jax.nn.sigmoid(c)
    o_ref[...] = jnp.dot(s, w_ref[...], preferred_element_type=F32) + b_ref[...]


def _ada_call(cvec, ada_w, ada_b):
    depth, d, n6 = ada_w.shape
    r = cvec.shape[0]
    tn = n6 // 4
    return pl.pallas_call(
        _ada_kernel,
        grid=(depth, n6 // tn),
        in_specs=[pl.BlockSpec((r, d), lambda l, j: (0, 0)),
                  pl.BlockSpec((None, d, tn), lambda l, j: (l, 0, j)),
                  pl.BlockSpec((None, 1, tn), lambda l, j: (l, 0, j))],
        out_specs=pl.BlockSpec((None, r, tn), lambda l, j: (l, 0, j)),
        out_shape=jax.ShapeDtypeStruct((depth, r, n6), F32),
        compiler_params=_params("arbitrary", "arbitrary"),
        name="adaln",
    )(cvec, ada_w, ada_b.reshape(depth, 1, n6))


def _two_source_specs(width, n_lat_tiles):
    lat = pl.BlockSpec((ROW_TILE, width), lambda i: (jnp.minimum(i, n_lat_tiles - 1), 0))
    ctx = pl.BlockSpec((ROW_TILE, width), lambda i: (jnp.maximum(i - n_lat_tiles, 0), 0))
    return lat, ctx


def _two_source(lat_ref, ctx_ref, n_lat_tiles):
    return jnp.where(pl.program_id(0) >= n_lat_tiles, ctx_ref[...], lat_ref[...])


def _proj_kernel(xl_ref, xc_ref, mod_ref, g_ref, w_ref, o_ref, *, n_lat_tiles):
    x = _two_source(xl_ref, xc_ref, n_lat_tiles)
    n = _modulated_norm(x, g_ref[...], mod_ref[0:1, :], mod_ref[1:2, :])
    o_ref[...] = jnp.dot(n.astype(BF16), w_ref[...], preferred_element_type=F32).astype(o_ref.dtype)


def _proj_res_kernel(h_ref, f_ref, pmod_ref, mod_ref, g_ref, w_ref, x_ref, o_ref):
    x = h_ref[...] + pmod_ref[5:6, :] * f_ref[...]
    x_ref[...] = x
    n = _modulated_norm(x, g_ref[...], mod_ref[0:1, :], mod_ref[1:2, :])
    o_ref[...] = jnp.dot(n.astype(BF16), w_ref[...], preferred_element_type=F32).astype(o_ref.dtype)


def _mod_spec(tiles_per_seq, n_batch):
    return pl.BlockSpec((None, 6, D_MODEL), lambda i: (jnp.minimum(i // tiles_per_seq, n_batch), 0, 0))


def _proj_call(x_lat, x_ctx, mod, g, w, tiles_per_seq, n_batch):
    d = x_lat.shape[1]
    n = x_lat.shape[0] + x_ctx.shape[0]
    n_out = w.shape[1]
    n_lat_tiles = x_lat.shape[0] // ROW_TILE
    lat, ctx = _two_source_specs(d, n_lat_tiles)
    return pl.pallas_call(
        functools.partial(_proj_kernel, n_lat_tiles=n_lat_tiles),
        grid=(n // ROW_TILE,),
        in_specs=[lat, ctx,
                  _mod_spec(tiles_per_seq, n_batch),
                  pl.BlockSpec((1, d), lambda i: (0, 0)),
                  pl.BlockSpec((d, n_out), lambda i: (0, 0))],
        out_specs=pl.BlockSpec((ROW_TILE, n_out), lambda i: (i, 0)),
        out_shape=jax.ShapeDtypeStruct((n, n_out), ACT_DTYPE),
        compiler_params=_params("arbitrary"),
        name="in_proj",
    )(x_lat, x_ctx, mod, g, w)


def _proj_res_call(h, ffn, pmod, mod, g, w, tiles_per_seq, n_batch):
    n, d = h.shape
    n_out = w.shape[1]
    row = pl.BlockSpec((ROW_TILE, d), lambda i: (i, 0))
    return pl.pallas_call(
        _proj_res_kernel,
        grid=(n // ROW_TILE,),
        in_specs=[row, row, _mod_spec(tiles_per_seq, n_batch), _mod_spec(tiles_per_seq, n_batch),
                  pl.BlockSpec((1, d), lambda i: (0, 0)),
                  pl.BlockSpec((d, n_out), lambda i: (0, 0))],
        out_specs=[row, pl.BlockSpec((ROW_TILE, n_out), lambda i: (i, 0))],
        out_shape=[jax.ShapeDtypeStruct((n, d), F32), jax.ShapeDtypeStruct((n, n_out), ACT_DTYPE)],
        compiler_params=_params("arbitrary"),
        name="in_proj_res",
    )(h, ffn, pmod, mod, g, w)


def _qkprep_kernel(q_ref, k_ref, v_ref, c_ref, s_ref, qg_ref, kg_ref, m_ref, qo_ref, kvo_ref):
    cos = c_ref[...]
    sin = s_ref[...]
    lane = lax.broadcasted_iota(jnp.int32, cos.shape, 1)
    first = (lane % 32) < 16
    low = lane < HEAD_DIM

    def norm_rope(x, g):
        ms = jnp.dot(jnp.concatenate(_split2(x * x), axis=1), m_ref[...], preferred_element_type=F32)
        xn = (x * lax.rsqrt(ms + EPS)) * g
        partner = jnp.where(first, pltpu.roll(xn, LANES - 16, 1), pltpu.roll(xn, 16, 1))
        return xn * cos + partner * sin

    for j in range(ATT_Q_WIDTH // LANES):
        qj = norm_rope(q_ref[:, j * LANES:(j + 1) * LANES].astype(F32), qg_ref[...])
        qo_ref[:, j * LANES:(j + 1) * LANES] = (qj * (HEAD_DIM ** -0.5 * LOG2E)).astype(BF16)
    kr = norm_rope(k_ref[...].astype(F32), kg_ref[...])
    ks = pltpu.roll(kr, HEAD_DIM, 1)
    kvo_ref[:, 0:LANES] = jnp.where(low, kr, ks).astype(BF16)
    kvo_ref[:, LANES:2 * LANES] = jnp.where(low, ks, kr).astype(BF16)
    kvo_ref[:, 2 * LANES:3 * LANES] = v_ref[...].astype(BF16)
    kvo_ref[:, 3 * LANES:4 * LANES] = jnp.ones((cos.shape[0], LANES), BF16)


def _qkprep_call(proj, cos_t, sin_t, qg, kg, seg_mean, tiles_per_seq, n_lat_tiles):
    n = proj.shape[0]
    qcol = (2 * LRU_WIDTH) // ATT_Q_WIDTH
    kcol = (2 * LRU_WIDTH + ATT_Q_WIDTH) // LANES

    def pos(i):
        return (jnp.where(i < n_lat_tiles, i % tiles_per_seq, tiles_per_seq), 0)

    return pl.pallas_call(
        _qkprep_kernel,
        grid=(n // ROW_TILE,),
        in_specs=[pl.BlockSpec((ROW_TILE, ATT_Q_WIDTH), lambda i: (i, qcol)),
                  pl.BlockSpec((ROW_TILE, LANES), lambda i: (i, kcol)),
                  pl.BlockSpec((ROW_TILE, LANES), lambda i: (i, kcol + 1)),
                  pl.BlockSpec((ROW_TILE, LANES), pos),
                  pl.BlockSpec((ROW_TILE, LANES), pos),
                  pl.BlockSpec((1, LANES), lambda i: (0, 0)),
                  pl.BlockSpec((1, LANES), lambda i: (0, 0)),
                  pl.BlockSpec((2 * LANES, LANES), lambda i: (0, 0))],
        out_specs=[pl.BlockSpec((ROW_TILE, ATT_Q_WIDTH), lambda i: (i, 0)),
                   pl.BlockSpec((ROW_TILE, 4 * LANES), lambda i: (i, 0))],
        out_shape=[jax.ShapeDtypeStruct((n, ATT_Q_WIDTH), BF16),
                   jax.ShapeDtypeStruct((n, 4 * LANES), BF16)],
        compiler_params=_params("arbitrary"),
        name="qk_prep",
    )(proj, proj, proj, cos_t, sin_t, qg, kg, seg_mean)


def _attend(q, kv, sink_ref, bias):
    rows = q.shape[0]
    lane = lax.broadcasted_iota(jnp.int32, (rows, LANES), 1)
    low = lane < HEAD_DIM
    zero = jnp.zeros((rows, LANES), BF16)
    vv = kv[:, 2 * LANES:4 * LANES]
    groups = []
    heads_per_kv = ATT_HEADS // ATT_KV_HEADS
    for kh in range(ATT_KV_HEADS):
        kk = kv[:, kh * LANES:(kh + 1) * LANES]
        parts = []
        sinks = []
        for j in range(heads_per_kv):
            h = kh * heads_per_kv + j
            qg = q[:, (h // 2) * LANES:(h // 2 + 1) * LANES]
            parts.append(jnp.where(low, qg, zero) if h % 2 == 0 else jnp.where(low, zero, qg))
            sinks.append(jnp.full((rows, 1), sink_ref[h] * LOG2E, F32))
        lhs = jnp.concatenate(parts, axis=0)
        sink = jnp.concatenate(sinks, axis=0)
        s = lax.dot_general(lhs, kk, (((1,), (1,)), ((), ())), preferred_element_type=F32)
        if bias is not None:
            s = s + bias
        m = jnp.maximum(jnp.max(s, axis=-1, keepdims=True), sink)
        e = jnp.exp2(s - m)
        ov = jnp.dot(e.astype(BF16), vv, preferred_element_type=F32)
        o = ov[:, 0:LANES] / (ov[:, LANES:2 * LANES] + jnp.exp2(sink - m))
        r = [o[j * rows:(j + 1) * rows] for j in range(heads_per_kv)]
        for p in range(heads_per_kv // 2):
            even, odd = r[2 * p], r[2 * p + 1]
            if kh == 0:
                groups.append(jnp.where(low, even, pltpu.roll(odd, HEAD_DIM, 1)))
            else:
                groups.append(jnp.where(low, pltpu.roll(even, HEAD_DIM, 1), odd))
    return jnp.concatenate(groups, axis=1)


def _attn_kernel(sink_ref, q_ref, kvp_ref, kvo_ref, kvn_ref, kvc_ref, bias0_ref, bias1_ref, o_ref):
    blk = ATT_BLOCK
    own = kvo_ref[...]
    ctx = kvc_ref[...]
    kv0 = jnp.concatenate([kvp_ref[...], own, ctx], axis=0)
    kv1 = jnp.concatenate([own, kvn_ref[...], ctx], axis=0)
    o_ref[0:blk, :] = _attend(q_ref[0:blk, :], kv0, sink_ref, bias0_ref[...]).astype(o_ref.dtype)
    o_ref[blk:2 * blk, :] = _attend(q_ref[blk:2 * blk, :], kv1, sink_ref, bias1_ref[...]).astype(o_ref.dtype)


def _window_bias(ctx_len):
    blk = ATT_BLOCK
    stacked = (ATT_HEADS // ATT_KV_HEADS) * blk
    qi = lax.broadcasted_iota(jnp.int32, (4, stacked, 3 * blk + ctx_len), 1) % blk
    ks = lax.broadcasted_iota(jnp.int32, (4, stacked, 3 * blk + ctx_len), 2)
    var = lax.broadcasted_iota(jnp.int32, (4, stacked, 3 * blk + ctx_len), 0)
    in_win = jnp.abs(ks - blk - qi) <= WINDOW
    in_seq = ((ks >= blk) | (var % 2 == 1)) & ((ks < 2 * blk) | (var >= 2))
    keep = (in_win & in_seq) | (ks >= 3 * blk)
    return jnp.where(keep, 0.0, -1e30).astype(F32)


def _ctx_attn_kernel(sink_ref, q_ref, kvc_ref, o_ref):
    o_ref[...] = _attend(q_ref[...], kvc_ref[...], sink_ref, None).astype(o_ref.dtype)


def _attn_call(qrot, kvrot, sink, n_batch, seq, ctx_len):
    nb = seq // ATT_BLOCK
    ctx0 = (n_batch * seq) // ctx_len
    kvw = kvrot.shape[1]
    bias = _window_bias(ctx_len)

    assert nb % 2 == 0
    pairs = nb // 2

    def variant0(b, m):
        return ((m > 0).astype(jnp.int32) + 2, 0, 0)

    def variant1(b, m):
        return (1 + 2 * (m < pairs - 1).astype(jnp.int32), 0, 0)

    blk = ATT_BLOCK
    return pl.pallas_call(
        _attn_kernel,
        grid=(n_batch, pairs),
        in_specs=[pl.BlockSpec(memory_space=pltpu.SMEM),
                  pl.BlockSpec((2 * blk, ATT_Q_WIDTH), lambda b, m: (b * pairs + m, 0)),
                  pl.BlockSpec((blk, kvw), lambda b, m: (b * nb + jnp.maximum(2 * m - 1, 0), 0)),
                  pl.BlockSpec((2 * blk, kvw), lambda b, m: (b * pairs + m, 0)),
                  pl.BlockSpec((blk, kvw), lambda b, m: (b * nb + jnp.minimum(2 * m + 2, nb - 1), 0)),
                  pl.BlockSpec((ctx_len, kvw), lambda b, m: (ctx0 + b, 0)),
                  pl.BlockSpec((None,) + bias.shape[1:], variant0),
                  pl.BlockSpec((None,) + bias.shape[1:], variant1)],
        out_specs=pl.BlockSpec((2 * blk, ATT_Q_WIDTH), lambda b, m: (b * pairs + m, 0)),
        out_shape=jax.ShapeDtypeStruct((n_batch * seq, ATT_Q_WIDTH), ACT_DTYPE),
        compiler_params=_params("arbitrary", "arbitrary"),
        name="window_attn",
    )(sink, qrot, kvrot, kvrot, kvrot, kvrot, bias, bias)


def _ctx_attn_call(qrot, kvrot, sink, n_batch, seq, ctx_len):
    ctx0 = (n_batch * seq) // ctx_len
    kvw = kvrot.shape[1]
    return pl.pallas_call(
        _ctx_attn_kernel,
        grid=(n_batch,),
        in_specs=[pl.BlockSpec(memory_space=pltpu.SMEM),
                  pl.BlockSpec((ctx_len, ATT_Q_WIDTH), lambda b: (ctx0 + b, 0)),
                  pl.BlockSpec((ctx_len, kvw), lambda b: (ctx0 + b, 0))],
        out_specs=pl.BlockSpec((ctx_len, ATT_Q_WIDTH), lambda b: (b, 0)),
        out_shape=jax.ShapeDtypeStruct((n_batch * ctx_len, ATT_Q_WIDTH), ACT_DTYPE),
        compiler_params=_params("arbitrary"),
        name="ctx_attn",
    )(sink, qrot, kvrot)


def _scan_rows(a, x, carry, reverse):
    rows, width = a.shape
    sub = lax.broadcasted_iota(jnp.int32, (rows, width), 0) % SUBLANES
    for s in (1, 2, 4):
        shift = rows - s if reverse else s
        a_sh = pltpu.roll(a, shift, 0)
        x_sh = pltpu.roll(x, shift, 0)
        keep = (sub <= SUBLANES - 1 - s) if reverse else (sub >= s)
        x = x + a * jnp.where(keep, x_sh, 0.0)
        a = a * jnp.where(keep, a_sh, 1.0)
    n_groups = rows // SUBLANES
    out = [None] * n_groups
    order = range(n_groups - 1, -1, -1) if reverse else range(n_groups)
    for g in order:
        hg = x[g * SUBLANES:(g + 1) * SUBLANES] + a[g * SUBLANES:(g + 1) * SUBLANES] * carry
        carry = hg[0:1] if reverse else hg[SUBLANES - 1:SUBLANES]
        out[g] = hg
    return jnp.concatenate(out, axis=0), carry


def _rglru_kernel(ul_ref, gl_ref, uc_ref, gc_ref, cw_ref, cb_ref, wr_ref, wi_ref, br_ref, bi_ref, lam_ref,
                  yl_ref, yc_ref, upad_ref, conv_ref, *, seq, ctx_len):
    width = ul_ref.shape[1]
    tc = LRU_CHUNK
    neg_lam = -lam_ref[...]
    softplus = jnp.maximum(neg_lam, 0.0) + jnp.log1p(jnp.exp(-jnp.abs(neg_lam)))
    zeros = jnp.zeros((SUBLANES, width), F32)

    def gates(u, d):
        ub = u.astype(BF16)
        r = jax.nn.sigmoid(jnp.dot(ub, wr_ref[d], preferred_element_type=F32) + br_ref[d:d + 1, :])
        i = jax.nn.sigmoid(jnp.dot(ub, wi_ref[d], preferred_element_type=F32) + bi_ref[d:d + 1, :])
        log_a = (-LRU_C) * r * softplus[d:d + 1, :]
        a = jnp.exp(log_a)
        y = 1.0 - a * a
        root = jnp.where(y > 0.0, y * lax.rsqrt(y), 0.0)
        return a, root * (i * u)

    def run(u_ref, g_ref, y_ref, n, h_fwd, h_bwd):
        n_chunks = n // tc
        upad_ref[0:SUBLANES, :] = zeros
        upad_ref[pl.ds(SUBLANES, n), :] = u_ref[...].astype(F32)
        upad_ref[pl.ds(SUBLANES + n, SUBLANES), :] = zeros

        def conv_body(c, carry):
            t0 = pl.multiple_of(c * tc, tc)
            win = upad_ref[pl.ds(t0, tc + 2 * SUBLANES), :]
            acc = cb_ref[...]
            for k in range(CONV_W):
                acc = acc + cw_ref[k:k + 1, :] * win[SUBLANES - 1 + k:SUBLANES - 1 + k + tc]
            conv_ref[pl.ds(t0, tc), :] = acc
            return carry

        lax.fori_loop(0, n_chunks, conv_body, 0)

        def fwd_body(c, carry):
            t0 = pl.multiple_of(c * tc, tc)
            a, x = gates(conv_ref[pl.ds(t0, tc), :], 0)
            h, carry = _scan_rows(a, x, carry, False)
            upad_ref[pl.ds(pl.multiple_of(t0 + SUBLANES, SUBLANES), tc), :] = h
            return carry

        h_fwd = lax.fori_loop(0, n_chunks, fwd_body, h_fwd, unroll=min(4, n_chunks))

        def bwd_body(c, carry):
            t0 = pl.multiple_of((n_chunks - 1 - c) * tc, tc)
            a, x = gates(conv_ref[pl.ds(t0, tc), :], 1)
            h, carry = _scan_rows(a, x, carry, True)
            h_both = upad_ref[pl.ds(pl.multiple_of(t0 + SUBLANES, SUBLANES), tc), :] + h
            y_ref[pl.ds(t0, tc), :] = (h_both * jax.nn.gelu(g_ref[pl.ds(t0, tc), :].astype(F32))).astype(y_ref.dtype)
            return carry

        h_bwd = lax.fori_loop(0, n_chunks, bwd_body, h_bwd, unroll=min(4, n_chunks))
        return h_fwd, h_bwd

    h0 = jnp.zeros((1, width), F32)
    h_fwd, h_bwd = run(uc_ref, gc_ref, yc_ref, ctx_len, h0, h0)
    run(ul_ref, gl_ref, yl_ref, seq, h_fwd, h_bwd)


def _rglru_call(proj, conv_w, conv_b, wr, wi, br, bi, lam, n_batch, seq, ctx_len):
    ctx0 = (n_batch * seq) // ctx_len
    halves = LRU_WIDTH // LRU_HALF
    w = LRU_HALF
    vec2 = pl.BlockSpec((2, w), lambda b, c: (0, c))
    return pl.pallas_call(
        functools.partial(_rglru_kernel, seq=seq, ctx_len=ctx_len),
        grid=(n_batch, halves),
        in_specs=[pl.BlockSpec((seq, w), lambda b, c: (b, c)),
                  pl.BlockSpec((seq, w), lambda b, c: (b, halves + c)),
                  pl.BlockSpec((ctx_len, w), lambda b, c: (ctx0 + b, c)),
                  pl.BlockSpec((ctx_len, w), lambda b, c: (ctx0 + b, halves + c)),
                  pl.BlockSpec((CONV_W, w), lambda b, c: (0, c)),
                  pl.BlockSpec((1, w), lambda b, c: (0, c)),
                  pl.BlockSpec((None, 2, w, w), lambda b, c: (c, 0, 0, 0)),
                  pl.BlockSpec((None, 2, w, w), lambda b, c: (c, 0, 0, 0)),
                  vec2, vec2, vec2],
        out_specs=[pl.BlockSpec((seq, w), lambda b, c: (b, c)),
                   pl.BlockSpec((ctx_len, w), lambda b, c: (b, c))],
        out_shape=[jax.ShapeDtypeStruct((n_batch * seq, LRU_WIDTH), ACT_DTYPE),
                   jax.ShapeDtypeStruct((n_batch * ctx_len, LRU_WIDTH), ACT_DTYPE)],
        scratch_shapes=[pltpu.VMEM((seq + 2 * SUBLANES, w), F32), pltpu.VMEM((seq, w), F32)],
        compiler_params=_params("arbitrary", "arbitrary"),
        name="rglru",
    )(proj, proj, proj, proj, conv_w, conv_b, wr, wi, br, bi, lam)


def _split2(x):
    hi = x.astype(BF16)
    return hi, (x - hi.astype(F32)).astype(BF16)


def _gla_kernel(q_ref, k_ref, v_ref, r_ref, a_ref, kc_ref, vc_ref, ac_ref, gw_ref, gb_ref, ng_ref,
                o_ref, of_ref, ob_ref, qd_ref, ki_ref, ke_ref, la_ref, eg_ref, st_ref, *, seq, ctx_len):
    ch = GLA_CHUNK
    blk = GLA_BLOCK
    per_blk = blk // ch
    row = lax.broadcasted_iota(jnp.int32, (blk, blk), 0)
    col = lax.broadcasted_iota(jnp.int32, (blk, blk), 1)
    same = (row // ch) == (col // ch)
    keep = (same & (row >= col), same & (row <= col))
    tri = (keep[0].astype(BF16),)
    scale = GLA_DK ** -0.5
    nt = (((1,), (1,)), ((), ()))
    tn = (((0,), (0,)), ((), ()))

    def mm(a, b):
        return jnp.dot(a, b, preferred_element_type=F32)

    def prepare(a_src, k_src, v_src, q_src, dst, n_rows, d):
        n_blk = n_rows // blk
        unroll = min(4, n_blk)

        def decay_body(i, carry):
            t0 = pl.multiple_of(i * blk, blk)
            both = mm(a_src[pl.ds(t0, blk), :], gw_ref[d])
            pre = (both[:, 0:GLA_DK] + both[:, GLA_DK:2 * GLA_DK]) + gb_ref[d:d + 1, :]
            la_ref[pl.ds(t0, blk), :] = (jnp.minimum(pre, 0.0) - jnp.log(1.0 + jnp.exp(-jnp.abs(pre)))) / GLA_TAU
            return carry

        lax.fori_loop(0, n_blk, decay_body, 0, unroll=unroll)

        def scale_body(i, carry):
            t0 = pl.multiple_of(i * blk, blk)
            log_a = la_ref[pl.ds(t0, blk), :]
            sums = mm(tri[0], jnp.concatenate(_split2(log_a), axis=1))
            prefix = sums[:, 0:GLA_DK] + sums[:, GLA_DK:2 * GLA_DK]
            total = jnp.concatenate(
                [jnp.broadcast_to(prefix[(j + 1) * ch - 1:(j + 1) * ch], (ch, GLA_DK)) for j in range(per_blk)],
                axis=0)
            suffix = total - prefix + log_a
            b = prefix if d == 0 else suffix
            to_end = (suffix if d == 0 else prefix) - log_a
            for j in range(per_blk):
                eg_ref[pl.ds(i * per_blk + j, 1), :] = jnp.exp(total[j * ch:j * ch + 1])
            k_blk = k_src[pl.ds(t0, blk), :].astype(F32)
            ke_ref[pl.ds(t0, blk), :] = (k_blk * jnp.exp(to_end)).astype(BF16)
            if q_src is not None:
                q_blk = q_src[pl.ds(t0, blk), :].astype(F32)
                qd_ref[pl.ds(t0, blk), :] = ((q_blk * scale) * jnp.exp(b)).astype(BF16)
                ki_ref[pl.ds(t0, blk), :] = (k_blk * jnp.exp(-b)).astype(BF16)
            return carry

        lax.fori_loop(0, n_blk, scale_body, 0, unroll=unroll)
        if q_src is None:
            return

        def intra_body(i, carry):
            t0 = pl.multiple_of(i * blk, blk)
            s = lax.dot_general(qd_ref[pl.ds(t0, blk), :], ki_ref[pl.ds(t0, blk), :], nt,
                                preferred_element_type=F32)
            p = jnp.where(keep[d], s, 0.0).astype(BF16)
            dst[pl.ds(t0, blk), :] = mm(p, v_src[pl.ds(t0, blk), :].astype(BF16))
            return carry

        lax.fori_loop(0, n_blk, intra_body, 0, unroll=min(8, n_blk))

    def recur(v_src, dst, n_rows, d):
        n_chunks = n_rows // ch

        def body(c, carry):
            n = c if d == 0 else n_chunks - 1 - c
            t0 = pl.multiple_of(n * ch, ch)
            st = st_ref[...]
            if dst is not None:
                dst[pl.ds(t0, ch), :] = dst[pl.ds(t0, ch), :] + lax.dot_general(
                    qd_ref[pl.ds(t0, ch), :], st.astype(BF16), nt, preferred_element_type=F32)
            kv = lax.dot_general(v_src[pl.ds(t0, ch), :].astype(BF16), ke_ref[pl.ds(t0, ch), :], tn,
                                 preferred_element_type=F32)
            st_ref[...] = st * eg_ref[pl.ds(n, 1), :] + kv
            return carry

        lax.fori_loop(0, n_chunks, body, 0, unroll=min(16, n_chunks))

    for d, dst in ((0, of_ref), (1, ob_ref)):
        st_ref[...] = jnp.zeros(st_ref.shape, F32)
        prepare(ac_ref, kc_ref, vc_ref, None, None, ctx_len, d)
        recur(vc_ref, None, ctx_len, d)
        prepare(a_ref, k_ref, v_ref, q_ref, dst, seq, d)
        recur(v_ref, dst, seq, d)

    def finish_body(c, carry):
        t0 = pl.multiple_of(c * blk, blk)
        o = of_ref[pl.ds(t0, blk), :] + ob_ref[pl.ds(t0, blk), :]
        on = (o * lax.rsqrt(jnp.mean(o * o, axis=-1, keepdims=True) + EPS)) * ng_ref[...]
        r = r_ref[pl.ds(t0, blk), :].astype(F32)
        o_ref[pl.ds(t0, blk), :] = (on * (r * jax.nn.sigmoid(r))).astype(o_ref.dtype)
        return carry

    lax.fori_loop(0, seq // blk, finish_body, 0)


def _gla_call(proj, gate_w, gate_b, norm_g, n_batch, seq, ctx_len):
    ctx0 = (n_batch * seq) // ctx_len
    kcol = GLA_K_WIDTH // GLA_DK
    vcol = (2 * GLA_K_WIDTH) // GLA_DV
    rcol = (2 * GLA_K_WIDTH + GLA_V_WIDTH) // GLA_DV
    acol = (2 * GLA_K_WIDTH + 2 * GLA_V_WIDTH) // LANES
    return pl.pallas_call(
        functools.partial(_gla_kernel, seq=seq, ctx_len=ctx_len),
        grid=(n_batch, GLA_HEADS),
        in_specs=[pl.BlockSpec((seq, GLA_DK), lambda b, h: (b, h)),
                  pl.BlockSpec((seq, GLA_DK), lambda b, h: (b, kcol + h)),
                  pl.BlockSpec((seq, GLA_DV), lambda b, h: (b, vcol + h)),
                  pl.BlockSpec((seq, GLA_DV), lambda b, h: (b, rcol + h)),
                  pl.BlockSpec((seq, LANES), lambda b, h: (b, acol)),
                  pl.BlockSpec((ctx_len, GLA_DK), lambda b, h: (ctx0 + b, kcol + h)),
                  pl.BlockSpec((ctx_len, GLA_DV), lambda b, h: (ctx0 + b, vcol + h)),
                  pl.BlockSpec((ctx_len, LANES), lambda b, h: (ctx0 + b, acol)),
                  pl.BlockSpec((None, 2, LANES, 2 * GLA_DK), lambda b, h: (h, 0, 0, 0)),
                  pl.BlockSpec((None, 2, GLA_DK), lambda b, h: (h, 0, 0)),
                  pl.BlockSpec((1, GLA_DV), lambda b, h: (0, 0))],
        out_specs=pl.BlockSpec((seq, GLA_DV), lambda b, h: (b, h)),
        out_shape=jax.ShapeDtypeStruct((n_batch * seq, GLA_V_WIDTH), ACT_DTYPE),
        scratch_shapes=[pltpu.VMEM((seq, GLA_DV), F32),
                        pltpu.VMEM((seq, GLA_DV), F32),
                        pltpu.VMEM((seq, GLA_DK), BF16),
                        pltpu.VMEM((seq, GLA_DK), BF16),
                        pltpu.VMEM((seq, GLA_DK), BF16),
                        pltpu.VMEM((seq, GLA_DK), F32),
                        pltpu.VMEM((seq // GLA_CHUNK, GLA_DK), F32),
                        pltpu.VMEM((GLA_DV, GLA_DK), F32)],
        compiler_params=_params("arbitrary", "arbitrary"),
        name="gla",
    )(proj, proj, proj, proj, proj, proj, proj, proj, gate_w, gate_b, norm_g)


def _route(rows, rows_bf16, rw_ref, rb_ref):
    n_rows = rows.shape[0]
    rows_lo = (rows - rows_bf16.astype(F32)).astype(BF16)
    both = jnp.dot(rows_bf16, rw_ref[...], preferred_element_type=F32)
    logits_rows = (both[:, 0:LANES] + both[:, LANES:2 * LANES]
                   + jnp.dot(rows_lo, rw_ref[:, 0:LANES], preferred_element_type=F32))
    logits = logits_rows.T[0:N_EXPERTS, :]
    scores = jax.nn.sigmoid(logits)
    sel = scores + rb_ref[...]
    eid = lax.broadcasted_iota(jnp.int32, (N_EXPERTS, n_rows), 0)
    gid = eid // EXPERTS_PER_GROUP
    neg = -jnp.inf
    big = N_EXPERTS

    def top2(masked):
        m1 = jnp.max(masked, axis=0, keepdims=True)
        i1 = jnp.min(jnp.where(masked == m1, eid, big), axis=0, keepdims=True)
        rest = jnp.where(eid == i1, neg, masked)
        m2 = jnp.max(rest, axis=0, keepdims=True)
        i2 = jnp.min(jnp.where(rest == m2, eid, big), axis=0, keepdims=True)
        return m1, i1, m2, i2

    best = None
    best_g = None
    for g in range(N_GROUPS):
        m1, _, m2, _ = top2(jnp.where(gid == g, sel, neg))
        gs = m1 + m2
        if best is None:
            best, best_g = gs, jnp.zeros((1, n_rows), jnp.int32)
        else:
            better = gs > best
            best_g = jnp.where(better, g, best_g)
            best = jnp.where(better, gs, best)
    _, i1, _, i2 = top2(jnp.where(gid == best_g, sel, neg))
    s1 = jnp.sum(jnp.where(eid == i1, scores, 0.0), axis=0, keepdims=True)
    s2 = jnp.sum(jnp.where(eid == i2, scores, 0.0), axis=0, keepdims=True)
    total = s1 + s2
    swap = i2 < i1
    lo = jnp.where(swap, i2, i1)
    hi = jnp.where(swap, i1, i2)
    w_lo = jnp.where(swap, s2, s1) / total
    w_hi = jnp.where(swap, s1, s2) / total
    pad = jnp.zeros((4, n_rows), F32)
    return jnp.concatenate([lo.astype(F32), hi.astype(F32), w_lo, w_hi, pad], axis=0)


def _finish_rows(y, h, mod_ref, g_ref, rw_ref, rb_ref, hn_ref, rows_ref, rt_ref, rc_ref):
    hn = h + mod_ref[2:3, :] * y
    hn_ref[...] = hn
    rows = _modulated_norm(hn, g_ref[...], mod_ref[3:4, :], mod_ref[4:5, :])
    rows_bf16 = rows.astype(BF16)
    rows_ref[...] = rows_bf16
    route = _route(rows, rows_bf16, rw_ref, rb_ref)
    rt_ref[...] = route
    padded = jnp.concatenate([route, jnp.zeros((LANES - SUBLANES, route.shape[1]), F32)], axis=0)
    rc_ref[...] = padded.T


def _outproj_even_kernel(al_ref, ac_ref, bl_ref, bc_ref, w_ref, hl_ref, hc_ref, mod_ref, g_ref, rw_ref, rb_ref,
                         hn_ref, rows_ref, rt_ref, rc_ref, *, n_lat_tiles):
    a = _two_source(al_ref, ac_ref, n_lat_tiles).astype(BF16)
    b = _two_source(bl_ref, bc_ref, n_lat_tiles).astype(BF16)
    y = (jnp.dot(a, w_ref[0:LRU_WIDTH, :], preferred_element_type=F32)
         + jnp.dot(b, w_ref[LRU_WIDTH:LRU_WIDTH + ATT_Q_WIDTH, :], preferred_element_type=F32))
    h = _two_source(hl_ref, hc_ref, n_lat_tiles)
    _finish_rows(y, h, mod_ref, g_ref, rw_ref, rb_ref, hn_ref, rows_ref, rt_ref, rc_ref)


def _outproj_odd_kernel(o_ref, w_ref, h_ref, mod_ref, g_ref, rw_ref, rb_ref, hn_ref, rows_ref, rt_ref, rc_ref):
    y = jnp.dot(o_ref[...].astype(BF16), w_ref[...], preferred_element_type=F32)
    _finish_rows(y, h_ref[...], mod_ref, g_ref, rw_ref, rb_ref, hn_ref, rows_ref, rt_ref, rc_ref)


def _outproj_specs(n_rows, tiles_per_seq, n_batch):
    d = D_MODEL
    row = pl.BlockSpec((ROW_TILE, d), lambda i: (i, 0))
    tail_in = [_mod_spec(tiles_per_seq, n_batch),
               pl.BlockSpec((1, d), lambda i: (0, 0)),
               pl.BlockSpec((d, 2 * LANES), lambda i: (0, 0)),
               pl.BlockSpec((N_EXPERTS, 1), lambda i: (0, 0))]
    out_specs = [row, row, pl.BlockSpec((SUBLANES, ROW_TILE), lambda i: (0, i)),
                 pl.BlockSpec((ROW_TILE, LANES), lambda i: (i, 0))]
    out_shape = [jax.ShapeDtypeStruct((n_rows, d), F32), jax.ShapeDtypeStruct((n_rows, d), ACT_DTYPE),
                 jax.ShapeDtypeStruct((SUBLANES, n_rows), F32), jax.ShapeDtypeStruct((n_rows, LANES), F32)]
    return tail_in, out_specs, out_shape


def _outproj_even_call(a_lat, a_ctx, b_lat, b_ctx, w, h_lat, h_ctx, mod, g, rw_t, rb, tiles_per_seq, n_batch):
    n_rows = h_lat.shape[0] + h_ctx.shape[0]
    n_lat_tiles = a_lat.shape[0] // ROW_TILE
    tail_in, out_specs, out_shape = _outproj_specs(n_rows, tiles_per_seq, n_batch)
    lat, ctx = _two_source_specs(LRU_WIDTH, n_lat_tiles)
    h_specs = list(_two_source_specs(D_MODEL, n_lat_tiles))
    return pl.pallas_call(
        functools.partial(_outproj_even_kernel, n_lat_tiles=n_lat_tiles),
        grid=(n_rows // ROW_TILE,),
        in_specs=[lat, ctx, lat, ctx, pl.BlockSpec(w.shape, lambda i: (0, 0))] + h_specs + tail_in,
        out_specs=out_specs,
        out_shape=out_shape,
        compiler_params=_params("arbitrary"),
        name="out_proj_even",
    )(a_lat, a_ctx, b_lat, b_ctx, w, h_lat, h_ctx, mod, g, rw_t, rb)


def _outproj_odd_call(o, w, h, mod, g, rw_t, rb, tiles_per_seq, n_batch):
    n_rows = o.shape[0]
    tail_in, out_specs, out_shape = _outproj_specs(n_rows, tiles_per_seq, n_batch)
    return pl.pallas_call(
        _outproj_odd_kernel,
        grid=(n_rows // ROW_TILE,),
        in_specs=[pl.BlockSpec((ROW_TILE, o.shape[1]), lambda i: (i, 0)),
                  pl.BlockSpec(w.shape, lambda i: (0, 0)),
                  pl.BlockSpec((ROW_TILE, D_MODEL), lambda i: (i, 0))] + tail_in,
        out_specs=out_specs,
        out_shape=out_shape,
        compiler_params=_params("arbitrary"),
        name="out_proj_odd",
    )(o, w, h, mod, g, rw_t, rb)


X_SLAB = D_MODEL // LANES
Y_SLAB = 2 * X_SLAB


def _for_rows(n_rows, fn):
    def body(g, carry):
        for u in range(SUBLANES):
            fn(g, u)
        return carry

    lax.fori_loop(0, n_rows // SUBLANES, body, 0)


def _dispatch_kernel(tail_ref, slot_ref, rows_ref, xs_ref, slab_ref, zero_ref, sem, zero_sem, *, steps, n_tiles):
    i = pl.program_id(0)
    k = i % 2
    rt = MOE_ROWS
    tile_rows = MOE_TILE * X_SLAB

    def slab_copy(buf, g, u, slot):
        return pltpu.make_async_copy(
            slab_ref.at[buf, g, :, pl.ds(u, 1), :],
            xs_ref.at[pl.ds(pl.multiple_of(slot * X_SLAB, X_SLAB), X_SLAB)], sem.at[buf])

    def wait_all(buf):
        pltpu.make_async_copy(slab_ref.at[buf], slab_ref.at[buf], sem.at[buf]).wait()

    @pl.when(i == 0)
    def _():
        zero_ref[...] = jnp.zeros(zero_ref.shape, F32)

        def fill(t):
            start = pl.multiple_of(t * tile_rows, tile_rows)
            return pltpu.make_async_copy(zero_ref, xs_ref.at[pl.ds(start, tile_rows)], zero_sem)

        n_used = tail_ref[1, N_CLASSES - 1]
        for phase in ("start", "wait"):
            for c in range(N_CLASSES):
                for cond, t in ((tail_ref[0, c] > 0, tail_ref[1, c] - 1), (n_used + c < n_tiles, n_used + c)):
                    @pl.when(cond)
                    def _():
                        getattr(fill(t), phase)()

    @pl.when(i >= 2)
    def _():
        wait_all(k)

    for j in range(X_SLAB):
        lanes = rows_ref[:, j * LANES:(j + 1) * LANES].astype(F32)
        slab_ref[k, :, j, :, :] = lanes.reshape(rt // SUBLANES, SUBLANES, LANES)

    _for_rows(rt, lambda g, u: slab_copy(k, g, u, slot_ref[0, 0, g * SUBLANES + u]).start())

    @pl.when(i == steps - 1)
    def _():
        wait_all(k)
        if steps >= 2:
            wait_all(1 - k)


def _dispatch_call(rows, slots, tails, n_slots):
    n, d = rows.shape
    steps = n // MOE_ROWS
    grid_spec = pltpu.PrefetchScalarGridSpec(
        num_scalar_prefetch=1,
        grid=(steps,),
        in_specs=[pl.BlockSpec((1, 1, MOE_ROWS), lambda i, tails: (i, 0, 0), memory_space=pltpu.SMEM),
                  pl.BlockSpec((MOE_ROWS, d), lambda i, tails: (i, 0))],
        out_specs=pl.BlockSpec(memory_space=pl.ANY),
        scratch_shapes=[pltpu.VMEM((2, MOE_ROWS // SUBLANES, X_SLAB, SUBLANES, LANES), F32),
                        pltpu.VMEM((MOE_TILE * X_SLAB, 1, LANES), F32),
                        pltpu.SemaphoreType.DMA((2,)), pltpu.SemaphoreType.DMA(())],
    )
    xs = pl.pallas_call(
        functools.partial(_dispatch_kernel, steps=steps, n_tiles=n_slots // MOE_TILE),
        grid_spec=grid_spec,
        out_shape=jax.ShapeDtypeStruct((n_slots * X_SLAB, 1, LANES), F32),
        compiler_params=_params("arbitrary"),
        name="moe_dispatch",
    )(tails, slots.reshape(steps, 1, MOE_ROWS), rows)
    return xs.reshape(n_slots * X_SLAB, LANES)


def _experts_kernel(elo_ref, ehi_ref, used_ref, x_ref, wg0_ref, wu0_ref, wd0_ref,
                    wg1_ref, wu1_ref, wd1_ref, y_ref, cg0_ref, cu0_ref, cd0_ref, cg1_ref, cu1_ref, cd1_ref):
    t = pl.program_id(0)
    tm = MOE_TILE
    prev = jnp.maximum(t - 1, 0)
    sides = ((elo_ref, (wg0_ref, wu0_ref, wd0_ref), (cg0_ref, cu0_ref, cd0_ref)),
             (ehi_ref, (wg1_ref, wu1_ref, wd1_ref), (cg1_ref, cu1_ref, cd1_ref)))

    @pl.when(t < used_ref[0])
    def _():
        for ids, weights, cached in sides:
            @pl.when((t == 0) | (ids[t] != ids[prev]))
            def _():
                for w_ref, c_ref in zip(weights, cached):
                    c_ref[...] = w_ref[...].astype(BF16)

        x = jnp.concatenate([x_ref[pl.ds(j, tm, stride=X_SLAB), :] for j in range(X_SLAB)],
                            axis=1).astype(BF16)
        for e, (_, _, (cg_ref, cu_ref, cd_ref)) in enumerate(sides):
            gate = jnp.dot(x, cg_ref[...], preferred_element_type=F32)
            up = jnp.dot(x, cu_ref[...], preferred_element_type=F32)
            act = ((gate * jax.nn.sigmoid(gate)) * up).astype(BF16)
            f = jnp.dot(act, cd_ref[...], preferred_element_type=F32)
            for j in range(X_SLAB):
                y_ref[pl.ds(e * X_SLAB + j, tm, stride=Y_SLAB), :] = f[:, j * LANES:(j + 1) * LANES]

    @pl.when(t >= used_ref[0])
    def _():
        y_ref[...] = jnp.zeros_like(y_ref)


def _experts_call(xs, e_lo, e_hi, n_used, w_gate, w_up, w_down, layer):
    d = D_MODEL
    tiles = xs.shape[0] // (MOE_TILE * X_SLAB)

    def xmap(t, elo, ehi, used):
        return (jnp.maximum(jnp.minimum(t, used[0] - 1), 0), 0)

    def lo(t, elo, ehi, used):
        return (layer, elo[t], 0, 0)

    def hi(t, elo, ehi, used):
        return (layer, ehi[t], 0, 0)

    up_spec = lambda m: pl.BlockSpec((None, None, d, D_EXPERT), m)
    down_spec = lambda m: pl.BlockSpec((None, None, D_EXPERT, d), m)
    cached = [pltpu.VMEM((d, D_EXPERT), BF16), pltpu.VMEM((d, D_EXPERT), BF16), pltpu.VMEM((D_EXPERT, d), BF16)]
    grid_spec = pltpu.PrefetchScalarGridSpec(
        num_scalar_prefetch=3,
        grid=(tiles,),
        in_specs=[pl.BlockSpec((MOE_TILE * X_SLAB, LANES), xmap),
                  up_spec(lo), up_spec(lo), down_spec(lo),
                  up_spec(hi), up_spec(hi), down_spec(hi)],
        out_specs=pl.BlockSpec((MOE_TILE * Y_SLAB, LANES), lambda t, elo, ehi, used: (t, 0)),
        scratch_shapes=cached + cached,
    )
    return pl.pallas_call(
        _experts_kernel,
        grid_spec=grid_spec,
        out_shape=jax.ShapeDtypeStruct((tiles * MOE_TILE * Y_SLAB, LANES), F32),
        compiler_params=_params("arbitrary"),
        name="experts",
    )(e_lo, e_hi, n_used, xs, w_gate, w_up, w_down, w_gate, w_up, w_down)


def _collect_kernel(slot_ref, next_slot_ref, ys_ref, rc_ref, *rest, steps, residual):
    if residual:
        h_ref, mod_ref, o_ref, buf_ref, sem = rest
    else:
        o_ref, buf_ref, sem = rest
    i = pl.program_id(0)
    k = i % 2
    rt = MOE_ROWS

    def slab_copy(buf, g, u, slot):
        return pltpu.make_async_copy(
            ys_ref.at[pl.ds(pl.multiple_of(slot * Y_SLAB, Y_SLAB), Y_SLAB)],
            buf_ref.at[buf, g, :, pl.ds(u, 1), :], sem.at[buf])

    def start_all(slots, buf):
        _for_rows(rt, lambda g, u: slab_copy(buf, g, u, slots[0, 0, g * SUBLANES + u]).start())

    @pl.when(i == 0)
    def _():
        start_all(slot_ref, 0)

    @pl.when(i + 1 < steps)
    def _():
        start_all(next_slot_ref, 1 - k)

    pltpu.make_async_copy(buf_ref.at[k], buf_ref.at[k], sem.at[k]).wait()

    rc = rc_ref[...]
    w_lo = rc[:, 2:3]
    w_hi = rc[:, 3:4]
    for j in range(X_SLAB):
        cols = slice(j * LANES, (j + 1) * LANES)
        f = (w_lo * buf_ref[k, :, j, :, :].reshape(rt, LANES)
             + w_hi * buf_ref[k, :, X_SLAB + j, :, :].reshape(rt, LANES))
        if residual:
            o_ref[:, cols] = h_ref[:, cols] + mod_ref[5:6, cols] * f
        else:
            o_ref[:, cols] = f


def _collect_call(ys, slots, route_cols, residual=None):
    n = slots.shape[0]
    d = D_MODEL
    steps = n // MOE_ROWS
    slots3 = slots.reshape(steps, 1, MOE_ROWS)
    row = pl.BlockSpec((MOE_ROWS, d), lambda i: (i, 0))
    in_specs = [pl.BlockSpec((1, 1, MOE_ROWS), lambda i: (i, 0, 0), memory_space=pltpu.SMEM),
                pl.BlockSpec((1, 1, MOE_ROWS), lambda i: (jnp.minimum(i + 1, steps - 1), 0, 0),
                             memory_space=pltpu.SMEM),
                pl.BlockSpec(memory_space=pl.ANY),
                pl.BlockSpec((MOE_ROWS, LANES), lambda i: (i, 0))]
    args = [slots3, slots3, ys.reshape(ys.shape[0], 1, LANES), route_cols]
    if residual is not None:
        h, mod, tiles_per_seq, n_batch = residual
        in_specs += [row, _mod_spec(tiles_per_seq, n_batch)]
        args += [h, mod]
    return pl.pallas_call(
        functools.partial(_collect_kernel, steps=steps, residual=residual is not None),
        grid=(steps,),
        in_specs=in_specs,
        out_specs=row,
        out_shape=jax.ShapeDtypeStruct((n, d), F32),
        scratch_shapes=[pltpu.VMEM((2, MOE_ROWS // SUBLANES, Y_SLAB, SUBLANES, LANES), F32),
                        pltpu.SemaphoreType.DMA((2,))],
        compiler_params=_params("arbitrary"),
        name="moe_collect",
    )(*args)


def _moe(rows, route, route_cols, w_gate, w_up, w_down, layer, residual=None):
    n = rows.shape[0]
    tm = MOE_TILE
    assert n % MOE_ROWS == 0
    max_tiles = -(-n // tm) + N_CLASSES
    p = max_tiles * tm
    lo = route[0].astype(jnp.int32)
    hi = route[1].astype(jnp.int32)
    lo_in = lo % EXPERTS_PER_GROUP
    hi_in = hi % EXPERTS_PER_GROUP
    pair = lo_in * 3 - (lo_in * (lo_in - 1)) // 2 + (hi_in - lo_in - 1)
    cls = (lo // EXPERTS_PER_GROUP) * PAIRS_PER_GROUP + pair
    onehot = (cls[:, None] == jnp.arange(N_CLASSES, dtype=jnp.int32)[None, :]).astype(jnp.int32)
    counts = jnp.sum(onehot, axis=0)
    rank = jnp.sum(jnp.cumsum(onehot, axis=0) * onehot, axis=1) - 1
    tiles_c = (counts + tm - 1) // tm
    tile_end = jnp.cumsum(tiles_c)
    tile_start = tile_end - tiles_c
    n_used = tile_end[-1]
    slot_of_token = (tile_start[cls] * tm + rank).astype(jnp.int32)
    tile_ids = jnp.arange(max_tiles, dtype=jnp.int32)
    tile_cls = jnp.minimum(jnp.sum((tile_end[None, :] <= tile_ids[:, None]).astype(jnp.int32), axis=1),
                           N_CLASSES - 1)
    group0 = (tile_cls // PAIRS_PER_GROUP) * EXPERTS_PER_GROUP
    e_lo = group0 + jnp.asarray(PAIR_LO, jnp.int32)[tile_cls % PAIRS_PER_GROUP]
    e_hi = group0 + jnp.asarray(PAIR_HI, jnp.int32)[tile_cls % PAIRS_PER_GROUP]
    tails = jnp.stack([tiles_c, tile_end]).astype(jnp.int32)
    xs = _dispatch_call(rows, slot_of_token, tails, p)
    ys = _experts_call(xs, e_lo.astype(jnp.int32), e_hi.astype(jnp.int32),
                       n_used.reshape(1).astype(jnp.int32), w_gate, w_up, w_down, layer)
    return _collect_call(ys, slot_of_token, route_cols, residual)


def _rope_tables(seq):
    rows = seq // GRID_W
    row = jnp.repeat(jnp.arange(rows), GRID_W).astype(F32)
    col = jnp.tile(jnp.arange(GRID_W), rows).astype(F32)
    n_freq = HEAD_DIM // 4
    inv_freq = ROPE_BASE ** (-jnp.arange(n_freq, dtype=F32) / n_freq)
    ang_r = row[:, None] * inv_freq
    ang_c = col[:, None] * inv_freq
    cos = jnp.concatenate([jnp.cos(ang_r), jnp.cos(ang_r), jnp.cos(ang_c), jnp.cos(ang_c)], axis=1)
    sin = jnp.concatenate([-jnp.sin(ang_r), jnp.sin(ang_r), -jnp.sin(ang_c), jnp.sin(ang_c)], axis=1)
    cos = jnp.concatenate([jnp.tile(cos, (1, 2)), jnp.ones((ROW_TILE, LANES), F32)], axis=0)
    sin = jnp.concatenate([jnp.tile(sin, (1, 2)), jnp.zeros((ROW_TILE, LANES), F32)], axis=0)
    return cos, sin


def _block_diag_halves(w):
    per_half = LRU_HALF // LRU_BLOCK
    halves = LRU_WIDTH // LRU_HALF
    out = jnp.zeros((halves, 2, LRU_HALF, LRU_HALF), F32)
    for c in range(halves):
        for j in range(per_half):
            s = slice(j * LRU_BLOCK, (j + 1) * LRU_BLOCK)
            out = out.at[c, :, s, s].set(w[:, c * per_half + j])
    return out.astype(BF16)


def _gla_gate_weights(gate_up, gate_b):
    w = jnp.zeros((GLA_HEADS, 2, LANES, GLA_DK), F32)
    for d in range(2):
        wd = gate_up[d].reshape(GLA_LOWRANK, GLA_HEADS, GLA_DK).transpose(1, 0, 2)
        w = w.at[:, d, d * GLA_LOWRANK:(d + 1) * GLA_LOWRANK, :].set(wd)
    b = gate_b.reshape(2, GLA_HEADS, GLA_DK).transpose(1, 0, 2)
    w_hi = w.astype(BF16)
    w_lo = (w - w_hi.astype(F32)).astype(BF16)
    return jnp.concatenate([w_hi, w_lo], axis=-1), b


def kernel(x, c, ctx, c_ctx, router_w, router_bias, ada_w, ada_b, norm_mix_g, norm_ffn_g, moe_w_gate, moe_w_up, moe_w_down, ev_w_in, ev_w_out, ev_conv_w, ev_conv_b, ev_rg_w, ev_rg_b, ev_ig_w, ev_ig_b, ev_lambda, ev_q_norm_g, ev_k_norm_g, ev_sink, od_w_in, od_w_out, od_gate_up, od_gate_b, od_out_norm_g):
    n_batch, seq, d = x.shape
    ctx_len = ctx.shape[1]
    n_lat = n_batch * seq
    n_ctx = n_batch * ctx_len
    assert ada_w.shape[0] == 2 and d == D_MODEL
    assert seq % ROW_TILE == 0 and n_ctx % ROW_TILE == 0 and seq % ctx_len == 0
    assert ctx_len % LRU_CHUNK == 0 and seq % GLA_BLOCK == 0 and ctx_len % GLA_BLOCK == 0
    tiles_per_seq = seq // ROW_TILE
    n_lat_tiles = n_lat // ROW_TILE

    x_lat = x.reshape(n_lat, d)
    x_ctx = ctx.reshape(n_ctx, d)
    mod_rows = -(-(n_batch + 1) // SUBLANES) * SUBLANES
    cvec = jnp.zeros((mod_rows, d), F32).at[:n_batch].set(c).at[n_batch].set(c_ctx)
    mods = _ada_call(cvec, ada_w, ada_b).reshape(2, mod_rows, 6, d)
    mod_mix = [mods[l][:n_batch + 1, 0:6] for l in range(2)]
    rw_pad = jnp.pad(router_w, ((0, 0), (0, LANES - N_EXPERTS)))
    rw_hi = rw_pad.astype(BF16)
    rw_t = jnp.concatenate([rw_hi, (rw_pad - rw_hi.astype(F32)).astype(BF16)], axis=1)
    rb = router_bias.reshape(N_EXPERTS, 1)

    w_in0 = ev_w_in[0].astype(BF16)
    w_out0 = ev_w_out[0].astype(BF16)
    proj0 = _proj_call(x_lat, x_ctx, mod_mix[0], norm_mix_g[0:1], w_in0, tiles_per_seq, n_batch)
    cos_t, sin_t = _rope_tables(seq)
    seg = np.kron(np.eye(LANES // HEAD_DIM, dtype=np.float32), np.full((HEAD_DIM, HEAD_DIM), 1.0 / HEAD_DIM, np.float32))
    seg = jnp.asarray(np.concatenate([seg, seg], axis=0)).astype(BF16)
    qrot, kvrot = _qkprep_call(proj0, cos_t, sin_t, jnp.tile(ev_q_norm_g[0], 2)[None, :],
                               jnp.tile(ev_k_norm_g[0], 2)[None, :], seg,
                               tiles_per_seq, n_lat_tiles)
    a_lat, a_ctx = _rglru_call(proj0, ev_conv_w[0], ev_conv_b[0:1], _block_diag_halves(ev_rg_w[0]),
                               _block_diag_halves(ev_ig_w[0]), ev_rg_b[0], ev_ig_b[0], ev_lambda[0],
                               n_batch, seq, ctx_len)
    b_lat = _attn_call(qrot, kvrot, ev_sink[0], n_batch, seq, ctx_len)
    b_ctx = _ctx_attn_call(qrot, kvrot, ev_sink[0], n_batch, seq, ctx_len)
    h, rows, route, route_cols = _outproj_even_call(a_lat, a_ctx, b_lat, b_ctx, w_out0, x_lat, x_ctx, mod_mix[0],
                                                    norm_ffn_g[0:1], rw_t, rb, tiles_per_seq, n_batch)
    ffn = _moe(rows, route, route_cols, moe_w_gate, moe_w_up, moe_w_down, 0)

    w_in1 = jnp.pad(od_w_in[0], ((0, 0), (0, ODD_IN_PAD - ODD_IN))).astype(BF16)
    h, proj1 = _proj_res_call(h, ffn, mod_mix[0], mod_mix[1], norm_mix_g[1:2], w_in1, tiles_per_seq, n_batch)
    gate_w, gate_b = _gla_gate_weights(od_gate_up[0], od_gate_b[0])
    o = _gla_call(proj1, gate_w, gate_b, od_out_norm_g[0:1], n_batch, seq, ctx_len)
    h_lat, rows, route, route_cols = _outproj_odd_call(o, od_w_out[0].astype(BF16), h, mod_mix[1],
                                                       norm_ffn_g[1:2], rw_t, rb, tiles_per_seq, n_batch)
    out = _moe(rows, route, route_cols, moe_w_gate, moe_w_up, moe_w_down, 1,
               residual=(h_lat, mod_mix[1], seq // MOE_ROWS, n_batch))
    return out.reshape(n_batch, seq, d)
```

```python
import functools

import jax
import jax.numpy as jnp
import numpy as np
from jax import lax
from jax.experimental import pallas as pl
from jax.experimental.pallas import tpu as pltpu

F32 = jnp.float32
BF16 = jnp.bfloat16
HIGHEST = lax.Precision.HIGHEST
ACT_DTYPE = BF16

D_MODEL = 1024
GRID_W = 64
EPS = 1e-6
LRU_WIDTH = 512
LRU_BLOCK = 64
CONV_W = 4
LRU_C = 8.0
ATT_HEADS = 8
ATT_KV_HEADS = 2
HEAD_DIM = 64
WINDOW = 128
ATT_BLOCK = 128
ROPE_BASE = 10000.0
LOG2E = 1.4426950408889634
ATT_Q_WIDTH = ATT_HEADS * HEAD_DIM
ATT_KV_WIDTH = ATT_KV_HEADS * HEAD_DIM
EVEN_IN = 2 * LRU_WIDTH + ATT_Q_WIDTH + 2 * ATT_KV_WIDTH
GLA_HEADS = 4
GLA_DK = 128
GLA_DV = 256
GLA_LOWRANK = 16
GLA_TAU = 16.0
GLA_CHUNK = 64
GLA_BLOCK = 256
GLA_K_WIDTH = GLA_HEADS * GLA_DK
GLA_V_WIDTH = GLA_HEADS * GLA_DV
ODD_IN = 2 * GLA_K_WIDTH + 2 * GLA_V_WIDTH + 2 * GLA_LOWRANK
N_EXPERTS = 16
N_GROUPS = 4
EXPERTS_PER_GROUP = 4
D_EXPERT = 512
PAIRS_PER_GROUP = 6
N_CLASSES = N_GROUPS * PAIRS_PER_GROUP
PAIR_LO = (0, 0, 0, 1, 1, 2)
PAIR_HI = (1, 2, 3, 2, 3, 3)

LANES = 128
SUBLANES = 8
VMEM_LIMIT_BYTES = 56 * 1024 * 1024

ROW_TILE = 512
MOE_TILE = 256
MOE_ROWS = 512
LRU_CHUNK = 128
LRU_HALF = 256
ODD_IN_PAD = ((ODD_IN + LANES - 1) // LANES) * LANES


def _params(*sem):
    return pltpu.CompilerParams(dimension_semantics=sem, vmem_limit_bytes=VMEM_LIMIT_BYTES)


def _modulated_norm(x, g, shift, scale):
    y = x * lax.rsqrt(jnp.mean(x * x, axis=-1, keepdims=True) + EPS)
    return (y * g) * (1.0 + scale) + shift


def _ada_kernel(c_ref, w_ref, b_ref, o_ref):
    c = c_ref[...]
    s = c * jax.nn.sigmoid(c)
    o_ref[...] = jnp.dot(s, w_ref[...], preferred_element_type=F32) + b_ref[...]


def _ada_call(cvec, ada_w, ada_b):
    depth, d, n6 = ada_w.shape
    r = cvec.shape[0]
    tn = n6 // 4
    return pl.pallas_call(
        _ada_kernel,
        grid=(depth, n6 // tn),
        in_specs=[pl.BlockSpec((r, d), lambda l, j: (0, 0)),
                  pl.BlockSpec((None, d, tn), lambda l, j: (l, 0, j)),
                  pl.BlockSpec((None, 1, tn), lambda l, j: (l, 0, j))],
        out_specs=pl.BlockSpec((None, r, tn), lambda l, j: (l, 0, j)),
        out_shape=jax.ShapeDtypeStruct((depth, r, n6), F32),
        compiler_params=_params("arbitrary", "arbitrary"),
        name="adaln",
    )(cvec, ada_w, ada_b.reshape(depth, 1, n6))


def _two_source_specs(width, n_lat_tiles):
    lat = pl.BlockSpec((ROW_TILE, width), lambda i: (jnp.minimum(i, n_lat_tiles - 1), 0))
    ctx = pl.BlockSpec((ROW_TILE, width), lambda i: (jnp.maximum(i - n_lat_tiles, 0), 0))
    return lat, ctx


def _two_source(lat_ref, ctx_ref, n_lat_tiles):
    return jnp.where(pl.program_id(0) >= n_lat_tiles, ctx_ref[...], lat_ref[...])


def _proj_kernel(xl_ref, xc_ref, mod_ref, g_ref, w_ref, o_ref, *, n_lat_tiles):
    x = _two_source(xl_ref, xc_ref, n_lat_tiles)
    n = _modulated_norm(x, g_ref[...], mod_ref[0:1, :], mod_ref[1:2, :])
    o_ref[...] = jnp.dot(n.astype(BF16), w_ref[...], preferred_element_type=F32).astype(o_ref.dtype)


def _proj_res_kernel(h_ref, f_ref, pmod_ref, mod_ref, g_ref, w_ref, x_ref, o_ref):
    x = h_ref[...] + pmod_ref[5:6, :] * f_ref[...]
    x_ref[...] = x
    n = _modulated_norm(x, g_ref[...], mod_ref[0:1, :], mod_ref[1:2, :])
    o_ref[...] = jnp.dot(n.astype(BF16), w_ref[...], preferred_element_type=F32).astype(o_ref.dtype)


def _mod_spec(tiles_per_seq, n_batch):
    return pl.BlockSpec((None, 6, D_MODEL), lambda i: (jnp.minimum(i // tiles_per_seq, n_batch), 0, 0))


def _proj_call(x_lat, x_ctx, mod, g, w, tiles_per_seq, n_batch):
    d = x_lat.shape[1]
    n = x_lat.shape[0] + x_ctx.shape[0]
    n_out = w.shape[1]
    n_lat_tiles = x_lat.shape[0] // ROW_TILE
    lat, ctx = _two_source_specs(d, n_lat_tiles)
    return pl.pallas_call(
        functools.partial(_proj_kernel, n_lat_tiles=n_lat_tiles),
        grid=(n // ROW_TILE,),
        in_specs=[lat, ctx,
                  _mod_spec(tiles_per_seq, n_batch),
                  pl.BlockSpec((1, d), lambda i: (0, 0)),
                  pl.BlockSpec((d, n_out), lambda i: (0, 0))],
        out_specs=pl.BlockSpec((ROW_TILE, n_out), lambda i: (i, 0)),
        out_shape=jax.ShapeDtypeStruct((n, n_out), ACT_DTYPE),
        compiler_params=_params("arbitrary"),
        name="in_proj",
    )(x_lat, x_ctx, mod, g, w)


def _proj_res_call(h, ffn, pmod, mod, g, w, tiles_per_seq, n_batch):
    n, d = h.shape
    n_out = w.shape[1]
    row = pl.BlockSpec((ROW_TILE, d), lambda i: (i, 0))
    return pl.pallas_call(
        _proj_res_kernel,
        grid=(n // ROW_TILE,),
        in_specs=[row, row, _mod_spec(tiles_per_seq, n_batch), _mod_spec(tiles_per_seq, n_batch),
                  pl.BlockSpec((1, d), lambda i: (0, 0)),
                  pl.BlockSpec((d, n_out), lambda i: (0, 0))],
        out_specs=[row, pl.BlockSpec((ROW_TILE, n_out), lambda i: (i, 0))],
        out_shape=[jax.ShapeDtypeStruct((n, d), F32), jax.ShapeDtypeStruct((n, n_out), ACT_DTYPE)],
        compiler_params=_params("arbitrary"),
        name="in_proj_res",
    )(h, ffn, pmod, mod, g, w)


def _qkprep_kernel(q_ref, k_ref, v_ref, c_ref, s_ref, qg_ref, kg_ref, m_ref, qo_ref, kvo_ref):
    cos = c_ref[...]
    sin = s_ref[...]
    lane = lax.broadcasted_iota(jnp.int32, cos.shape, 1)
    first = (lane % 32) < 16
    low = lane < HEAD_DIM

    def norm_rope(x, g):
        ms = jnp.dot(jnp.concatenate(_split2(x * x), axis=1), m_ref[...], preferred_element_type=F32)
        xn = (x * lax.rsqrt(ms + EPS)) * g
        partner = jnp.where(first, pltpu.roll(xn, LANES - 16, 1), pltpu.roll(xn, 16, 1))
        return xn * cos + partner * sin

    for j in range(ATT_Q_WIDTH // LANES):
        qj = norm_rope(q_ref[:, j * LANES:(j + 1) * LANES].astype(F32), qg_ref[...])
        qo_ref[:, j * LANES:(j + 1) * LANES] = (qj * (HEAD_DIM ** -0.5 * LOG2E)).astype(BF16)
    kr = norm_rope(k_ref[...].astype(F32), kg_ref[...])
    ks = pltpu.roll(kr, HEAD_DIM, 1)
    kvo_ref[:, 0:LANES] = jnp.where(low, kr, ks).astype(BF16)
    kvo_ref[:, LANES:2 * LANES] = jnp.where(low, ks, kr).astype(BF16)
    kvo_ref[:, 2 * LANES:3 * LANES] = v_ref[...].astype(BF16)
    kvo_ref[:, 3 * LANES:4 * LANES] = jnp.ones((cos.shape[0], LANES), BF16)


def _qkprep_call(proj, cos_t, sin_t, qg, kg, seg_mean, tiles_per_seq, n_lat_tiles):
    n = proj.shape[0]
    qcol = (2 * LRU_WIDTH) // ATT_Q_WIDTH
    kcol = (2 * LRU_WIDTH + ATT_Q_WIDTH) // LANES

    def pos(i):
        return (jnp.where(i < n_lat_tiles, i % tiles_per_seq, tiles_per_seq), 0)

    return pl.pallas_call(
        _qkprep_kernel,
        grid=(n // ROW_TILE,),
        in_specs=[pl.BlockSpec((ROW_TILE, ATT_Q_WIDTH), lambda i: (i, qcol)),
                  pl.BlockSpec((ROW_TILE, LANES), lambda i: (i, kcol)),
                  pl.BlockSpec((ROW_TILE, LANES), lambda i: (i, kcol + 1)),
                  pl.BlockSpec((ROW_TILE, LANES), pos),
                  pl.BlockSpec((ROW_TILE, LANES), pos),
                  pl.BlockSpec((1, LANES), lambda i: (0, 0)),
                  pl.BlockSpec((1, LANES), lambda i: (0, 0)),
                  pl.BlockSpec((2 * LANES, LANES), lambda i: (0, 0))],
        out_specs=[pl.BlockSpec((ROW_TILE, ATT_Q_WIDTH), lambda i: (i, 0)),
                   pl.BlockSpec((ROW_TILE, 4 * LANES), lambda i: (i, 0))],
        out_shape=[jax.ShapeDtypeStruct((n, ATT_Q_WIDTH), BF16),
                   jax.ShapeDtypeStruct((n, 4 * LANES), BF16)],
        compiler_params=_params("arbitrary"),
        name="qk_prep",
    )(proj, proj, proj, cos_t, sin_t, qg, kg, seg_mean)


def _attend(q, kv, sink_ref, bias):
    rows = q.shape[0]
    lane = lax.broadcasted_iota(jnp.int32, (rows, LANES), 1)
    low = lane < HEAD_DIM
    zero = jnp.zeros((rows, LANES), BF16)
    vv = kv[:, 2 * LANES:4 * LANES]
    groups = []
    heads_per_kv = ATT_HEADS // ATT_KV_HEADS
    for kh in range(ATT_KV_HEADS):
        kk = kv[:, kh * LANES:(kh + 1) * LANES]
        parts = []
        sinks = []
        for j in range(heads_per_kv):
            h = kh * heads_per_kv + j
            qg = q[:, (h // 2) * LANES:(h // 2 + 1) * LANES]
            parts.append(jnp.where(low, qg, zero) if h % 2 == 0 else jnp.where(low, zero, qg))
            sinks.append(jnp.full((rows, 1), sink_ref[h] * LOG2E, F32))
        lhs = jnp.concatenate(parts, axis=0)
        sink = jnp.concatenate(sinks, axis=0)
        s = lax.dot_general(lhs, kk, (((1,), (1,)), ((), ())), preferred_element_type=F32)
        if bias is not None:
            s = s + bias
        m = jnp.maximum(jnp.max(s, axis=-1, keepdims=True), sink)
        e = jnp.exp2(s - m)
        ov = jnp.dot(e.astype(BF16), vv, preferred_element_type=F32)
        o = ov[:, 0:LANES] / (ov[:, LANES:2 * LANES] + jnp.exp2(sink - m))
        r = [o[j * rows:(j + 1) * rows] for j in range(heads_per_kv)]
        for p in range(heads_per_kv // 2):
            even, odd = r[2 * p], r[2 * p + 1]
            if kh == 0:
                groups.append(jnp.where(low, even, pltpu.roll(odd, HEAD_DIM, 1)))
            else:
                groups.append(jnp.where(low, pltpu.roll(even, HEAD_DIM, 1), odd))
    return jnp.concatenate(groups, axis=1)


def _attn_kernel(sink_ref, q_ref, kvp_ref, kvo_ref, kvn_ref, kvc_ref, bias0_ref, bias1_ref, o_ref):
    blk = ATT_BLOCK
    own = kvo_ref[...]
    ctx = kvc_ref[...]
    kv0 = jnp.concatenate([kvp_ref[...], own, ctx], axis=0)
    kv1 = jnp.concatenate([own, kvn_ref[...], ctx], axis=0)
    o_ref[0:blk, :] = _attend(q_ref[0:blk, :], kv0, sink_ref, bias0_ref[...]).astype(o_ref.dtype)
    o_ref[blk:2 * blk, :] = _attend(q_ref[blk:2 * blk, :], kv1, sink_ref, bias1_ref[...]).astype(o_ref.dtype)


def _window_bias(ctx_len):
    blk = ATT_BLOCK
    stacked = (ATT_HEADS // ATT_KV_HEADS) * blk
    qi = lax.broadcasted_iota(jnp.int32, (4, stacked, 3 * blk + ctx_len), 1) % blk
    ks = lax.broadcasted_iota(jnp.int32, (4, stacked, 3 * blk + ctx_len), 2)
    var = lax.broadcasted_iota(jnp.int32, (4, stacked, 3 * blk + ctx_len), 0)
    in_win = jnp.abs(ks - blk - qi) <= WINDOW
    in_seq = ((ks >= blk) | (var % 2 == 1)) & ((ks < 2 * blk) | (var >= 2))
    keep = (in_win & in_seq) | (ks >= 3 * blk)
    return jnp.where(keep, 0.0, -1e30).astype(F32)


def _ctx_attn_kernel(sink_ref, q_ref, kvc_ref, o_ref):
    o_ref[...] = _attend(q_ref[...], kvc_ref[...], sink_ref, None).astype(o_ref.dtype)


def _attn_call(qrot, kvrot, sink, n_batch, seq, ctx_len):
    nb = seq // ATT_BLOCK
    ctx0 = (n_batch * seq) // ctx_len
    kvw = kvrot.shape[1]
    bias = _window_bias(ctx_len)

    assert nb % 2 == 0
    pairs = nb // 2

    def variant0(b, m):
        return ((m > 0).astype(jnp.int32) + 2, 0, 0)

    def variant1(b, m):
        return (1 + 2 * (m < pairs - 1).astype(jnp.int32), 0, 0)

    blk = ATT_BLOCK
    return pl.pallas_call(
        _attn_kernel,
        grid=(n_batch, pairs),
        in_specs=[pl.BlockSpec(memory_space=pltpu.SMEM),
                  pl.BlockSpec((2 * blk, ATT_Q_WIDTH), lambda b, m: (b * pairs + m, 0)),
                  pl.BlockSpec((blk, kvw), lambda b, m: (b * nb + jnp.maximum(2 * m - 1, 0), 0)),
                  pl.BlockSpec((2 * blk, kvw), lambda b, m: (b * pairs + m, 0)),
                  pl.BlockSpec((blk, kvw), lambda b, m: (b * nb + jnp.minimum(2 * m + 2, nb - 1), 0)),
                  pl.BlockSpec((ctx_len, kvw), lambda b, m: (ctx0 + b, 0)),
                  pl.BlockSpec((None,) + bias.shape[1:], variant0),
                  pl.BlockSpec((None,) + bias.shape[1:], variant1)],
        out_specs=pl.BlockSpec((2 * blk, ATT_Q_WIDTH), lambda b, m: (b * pairs + m, 0)),
        out_shape=jax.ShapeDtypeStruct((n_batch * seq, ATT_Q_WIDTH), ACT_DTYPE),
        compiler_params=_params("arbitrary", "arbitrary"),
        name="window_attn",
    )(sink, qrot, kvrot, kvrot, kvrot, kvrot, bias, bias)


def _ctx_attn_call(qrot, kvrot, sink, n_batch, seq, ctx_len):
    ctx0 = (n_batch * seq) // ctx_len
    kvw = kvrot.shape[1]
    return pl.pallas_call(
        _ctx_attn_kernel,
        grid=(n_batch,),
        in_specs=[pl.BlockSpec(memory_space=pltpu.SMEM),
                  pl.BlockSpec((ctx_len, ATT_Q_WIDTH), lambda b: (ctx0 + b, 0)),
                  pl.BlockSpec((ctx_len, kvw), lambda b: (ctx0 + b, 0))],
        out_specs=pl.BlockSpec((ctx_len, ATT_Q_WIDTH), lambda b: (b, 0)),
        out_shape=jax.ShapeDtypeStruct((n_batch * ctx_len, ATT_Q_WIDTH), ACT_DTYPE),
        compiler_params=_params("arbitrary"),
        name="ctx_attn",
    )(sink, qrot, kvrot)


def _scan_rows(a, x, carry, reverse):
    rows, width = a.shape
    sub = lax.broadcasted_iota(jnp.int32, (rows, width), 0) % SUBLANES
    for s in (1, 2, 4):
        shift = rows - s if reverse else s
        a_sh = pltpu.roll(a, shift, 0)
        x_sh = pltpu.roll(x, shift, 0)
        keep = (sub <= SUBLANES - 1 - s) if reverse else (sub >= s)
        x = x + a * jnp.where(keep, x_sh, 0.0)
        a = a * jnp.where(keep, a_sh, 1.0)
    n_groups = rows // SUBLANES
    out = [None] * n_groups
    order = range(n_groups - 1, -1, -1) if reverse else range(n_groups)
    for g in order:
        hg = x[g * SUBLANES:(g + 1) * SUBLANES] + a[g * SUBLANES:(g + 1) * SUBLANES] * carry
        carry = hg[0:1] if reverse else hg[SUBLANES - 1:SUBLANES]
        out[g] = hg
    return jnp.concatenate(out, axis=0), carry


def _rglru_kernel(ul_ref, gl_ref, uc_ref, gc_ref, cw_ref, cb_ref, wr_ref, wi_ref, br_ref, bi_ref, lam_ref,
                  yl_ref, yc_ref, upad_ref, conv_ref, *, seq, ctx_len):
    width = ul_ref.shape[1]
    tc = LRU_CHUNK
    neg_lam = -lam_ref[...]
    softplus = jnp.maximum(neg_lam, 0.0) + jnp.log1p(jnp.exp(-jnp.abs(neg_lam)))
    zeros = jnp.zeros((SUBLANES, width), F32)

    def gates(u, d):
        ub = u.astype(BF16)
        r = jax.nn.sigmoid(jnp.dot(ub, wr_ref[d], preferred_element_type=F32) + br_ref[d:d + 1, :])
        i = jax.nn.sigmoid(jnp.dot(ub, wi_ref[d], preferred_element_type=F32) + bi_ref[d:d + 1, :])
        log_a = (-LRU_C) * r * softplus[d:d + 1, :]
        a = jnp.exp(log_a)
        y = 1.0 - a * a
        root = jnp.where(y > 0.0, y * lax.rsqrt(y), 0.0)
        return a, root * (i * u)

    def run(u_ref, g_ref, y_ref, n, h_fwd, h_bwd):
        n_chunks = n // tc
        upad_ref[0:SUBLANES, :] = zeros
        upad_ref[pl.ds(SUBLANES, n), :] = u_ref[...].astype(F32)
        upad_ref[pl.ds(SUBLANES + n, SUBLANES), :] = zeros

        def conv_body(c, carry):
            t0 = pl.multiple_of(c * tc, tc)
            win = upad_ref[pl.ds(t0, tc + 2 * SUBLANES), :]
            acc = cb_ref[...]
            for k in range(CONV_W):
                acc = acc + cw_ref[k:k + 1, :] * win[SUBLANES - 1 + k:SUBLANES - 1 + k + tc]
            conv_ref[pl.ds(t0, tc), :] = acc
            return carry

        lax.fori_loop(0, n_chunks, conv_body, 0)

        def fwd_body(c, carry):
            t0 = pl.multiple_of(c * tc, tc)
            a, x = gates(conv_ref[pl.ds(t0, tc), :], 0)
            h, carry = _scan_rows(a, x, carry, False)
            upad_ref[pl.ds(pl.multiple_of(t0 + SUBLANES, SUBLANES), tc), :] = h
            return carry

        h_fwd = lax.fori_loop(0, n_chunks, fwd_body, h_fwd, unroll=min(4, n_chunks))

        def bwd_body(c, carry):
            t0 = pl.multiple_of((n_chunks - 1 - c) * tc, tc)
            a, x = gates(conv_ref[pl.ds(t0, tc), :], 1)
            h, carry = _scan_rows(a, x, carry, True)
            h_both = upad_ref[pl.ds(pl.multiple_of(t0 + SUBLANES, SUBLANES), tc), :] + h
            y_ref[pl.ds(t0, tc), :] = (h_both * jax.nn.gelu(g_ref[pl.ds(t0, tc), :].astype(F32))).astype(y_ref.dtype)
            return carry

        h_bwd = lax.fori_loop(0, n_chunks, bwd_body, h_bwd, unroll=min(4, n_chunks))
        return h_fwd, h_bwd

    h0 = jnp.zeros((1, width), F32)
    h_fwd, h_bwd = run(uc_ref, gc_ref, yc_ref, ctx_len, h0, h0)
    run(ul_ref, gl_ref, yl_ref, seq, h_fwd, h_bwd)


def _rglru_call(proj, conv_w, conv_b, wr, wi, br, bi, lam, n_batch, seq, ctx_len):
    ctx0 = (n_batch * seq) // ctx_len
    halves = LRU_WIDTH // LRU_HALF
    w = LRU_HALF
    vec2 = pl.BlockSpec((2, w), lambda b, c: (0, c))
    return pl.pallas_call(
        functools.partial(_rglru_kernel, seq=seq, ctx_len=ctx_len),
        grid=(n_batch, halves),
        in_specs=[pl.BlockSpec((seq, w), lambda b, c: (b, c)),
                  pl.BlockSpec((seq, w), lambda b, c: (b, halves + c)),
                  pl.BlockSpec((ctx_len, w), lambda b, c: (ctx0 + b, c)),
                  pl.BlockSpec((ctx_len, w), lambda b, c: (ctx0 + b, halves + c)),
                  pl.BlockSpec((CONV_W, w), lambda b, c: (0, c)),
                  pl.BlockSpec((1, w), lambda b, c: (0, c)),
                  pl.BlockSpec((None, 2, w, w), lambda b, c: (c, 0, 0, 0)),
                  pl.BlockSpec((None, 2, w, w), lambda b, c: (c, 0, 0, 0)),
                  vec2, vec2, vec2],
        out_specs=[pl.BlockSpec((seq, w), lambda b, c: (b, c)),
                   pl.BlockSpec((ctx_len, w), lambda b, c: (b, c))],
        out_shape=[jax.ShapeDtypeStruct((n_batch * seq, LRU_WIDTH), ACT_DTYPE),
                   jax.ShapeDtypeStruct((n_batch * ctx_len, LRU_WIDTH), ACT_DTYPE)],
        scratch_shapes=[pltpu.VMEM((seq + 2 * SUBLANES, w), F32), pltpu.VMEM((seq, w), F32)],
        compiler_params=_params("arbitrary", "arbitrary"),
        name="rglru",
    )(proj, proj, proj, proj, conv_w, conv_b, wr, wi, br, bi, lam)


def _split2(x):
    hi = x.astype(BF16)
    return hi, (x - hi.astype(F32)).astype(BF16)


def _gla_kernel(q_ref, k_ref, v_ref, r_ref, a_ref, kc_ref, vc_ref, ac_ref, gw_ref, gb_ref, ng_ref,
                o_ref, of_ref, ob_ref, qd_ref, ki_ref, ke_ref, la_ref, eg_ref, st_ref, *, seq, ctx_len):
    ch = GLA_CHUNK
    blk = GLA_BLOCK
    per_blk = blk // ch
    row = lax.broadcasted_iota(jnp.int32, (blk, blk), 0)
    col = lax.broadcasted_iota(jnp.int32, (blk, blk), 1)
    same = (row // ch) == (col // ch)
    keep = (same & (row >= col), same & (row <= col))
    tri = (keep[0].astype(BF16),)
    scale = GLA_DK ** -0.5
    nt = (((1,), (1,)), ((), ()))
    tn = (((0,), (0,)), ((), ()))

    def mm(a, b):
        return jnp.dot(a, b, preferred_element_type=F32)

    def prepare(a_src, k_src, v_src, q_src, dst, n_rows, d):
        n_blk = n_rows // blk
        unroll = min(4, n_blk)

        def decay_body(i, carry):
            t0 = pl.multiple_of(i * blk, blk)
            both = mm(a_src[pl.ds(t0, blk), :], gw_ref[d])
            pre = (both[:, 0:GLA_DK] + both[:, GLA_DK:2 * GLA_DK]) + gb_ref[d:d + 1, :]
            la_ref[pl.ds(t0, blk), :] = (jnp.minimum(pre, 0.0) - jnp.log(1.0 + jnp.exp(-jnp.abs(pre)))) / GLA_TAU
            return carry

        lax.fori_loop(0, n_blk, decay_body, 0, unroll=unroll)

        def scale_body(i, carry):
            t0 = pl.multiple_of(i * blk, blk)
            log_a = la_ref[pl.ds(t0, blk), :]
            sums = mm(tri[0], jnp.concatenate(_split2(log_a), axis=1))
            prefix = sums[:, 0:GLA_DK] + sums[:, GLA_DK:2 * GLA_DK]
            total = jnp.concatenate(
                [jnp.broadcast_to(prefix[(j + 1) * ch - 1:(j + 1) * ch], (ch, GLA_DK)) for j in range(per_blk)],
                axis=0)
            suffix = total - prefix + log_a
            b = prefix if d == 0 else suffix
            to_end = (suffix if d == 0 else prefix) - log_a
            for j in range(per_blk):
                eg_ref[pl.ds(i * per_blk + j, 1), :] = jnp.exp(total[j * ch:j * ch + 1])
            k_blk = k_src[pl.ds(t0, blk), :].astype(F32)
            ke_ref[pl.ds(t0, blk), :] = (k_blk * jnp.exp(to_end)).astype(BF16)
            if q_src is not None:
                q_blk = q_src[pl.ds(t0, blk), :].astype(F32)
                qd_ref[pl.ds(t0, blk), :] = ((q_blk * scale) * jnp.exp(b)).astype(BF16)
                ki_ref[pl.ds(t0, blk), :] = (k_blk * jnp.exp(-b)).astype(BF16)
            return carry

        lax.fori_loop(0, n_blk, scale_body, 0, unroll=unroll)
        if q_src is None:
            return

        def intra_body(i, carry):
            t0 = pl.multiple_of(i * blk, blk)
            s = lax.dot_general(qd_ref[pl.ds(t0, blk), :], ki_ref[pl.ds(t0, blk), :], nt,
                                preferred_element_type=F32)
            p = jnp.where(keep[d], s, 0.0).astype(BF16)
            dst[pl.ds(t0, blk), :] = mm(p, v_src[pl.ds(t0, blk), :].astype(BF16))
            return carry

        lax.fori_loop(0, n_blk, intra_body, 0, unroll=min(8, n_blk))

    def recur(v_src, dst, n_rows, d):
        n_chunks = n_rows // ch

        def body(c, carry):
            n = c if d == 0 else n_chunks - 1 - c
            t0 = pl.multiple_of(n * ch, ch)
            st = st_ref[...]
            if dst is not None:
                dst[pl.ds(t0, ch), :] = dst[pl.ds(t0, ch), :] + lax.dot_general(
                    qd_ref[pl.ds(t0, ch), :], st.astype(BF16), nt, preferred_element_type=F32)
            kv = lax.dot_general(v_src[pl.ds(t0, ch), :].astype(BF16), ke_ref[pl.ds(t0, ch), :], tn,
                                 preferred_element_type=F32)
            st_ref[...] = st * eg_ref[pl.ds(n, 1), :] + kv
            return carry

        lax.fori_loop(0, n_chunks, body, 0, unroll=min(16, n_chunks))

    for d, dst in ((0, of_ref), (1, ob_ref)):
        st_ref[...] = jnp.zeros(st_ref.shape, F32)
        prepare(ac_ref, kc_ref, vc_ref, None, None, ctx_len, d)
        recur(vc_ref, None, ctx_len, d)
        prepare(a_ref, k_ref, v_ref, q_ref, dst, seq, d)
        recur(v_ref, dst, seq, d)

    def finish_body(c, carry):
        t0 = pl.multiple_of(c * blk, blk)
        o = of_ref[pl.ds(t0, blk), :] + ob_ref[pl.ds(t0, blk), :]
        on = (o * lax.rsqrt(jnp.mean(o * o, axis=-1, keepdims=True) + EPS)) * ng_ref[...]
        r = r_ref[pl.ds(t0, blk), :].astype(F32)
        o_ref[pl.ds(t0, blk), :] = (on * (r * jax.nn.sigmoid(r))).astype(o_ref.dtype)
        return carry

    lax.fori_loop(0, seq // blk, finish_body, 0)


def _gla_call(proj, gate_w, gate_b, norm_g, n_batch, seq, ctx_len):
    ctx0 = (n_batch * seq) // ctx_len
    kcol = GLA_K_WIDTH // GLA_DK
    vcol = (2 * GLA_K_WIDTH) // GLA_DV
    rcol = (2 * GLA_K_WIDTH + GLA_V_WIDTH) // GLA_DV
    acol = (2 * GLA_K_WIDTH + 2 * GLA_V_WIDTH) // LANES
    return pl.pallas_call(
        functools.partial(_gla_kernel, seq=seq, ctx_len=ctx_len),
        grid=(n_batch, GLA_HEADS),
        in_specs=[pl.BlockSpec((seq, GLA_DK), lambda b, h: (b, h)),
                  pl.BlockSpec((seq, GLA_DK), lambda b, h: (b, kcol + h)),
                  pl.BlockSpec((seq, GLA_DV), lambda b, h: (b, vcol + h)),
                  pl.BlockSpec((seq, GLA_DV), lambda b, h: (b, rcol + h)),
                  pl.BlockSpec((seq, LANES), lambda b, h: (b, acol)),
                  pl.BlockSpec((ctx_len, GLA_DK), lambda b, h: (ctx0 + b, kcol + h)),
                  pl.BlockSpec((ctx_len, GLA_DV), lambda b, h: (ctx0 + b, vcol + h)),
                  pl.BlockSpec((ctx_len, LANES), lambda b, h: (ctx0 + b, acol)),
                  pl.BlockSpec((None, 2, LANES, 2 * GLA_DK), lambda b, h: (h, 0, 0, 0)),
                  pl.BlockSpec((None, 2, GLA_DK), lambda b, h: (h, 0, 0)),
                  pl.BlockSpec((1, GLA_DV), lambda b, h: (0, 0))],
        out_specs=pl.BlockSpec((seq, GLA_DV), lambda b, h: (b, h)),
        out_shape=jax.ShapeDtypeStruct((n_batch * seq, GLA_V_WIDTH), ACT_DTYPE),
        scratch_shapes=[pltpu.VMEM((seq, GLA_DV), F32),
                        pltpu.VMEM((seq, GLA_DV), F32),
                        pltpu.VMEM((seq, GLA_DK), BF16),
                        pltpu.VMEM((seq, GLA_DK), BF16),
                        pltpu.VMEM((seq, GLA_DK), BF16),
                        pltpu.VMEM((seq, GLA_DK), F32),
                        pltpu.VMEM((seq // GLA_CHUNK, GLA_DK), F32),
                        pltpu.VMEM((GLA_DV, GLA_DK), F32)],
        compiler_params=_params("arbitrary", "arbitrary"),
        name="gla",
    )(proj, proj, proj, proj, proj, proj, proj, proj, gate_w, gate_b, norm_g)


def _route(rows, rows_bf16, rw_ref, rb_ref):
    n_rows = rows.shape[0]
    rows_lo = (rows - rows_bf16.astype(F32)).astype(BF16)
    both = jnp.dot(rows_bf16, rw_ref[...], preferred_element_type=F32)
    logits_rows = (both[:, 0:LANES] + both[:, LANES:2 * LANES]
                   + jnp.dot(rows_lo, rw_ref[:, 0:LANES], preferred_element_type=F32))
    logits = logits_rows.T[0:N_EXPERTS, :]
    scores = jax.nn.sigmoid(logits)
    sel = scores + rb_ref[...]
    eid = lax.broadcasted_iota(jnp.int32, (N_EXPERTS, n_rows), 0)
    gid = eid // EXPERTS_PER_GROUP
    neg = -jnp.inf
    big = N_EXPERTS

    def top2(masked):
        m1 = jnp.max(masked, axis=0, keepdims=True)
        i1 = jnp.min(jnp.where(masked == m1, eid, big), axis=0, keepdims=True)
        rest = jnp.where(eid == i1, neg, masked)
        m2 = jnp.max(rest, axis=0, keepdims=True)
        i2 = jnp.min(jnp.where(rest == m2, eid, big), axis=0, keepdims=True)
        return m1, i1, m2, i2

    best = None
    best_g = None
    for g in range(N_GROUPS):
        m1, _, m2, _ = top2(jnp.where(gid == g, sel, neg))
        gs = m1 + m2
        if best is None:
            best, best_g = gs, jnp.zeros((1, n_rows), jnp.int32)
        else:
            better = gs > best
            best_g = jnp.where(better, g, best_g)
            best = jnp.where(better, gs, best)
    _, i1, _, i2 = top2(jnp.where(gid == best_g, sel, neg))
    s1 = jnp.sum(jnp.where(eid == i1, scores, 0.0), axis=0, keepdims=True)
    s2 = jnp.sum(jnp.where(eid == i2, scores, 0.0), axis=0, keepdims=True)
    total = s1 + s2
    swap = i2 < i1
    lo = jnp.where(swap, i2, i1)
    hi = jnp.where(swap, i1, i2)
    w_lo = jnp.where(swap, s2, s1) / total
    w_hi = jnp.where(swap, s1, s2) / total
    pad = jnp.zeros((4, n_rows), F32)
    return jnp.concatenate([lo.astype(F32), hi.astype(F32), w_lo, w_hi, pad], axis=0)


def _finish_rows(y, h, mod_ref, g_ref, rw_ref, rb_ref, hn_ref, rows_ref, rt_ref, rc_ref):
    hn = h + mod_ref[2:3, :] * y
    hn_ref[...] = hn
    rows = _modulated_norm(hn, g_ref[...], mod_ref[3:4, :], mod_ref[4:5, :])
    rows_bf16 = rows.astype(BF16)
    rows_ref[...] = rows_bf16
    route = _route(rows, rows_bf16, rw_ref, rb_ref)
    rt_ref[...] = route
    padded = jnp.concatenate([route, jnp.zeros((LANES - SUBLANES, route.shape[1]), F32)], axis=0)
    rc_ref[...] = padded.T


def _outproj_even_kernel(al_ref, ac_ref, bl_ref, bc_ref, w_ref, hl_ref, hc_ref, mod_ref, g_ref, rw_ref, rb_ref,
                         hn_ref, rows_ref, rt_ref, rc_ref, *, n_lat_tiles):
    a = _two_source(al_ref, ac_ref, n_lat_tiles).astype(BF16)
    b = _two_source(bl_ref, bc_ref, n_lat_tiles).astype(BF16)
    y = (jnp.dot(a, w_ref[0:LRU_WIDTH, :], preferred_element_type=F32)
         + jnp.dot(b, w_ref[LRU_WIDTH:LRU_WIDTH + ATT_Q_WIDTH, :], preferred_element_type=F32))
    h = _two_source(hl_ref, hc_ref, n_lat_tiles)
    _finish_rows(y, h, mod_ref, g_ref, rw_ref, rb_ref, hn_ref, rows_ref, rt_ref, rc_ref)


def _outproj_odd_kernel(o_ref, w_ref, h_ref, mod_ref, g_ref, rw_ref, rb_ref, hn_ref, rows_ref, rt_ref, rc_ref):
    y = jnp.dot(o_ref[...].astype(BF16), w_ref[...], preferred_element_type=F32)
    _finish_rows(y, h_ref[...], mod_ref, g_ref, rw_ref, rb_ref, hn_ref, rows_ref, rt_ref, rc_ref)


def _outproj_specs(n_rows, tiles_per_seq, n_batch):
    d = D_MODEL
    row = pl.BlockSpec((ROW_TILE, d), lambda i: (i, 0))
    tail_in = [_mod_spec(tiles_per_seq, n_batch),
               pl.BlockSpec((1, d), lambda i: (0, 0)),
               pl.BlockSpec((d, 2 * LANES), lambda i: (0, 0)),
               pl.BlockSpec((N_EXPERTS, 1), lambda i: (0, 0))]
    out_specs = [row, row, pl.BlockSpec((SUBLANES, ROW_TILE), lambda i: (0, i)),
                 pl.BlockSpec((ROW_TILE, LANES), lambda i: (i, 0))]
    out_shape = [jax.ShapeDtypeStruct((n_rows, d), F32), jax.ShapeDtypeStruct((n_rows, d), ACT_DTYPE),
                 jax.ShapeDtypeStruct((SUBLANES, n_rows), F32), jax.ShapeDtypeStruct((n_rows, LANES), F32)]
    return tail_in, out_specs, out_shape


def _outproj_even_call(a_lat, a_ctx, b_lat, b_ctx, w, h_lat, h_ctx, mod, g, rw_t, rb, tiles_per_seq, n_batch):
    n_rows = h_lat.shape[0] + h_ctx.shape[0]
    n_lat_tiles = a_lat.shape[0] // ROW_TILE
    tail_in, out_specs, out_shape = _outproj_specs(n_rows, tiles_per_seq, n_batch)
    lat, ctx = _two_source_specs(LRU_WIDTH, n_lat_tiles)
    h_specs = list(_two_source_specs(D_MODEL, n_lat_tiles))
    return pl.pallas_call(
        functools.partial(_outproj_even_kernel, n_lat_tiles=n_lat_tiles),
        grid=(n_rows // ROW_TILE,),
        in_specs=[lat, ctx, lat, ctx, pl.BlockSpec(w.shape, lambda i: (0, 0))] + h_specs + tail_in,
        out_specs=out_specs,
        out_shape=out_shape,
        compiler_params=_params("arbitrary"),
        name="out_proj_even",
    )(a_lat, a_ctx, b_lat, b_ctx, w, h_lat, h_ctx, mod, g, rw_t, rb)


def _outproj_odd_call(o, w, h, mod, g, rw_t, rb, tiles_per_seq, n_batch):
    n_rows = o.shape[0]
    tail_in, out_specs, out_shape = _outproj_specs(n_rows, tiles_per_seq, n_batch)
    return pl.pallas_call(
        _outproj_odd_kernel,
        grid=(n_rows // ROW_TILE,),
        in_specs=[pl.BlockSpec((ROW_TILE, o.shape[1]), lambda i: (i, 0)),
                  pl.BlockSpec(w.shape, lambda i: (0, 0)),
                  pl.BlockSpec((ROW_TILE, D_MODEL), lambda i: (i, 0))] + tail_in,
        out_specs=out_specs,
        out_shape=out_shape,
        compiler_params=_params("arbitrary"),
        name="out_proj_odd",
    )(o, w, h, mod, g, rw_t, rb)


X_SLAB = D_MODEL // LANES
Y_SLAB = 2 * X_SLAB


def _for_rows(n_rows, fn):
    for g in range(n_rows // SUBLANES):
        for u in range(SUBLANES):
            fn(g, u)


def _dispatch_kernel(tail_ref, slot_ref, rows_ref, xs_ref, slab_ref, zero_ref, sem, zero_sem, *, steps, n_tiles):
    i = pl.program_id(0)
    k = i % 2
    rt = MOE_ROWS
    tile_rows = MOE_TILE * X_SLAB

    def slab_copy(buf, g, u, slot):
        return pltpu.make_async_copy(
            slab_ref.at[buf, g, :, pl.ds(u, 1), :],
            xs_ref.at[pl.ds(pl.multiple_of(slot * X_SLAB, X_SLAB), X_SLAB)], sem.at[buf])

    def wait_all(buf):
        pltpu.make_async_copy(slab_ref.at[buf], slab_ref.at[buf], sem.at[buf]).wait()

    @pl.when(i == 0)
    def _():
        zero_ref[...] = jnp.zeros(zero_ref.shape, F32)

        def fill(t):
            start = pl.multiple_of(t * tile_rows, tile_rows)
            return pltpu.make_async_copy(zero_ref, xs_ref.at[pl.ds(start, tile_rows)], zero_sem)

        n_used = tail_ref[1, N_CLASSES - 1]
        for phase in ("start", "wait"):
            for c in range(N_CLASSES):
                for cond, t in ((tail_ref[0, c] > 0, tail_ref[1, c] - 1), (n_used + c < n_tiles, n_used + c)):
                    @pl.when(cond)
                    def _():
                        getattr(fill(t), phase)()

    @pl.when(i >= 2)
    def _():
        wait_all(k)

    for j in range(X_SLAB):
        lanes = rows_ref[:, j * LANES:(j + 1) * LANES].astype(F32)
        slab_ref[k, :, j, :, :] = lanes.reshape(rt // SUBLANES, SUBLANES, LANES)

    _for_rows(rt, lambda g, u: slab_copy(k, g, u, slot_ref[0, 0, g * SUBLANES + u]).start())

    @pl.when(i == steps - 1)
    def _():
        wait_all(k)
        if steps >= 2:
            wait_all(1 - k)


def _dispatch_call(rows, slots, tails, n_slots):
    n, d = rows.shape
    steps = n // MOE_ROWS
    grid_spec = pltpu.PrefetchScalarGridSpec(
        num_scalar_prefetch=1,
        grid=(steps,),
        in_specs=[pl.BlockSpec((1, 1, MOE_ROWS), lambda i, tails: (i, 0, 0), memory_space=pltpu.SMEM),
                  pl.BlockSpec((MOE_ROWS, d), lambda i, tails: (i, 0))],
        out_specs=pl.BlockSpec(memory_space=pl.ANY),
        scratch_shapes=[pltpu.VMEM((2, MOE_ROWS // SUBLANES, X_SLAB, SUBLANES, LANES), F32),
                        pltpu.VMEM((MOE_TILE * X_SLAB, 1, LANES), F32),
                        pltpu.SemaphoreType.DMA((2,)), pltpu.SemaphoreType.DMA(())],
    )
    xs = pl.pallas_call(
        functools.partial(_dispatch_kernel, steps=steps, n_tiles=n_slots // MOE_TILE),
        grid_spec=grid_spec,
        out_shape=jax.ShapeDtypeStruct((n_slots * X_SLAB, 1, LANES), F32),
        compiler_params=_params("arbitrary"),
        name="moe_dispatch",
    )(tails, slots.reshape(steps, 1, MOE_ROWS), rows)
    return xs.reshape(n_slots * X_SLAB, LANES)


def _experts_kernel(elo_ref, ehi_ref, used_ref, x_ref, wg0_ref, wu0_ref, wd0_ref,
                    wg1_ref, wu1_ref, wd1_ref, y_ref, cg0_ref, cu0_ref, cd0_ref, cg1_ref, cu1_ref, cd1_ref):
    t = pl.program_id(0)
    tm = MOE_TILE
    prev = jnp.maximum(t - 1, 0)
    sides = ((elo_ref, (wg0_ref, wu0_ref, wd0_ref), (cg0_ref, cu0_ref, cd0_ref)),
             (ehi_ref, (wg1_ref, wu1_ref, wd1_ref), (cg1_ref, cu1_ref, cd1_ref)))

    @pl.when(t < used_ref[0])
    def _():
        for ids, weights, cached in sides:
            @pl.when((t == 0) | (ids[t] != ids[prev]))
            def _():
                for w_ref, c_ref in zip(weights, cached):
                    c_ref[...] = w_ref[...].astype(BF16)

        x = jnp.concatenate([x_ref[pl.ds(j, tm, stride=X_SLAB), :] for j in range(X_SLAB)],
                            axis=1).astype(BF16)
        for e, (_, _, (cg_ref, cu_ref, cd_ref)) in enumerate(sides):
            gate = jnp.dot(x, cg_ref[...], preferred_element_type=F32)
            up = jnp.dot(x, cu_ref[...], preferred_element_type=F32)
            act = ((gate * jax.nn.sigmoid(gate)) * up).astype(BF16)
            f = jnp.dot(act, cd_ref[...], preferred_element_type=F32)
            for j in range(X_SLAB):
                y_ref[pl.ds(e * X_SLAB + j, tm, stride=Y_SLAB), :] = f[:, j * LANES:(j + 1) * LANES]

    @pl.when(t >= used_ref[0])
    def _():
        y_ref[...] = jnp.zeros_like(y_ref)


def _experts_call(xs, e_lo, e_hi, n_used, w_gate, w_up, w_down, layer):
    d = D_MODEL
    tiles = xs.shape[0] // (MOE_TILE * X_SLAB)

    def xmap(t, elo, ehi, used):
        return (jnp.maximum(jnp.minimum(t, used[0] - 1), 0), 0)

    def lo(t, elo, ehi, used):
        return (layer, elo[t], 0, 0)

    def hi(t, elo, ehi, used):
        return (layer, ehi[t], 0, 0)

    up_spec = lambda m: pl.BlockSpec((None, None, d, D_EXPERT), m)
    down_spec = lambda m: pl.BlockSpec((None, None, D_EXPERT, d), m)
    cached = [pltpu.VMEM((d, D_EXPERT), BF16), pltpu.VMEM((d, D_EXPERT), BF16), pltpu.VMEM((D_EXPERT, d), BF16)]
    grid_spec = pltpu.PrefetchScalarGridSpec(
        num_scalar_prefetch=3,
        grid=(tiles,),
        in_specs=[pl.BlockSpec((MOE_TILE * X_SLAB, LANES), xmap),
                  up_spec(lo), up_spec(lo), down_spec(lo),
                  up_spec(hi), up_spec(hi), down_spec(hi)],
        out_specs=pl.BlockSpec((MOE_TILE * Y_SLAB, LANES), lambda t, elo, ehi, used: (t, 0)),
        scratch_shapes=cached + cached,
    )
    return pl.pallas_call(
        _experts_kernel,
        grid_spec=grid_spec,
        out_shape=jax.ShapeDtypeStruct((tiles * MOE_TILE * Y_SLAB, LANES), F32),
        compiler_params=_params("arbitrary"),
        name="experts",
    )(e_lo, e_hi, n_used, xs, w_gate, w_up, w_down, w_gate, w_up, w_down)


def _collect_kernel(slot_ref, next_slot_ref, ys_ref, rc_ref, *rest, steps, residual):
    if residual:
        h_ref, mod_ref, o_ref, buf_ref, sem = rest
    else:
        o_ref, buf_ref, sem = rest
    i = pl.program_id(0)
    k = i % 2
    rt = MOE_ROWS

    def slab_copy(buf, g, u, slot):
        return pltpu.make_async_copy(
            ys_ref.at[pl.ds(pl.multiple_of(slot * Y_SLAB, Y_SLAB), Y_SLAB)],
            buf_ref.at[buf, g, :, pl.ds(u, 1), :], sem.at[buf])

    def start_all(slots, buf):
        _for_rows(rt, lambda g, u: slab_copy(buf, g, u, slots[0, 0, g * SUBLANES + u]).start())

    @pl.when(i == 0)
    def _():
        start_all(slot_ref, 0)

    @pl.when(i + 1 < steps)
    def _():
        start_all(next_slot_ref, 1 - k)

    pltpu.make_async_copy(buf_ref.at[k], buf_ref.at[k], sem.at[k]).wait()

    rc = rc_ref[...]
    w_lo = rc[:, 2:3]
    w_hi = rc[:, 3:4]
    for j in range(X_SLAB):
        cols = slice(j * LANES, (j + 1) * LANES)
        f = (w_lo * buf_ref[k, :, j, :, :].reshape(rt, LANES)
             + w_hi * buf_ref[k, :, X_SLAB + j, :, :].reshape(rt, LANES))
        if residual:
            o_ref[:, cols] = h_ref[:, cols] + mod_ref[5:6, cols] * f
        else:
            o_ref[:, cols] = f


def _collect_call(ys, slots, route_cols, residual=None):
    n = slots.shape[0]
    d = D_MODEL
    steps = n // MOE_ROWS
    slots3 = slots.reshape(steps, 1, MOE_ROWS)
    row = pl.BlockSpec((MOE_ROWS, d), lambda i: (i, 0))
    in_specs = [pl.BlockSpec((1, 1, MOE_ROWS), lambda i: (i, 0, 0), memory_space=pltpu.SMEM),
                pl.BlockSpec((1, 1, MOE_ROWS), lambda i: (jnp.minimum(i + 1, steps - 1), 0, 0),
                             memory_space=pltpu.SMEM),
                pl.BlockSpec(memory_space=pl.ANY),
                pl.BlockSpec((MOE_ROWS, LANES), lambda i: (i, 0))]
    args = [slots3, slots3, ys.reshape(ys.shape[0], 1, LANES), route_cols]
    if residual is not None:
        h, mod, tiles_per_seq, n_batch = residual
        in_specs += [row, _mod_spec(tiles_per_seq, n_batch)]
        args += [h, mod]
    return pl.pallas_call(
        functools.partial(_collect_kernel, steps=steps, residual=residual is not None),
        grid=(steps,),
        in_specs=in_specs,
        out_specs=row,
        out_shape=jax.ShapeDtypeStruct((n, d), F32),
        scratch_shapes=[pltpu.VMEM((2, MOE_ROWS // SUBLANES, Y_SLAB, SUBLANES, LANES), F32),
                        pltpu.SemaphoreType.DMA((2,))],
        compiler_params=_params("arbitrary"),
        name="moe_collect",
    )(*args)


def _moe(rows, route, route_cols, w_gate, w_up, w_down, layer, residual=None):
    n = rows.shape[0]
    tm = MOE_TILE
    assert n % MOE_ROWS == 0
    max_tiles = -(-n // tm) + N_CLASSES
    p = max_tiles * tm
    lo = route[0].astype(jnp.int32)
    hi = route[1].astype(jnp.int32)
    lo_in = lo % EXPERTS_PER_GROUP
    hi_in = hi % EXPERTS_PER_GROUP
    pair = lo_in * 3 - (lo_in * (lo_in - 1)) // 2 + (hi_in - lo_in - 1)
    cls = (lo // EXPERTS_PER_GROUP) * PAIRS_PER_GROUP + pair
    onehot = (cls[:, None] == jnp.arange(N_CLASSES, dtype=jnp.int32)[None, :]).astype(jnp.int32)
    counts = jnp.sum(onehot, axis=0)
    rank = jnp.sum(jnp.cumsum(onehot, axis=0) * onehot, axis=1) - 1
    tiles_c = (counts + tm - 1) // tm
    tile_end = jnp.cumsum(tiles_c)
    tile_start = tile_end - tiles_c
    n_used = tile_end[-1]
    slot_of_token = (tile_start[cls] * tm + rank).astype(jnp.int32)
    tile_ids = jnp.arange(max_tiles, dtype=jnp.int32)
    tile_cls = jnp.minimum(jnp.sum((tile_end[None, :] <= tile_ids[:, None]).astype(jnp.int32), axis=1),
                           N_CLASSES - 1)
    group0 = (tile_cls // PAIRS_PER_GROUP) * EXPERTS_PER_GROUP
    e_lo = group0 + jnp.asarray(PAIR_LO, jnp.int32)[tile_cls % PAIRS_PER_GROUP]
    e_hi = group0 + jnp.asarray(PAIR_HI, jnp.int32)[tile_cls % PAIRS_PER_GROUP]
    tails = jnp.stack([tiles_c, tile_end]).astype(jnp.int32)
    xs = _dispatch_call(rows, slot_of_token, tails, p)
    ys = _experts_call(xs, e_lo.astype(jnp.int32), e_hi.astype(jnp.int32),
                       n_used.reshape(1).astype(jnp.int32), w_gate, w_up, w_down, layer)
    return _collect_call(ys, slot_of_token, route_cols, residual)


def _rope_tables(seq):
    rows = seq // GRID_W
    row = jnp.repeat(jnp.arange(rows), GRID_W).astype(F32)
    col = jnp.tile(jnp.arange(GRID_W), rows).astype(F32)
    n_freq = HEAD_DIM // 4
    inv_freq = ROPE_BASE ** (-jnp.arange(n_freq, dtype=F32) / n_freq)
    ang_r = row[:, None] * inv_freq
    ang_c = col[:, None] * inv_freq
    cos = jnp.concatenate([jnp.cos(ang_r), jnp.cos(ang_r), jnp.cos(ang_c), jnp.cos(ang_c)], axis=1)
    sin = jnp.concatenate([-jnp.sin(ang_r), jnp.sin(ang_r), -jnp.sin(ang_c), jnp.sin(ang_c)], axis=1)
    cos = jnp.concatenate([jnp.tile(cos, (1, 2)), jnp.ones((ROW_TILE, LANES), F32)], axis=0)
    sin = jnp.concatenate([jnp.tile(sin, (1, 2)), jnp.zeros((ROW_TILE, LANES), F32)], axis=0)
    return cos, sin


def _block_diag_halves(w):
    per_half = LRU_HALF // LRU_BLOCK
    halves = LRU_WIDTH // LRU_HALF
    out = jnp.zeros((halves, 2, LRU_HALF, LRU_HALF), F32)
    for c in range(halves):
        for j in range(per_half):
            s = slice(j * LRU_BLOCK, (j + 1) * LRU_BLOCK)
            out = out.at[c, :, s, s].set(w[:, c * per_half + j])
    return out.astype(BF16)


def _gla_gate_weights(gate_up, gate_b):
    w = jnp.zeros((GLA_HEADS, 2, LANES, GLA_DK), F32)
    for d in range(2):
        wd = gate_up[d].reshape(GLA_LOWRANK, GLA_HEADS, GLA_DK).transpose(1, 0, 2)
        w = w.at[:, d, d * GLA_LOWRANK:(d + 1) * GLA_LOWRANK, :].set(wd)
    b = gate_b.reshape(2, GLA_HEADS, GLA_DK).transpose(1, 0, 2)
    w_hi = w.astype(BF16)
    w_lo = (w - w_hi.astype(F32)).astype(BF16)
    return jnp.concatenate([w_hi, w_lo], axis=-1), b


def kernel(x, c, ctx, c_ctx, router_w, router_bias, ada_w, ada_b, norm_mix_g, norm_ffn_g, moe_w_gate, moe_w_up, moe_w_down, ev_w_in, ev_w_out, ev_conv_w, ev_conv_b, ev_rg_w, ev_rg_b, ev_ig_w, ev_ig_b, ev_lambda, ev_q_norm_g, ev_k_norm_g, ev_sink, od_w_in, od_w_out, od_gate_up, od_gate_b, od_out_norm_g):
    n_batch, seq, d = x.shape
    ctx_len = ctx.shape[1]
    n_lat = n_batch * seq
    n_ctx = n_batch * ctx_len
    assert ada_w.shape[0] == 2 and d == D_MODEL
    assert seq % ROW_TILE == 0 and n_ctx % ROW_TILE == 0 and seq % ctx_len == 0
    assert ctx_len % LRU_CHUNK == 0 and seq % GLA_BLOCK == 0 and ctx_len % GLA_BLOCK == 0
    tiles_per_seq = seq // ROW_TILE
    n_lat_tiles = n_lat // ROW_TILE

    x_lat = x.reshape(n_lat, d)
    x_ctx = ctx.reshape(n_ctx, d)
    mod_rows = -(-(n_batch + 1) // SUBLANES) * SUBLANES
    cvec = jnp.zeros((mod_rows, d), F32).at[:n_batch].set(c).at[n_batch].set(c_ctx)
    mods = _ada_call(cvec, ada_w, ada_b).reshape(2, mod_rows, 6, d)
    mod_mix = [mods[l][:n_batch + 1, 0:6] for l in range(2)]
    rw_pad = jnp.pad(router_w, ((0, 0), (0, LANES - N_EXPERTS)))
    rw_hi = rw_pad.astype(BF16)
    rw_t = jnp.concatenate([rw_hi, (rw_pad - rw_hi.astype(F32)).astype(BF16)], axis=1)
    rb = router_bias.reshape(N_EXPERTS, 1)

    w_in0 = ev_w_in[0].astype(BF16)
    w_out0 = ev_w_out[0].astype(BF16)
    proj0 = _proj_call(x_lat, x_ctx, mod_mix[0], norm_mix_g[0:1], w_in0, tiles_per_seq, n_batch)
    cos_t, sin_t = _rope_tables(seq)
    seg = np.kron(np.eye(LANES // HEAD_DIM, dtype=np.float32), np.full((HEAD_DIM, HEAD_DIM), 1.0 / HEAD_DIM, np.float32))
    seg = jnp.asarray(np.concatenate([seg, seg], axis=0)).astype(BF16)
    qrot, kvrot = _qkprep_call(proj0, cos_t, sin_t, jnp.tile(ev_q_norm_g[0], 2)[None, :],
                               jnp.tile(ev_k_norm_g[0], 2)[None, :], seg,
                               tiles_per_seq, n_lat_tiles)
    a_lat, a_ctx = _rglru_call(proj0, ev_conv_w[0], ev_conv_b[0:1], _block_diag_halves(ev_rg_w[0]),
                               _block_diag_halves(ev_ig_w[0]), ev_rg_b[0], ev_ig_b[0], ev_lambda[0],
                               n_batch, seq, ctx_len)
    b_lat = _attn_call(qrot, kvrot, ev_sink[0], n_batch, seq, ctx_len)
    b_ctx = _ctx_attn_call(qrot, kvrot, ev_sink[0], n_batch, seq, ctx_len)
    h, rows, route, route_cols = _outproj_even_call(a_lat, a_ctx, b_lat, b_ctx, w_out0, x_lat, x_ctx, mod_mix[0],
                                                    norm_ffn_g[0:1], rw_t, rb, tiles_per_seq, n_batch)
    ffn = _moe(rows, route, route_cols, moe_w_gate, moe_w_up, moe_w_down, 0)

    w_in1 = jnp.pad(od_w_in[0], ((0, 0), (0, ODD_IN_PAD - ODD_IN))).astype(BF16)
    h, proj1 = _proj_res_call(h, ffn, mod_mix[0], mod_mix[1], norm_mix_g[1:2], w_in1, tiles_per_seq, n_batch)
    gate_w, gate_b = _gla_gate_weights(od_gate_up[0], od_gate_b[0])
    o = _gla_call(proj1, gate_w, gate_b, od_out_norm_g[0:1], n_batch, seq, ctx_len)
    h_lat, rows, route, route_cols = _outproj_odd_call(o, od_w_out[0].astype(BF16), h, mod_mix[1],
                                                       norm_ffn_g[1:2], rw_t, rb, tiles_per_seq, n_batch)
    out = _moe(rows, route, route_cols, moe_w_gate, moe_w_up, moe_w_down, 1,
               residual=(h_lat, mod_mix[1], seq // MOE_ROWS, n_batch))
    return out.reshape(n_batch, seq, d)
```

```python
import functools

import jax
import jax.numpy as jnp
import numpy as np
from jax import lax
from jax.experimental import pallas as pl
from jax.experimental.pallas import tpu as pltpu

F32 = jnp.float32
BF16 = jnp.bfloat16
HIGHEST = lax.Precision.HIGHEST
ACT_DTYPE = BF16

D_MODEL = 1024
GRID_W = 64
EPS = 1e-6
LRU_WIDTH = 512
LRU_BLOCK = 64
CONV_W = 4
LRU_C = 8.0
ATT_HEADS = 8
ATT_KV_HEADS = 2
HEAD_DIM = 64
WINDOW = 128
ATT_BLOCK = 128
ROPE_BASE = 10000.0
LOG2E = 1.4426950408889634
ATT_Q_WIDTH = ATT_HEADS * HEAD_DIM
ATT_KV_WIDTH = ATT_KV_HEADS * HEAD_DIM
EVEN_IN = 2 * LRU_WIDTH + ATT_Q_WIDTH + 2 * ATT_KV_WIDTH
GLA_HEADS = 4
GLA_DK = 128
GLA_DV = 256
GLA_LOWRANK = 16
GLA_TAU = 16.0
GLA_CHUNK = 64
GLA_BLOCK = 256
GLA_K_WIDTH = GLA_HEADS * GLA_DK
GLA_V_WIDTH = GLA_HEADS * GLA_DV
ODD_IN = 2 * GLA_K_WIDTH + 2 * GLA_V_WIDTH + 2 * GLA_LOWRANK
N_EXPERTS = 16
N_GROUPS = 4
EXPERTS_PER_GROUP = 4
D_EXPERT = 512
PAIRS_PER_GROUP = 6
N_CLASSES = N_GROUPS * PAIRS_PER_GROUP
PAIR_LO = (0, 0, 0, 1, 1, 2)
PAIR_HI = (1, 2, 3, 2, 3, 3)

LANES = 128
SUBLANES = 8
VMEM_LIMIT_BYTES = 56 * 1024 * 1024

ROW_TILE = 512
MOE_TILE = 256
MOE_ROWS = 512
LRU_CHUNK = 128
LRU_HALF = 256
ODD_IN_PAD = ((ODD_IN + LANES - 1) // LANES) * LANES


def _params(*sem):
    return pltpu.CompilerParams(dimension_semantics=sem, vmem_limit_bytes=VMEM_LIMIT_BYTES)


def _modulated_norm(x, g, shift, scale):
    y = x * lax.rsqrt(jnp.mean(x * x, axis=-1, keepdims=True) + EPS)
    return (y * g) * (1.0 + scale) + shift


def _ada_kernel(c_ref, w_ref, b_ref, o_ref):
    c = c_ref[...]
    s = c * jax.nn.sigmoid(c)
    o_ref[...] = jnp.dot(s, w_ref[...], preferred_element_type=F32) + b_ref[...]


def _ada_call(cvec, ada_w, ada_b):
    depth, d, n6 = ada_w.shape
    r = cvec.shape[0]
    tn = n6 // 4
    return pl.pallas_call(
        _ada_kernel,
        grid=(depth, n6 // tn),
        in_specs=[pl.BlockSpec((r, d), lambda l, j: (0, 0)),
                  pl.BlockSpec((None, d, tn), lambda l, j: (l, 0, j)),
                  pl.BlockSpec((None, 1, tn), lambda l, j: (l, 0, j))],
        out_specs=pl.BlockSpec((None, r, tn), lambda l, j: (l, 0, j)),
        out_shape=jax.ShapeDtypeStruct((depth, r, n6), F32),
        compiler_params=_params("arbitrary", "arbitrary"),
        name="adaln",
    )(cvec, ada_w, ada_b.reshape(depth, 1, n6))


def _two_source_specs(width, n_lat_tiles):
    lat = pl.BlockSpec((ROW_TILE, width), lambda i: (jnp.minimum(i, n_lat_tiles - 1), 0))
    ctx = pl.BlockSpec((ROW_TILE, width), lambda i: (jnp.maximum(i - n_lat_tiles, 0), 0))
    return lat, ctx


def _two_source(lat_ref, ctx_ref, n_lat_tiles):
    return jnp.where(pl.program_id(0) >= n_lat_tiles, ctx_ref[...], lat_ref[...])


def _proj_kernel(xl_ref, xc_ref, mod_ref, g_ref, w_ref, o_ref, *, n_lat_tiles):
    x = _two_source(xl_ref, xc_ref, n_lat_tiles)
    n = _modulated_norm(x, g_ref[...], mod_ref[0:1, :], mod_ref[1:2, :])
    o_ref[...] = jnp.dot(n.astype(BF16), w_ref[...], preferred_element_type=F32).astype(o_ref.dtype)


def _proj_res_kernel(h_ref, f_ref, pmod_ref, mod_ref, g_ref, w_ref, x_ref, o_ref):
    x = h_ref[...] + pmod_ref[5:6, :] * f_ref[...]
    x_ref[...] = x
    n = _modulated_norm(x, g_ref[...], mod_ref[0:1, :], mod_ref[1:2, :])
    o_ref[...] = jnp.dot(n.astype(BF16), w_ref[...], preferred_element_type=F32).astype(o_ref.dtype)


def _mod_spec(tiles_per_seq, n_batch):
    return pl.BlockSpec((None, 6, D_MODEL), lambda i: (jnp.minimum(i // tiles_per_seq, n_batch), 0, 0))


def _proj_call(x_lat, x_ctx, mod, g, w, tiles_per_seq, n_batch):
    d = x_lat.shape[1]
    n = x_lat.shape[0] + x_ctx.shape[0]
    n_out = w.shape[1]
    n_lat_tiles = x_lat.shape[0] // ROW_TILE
    lat, ctx = _two_source_specs(d, n_lat_tiles)
    return pl.pallas_call(
        functools.partial(_proj_kernel, n_lat_tiles=n_lat_tiles),
        grid=(n // ROW_TILE,),
        in_specs=[lat, ctx,
                  _mod_spec(tiles_per_seq, n_batch),
                  pl.BlockSpec((1, d), lambda i: (0, 0)),
                  pl.BlockSpec((d, n_out), lambda i: (0, 0))],
        out_specs=pl.BlockSpec((ROW_TILE, n_out), lambda i: (i, 0)),
        out_shape=jax.ShapeDtypeStruct((n, n_out), ACT_DTYPE),
        compiler_params=_params("arbitrary"),
        name="in_proj",
    )(x_lat, x_ctx, mod, g, w)


def _proj_res_call(h, ffn, pmod, mod, g, w, tiles_per_seq, n_batch):
    n, d = h.shape
    n_out = w.shape[1]
    row = pl.BlockSpec((ROW_TILE, d), lambda i: (i, 0))
    return pl.pallas_call(
        _proj_res_kernel,
        grid=(n // ROW_TILE,),
        in_specs=[row, row, _mod_spec(tiles_per_seq, n_batch), _mod_spec(tiles_per_seq, n_batch),
                  pl.BlockSpec((1, d), lambda i: (0, 0)),
                  pl.BlockSpec((d, n_out), lambda i: (0, 0))],
        out_specs=[row, pl.BlockSpec((ROW_TILE, n_out), lambda i: (i, 0))],
        out_shape=[jax.ShapeDtypeStruct((n, d), F32), jax.ShapeDtypeStruct((n, n_out), ACT_DTYPE)],
        compiler_params=_params("arbitrary"),
        name="in_proj_res",
    )(h, ffn, pmod, mod, g, w)


def _qkprep_kernel(q_ref, k_ref, v_ref, c_ref, s_ref, qg_ref, kg_ref, m_ref, qo_ref, kvo_ref):
    cos = c_ref[...]
    sin = s_ref[...]
    lane = lax.broadcasted_iota(jnp.int32, cos.shape, 1)
    first = (lane % 32) < 16
    low = lane < HEAD_DIM

    def norm_rope(x, g):
        ms = jnp.dot(jnp.concatenate(_split2(x * x), axis=1), m_ref[...], preferred_element_type=F32)
        xn = (x * lax.rsqrt(ms + EPS)) * g
        partner = jnp.where(first, pltpu.roll(xn, LANES - 16, 1), pltpu.roll(xn, 16, 1))
        return xn * cos + partner * sin

    for j in range(ATT_Q_WIDTH // LANES):
        qj = norm_rope(q_ref[:, j * LANES:(j + 1) * LANES].astype(F32), qg_ref[...])
        qo_ref[:, j * LANES:(j + 1) * LANES] = (qj * (HEAD_DIM ** -0.5 * LOG2E)).astype(BF16)
    kr = norm_rope(k_ref[...].astype(F32), kg_ref[...])
    ks = pltpu.roll(kr, HEAD_DIM, 1)
    kvo_ref[:, 0:LANES] = jnp.where(low, kr, ks).astype(BF16)
    kvo_ref[:, LANES:2 * LANES] = jnp.where(low, ks, kr).astype(BF16)
    kvo_ref[:, 2 * LANES:3 * LANES] = v_ref[...].astype(BF16)
    kvo_ref[:, 3 * LANES:4 * LANES] = jnp.ones((cos.shape[0], LANES), BF16)


def _qkprep_call(proj, cos_t, sin_t, qg, kg, seg_mean, tiles_per_seq, n_lat_tiles):
    n = proj.shape[0]
    qcol = (2 * LRU_WIDTH) // ATT_Q_WIDTH
    kcol = (2 * LRU_WIDTH + ATT_Q_WIDTH) // LANES

    def pos(i):
        return (jnp.where(i < n_lat_tiles, i % tiles_per_seq, tiles_per_seq), 0)

    return pl.pallas_call(
        _qkprep_kernel,
        grid=(n // ROW_TILE,),
        in_specs=[pl.BlockSpec((ROW_TILE, ATT_Q_WIDTH), lambda i: (i, qcol)),
                  pl.BlockSpec((ROW_TILE, LANES), lambda i: (i, kcol)),
                  pl.BlockSpec((ROW_TILE, LANES), lambda i: (i, kcol + 1)),
                  pl.BlockSpec((ROW_TILE, LANES), pos),
                  pl.BlockSpec((ROW_TILE, LANES), pos),
                  pl.BlockSpec((1, LANES), lambda i: (0, 0)),
                  pl.BlockSpec((1, LANES), lambda i: (0, 0)),
                  pl.BlockSpec((2 * LANES, LANES), lambda i: (0, 0))],
        out_specs=[pl.BlockSpec((ROW_TILE, ATT_Q_WIDTH), lambda i: (i, 0)),
                   pl.BlockSpec((ROW_TILE, 4 * LANES), lambda i: (i, 0))],
        out_shape=[jax.ShapeDtypeStruct((n, ATT_Q_WIDTH), BF16),
                   jax.ShapeDtypeStruct((n, 4 * LANES), BF16)],
        compiler_params=_params("arbitrary"),
        name="qk_prep",
    )(proj, proj, proj, cos_t, sin_t, qg, kg, seg_mean)


def _attend(q, kv, sink_ref, bias):
    rows = q.shape[0]
    lane = lax.broadcasted_iota(jnp.int32, (rows, LANES), 1)
    low = lane < HEAD_DIM
    zero = jnp.zeros((rows, LANES), BF16)
    vv = kv[:, 2 * LANES:4 * LANES]
    groups = []
    heads_per_kv = ATT_HEADS // ATT_KV_HEADS
    for kh in range(ATT_KV_HEADS):
        kk = kv[:, kh * LANES:(kh + 1) * LANES]
        parts = []
        sinks = []
        for j in range(heads_per_kv):
            h = kh * heads_per_kv + j
            qg = q[:, (h // 2) * LANES:(h // 2 + 1) * LANES]
            parts.append(jnp.where(low, qg, zero) if h % 2 == 0 else jnp.where(low, zero, qg))
            sinks.append(jnp.full((rows, 1), sink_ref[h] * LOG2E, F32))
        lhs = jnp.concatenate(parts, axis=0)
        sink = jnp.concatenate(sinks, axis=0)
        s = lax.dot_general(lhs, kk, (((1,), (1,)), ((), ())), preferred_element_type=F32)
        if bias is not None:
            s = s + bias
        m = jnp.maximum(jnp.max(s, axis=-1, keepdims=True), sink)
        e = jnp.exp2(s - m)
        ov = jnp.dot(e.astype(BF16), vv, preferred_element_type=F32)
        o = ov[:, 0:LANES] / (ov[:, LANES:2 * LANES] + jnp.exp2(sink - m))
        r = [o[j * rows:(j + 1) * rows] for j in range(heads_per_kv)]
        for p in range(heads_per_kv // 2):
            even, odd = r[2 * p], r[2 * p + 1]
            if kh == 0:
                groups.append(jnp.where(low, even, pltpu.roll(odd, HEAD_DIM, 1)))
            else:
                groups.append(jnp.where(low, pltpu.roll(even, HEAD_DIM, 1), odd))
    return jnp.concatenate(groups, axis=1)


def _attn_kernel(sink_ref, q_ref, kvp_ref, kvo_ref, kvn_ref, kvc_ref, bias0_ref, bias1_ref, o_ref):
    blk = ATT_BLOCK
    own = kvo_ref[...]
    ctx = kvc_ref[...]
    kv0 = jnp.concatenate([kvp_ref[...], own, ctx], axis=0)
    kv1 = jnp.concatenate([own, kvn_ref[...], ctx], axis=0)
    o_ref[0:blk, :] = _attend(q_ref[0:blk, :], kv0, sink_ref, bias0_ref[...]).astype(o_ref.dtype)
    o_ref[blk:2 * blk, :] = _attend(q_ref[blk:2 * blk, :], kv1, sink_ref, bias1_ref[...]).astype(o_ref.dtype)


def _window_bias(ctx_len):
    blk = ATT_BLOCK
    stacked = (ATT_HEADS // ATT_KV_HEADS) * blk
    qi = lax.broadcasted_iota(jnp.int32, (4, stacked, 3 * blk + ctx_len), 1) % blk
    ks = lax.broadcasted_iota(jnp.int32, (4, stacked, 3 * blk + ctx_len), 2)
    var = lax.broadcasted_iota(jnp.int32, (4, stacked, 3 * blk + ctx_len), 0)
    in_win = jnp.abs(ks - blk - qi) <= WINDOW
    in_seq = ((ks >= blk) | (var % 2 == 1)) & ((ks < 2 * blk) | (var >= 2))
    keep = (in_win & in_seq) | (ks >= 3 * blk)
    return jnp.where(keep, 0.0, -1e30).astype(F32)


def _ctx_attn_kernel(sink_ref, q_ref, kvc_ref, o_ref):
    o_ref[...] = _attend(q_ref[...], kvc_ref[...], sink_ref, None).astype(o_ref.dtype)


def _attn_call(qrot, kvrot, sink, n_batch, seq, ctx_len):
    nb = seq // ATT_BLOCK
    ctx0 = (n_batch * seq) // ctx_len
    kvw = kvrot.shape[1]
    bias = _window_bias(ctx_len)

    assert nb % 2 == 0
    pairs = nb // 2

    def variant0(b, m):
        return ((m > 0).astype(jnp.int32) + 2, 0, 0)

    def variant1(b, m):
        return (1 + 2 * (m < pairs - 1).astype(jnp.int32), 0, 0)

    blk = ATT_BLOCK
    return pl.pallas_call(
        _attn_kernel,
        grid=(n_batch, pairs),
        in_specs=[pl.BlockSpec(memory_space=pltpu.SMEM),
                  pl.BlockSpec((2 * blk, ATT_Q_WIDTH), lambda b, m: (b * pairs + m, 0)),
                  pl.BlockSpec((blk, kvw), lambda b, m: (b * nb + jnp.maximum(2 * m - 1, 0), 0)),
                  pl.BlockSpec((2 * blk, kvw), lambda b, m: (b * pairs + m, 0)),
                  pl.BlockSpec((blk, kvw), lambda b, m: (b * nb + jnp.minimum(2 * m + 2, nb - 1), 0)),
                  pl.BlockSpec((ctx_len, kvw), lambda b, m: (ctx0 + b, 0)),
                  pl.BlockSpec((None,) + bias.shape[1:], variant0),
                  pl.BlockSpec((None,) + bias.shape[1:], variant1)],
        out_specs=pl.BlockSpec((2 * blk, ATT_Q_WIDTH), lambda b, m: (b * pairs + m, 0)),
        out_shape=jax.ShapeDtypeStruct((n_batch * seq, ATT_Q_WIDTH), ACT_DTYPE),
        compiler_params=_params("arbitrary", "arbitrary"),
        name="window_attn",
    )(sink, qrot, kvrot, kvrot, kvrot, kvrot, bias, bias)


def _ctx_attn_call(qrot, kvrot, sink, n_batch, seq, ctx_len):
    ctx0 = (n_batch * seq) // ctx_len
    kvw = kvrot.shape[1]
    return pl.pallas_call(
        _ctx_attn_kernel,
        grid=(n_batch,),
        in_specs=[pl.BlockSpec(memory_space=pltpu.SMEM),
                  pl.BlockSpec((ctx_len, ATT_Q_WIDTH), lambda b: (ctx0 + b, 0)),
                  pl.BlockSpec((ctx_len, kvw), lambda b: (ctx0 + b, 0))],
        out_specs=pl.BlockSpec((ctx_len, ATT_Q_WIDTH), lambda b: (b, 0)),
        out_shape=jax.ShapeDtypeStruct((n_batch * ctx_len, ATT_Q_WIDTH), ACT_DTYPE),
        compiler_params=_params("arbitrary"),
        name="ctx_attn",
    )(sink, qrot, kvrot)


def _scan_rows(a, x, carry, reverse):
    rows, width = a.shape
    n_groups = rows // SUBLANES
    a = a.reshape(n_groups, SUBLANES, width)
    x = x.reshape(n_groups, SUBLANES, width)
    sub = lax.broadcasted_iota(jnp.int32, a.shape, 1)
    for s in (1, 2, 4):
        shift = SUBLANES - s if reverse else s
        a_sh = pltpu.roll(a, shift, 1)
        x_sh = pltpu.roll(x, shift, 1)
        keep = (sub <= SUBLANES - 1 - s) if reverse else (sub >= s)
        x = x + a * jnp.where(keep, x_sh, 0.0)
        a = a * jnp.where(keep, a_sh, 1.0)
    a = a.reshape(rows, width)
    x = x.reshape(rows, width)
    out = [None] * n_groups
    order = range(n_groups - 1, -1, -1) if reverse else range(n_groups)
    for g in order:
        hg = x[g * SUBLANES:(g + 1) * SUBLANES] + a[g * SUBLANES:(g + 1) * SUBLANES] * carry
        carry = hg[0:1] if reverse else hg[SUBLANES - 1:SUBLANES]
        out[g] = hg
    return jnp.concatenate(out, axis=0), carry


def _rglru_kernel(ul_ref, gl_ref, uc_ref, gc_ref, cw_ref, cb_ref, wr_ref, wi_ref, br_ref, bi_ref, lam_ref,
                  yl_ref, yc_ref, upad_ref, conv_ref, *, seq, ctx_len):
    width = ul_ref.shape[1]
    tc = LRU_CHUNK
    neg_lam = -lam_ref[...]
    softplus = jnp.maximum(neg_lam, 0.0) + jnp.log1p(jnp.exp(-jnp.abs(neg_lam)))
    zeros = jnp.zeros((SUBLANES, width), F32)

    def gates(u, d):
        ub = u.astype(BF16)
        r = jax.nn.sigmoid(jnp.dot(ub, wr_ref[d], preferred_element_type=F32) + br_ref[d:d + 1, :])
        i = jax.nn.sigmoid(jnp.dot(ub, wi_ref[d], preferred_element_type=F32) + bi_ref[d:d + 1, :])
        log_a = (-LRU_C) * r * softplus[d:d + 1, :]
        a = jnp.exp(log_a)
        y = 1.0 - a * a
        root = jnp.where(y > 0.0, y * lax.rsqrt(y), 0.0)
        return a, root * (i * u)

    def run(u_ref, g_ref, y_ref, n, h_fwd, h_bwd):
        n_chunks = n // tc
        upad_ref[0:SUBLANES, :] = zeros
        upad_ref[pl.ds(SUBLANES, n), :] = u_ref[...].astype(F32)
        upad_ref[pl.ds(SUBLANES + n, SUBLANES), :] = zeros

        def conv_body(c, carry):
            t0 = pl.multiple_of(c * tc, tc)
            win = upad_ref[pl.ds(t0, tc + 2 * SUBLANES), :]
            acc = cb_ref[...]
            for k in range(CONV_W):
                acc = acc + cw_ref[k:k + 1, :] * win[SUBLANES - 1 + k:SUBLANES - 1 + k + tc]
            conv_ref[pl.ds(t0, tc), :] = acc
            return carry

        lax.fori_loop(0, n_chunks, conv_body, 0)

        def fwd_body(c, carry):
            t0 = pl.multiple_of(c * tc, tc)
            a, x = gates(conv_ref[pl.ds(t0, tc), :], 0)
            h, carry = _scan_rows(a, x, carry, False)
            upad_ref[pl.ds(pl.multiple_of(t0 + SUBLANES, SUBLANES), tc), :] = h
            return carry

        h_fwd = lax.fori_loop(0, n_chunks, fwd_body, h_fwd, unroll=min(4, n_chunks))

        def bwd_body(c, carry):
            t0 = pl.multiple_of((n_chunks - 1 - c) * tc, tc)
            a, x = gates(conv_ref[pl.ds(t0, tc), :], 1)
            h, carry = _scan_rows(a, x, carry, True)
            h_both = upad_ref[pl.ds(pl.multiple_of(t0 + SUBLANES, SUBLANES), tc), :] + h
            y_ref[pl.ds(t0, tc), :] = (h_both * jax.nn.gelu(g_ref[pl.ds(t0, tc), :].astype(F32))).astype(y_ref.dtype)
            return carry

        h_bwd = lax.fori_loop(0, n_chunks, bwd_body, h_bwd, unroll=min(4, n_chunks))
        return h_fwd, h_bwd

    h0 = jnp.zeros((1, width), F32)
    h_fwd, h_bwd = run(uc_ref, gc_ref, yc_ref, ctx_len, h0, h0)
    run(ul_ref, gl_ref, yl_ref, seq, h_fwd, h_bwd)


def _rglru_call(proj, conv_w, conv_b, wr, wi, br, bi, lam, n_batch, seq, ctx_len):
    ctx0 = (n_batch * seq) // ctx_len
    halves = LRU_WIDTH // LRU_HALF
    w = LRU_HALF
    vec2 = pl.BlockSpec((2, w), lambda b, c: (0, c))
    return pl.pallas_call(
        functools.partial(_rglru_kernel, seq=seq, ctx_len=ctx_len),
        grid=(n_batch, halves),
        in_specs=[pl.BlockSpec((seq, w), lambda b, c: (b, c)),
                  pl.BlockSpec((seq, w), lambda b, c: (b, halves + c)),
                  pl.BlockSpec((ctx_len, w), lambda b, c: (ctx0 + b, c)),
                  pl.BlockSpec((ctx_len, w), lambda b, c: (ctx0 + b, halves + c)),
                  pl.BlockSpec((CONV_W, w), lambda b, c: (0, c)),
                  pl.BlockSpec((1, w), lambda b, c: (0, c)),
                  pl.BlockSpec((None, 2, w, w), lambda b, c: (c, 0, 0, 0)),
                  pl.BlockSpec((None, 2, w, w), lambda b, c: (c, 0, 0, 0)),
                  vec2, vec2, vec2],
        out_specs=[pl.BlockSpec((seq, w), lambda b, c: (b, c)),
                   pl.BlockSpec((ctx_len, w), lambda b, c: (b, c))],
        out_shape=[jax.ShapeDtypeStruct((n_batch * seq, LRU_WIDTH), ACT_DTYPE),
                   jax.ShapeDtypeStruct((n_batch * ctx_len, LRU_WIDTH), ACT_DTYPE)],
        scratch_shapes=[pltpu.VMEM((seq + 2 * SUBLANES, w), F32), pltpu.VMEM((seq, w), F32)],
        compiler_params=_params("arbitrary", "arbitrary"),
        name="rglru",
    )(proj, proj, proj, proj, conv_w, conv_b, wr, wi, br, bi, lam)


def _split2(x):
    hi = x.astype(BF16)
    return hi, (x - hi.astype(F32)).astype(BF16)


def _gla_kernel(q_ref, k_ref, v_ref, r_ref, a_ref, kc_ref, vc_ref, ac_ref, gw_ref, gb_ref, ng_ref,
                o_ref, of_ref, ob_ref, qd_ref, ki_ref, ke_ref, la_ref, eg_ref, st_ref, *, seq, ctx_len):
    ch = GLA_CHUNK
    blk = GLA_BLOCK
    per_blk = blk // ch
    row = lax.broadcasted_iota(jnp.int32, (blk, blk), 0)
    col = lax.broadcasted_iota(jnp.int32, (blk, blk), 1)
    same = (row // ch) == (col // ch)
    keep = (same & (row >= col), same & (row <= col))
    tri = (keep[0].astype(BF16),)
    scale = GLA_DK ** -0.5
    nt = (((1,), (1,)), ((), ()))
    tn = (((0,), (0,)), ((), ()))

    def mm(a, b):
        return jnp.dot(a, b, preferred_element_type=F32)

    def prepare(a_src, k_src, v_src, q_src, dst, n_rows, d):
        n_blk = n_rows // blk
        unroll = min(4, n_blk)

        def decay_body(i, carry):
            t0 = pl.multiple_of(i * blk, blk)
            both = mm(a_src[pl.ds(t0, blk), :], gw_ref[d])
            pre = (both[:, 0:GLA_DK] + both[:, GLA_DK:2 * GLA_DK]) + gb_ref[d:d + 1, :]
            la_ref[pl.ds(t0, blk), :] = (jnp.minimum(pre, 0.0) - jnp.log(1.0 + jnp.exp(-jnp.abs(pre)))) / GLA_TAU
            return carry

        lax.fori_loop(0, n_blk, decay_body, 0, unroll=unroll)

        def scale_body(i, carry):
            t0 = pl.multiple_of(i * blk, blk)
            log_a = la_ref[pl.ds(t0, blk), :]
            sums = mm(tri[0], jnp.concatenate(_split2(log_a), axis=1))
            prefix = sums[:, 0:GLA_DK] + sums[:, GLA_DK:2 * GLA_DK]
            total = jnp.concatenate(
                [jnp.broadcast_to(prefix[(j + 1) * ch - 1:(j + 1) * ch], (ch, GLA_DK)) for j in range(per_blk)],
                axis=0)
            suffix = total - prefix + log_a
            b = prefix if d == 0 else suffix
            to_end = (suffix if d == 0 else prefix) - log_a
            for j in range(per_blk):
                eg_ref[pl.ds(i * per_blk + j, 1), :] = jnp.exp(total[j * ch:j * ch + 1])
            k_blk = k_src[pl.ds(t0, blk), :].astype(F32)
            ke_ref[pl.ds(t0, blk), :] = (k_blk * jnp.exp(to_end)).astype(BF16)
            if q_src is not None:
                q_blk = q_src[pl.ds(t0, blk), :].astype(F32)
                qd_ref[pl.ds(t0, blk), :] = ((q_blk * scale) * jnp.exp(b)).astype(BF16)
                ki_ref[pl.ds(t0, blk), :] = (k_blk * jnp.exp(-b)).astype(BF16)
            return carry

        lax.fori_loop(0, n_blk, scale_body, 0, unroll=unroll)
        if q_src is None:
            return

        def intra_body(i, carry):
            t0 = pl.multiple_of(i * blk, blk)
            s = lax.dot_general(qd_ref[pl.ds(t0, blk), :], ki_ref[pl.ds(t0, blk), :], nt,
                                preferred_element_type=F32)
            p = jnp.where(keep[d], s, 0.0).astype(BF16)
            dst[pl.ds(t0, blk), :] = mm(p, v_src[pl.ds(t0, blk), :].astype(BF16))
            return carry

        lax.fori_loop(0, n_blk, intra_body, 0, unroll=min(8, n_blk))

    def recur(v_src, dst, n_rows, d):
        n_chunks = n_rows // ch

        def body(c, carry):
            n = c if d == 0 else n_chunks - 1 - c
            t0 = pl.multiple_of(n * ch, ch)
            st = st_ref[...]
            if dst is not None:
                dst[pl.ds(t0, ch), :] = dst[pl.ds(t0, ch), :] + lax.dot_general(
                    qd_ref[pl.ds(t0, ch), :], st.astype(BF16), nt, preferred_element_type=F32)
            kv = lax.dot_general(v_src[pl.ds(t0, ch), :].astype(BF16), ke_ref[pl.ds(t0, ch), :], tn,
                                 preferred_element_type=F32)
            st_ref[...] = st * eg_ref[pl.ds(n, 1), :] + kv
            return carry

        lax.fori_loop(0, n_chunks, body, 0, unroll=min(16, n_chunks))

    for d, dst in ((0, of_ref), (1, ob_ref)):
        st_ref[...] = jnp.zeros(st_ref.shape, F32)
        prepare(ac_ref, kc_ref, vc_ref, None, None, ctx_len, d)
        recur(vc_ref, None, ctx_len, d)
        prepare(a_ref, k_ref, v_ref, q_ref, dst, seq, d)
        recur(v_ref, dst, seq, d)

    def finish_body(c, carry):
        t0 = pl.multiple_of(c * blk, blk)
        o = of_ref[pl.ds(t0, blk), :] + ob_ref[pl.ds(t0, blk), :]
        on = (o * lax.rsqrt(jnp.mean(o * o, axis=-1, keepdims=True) + EPS)) * ng_ref[...]
        r = r_ref[pl.ds(t0, blk), :].astype(F32)
        o_ref[pl.ds(t0, blk), :] = (on * (r * jax.nn.sigmoid(r))).astype(o_ref.dtype)
        return carry

    lax.fori_loop(0, seq // blk, finish_body, 0)


def _gla_call(proj, gate_w, gate_b, norm_g, n_batch, seq, ctx_len):
    ctx0 = (n_batch * seq) // ctx_len
    kcol = GLA_K_WIDTH // GLA_DK
    vcol = (2 * GLA_K_WIDTH) // GLA_DV
    rcol = (2 * GLA_K_WIDTH + GLA_V_WIDTH) // GLA_DV
    acol = (2 * GLA_K_WIDTH + 2 * GLA_V_WIDTH) // LANES
    return pl.pallas_call(
        functools.partial(_gla_kernel, seq=seq, ctx_len=ctx_len),
        grid=(n_batch, GLA_HEADS),
        in_specs=[pl.BlockSpec((seq, GLA_DK), lambda b, h: (b, h)),
                  pl.BlockSpec((seq, GLA_DK), lambda b, h: (b, kcol + h)),
                  pl.BlockSpec((seq, GLA_DV), lambda b, h: (b, vcol + h)),
                  pl.BlockSpec((seq, GLA_DV), lambda b, h: (b, rcol + h)),
                  pl.BlockSpec((seq, LANES), lambda b, h: (b, acol)),
                  pl.BlockSpec((ctx_len, GLA_DK), lambda b, h: (ctx0 + b, kcol + h)),
                  pl.BlockSpec((ctx_len, GLA_DV), lambda b, h: (ctx0 + b, vcol + h)),
                  pl.BlockSpec((ctx_len, LANES), lambda b, h: (ctx0 + b, acol)),
                  pl.BlockSpec((None, 2, LANES, 2 * GLA_DK), lambda b, h: (h, 0, 0, 0)),
                  pl.BlockSpec((None, 2, GLA_DK), lambda b, h: (h, 0, 0)),
                  pl.BlockSpec((1, GLA_DV), lambda b, h: (0, 0))],
        out_specs=pl.BlockSpec((seq, GLA_DV), lambda b, h: (b, h)),
        out_shape=jax.ShapeDtypeStruct((n_batch * seq, GLA_V_WIDTH), ACT_DTYPE),
        scratch_shapes=[pltpu.VMEM((seq, GLA_DV), F32),
                        pltpu.VMEM((seq, GLA_DV), F32),
                        pltpu.VMEM((seq, GLA_DK), BF16),
                        pltpu.VMEM((seq, GLA_DK), BF16),
                        pltpu.VMEM((seq, GLA_DK), BF16),
                        pltpu.VMEM((seq, GLA_DK), F32),
                        pltpu.VMEM((seq // GLA_CHUNK, GLA_DK), F32),
                        pltpu.VMEM((GLA_DV, GLA_DK), F32)],
        compiler_params=_params("arbitrary", "arbitrary"),
        name="gla",
    )(proj, proj, proj, proj, proj, proj, proj, proj, gate_w, gate_b, norm_g)


def _route(rows, rows_bf16, rw_ref, rb_ref):
    n_rows = rows.shape[0]
    rows_lo = (rows - rows_bf16.astype(F32)).astype(BF16)
    both = jnp.dot(rows_bf16, rw_ref[...], preferred_element_type=F32)
    logits_rows = (both[:, 0:LANES] + both[:, LANES:2 * LANES]
                   + jnp.dot(rows_lo, rw_ref[:, 0:LANES], preferred_element_type=F32))
    logits = logits_rows.T[0:N_EXPERTS, :]
    scores = jax.nn.sigmoid(logits)
    sel = scores + rb_ref[...]
    eid = lax.broadcasted_iota(jnp.int32, (N_EXPERTS, n_rows), 0)
    gid = eid // EXPERTS_PER_GROUP
    neg = -jnp.inf
    big = N_EXPERTS

    def top2(masked):
        m1 = jnp.max(masked, axis=0, keepdims=True)
        i1 = jnp.min(jnp.where(masked == m1, eid, big), axis=0, keepdims=True)
        rest = jnp.where(eid == i1, neg, masked)
        m2 = jnp.max(rest, axis=0, keepdims=True)
        i2 = jnp.min(jnp.where(rest == m2, eid, big), axis=0, keepdims=True)
        return m1, i1, m2, i2

    best = None
    best_g = None
    for g in range(N_GROUPS):
        m1, _, m2, _ = top2(jnp.where(gid == g, sel, neg))
        gs = m1 + m2
        if best is None:
            best, best_g = gs, jnp.zeros((1, n_rows), jnp.int32)
        else:
            better = gs > best
            best_g = jnp.where(better, g, best_g)
            best = jnp.where(better, gs, best)
    _, i1, _, i2 = top2(jnp.where(gid == best_g, sel, neg))
    s1 = jnp.sum(jnp.where(eid == i1, scores, 0.0), axis=0, keepdims=True)
    s2 = jnp.sum(jnp.where(eid == i2, scores, 0.0), axis=0, keepdims=True)
    total = s1 + s2
    swap = i2 < i1
    lo = jnp.where(swap, i2, i1)
    hi = jnp.where(swap, i1, i2)
    w_lo = jnp.where(swap, s2, s1) / total
    w_hi = jnp.where(swap, s1, s2) / total
    pad = jnp.zeros((4, n_rows), F32)
    return jnp.concatenate([lo.astype(F32), hi.astype(F32), w_lo, w_hi, pad], axis=0)


def _finish_rows(y, h, mod_ref, g_ref, rw_ref, rb_ref, hn_ref, rows_ref, rt_ref, rc_ref):
    hn = h + mod_ref[2:3, :] * y
    hn_ref[...] = hn
    rows = _modulated_norm(hn, g_ref[...], mod_ref[3:4, :], mod_ref[4:5, :])
    rows_bf16 = rows.astype(BF16)
    rows_ref[...] = rows_bf16
    route = _route(rows, rows_bf16, rw_ref, rb_ref)
    rt_ref[...] = route
    padded = jnp.concatenate([route, jnp.zeros((LANES - SUBLANES, route.shape[1]), F32)], axis=0)
    rc_ref[...] = padded.T


def _outproj_even_kernel(al_ref, ac_ref, bl_ref, bc_ref, w_ref, hl_ref, hc_ref, mod_ref, g_ref, rw_ref, rb_ref,
                         hn_ref, rows_ref, rt_ref, rc_ref, *, n_lat_tiles):
    a = _two_source(al_ref, ac_ref, n_lat_tiles).astype(BF16)
    b = _two_source(bl_ref, bc_ref, n_lat_tiles).astype(BF16)
    y = (jnp.dot(a, w_ref[0:LRU_WIDTH, :], preferred_element_type=F32)
         + jnp.dot(b, w_ref[LRU_WIDTH:LRU_WIDTH + ATT_Q_WIDTH, :], preferred_element_type=F32))
    h = _two_source(hl_ref, hc_ref, n_lat_tiles)
    _finish_rows(y, h, mod_ref, g_ref, rw_ref, rb_ref, hn_ref, rows_ref, rt_ref, rc_ref)


def _outproj_odd_kernel(o_ref, w_ref, h_ref, mod_ref, g_ref, rw_ref, rb_ref, hn_ref, rows_ref, rt_ref, rc_ref):
    y = jnp.dot(o_ref[...].astype(BF16), w_ref[...], preferred_element_type=F32)
    _finish_rows(y, h_ref[...], mod_ref, g_ref, rw_ref, rb_ref, hn_ref, rows_ref, rt_ref, rc_ref)


def _outproj_specs(n_rows, tiles_per_seq, n_batch):
    d = D_MODEL
    row = pl.BlockSpec((ROW_TILE, d), lambda i: (i, 0))
    tail_in = [_mod_spec(tiles_per_seq, n_batch),
               pl.BlockSpec((1, d), lambda i: (0, 0)),
               pl.BlockSpec((d, 2 * LANES), lambda i: (0, 0)),
               pl.BlockSpec((N_EXPERTS, 1), lambda i: (0, 0))]
    out_specs = [row, row, pl.BlockSpec((SUBLANES, ROW_TILE), lambda i: (0, i)),
                 pl.BlockSpec((ROW_TILE, LANES), lambda i: (i, 0))]
    out_shape = [jax.ShapeDtypeStruct((n_rows, d), F32), jax.ShapeDtypeStruct((n_rows, d), ACT_DTYPE),
                 jax.ShapeDtypeStruct((SUBLANES, n_rows), F32), jax.ShapeDtypeStruct((n_rows, LANES), F32)]
    return tail_in, out_specs, out_shape


def _outproj_even_call(a_lat, a_ctx, b_lat, b_ctx, w, h_lat, h_ctx, mod, g, rw_t, rb, tiles_per_seq, n_batch):
    n_rows = h_lat.shape[0] + h_ctx.shape[0]
    n_lat_tiles = a_lat.shape[0] // ROW_TILE
    tail_in, out_specs, out_shape = _outproj_specs(n_rows, tiles_per_seq, n_batch)
    lat, ctx = _two_source_specs(LRU_WIDTH, n_lat_tiles)
    h_specs = list(_two_source_specs(D_MODEL, n_lat_tiles))
    return pl.pallas_call(
        functools.partial(_outproj_even_kernel, n_lat_tiles=n_lat_tiles),
        grid=(n_rows // ROW_TILE,),
        in_specs=[lat, ctx, lat, ctx, pl.BlockSpec(w.shape, lambda i: (0, 0))] + h_specs + tail_in,
        out_specs=out_specs,
        out_shape=out_shape,
        compiler_params=_params("arbitrary"),
        name="out_proj_even",
    )(a_lat, a_ctx, b_lat, b_ctx, w, h_lat, h_ctx, mod, g, rw_t, rb)


def _outproj_odd_call(o, w, h, mod, g, rw_t, rb, tiles_per_seq, n_batch):
    n_rows = o.shape[0]
    tail_in, out_specs, out_shape = _outproj_specs(n_rows, tiles_per_seq, n_batch)
    return pl.pallas_call(
        _outproj_odd_kernel,
        grid=(n_rows // ROW_TILE,),
        in_specs=[pl.BlockSpec((ROW_TILE, o.shape[1]), lambda i: (i, 0)),
                  pl.BlockSpec(w.shape, lambda i: (0, 0)),
                  pl.BlockSpec((ROW_TILE, D_MODEL), lambda i: (i, 0))] + tail_in,
        out_specs=out_specs,
        out_shape=out_shape,
        compiler_params=_params("arbitrary"),
        name="out_proj_odd",
    )(o, w, h, mod, g, rw_t, rb)


X_SLAB = D_MODEL // LANES
Y_SLAB = 2 * X_SLAB


def _for_rows(n_rows, fn):
    for g in range(n_rows // SUBLANES):
        for u in range(SUBLANES):
            fn(g, u)


def _dispatch_kernel(tail_ref, slot_ref, rows_ref, xs_ref, slab_ref, zero_ref, sem, zero_sem, *, steps, n_tiles):
    i = pl.program_id(0)
    k = i % 2
    rt = MOE_ROWS
    tile_rows = MOE_TILE * X_SLAB

    def slab_copy(buf, g, u, slot):
        return pltpu.make_async_copy(
            slab_ref.at[buf, g, :, pl.ds(u, 1), :],
            xs_ref.at[pl.ds(pl.multiple_of(slot * X_SLAB, X_SLAB), X_SLAB)], sem.at[buf])

    def wait_all(buf):
        pltpu.make_async_copy(slab_ref.at[buf], slab_ref.at[buf], sem.at[buf]).wait()

    @pl.when(i == 0)
    def _():
        zero_ref[...] = jnp.zeros(zero_ref.shape, F32)

        def fill(t):
            start = pl.multiple_of(t * tile_rows, tile_rows)
            return pltpu.make_async_copy(zero_ref, xs_ref.at[pl.ds(start, tile_rows)], zero_sem)

        n_used = tail_ref[1, N_CLASSES - 1]
        for phase in ("start", "wait"):
            for c in range(N_CLASSES):
                for cond, t in ((tail_ref[0, c] > 0, tail_ref[1, c] - 1), (n_used + c < n_tiles, n_used + c)):
                    @pl.when(cond)
                    def _():
                        getattr(fill(t), phase)()

    @pl.when(i >= 2)
    def _():
        wait_all(k)

    for j in range(X_SLAB):
        lanes = rows_ref[:, j * LANES:(j + 1) * LANES].astype(F32)
        slab_ref[k, :, j, :, :] = lanes.reshape(rt // SUBLANES, SUBLANES, LANES)

    _for_rows(rt, lambda g, u: slab_copy(k, g, u, slot_ref[0, 0, g * SUBLANES + u]).start(priority=u % 2))

    @pl.when(i == steps - 1)
    def _():
        wait_all(k)
        if steps >= 2:
            wait_all(1 - k)


def _dispatch_call(rows, slots, tails, n_slots):
    n, d = rows.shape
    steps = n // MOE_ROWS
    grid_spec = pltpu.PrefetchScalarGridSpec(
        num_scalar_prefetch=1,
        grid=(steps,),
        in_specs=[pl.BlockSpec((1, 1, MOE_ROWS), lambda i, tails: (i, 0, 0), memory_space=pltpu.SMEM),
                  pl.BlockSpec((MOE_ROWS, d), lambda i, tails: (i, 0))],
        out_specs=pl.BlockSpec(memory_space=pl.ANY),
        scratch_shapes=[pltpu.VMEM((2, MOE_ROWS // SUBLANES, X_SLAB, SUBLANES, LANES), F32),
                        pltpu.VMEM((MOE_TILE * X_SLAB, 1, LANES), F32),
                        pltpu.SemaphoreType.DMA((2,)), pltpu.SemaphoreType.DMA(())],
    )
    xs = pl.pallas_call(
        functools.partial(_dispatch_kernel, steps=steps, n_tiles=n_slots // MOE_TILE),
        grid_spec=grid_spec,
        out_shape=jax.ShapeDtypeStruct((n_slots * X_SLAB, 1, LANES), F32),
        compiler_params=_params("arbitrary"),
        name="moe_dispatch",
    )(tails, slots.reshape(steps, 1, MOE_ROWS), rows)
    return xs.reshape(n_slots * X_SLAB, LANES)


def _experts_kernel(elo_ref, ehi_ref, used_ref, x_ref, wg0_ref, wu0_ref, wd0_ref,
                    wg1_ref, wu1_ref, wd1_ref, y_ref, cg0_ref, cu0_ref, cd0_ref, cg1_ref, cu1_ref, cd1_ref):
    t = pl.program_id(0)
    tm = MOE_TILE
    prev = jnp.maximum(t - 1, 0)
    sides = ((elo_ref, (wg0_ref, wu0_ref, wd0_ref), (cg0_ref, cu0_ref, cd0_ref)),
             (ehi_ref, (wg1_ref, wu1_ref, wd1_ref), (cg1_ref, cu1_ref, cd1_ref)))

    @pl.when(t < used_ref[0])
    def _():
        for ids, weights, cached in sides:
            @pl.when((t == 0) | (ids[t] != ids[prev]))
            def _():
                for w_ref, c_ref in zip(weights, cached):
                    c_ref[...] = w_ref[...].astype(BF16)

        x = jnp.concatenate([x_ref[pl.ds(j, tm, stride=X_SLAB), :] for j in range(X_SLAB)],
                            axis=1).astype(BF16)
        for e, (_, _, (cg_ref, cu_ref, cd_ref)) in enumerate(sides):
            gate = jnp.dot(x, cg_ref[...], preferred_element_type=F32)
            up = jnp.dot(x, cu_ref[...], preferred_element_type=F32)
            act = ((gate * jax.nn.sigmoid(gate)) * up).astype(BF16)
            f = jnp.dot(act, cd_ref[...], preferred_element_type=F32)
            for j in range(X_SLAB):
                y_ref[pl.ds(e * X_SLAB + j, tm, stride=Y_SLAB), :] = f[:, j * LANES:(j + 1) * LANES]

    @pl.when(t >= used_ref[0])
    def _():
        y_ref[...] = jnp.zeros_like(y_ref)


def _experts_call(xs, e_lo, e_hi, n_used, w_gate, w_up, w_down, layer):
    d = D_MODEL
    tiles = xs.shape[0] // (MOE_TILE * X_SLAB)

    def xmap(t, elo, ehi, used):
        return (jnp.maximum(jnp.minimum(t, used[0] - 1), 0), 0)

    def lo(t, elo, ehi, used):
        return (layer, elo[t], 0, 0)

    def hi(t, elo, ehi, used):
        return (layer, ehi[t], 0, 0)

    up_spec = lambda m: pl.BlockSpec((None, None, d, D_EXPERT), m)
    down_spec = lambda m: pl.BlockSpec((None, None, D_EXPERT, d), m)
    cached = [pltpu.VMEM((d, D_EXPERT), BF16), pltpu.VMEM((d, D_EXPERT), BF16), pltpu.VMEM((D_EXPERT, d), BF16)]
    grid_spec = pltpu.PrefetchScalarGridSpec(
        num_scalar_prefetch=3,
        grid=(tiles,),
        in_specs=[pl.BlockSpec((MOE_TILE * X_SLAB, LANES), xmap),
                  up_spec(lo), up_spec(lo), down_spec(lo),
                  up_spec(hi), up_spec(hi), down_spec(hi)],
        out_specs=pl.BlockSpec((MOE_TILE * Y_SLAB, LANES), lambda t, elo, ehi, used: (t, 0)),
        scratch_shapes=cached + cached,
    )
    return pl.pallas_call(
        _experts_kernel,
        grid_spec=grid_spec,
        out_shape=jax.ShapeDtypeStruct((tiles * MOE_TILE * Y_SLAB, LANES), F32),
        compiler_params=_params("arbitrary"),
        name="experts",
    )(e_lo, e_hi, n_used, xs, w_gate, w_up, w_down, w_gate, w_up, w_down)


def _collect_kernel(slot_ref, next_slot_ref, ys_ref, rc_ref, *rest, steps, residual):
    if residual:
        h_ref, mod_ref, o_ref, buf_ref, sem = rest
    else:
        o_ref, buf_ref, sem = rest
    i = pl.program_id(0)
    k = i % 2
    rt = MOE_ROWS

    def slab_copy(buf, g, u, slot):
        return pltpu.make_async_copy(
            ys_ref.at[pl.ds(pl.multiple_of(slot * Y_SLAB, Y_SLAB), Y_SLAB)],
            buf_ref.at[buf, g, :, pl.ds(u, 1), :], sem.at[buf])

    def start_all(slots, buf):
        _for_rows(rt, lambda g, u: slab_copy(buf, g, u, slots[0, 0, g * SUBLANES + u]).start(priority=u % 2))

    @pl.when(i == 0)
    def _():
        start_all(slot_ref, 0)

    @pl.when(i + 1 < steps)
    def _():
        start_all(next_slot_ref, 1 - k)

    pltpu.make_async_copy(buf_ref.at[k], buf_ref.at[k], sem.at[k]).wait()

    rc = rc_ref[...]
    w_lo = rc[:, 2:3]
    w_hi = rc[:, 3:4]
    for j in range(X_SLAB):
        cols = slice(j * LANES, (j + 1) * LANES)
        f = (w_lo * buf_ref[k, :, j, :, :].reshape(rt, LANES)
             + w_hi * buf_ref[k, :, X_SLAB + j, :, :].reshape(rt, LANES))
        if residual:
            o_ref[:, cols] = h_ref[:, cols] + mod_ref[5:6, cols] * f
        else:
            o_ref[:, cols] = f


def _collect_call(ys, slots, route_cols, residual=None):
    n = slots.shape[0]
    d = D_MODEL
    steps = n // MOE_ROWS
    slots3 = slots.reshape(steps, 1, MOE_ROWS)
    row = pl.BlockSpec((MOE_ROWS, d), lambda i: (i, 0))
    in_specs = [pl.BlockSpec((1, 1, MOE_ROWS), lambda i: (i, 0, 0), memory_space=pltpu.SMEM),
                pl.BlockSpec((1, 1, MOE_ROWS), lambda i: (jnp.minimum(i + 1, steps - 1), 0, 0),
                             memory_space=pltpu.SMEM),
                pl.BlockSpec(memory_space=pl.ANY),
                pl.BlockSpec((MOE_ROWS, LANES), lambda i: (i, 0))]
    args = [slots3, slots3, ys.reshape(ys.shape[0], 1, LANES), route_cols]
    if residual is not None:
        h, mod, tiles_per_seq, n_batch = residual
        in_specs += [row, _mod_spec(tiles_per_seq, n_batch)]
        args += [h, mod]
    return pl.pallas_call(
        functools.partial(_collect_kernel, steps=steps, residual=residual is not None),
        grid=(steps,),
        in_specs=in_specs,
        out_specs=row,
        out_shape=jax.ShapeDtypeStruct((n, d), F32),
        scratch_shapes=[pltpu.VMEM((2, MOE_ROWS // SUBLANES, Y_SLAB, SUBLANES, LANES), F32),
                        pltpu.SemaphoreType.DMA((2,))],
        compiler_params=_params("arbitrary"),
        name="moe_collect",
    )(*args)


def _moe(rows, route, route_cols, w_gate, w_up, w_down, layer, residual=None):
    n = rows.shape[0]
    tm = MOE_TILE
    assert n % MOE_ROWS == 0
    max_tiles = -(-n // tm) + N_CLASSES
    p = max_tiles * tm
    lo = route[0].astype(jnp.int32)
    hi = route[1].astype(jnp.int32)
    lo_in = lo % EXPERTS_PER_GROUP
    hi_in = hi % EXPERTS_PER_GROUP
    pair = lo_in * 3 - (lo_in * (lo_in - 1)) // 2 + (hi_in - lo_in - 1)
    cls = (lo // EXPERTS_PER_GROUP) * PAIRS_PER_GROUP + pair
    onehot = (cls[:, None] == jnp.arange(N_CLASSES, dtype=jnp.int32)[None, :]).astype(jnp.int32)
    counts = jnp.sum(onehot, axis=0)
    rank = jnp.sum(jnp.cumsum(onehot, axis=0) * onehot, axis=1) - 1
    tiles_c = (counts + tm - 1) // tm
    tile_end = jnp.cumsum(tiles_c)
    tile_start = tile_end - tiles_c
    n_used = tile_end[-1]
    slot_of_token = (tile_start[cls] * tm + rank).astype(jnp.int32)
    tile_ids = jnp.arange(max_tiles, dtype=jnp.int32)
    tile_cls = jnp.minimum(jnp.sum((tile_end[None, :] <= tile_ids[:, None]).astype(jnp.int32), axis=1),
                           N_CLASSES - 1)
    group0 = (tile_cls // PAIRS_PER_GROUP) * EXPERTS_PER_GROUP
    e_lo = group0 + jnp.asarray(PAIR_LO, jnp.int32)[tile_cls % PAIRS_PER_GROUP]
    e_hi = group0 + jnp.asarray(PAIR_HI, jnp.int32)[tile_cls % PAIRS_PER_GROUP]
    tails = jnp.stack([tiles_c, tile_end]).astype(jnp.int32)
    xs = _dispatch_call(rows, slot_of_token, tails, p)
    ys = _experts_call(xs, e_lo.astype(jnp.int32), e_hi.astype(jnp.int32),
                       n_used.reshape(1).astype(jnp.int32), w_gate, w_up, w_down, layer)
    return _collect_call(ys, slot_of_token, route_cols, residual)


def _rope_tables(seq):
    rows = seq // GRID_W
    row = jnp.repeat(jnp.arange(rows), GRID_W).astype(F32)
    col = jnp.tile(jnp.arange(GRID_W), rows).astype(F32)
    n_freq = HEAD_DIM // 4
    inv_freq = ROPE_BASE ** (-jnp.arange(n_freq, dtype=F32) / n_freq)
    ang_r = row[:, None] * inv_freq
    ang_c = col[:, None] * inv_freq
    cos = jnp.concatenate([jnp.cos(ang_r), jnp.cos(ang_r), jnp.cos(ang_c), jnp.cos(ang_c)], axis=1)
    sin = jnp.concatenate([-jnp.sin(ang_r), jnp.sin(ang_r), -jnp.sin(ang_c), jnp.sin(ang_c)], axis=1)
    cos = jnp.concatenate([jnp.tile(cos, (1, 2)), jnp.ones((ROW_TILE, LANES), F32)], axis=0)
    sin = jnp.concatenate([jnp.tile(sin, (1, 2)), jnp.zeros((ROW_TILE, LANES), F32)], axis=0)
    return cos, sin


def _block_diag_halves(w):
    per_half = LRU_HALF // LRU_BLOCK
    halves = LRU_WIDTH // LRU_HALF
    out = jnp.zeros((halves, 2, LRU_HALF, LRU_HALF), F32)
    for c in range(halves):
        for j in range(per_half):
            s = slice(j * LRU_BLOCK, (j + 1) * LRU_BLOCK)
            out = out.at[c, :, s, s].set(w[:, c * per_half + j])
    return out.astype(BF16)


def _gla_gate_weights(gate_up, gate_b):
    w = jnp.zeros((GLA_HEADS, 2, LANES, GLA_DK), F32)
    for d in range(2):
        wd = gate_up[d].reshape(GLA_LOWRANK, GLA_HEADS, GLA_DK).transpose(1, 0, 2)
        w = w.at[:, d, d * GLA_LOWRANK:(d + 1) * GLA_LOWRANK, :].set(wd)
    b = gate_b.reshape(2, GLA_HEADS, GLA_DK).transpose(1, 0, 2)
    w_hi = w.astype(BF16)
    w_lo = (w - w_hi.astype(F32)).astype(BF16)
    return jnp.concatenate([w_hi, w_lo], axis=-1), b


def kernel(x, c, ctx, c_ctx, router_w, router_bias, ada_w, ada_b, norm_mix_g, norm_ffn_g, moe_w_gate, moe_w_up, moe_w_down, ev_w_in, ev_w_out, ev_conv_w, ev_conv_b, ev_rg_w, ev_rg_b, ev_ig_w, ev_ig_b, ev_lambda, ev_q_norm_g, ev_k_norm_g, ev_sink, od_w_in, od_w_out, od_gate_up, od_gate_b, od_out_norm_g):
    n_batch, seq, d = x.shape
    ctx_len = ctx.shape[1]
    n_lat = n_batch * seq
    n_ctx = n_batch * ctx_len
    assert ada_w.shape[0] == 2 and d == D_MODEL
    assert seq % ROW_TILE == 0 and n_ctx % ROW_TILE == 0 and seq % ctx_len == 0
    assert ctx_len % LRU_CHUNK == 0 and seq % GLA_BLOCK == 0 and ctx_len % GLA_BLOCK == 0
    tiles_per_seq = seq // ROW_TILE
    n_lat_tiles = n_lat // ROW_TILE

    x_lat = x.reshape(n_lat, d)
    x_ctx = ctx.reshape(n_ctx, d)
    mod_rows = -(-(n_batch + 1) // SUBLANES) * SUBLANES
    cvec = jnp.zeros((mod_rows, d), F32).at[:n_batch].set(c).at[n_batch].set(c_ctx)
    mods = _ada_call(cvec, ada_w, ada_b).reshape(2, mod_rows, 6, d)
    mod_mix = [mods[l][:n_batch + 1, 0:6] for l in range(2)]
    rw_pad = jnp.pad(router_w, ((0, 0), (0, LANES - N_EXPERTS)))
    rw_hi = rw_pad.astype(BF16)
    rw_t = jnp.concatenate([rw_hi, (rw_pad - rw_hi.astype(F32)).astype(BF16)], axis=1)
    rb = router_bias.reshape(N_EXPERTS, 1)

    w_in0 = ev_w_in[0].astype(BF16)
    w_out0 = ev_w_out[0].astype(BF16)
    proj0 = _proj_call(x_lat, x_ctx, mod_mix[0], norm_mix_g[0:1], w_in0, tiles_per_seq, n_batch)
    cos_t, sin_t = _rope_tables(seq)
    seg = np.kron(np.eye(LANES // HEAD_DIM, dtype=np.float32), np.full((HEAD_DIM, HEAD_DIM), 1.0 / HEAD_DIM, np.float32))
    seg = jnp.asarray(np.concatenate([seg, seg], axis=0)).astype(BF16)
    qrot, kvrot = _qkprep_call(proj0, cos_t, sin_t, jnp.tile(ev_q_norm_g[0], 2)[None, :],
                               jnp.tile(ev_k_norm_g[0], 2)[None, :], seg,
                               tiles_per_seq, n_lat_tiles)
    a_lat, a_ctx = _rglru_call(proj0, ev_conv_w[0], ev_conv_b[0:1], _block_diag_halves(ev_rg_w[0]),
                               _block_diag_halves(ev_ig_w[0]), ev_rg_b[0], ev_ig_b[0], ev_lambda[0],
                               n_batch, seq, ctx_len)
    b_lat = _attn_call(qrot, kvrot, ev_sink[0], n_batch, seq, ctx_len)
    b_ctx = _ctx_attn_call(qrot, kvrot, ev_sink[0], n_batch, seq, ctx_len)
    h, rows, route, route_cols = _outproj_even_call(a_lat, a_ctx, b_lat, b_ctx, w_out0, x_lat, x_ctx, mod_mix[0],
                                                    norm_ffn_g[0:1], rw_t, rb, tiles_per_seq, n_batch)
    ffn = _moe(rows, route, route_cols, moe_w_gate, moe_w_up, moe_w_down, 0)

    w_in1 = jnp.pad(od_w_in[0], ((0, 0), (0, ODD_IN_PAD - ODD_IN))).astype(BF16)
    h, proj1 = _proj_res_call(h, ffn, mod_mix[0], mod_mix[1], norm_mix_g[1:2], w_in1, tiles_per_seq, n_batch)
    gate_w, gate_b = _gla_gate_weights(od_gate_up[0], od_gate_b[0])
    o = _gla_call(proj1, gate_w, gate_b, od_out_norm_g[0:1], n_batch, seq, ctx_len)
    h_lat, rows, route, route_cols = _outproj_odd_call(o, od_w_out[0].astype(BF16), h, mod_mix[1],
                                                       norm_ffn_g[1:2], rw_t, rb, tiles_per_seq, n_batch)
    out = _moe(rows, route, route_cols, moe_w_gate, moe_w_up, moe_w_down, 1,
               residual=(h_lat, mod_mix[1], seq // MOE_ROWS, n_batch))
    return out.reshape(n_batch, seq, d)
```

```python
import functools

import jax
import jax.numpy as jnp
import numpy as np
from jax import lax
from jax.experimental import pallas as pl
from jax.experimental.pallas import tpu as pltpu

F32 = jnp.float32
BF16 = jnp.bfloat16
HIGHEST = lax.Precision.HIGHEST
ACT_DTYPE = BF16

D_MODEL = 1024
GRID_W = 64
EPS = 1e-6
LRU_WIDTH = 512
LRU_BLOCK = 64
CONV_W = 4
LRU_C = 8.0
ATT_HEADS = 8
ATT_KV_HEADS = 2
HEAD_DIM = 64
WINDOW = 128
ATT_BLOCK = 128
ROPE_BASE = 10000.0
LOG2E = 1.4426950408889634
ATT_Q_WIDTH = ATT_HEADS * HEAD_DIM
ATT_KV_WIDTH = ATT_KV_HEADS * HEAD_DIM
EVEN_IN = 2 * LRU_WIDTH + ATT_Q_WIDTH + 2 * ATT_KV_WIDTH
GLA_HEADS = 4
GLA_DK = 128
GLA_DV = 256
GLA_LOWRANK = 16
GLA_TAU = 16.0
GLA_CHUNK = 64
GLA_BLOCK = 256
GLA_K_WIDTH = GLA_HEADS * GLA_DK
GLA_V_WIDTH = GLA_HEADS * GLA_DV
ODD_IN = 2 * GLA_K_WIDTH + 2 * GLA_V_WIDTH + 2 * GLA_LOWRANK
N_EXPERTS = 16
N_GROUPS = 4
EXPERTS_PER_GROUP = 4
D_EXPERT = 512
PAIRS_PER_GROUP = 6
N_CLASSES = N_GROUPS * PAIRS_PER_GROUP
PAIR_LO = (0, 0, 0, 1, 1, 2)
PAIR_HI = (1, 2, 3, 2, 3, 3)

LANES = 128
SUBLANES = 8
VMEM_LIMIT_BYTES = 56 * 1024 * 1024

ROW_TILE = 512
MOE_TILE = 256
MOE_ROWS = 512
LRU_CHUNK = 128
LRU_HALF = 256
ODD_IN_PAD = ((ODD_IN + LANES - 1) // LANES) * LANES


def _params(*sem):
    return pltpu.CompilerParams(dimension_semantics=sem, vmem_limit_bytes=VMEM_LIMIT_BYTES)


def _modulated_norm(x, g, shift, scale):
    y = x * lax.rsqrt(jnp.mean(x * x, axis=-1, keepdims=True) + EPS)
    return (y * g) * (1.0 + scale) + shift


def _ada_kernel(c_ref, w_ref, b_ref, o_ref):
    c = c_ref[...]
    s = c * jax.nn.sigmoid(c)
    o_ref[...] = jnp.dot(s, w_ref[...], preferred_element_type=F32) + b_ref[...]


def _ada_call(cvec, ada_w, ada_b):
    depth, d, n6 = ada_w.shape
    r = cvec.shape[0]
    tn = n6 // 4
    return pl.pallas_call(
        _ada_kernel,
        grid=(depth, n6 // tn),
        in_specs=[pl.BlockSpec((r, d), lambda l, j: (0, 0)),
                  pl.BlockSpec((None, d, tn), lambda l, j: (l, 0, j)),
                  pl.BlockSpec((None, 1, tn), lambda l, j: (l, 0, j))],
        out_specs=pl.BlockSpec((None, r, tn), lambda l, j: (l, 0, j)),
        out_shape=jax.ShapeDtypeStruct((depth, r, n6), F32),
        compiler_params=_params("arbitrary", "arbitrary"),
        name="adaln",
    )(cvec, ada_w, ada_b.reshape(depth, 1, n6))


def _two_source_specs(width, n_lat_tiles):
    lat = pl.BlockSpec((ROW_TILE, width), lambda i: (jnp.minimum(i, n_lat_tiles - 1), 0))
    ctx = pl.BlockSpec((ROW_TILE, width), lambda i: (jnp.maximum(i - n_lat_tiles, 0), 0))
    return lat, ctx


def _two_source(lat_ref, ctx_ref, n_lat_tiles):
    return jnp.where(pl.program_id(0) >= n_lat_tiles, ctx_ref[...], lat_ref[...])


def _proj_kernel(xl_ref, xc_ref, mod_ref, g_ref, w_ref, o_ref, *, n_lat_tiles):
    x = _two_source(xl_ref, xc_ref, n_lat_tiles)
    n = _modulated_norm(x, g_ref[...], mod_ref[0:1, :], mod_ref[1:2, :])
    o_ref[...] = jnp.dot(n.astype(BF16), w_ref[...], preferred_element_type=F32).astype(o_ref.dtype)


def _proj_res_kernel(h_ref, f_ref, pmod_ref, mod_ref, g_ref, w_ref, x_ref, o_ref):
    x = h_ref[...] + pmod_ref[5:6, :] * f_ref[...]
    x_ref[...] = x
    n = _modulated_norm(x, g_ref[...], mod_ref[0:1, :], mod_ref[1:2, :])
    o_ref[...] = jnp.dot(n.astype(BF16), w_ref[...], preferred_element_type=F32).astype(o_ref.dtype)


def _mod_spec(tiles_per_seq, n_batch):
    return pl.BlockSpec((None, 6, D_MODEL), lambda i: (jnp.minimum(i // tiles_per_seq, n_batch), 0, 0))


def _proj_call(x_lat, x_ctx, mod, g, w, tiles_per_seq, n_batch):
    d = x_lat.shape[1]
    n = x_lat.shape[0] + x_ctx.shape[0]
    n_out = w.shape[1]
    n_lat_tiles = x_lat.shape[0] // ROW_TILE
    lat, ctx = _two_source_specs(d, n_lat_tiles)
    return pl.pallas_call(
        functools.partial(_proj_kernel, n_lat_tiles=n_lat_tiles),
        grid=(n // ROW_TILE,),
        in_specs=[lat, ctx,
                  _mod_spec(tiles_per_seq, n_batch),
                  pl.BlockSpec((1, d), lambda i: (0, 0)),
                  pl.BlockSpec((d, n_out), lambda i: (0, 0))],
        out_specs=pl.BlockSpec((ROW_TILE, n_out), lambda i: (i, 0)),
        out_shape=jax.ShapeDtypeStruct((n, n_out), ACT_DTYPE),
        compiler_params=_params("arbitrary"),
        name="in_proj",
    )(x_lat, x_ctx, mod, g, w)


def _proj_res_call(h, ffn, pmod, mod, g, w, tiles_per_seq, n_batch):
    n, d = h.shape
    n_out = w.shape[1]
    row = pl.BlockSpec((ROW_TILE, d), lambda i: (i, 0))
    return pl.pallas_call(
        _proj_res_kernel,
        grid=(n // ROW_TILE,),
        in_specs=[row, row, _mod_spec(tiles_per_seq, n_batch), _mod_spec(tiles_per_seq, n_batch),
                  pl.BlockSpec((1, d), lambda i: (0, 0)),
                  pl.BlockSpec((d, n_out), lambda i: (0, 0))],
        out_specs=[row, pl.BlockSpec((ROW_TILE, n_out), lambda i: (i, 0))],
        out_shape=[jax.ShapeDtypeStruct((n, d), F32), jax.ShapeDtypeStruct((n, n_out), ACT_DTYPE)],
        compiler_params=_params("arbitrary"),
        name="in_proj_res",
    )(h, ffn, pmod, mod, g, w)


def _qkprep_kernel(q_ref, k_ref, v_ref, c_ref, s_ref, qg_ref, kg_ref, m_ref, qo_ref, kvo_ref):
    cos = c_ref[...]
    sin = s_ref[...]
    lane = lax.broadcasted_iota(jnp.int32, cos.shape, 1)
    first = (lane % 32) < 16
    low = lane < HEAD_DIM

    def norm_rope(x, g):
        ms = jnp.dot(jnp.concatenate(_split2(x * x), axis=1), m_ref[...], preferred_element_type=F32)
        xn = (x * lax.rsqrt(ms + EPS)) * g
        partner = jnp.where(first, pltpu.roll(xn, LANES - 16, 1), pltpu.roll(xn, 16, 1))
        return xn * cos + partner * sin

    for j in range(ATT_Q_WIDTH // LANES):
        qj = norm_rope(q_ref[:, j * LANES:(j + 1) * LANES].astype(F32), qg_ref[...])
        qo_ref[:, j * LANES:(j + 1) * LANES] = (qj * (HEAD_DIM ** -0.5 * LOG2E)).astype(BF16)
    kr = norm_rope(k_ref[...].astype(F32), kg_ref[...])
    ks = pltpu.roll(kr, HEAD_DIM, 1)
    kvo_ref[:, 0:LANES] = jnp.where(low, kr, ks).astype(BF16)
    kvo_ref[:, LANES:2 * LANES] = jnp.where(low, ks, kr).astype(BF16)
    kvo_ref[:, 2 * LANES:3 * LANES] = v_ref[...].astype(BF16)
    kvo_ref[:, 3 * LANES:4 * LANES] = jnp.ones((cos.shape[0], LANES), BF16)


def _qkprep_call(proj, cos_t, sin_t, qg, kg, seg_mean, tiles_per_seq, n_lat_tiles):
    n = proj.shape[0]
    qcol = (2 * LRU_WIDTH) // ATT_Q_WIDTH
    kcol = (2 * LRU_WIDTH + ATT_Q_WIDTH) // LANES

    def pos(i):
        return (jnp.where(i < n_lat_tiles, i % tiles_per_seq, tiles_per_seq), 0)

    return pl.pallas_call(
        _qkprep_kernel,
        grid=(n // ROW_TILE,),
        in_specs=[pl.BlockSpec((ROW_TILE, ATT_Q_WIDTH), lambda i: (i, qcol)),
                  pl.BlockSpec((ROW_TILE, LANES), lambda i: (i, kcol)),
                  pl.BlockSpec((ROW_TILE, LANES), lambda i: (i, kcol + 1)),
                  pl.BlockSpec((ROW_TILE, LANES), pos),
                  pl.BlockSpec((ROW_TILE, LANES), pos),
                  pl.BlockSpec((1, LANES), lambda i: (0, 0)),
                  pl.BlockSpec((1, LANES), lambda i: (0, 0)),
                  pl.BlockSpec((2 * LANES, LANES), lambda i: (0, 0))],
        out_specs=[pl.BlockSpec((ROW_TILE, ATT_Q_WIDTH), lambda i: (i, 0)),
                   pl.BlockSpec((ROW_TILE, 4 * LANES), lambda i: (i, 0))],
        out_shape=[jax.ShapeDtypeStruct((n, ATT_Q_WIDTH), BF16),
                   jax.ShapeDtypeStruct((n, 4 * LANES), BF16)],
        compiler_params=_params("arbitrary"),
        name="qk_prep",
    )(proj, proj, proj, cos_t, sin_t, qg, kg, seg_mean)


def _attend(q, kv, sink_ref, bias):
    rows = q.shape[0]
    lane = lax.broadcasted_iota(jnp.int32, (rows, LANES), 1)
    low = lane < HEAD_DIM
    zero = jnp.zeros((rows, LANES), BF16)
    vv = kv[:, 2 * LANES:4 * LANES]
    groups = []
    heads_per_kv = ATT_HEADS // ATT_KV_HEADS
    for kh in range(ATT_KV_HEADS):
        kk = kv[:, kh * LANES:(kh + 1) * LANES]
        parts = []
        sinks = []
        for j in range(heads_per_kv):
            h = kh * heads_per_kv + j
            qg = q[:, (h // 2) * LANES:(h // 2 + 1) * LANES]
            parts.append(jnp.where(low, qg, zero) if h % 2 == 0 else jnp.where(low, zero, qg))
            sinks.append(jnp.full((rows, 1), sink_ref[h] * LOG2E, F32))
        lhs = jnp.concatenate(parts, axis=0)
        sink = jnp.concatenate(sinks, axis=0)
        s = lax.dot_general(lhs, kk, (((1,), (1,)), ((), ())), preferred_element_type=F32)
        if bias is not None:
            s = s + bias
        m = jnp.maximum(jnp.max(s, axis=-1, keepdims=True), sink)
        e = jnp.exp2(s - m)
        ov = jnp.dot(e.astype(BF16), vv, preferred_element_type=F32)
        o = ov[:, 0:LANES] / (ov[:, LANES:2 * LANES] + jnp.exp2(sink - m))
        r = [o[j * rows:(j + 1) * rows] for j in range(heads_per_kv)]
        for p in range(heads_per_kv // 2):
            even, odd = r[2 * p], r[2 * p + 1]
            if kh == 0:
                groups.append(jnp.where(low, even, pltpu.roll(odd, HEAD_DIM, 1)))
            else:
                groups.append(jnp.where(low, pltpu.roll(even, HEAD_DIM, 1), odd))
    return jnp.concatenate(groups, axis=1)


def _attn_kernel(sink_ref, q_ref, kvp_ref, kvo_ref, kvn_ref, kvc_ref, bias0_ref, bias1_ref, o_ref):
    blk = ATT_BLOCK
    own = kvo_ref[...]
    ctx = kvc_ref[...]
    kv0 = jnp.concatenate([kvp_ref[...], own, ctx], axis=0)
    kv1 = jnp.concatenate([own, kvn_ref[...], ctx], axis=0)
    o_ref[0:blk, :] = _attend(q_ref[0:blk, :], kv0, sink_ref, bias0_ref[...]).astype(o_ref.dtype)
    o_ref[blk:2 * blk, :] = _attend(q_ref[blk:2 * blk, :], kv1, sink_ref, bias1_ref[...]).astype(o_ref.dtype)


def _window_bias(ctx_len):
    blk = ATT_BLOCK
    stacked = (ATT_HEADS // ATT_KV_HEADS) * blk
    qi = lax.broadcasted_iota(jnp.int32, (4, stacked, 3 * blk + ctx_len), 1) % blk
    ks = lax.broadcasted_iota(jnp.int32, (4, stacked, 3 * blk + ctx_len), 2)
    var = lax.broadcasted_iota(jnp.int32, (4, stacked, 3 * blk + ctx_len), 0)
    in_win = jnp.abs(ks - blk - qi) <= WINDOW
    in_seq = ((ks >= blk) | (var % 2 == 1)) & ((ks < 2 * blk) | (var >= 2))
    keep = (in_win & in_seq) | (ks >= 3 * blk)
    return jnp.where(keep, 0.0, -1e30).astype(F32)


def _ctx_attn_kernel(sink_ref, q_ref, kvc_ref, o_ref):
    o_ref[...] = _attend(q_ref[...], kvc_ref[...], sink_ref, None).astype(o_ref.dtype)


def _attn_call(qrot, kvrot, sink, n_batch, seq, ctx_len):
    nb = seq // ATT_BLOCK
    ctx0 = (n_batch * seq) // ctx_len
    kvw = kvrot.shape[1]
    bias = _window_bias(ctx_len)

    assert nb % 2 == 0
    pairs = nb // 2

    def variant0(b, m):
        return ((m > 0).astype(jnp.int32) + 2, 0, 0)

    def variant1(b, m):
        return (1 + 2 * (m < pairs - 1).astype(jnp.int32), 0, 0)

    blk = ATT_BLOCK
    return pl.pallas_call(
        _attn_kernel,
        grid=(n_batch, pairs),
        in_specs=[pl.BlockSpec(memory_space=pltpu.SMEM),
                  pl.BlockSpec((2 * blk, ATT_Q_WIDTH), lambda b, m: (b * pairs + m, 0)),
                  pl.BlockSpec((blk, kvw), lambda b, m: (b * nb + jnp.maximum(2 * m - 1, 0), 0)),
                  pl.BlockSpec((2 * blk, kvw), lambda b, m: (b * pairs + m, 0)),
                  pl.BlockSpec((blk, kvw), lambda b, m: (b * nb + jnp.minimum(2 * m + 2, nb - 1), 0)),
                  pl.BlockSpec((ctx_len, kvw), lambda b, m: (ctx0 + b, 0)),
                  pl.BlockSpec((None,) + bias.shape[1:], variant0),
                  pl.BlockSpec((None,) + bias.shape[1:], variant1)],
        out_specs=pl.BlockSpec((2 * blk, ATT_Q_WIDTH), lambda b, m: (b * pairs + m, 0)),
        out_shape=jax.ShapeDtypeStruct((n_batch * seq, ATT_Q_WIDTH), ACT_DTYPE),
        compiler_params=_params("arbitrary", "arbitrary"),
        name="window_attn",
    )(sink, qrot, kvrot, kvrot, kvrot, kvrot, bias, bias)


def _ctx_attn_call(qrot, kvrot, sink, n_batch, seq, ctx_len):
    ctx0 = (n_batch * seq) // ctx_len
    kvw = kvrot.shape[1]
    return pl.pallas_call(
        _ctx_attn_kernel,
        grid=(n_batch,),
        in_specs=[pl.BlockSpec(memory_space=pltpu.SMEM),
                  pl.BlockSpec((ctx_len, ATT_Q_WIDTH), lambda b: (ctx0 + b, 0)),
                  pl.BlockSpec((ctx_len, kvw), lambda b: (ctx0 + b, 0))],
        out_specs=pl.BlockSpec((ctx_len, ATT_Q_WIDTH), lambda b: (b, 0)),
        out_shape=jax.ShapeDtypeStruct((n_batch * ctx_len, ATT_Q_WIDTH), ACT_DTYPE),
        compiler_params=_params("arbitrary"),
        name="ctx_attn",
    )(sink, qrot, kvrot)


def _scan_rows(a, x, carry, reverse):
    rows, width = a.shape
    n_groups = rows // SUBLANES
    a = a.reshape(n_groups, SUBLANES, width)
    x = x.reshape(n_groups, SUBLANES, width)
    sub = lax.broadcasted_iota(jnp.int32, a.shape, 1)
    for s in (1, 2, 4):
        shift = SUBLANES - s if reverse else s
        a_sh = pltpu.roll(a, shift, 1)
        x_sh = pltpu.roll(x, shift, 1)
        keep = (sub <= SUBLANES - 1 - s) if reverse else (sub >= s)
        x = x + a * jnp.where(keep, x_sh, 0.0)
        a = a * jnp.where(keep, a_sh, 1.0)
    a = a.reshape(rows, width)
    x = x.reshape(rows, width)
    out = [None] * n_groups
    order = range(n_groups - 1, -1, -1) if reverse else range(n_groups)
    for g in order:
        hg = x[g * SUBLANES:(g + 1) * SUBLANES] + a[g * SUBLANES:(g + 1) * SUBLANES] * carry
        carry = hg[0:1] if reverse else hg[SUBLANES - 1:SUBLANES]
        out[g] = hg
    return jnp.concatenate(out, axis=0), carry


def _rglru_kernel(ul_ref, gl_ref, uc_ref, gc_ref, cw_ref, cb_ref, wr_ref, wi_ref, br_ref, bi_ref, lam_ref,
                  yl_ref, yc_ref, upad_ref, conv_ref, *, seq, ctx_len):
    width = ul_ref.shape[1]
    tc = LRU_CHUNK
    neg_lam = -lam_ref[...]
    softplus = jnp.maximum(neg_lam, 0.0) + jnp.log1p(jnp.exp(-jnp.abs(neg_lam)))
    rate = (-LRU_C * LOG2E) * softplus
    zeros = jnp.zeros((SUBLANES, width), F32)

    def gates(u, d):
        ub = u.astype(BF16)
        r = jax.nn.sigmoid(jnp.dot(ub, wr_ref[d], preferred_element_type=F32) + br_ref[d:d + 1, :])
        i = jax.nn.sigmoid(jnp.dot(ub, wi_ref[d], preferred_element_type=F32) + bi_ref[d:d + 1, :])
        a = jnp.exp2(r * rate[d:d + 1, :])
        y = 1.0 - a * a
        root = jnp.where(y > 0.0, y * lax.rsqrt(y), 0.0)
        return a, root * (i * u)

    def run(u_ref, g_ref, y_ref, n, h_fwd, h_bwd):
        n_chunks = n // tc
        upad_ref[0:SUBLANES, :] = zeros
        upad_ref[pl.ds(SUBLANES, n), :] = u_ref[...].astype(F32)
        upad_ref[pl.ds(SUBLANES + n, SUBLANES), :] = zeros

        def conv_body(c, carry):
            t0 = pl.multiple_of(c * tc, tc)
            win = upad_ref[pl.ds(t0, tc + 2 * SUBLANES), :]
            acc = cb_ref[...]
            for k in range(CONV_W):
                acc = acc + cw_ref[k:k + 1, :] * win[SUBLANES - 1 + k:SUBLANES - 1 + k + tc]
            conv_ref[pl.ds(t0, tc), :] = acc
            return carry

        lax.fori_loop(0, n_chunks, conv_body, 0)

        def fwd_body(c, carry):
            t0 = pl.multiple_of(c * tc, tc)
            a, x = gates(conv_ref[pl.ds(t0, tc), :], 0)
            h, carry = _scan_rows(a, x, carry, False)
            upad_ref[pl.ds(pl.multiple_of(t0 + SUBLANES, SUBLANES), tc), :] = h
            return carry

        h_fwd = lax.fori_loop(0, n_chunks, fwd_body, h_fwd, unroll=min(4, n_chunks))

        def bwd_body(c, carry):
            t0 = pl.multiple_of((n_chunks - 1 - c) * tc, tc)
            a, x = gates(conv_ref[pl.ds(t0, tc), :], 1)
            h, carry = _scan_rows(a, x, carry, True)
            h_both = upad_ref[pl.ds(pl.multiple_of(t0 + SUBLANES, SUBLANES), tc), :] + h
            y_ref[pl.ds(t0, tc), :] = (h_both * jax.nn.gelu(g_ref[pl.ds(t0, tc), :].astype(F32))).astype(y_ref.dtype)
            return carry

        h_bwd = lax.fori_loop(0, n_chunks, bwd_body, h_bwd, unroll=min(4, n_chunks))
        return h_fwd, h_bwd

    h0 = jnp.zeros((1, width), F32)
    h_fwd, h_bwd = run(uc_ref, gc_ref, yc_ref, ctx_len, h0, h0)
    run(ul_ref, gl_ref, yl_ref, seq, h_fwd, h_bwd)


def _rglru_call(proj, conv_w, conv_b, wr, wi, br, bi, lam, n_batch, seq, ctx_len):
    ctx0 = (n_batch * seq) // ctx_len
    halves = LRU_WIDTH // LRU_HALF
    w = LRU_HALF
    vec2 = pl.BlockSpec((2, w), lambda b, c: (0, c))
    return pl.pallas_call(
        functools.partial(_rglru_kernel, seq=seq, ctx_len=ctx_len),
        grid=(n_batch, halves),
        in_specs=[pl.BlockSpec((seq, w), lambda b, c: (b, c)),
                  pl.BlockSpec((seq, w), lambda b, c: (b, halves + c)),
                  pl.BlockSpec((ctx_len, w), lambda b, c: (ctx0 + b, c)),
                  pl.BlockSpec((ctx_len, w), lambda b, c: (ctx0 + b, halves + c)),
                  pl.BlockSpec((CONV_W, w), lambda b, c: (0, c)),
                  pl.BlockSpec((1, w), lambda b, c: (0, c)),
                  pl.BlockSpec((None, 2, w, w), lambda b, c: (c, 0, 0, 0)),
                  pl.BlockSpec((None, 2, w, w), lambda b, c: (c, 0, 0, 0)),
                  vec2, vec2, vec2],
        out_specs=[pl.BlockSpec((seq, w), lambda b, c: (b, c)),
                   pl.BlockSpec((ctx_len, w), lambda b, c: (b, c))],
        out_shape=[jax.ShapeDtypeStruct((n_batch * seq, LRU_WIDTH), ACT_DTYPE),
                   jax.ShapeDtypeStruct((n_batch * ctx_len, LRU_WIDTH), ACT_DTYPE)],
        scratch_shapes=[pltpu.VMEM((seq + 2 * SUBLANES, w), F32), pltpu.VMEM((seq, w), F32)],
        compiler_params=_params("arbitrary", "arbitrary"),
        name="rglru",
    )(proj, proj, proj, proj, conv_w, conv_b, wr, wi, br, bi, lam)


def _split2(x):
    hi = x.astype(BF16)
    return hi, (x - hi.astype(F32)).astype(BF16)


def _gla_kernel(q_ref, k_ref, v_ref, r_ref, a_ref, kc_ref, vc_ref, ac_ref, gw_ref, gb_ref, ng_ref,
                o_ref, of_ref, ob_ref, qd_ref, ki_ref, ke_ref, la_ref, eg_ref, kv_ref, st_ref, *, seq, ctx_len):
    ch = GLA_CHUNK
    blk = GLA_BLOCK
    per_blk = blk // ch
    chunk_of_row = lax.broadcasted_iota(jnp.int32, (blk, GLA_DK), 0) // ch
    row = lax.broadcasted_iota(jnp.int32, (blk, blk), 0)
    col = lax.broadcasted_iota(jnp.int32, (blk, blk), 1)
    same = (row // ch) == (col // ch)
    keep = (same & (row >= col), same & (row <= col))
    tri = (keep[0].astype(BF16),)
    scale = GLA_DK ** -0.5
    nt = (((1,), (1,)), ((), ()))
    tn = (((0,), (0,)), ((), ()))

    def mm(a, b):
        return jnp.dot(a, b, preferred_element_type=F32)

    def prepare(a_src, k_src, v_src, q_src, dst, n_rows, d):
        n_blk = n_rows // blk
        unroll = min(4, n_blk)

        def decay_body(i, carry):
            t0 = pl.multiple_of(i * blk, blk)
            both = mm(a_src[pl.ds(t0, blk), :], gw_ref[d])
            pre = (both[:, 0:GLA_DK] + both[:, GLA_DK:2 * GLA_DK]) + gb_ref[d:d + 1, :]
            la_ref[pl.ds(t0, blk), :] = (jnp.minimum(pre, 0.0) - jnp.log(1.0 + jnp.exp(-jnp.abs(pre)))) / GLA_TAU
            return carry

        lax.fori_loop(0, n_blk, decay_body, 0, unroll=min(8, n_blk))

        def scale_body(i, carry):
            t0 = pl.multiple_of(i * blk, blk)
            log_a = la_ref[pl.ds(t0, blk), :]
            sums = mm(tri[0], jnp.concatenate(_split2(log_a), axis=1))
            prefix = sums[:, 0:GLA_DK] + sums[:, GLA_DK:2 * GLA_DK]
            total = jnp.concatenate(
                [jnp.broadcast_to(prefix[(j + 1) * ch - 1:(j + 1) * ch], (ch, GLA_DK)) for j in range(per_blk)],
                axis=0)
            suffix = total - prefix + log_a
            b = prefix if d == 0 else suffix
            to_end = (suffix if d == 0 else prefix) - log_a
            for j in range(per_blk):
                eg_ref[pl.ds(i * per_blk + j, 1), :] = jnp.exp(total[j * ch:j * ch + 1])
            k_blk = k_src[pl.ds(t0, blk), :].astype(F32)
            ke_ref[pl.ds(t0, blk), :] = (k_blk * jnp.exp(to_end)).astype(BF16)
            if q_src is not None:
                q_blk = q_src[pl.ds(t0, blk), :].astype(F32)
                qd_ref[pl.ds(t0, blk), :] = ((q_blk * scale) * jnp.exp(b)).astype(BF16)
                ki_ref[pl.ds(t0, blk), :] = (k_blk * jnp.exp(-b)).astype(BF16)
            return carry

        lax.fori_loop(0, n_blk, scale_body, 0, unroll=unroll)

        def intra_body(i, carry):
            t0 = pl.multiple_of(i * blk, blk)
            v_blk = v_src[pl.ds(t0, blk), :].astype(BF16)
            k_end = ke_ref[pl.ds(t0, blk), :]
            spread = jnp.concatenate([jnp.where(chunk_of_row == j, k_end, jnp.zeros_like(k_end))
                                      for j in range(per_blk)], axis=1)
            kv = lax.dot_general(v_blk, spread, tn, preferred_element_type=F32)
            for j in range(per_blk):
                kv_ref[i * per_blk + j] = kv[:, j * GLA_DK:(j + 1) * GLA_DK]
            if q_src is not None:
                s = lax.dot_general(qd_ref[pl.ds(t0, blk), :], ki_ref[pl.ds(t0, blk), :], nt,
                                    preferred_element_type=F32)
                p = jnp.where(keep[d], s, 0.0).astype(BF16)
                dst[pl.ds(t0, blk), :] = mm(p, v_blk)
            return carry

        lax.fori_loop(0, n_blk, intra_body, 0, unroll=min(8, n_blk))

    def recur(dst, n_rows, d):
        n_chunks = n_rows // ch

        def body(c, carry):
            n = c if d == 0 else n_chunks - 1 - c
            t0 = pl.multiple_of(n * ch, ch)
            st = st_ref[...]
            if dst is not None:
                dst[pl.ds(t0, ch), :] = dst[pl.ds(t0, ch), :] + lax.dot_general(
                    qd_ref[pl.ds(t0, ch), :], st.astype(BF16), nt, preferred_element_type=F32)
            st_ref[...] = st * eg_ref[pl.ds(n, 1), :] + kv_ref[n]
            return carry

        lax.fori_loop(0, n_chunks, body, 0, unroll=min(16, n_chunks))

    for d, dst in ((0, of_ref), (1, ob_ref)):
        st_ref[...] = jnp.zeros(st_ref.shape, F32)
        prepare(ac_ref, kc_ref, vc_ref, None, None, ctx_len, d)
        recur(None, ctx_len, d)
        prepare(a_ref, k_ref, v_ref, q_ref, dst, seq, d)
        recur(dst, seq, d)

    def finish_body(c, carry):
        t0 = pl.multiple_of(c * blk, blk)
        o = of_ref[pl.ds(t0, blk), :] + ob_ref[pl.ds(t0, blk), :]
        on = (o * lax.rsqrt(jnp.mean(o * o, axis=-1, keepdims=True) + EPS)) * ng_ref[...]
        r = r_ref[pl.ds(t0, blk), :].astype(F32)
        o_ref[pl.ds(t0, blk), :] = (on * (r * jax.nn.sigmoid(r))).astype(o_ref.dtype)
        return carry

    lax.fori_loop(0, seq // blk, finish_body, 0)


def _gla_call(proj, gate_w, gate_b, norm_g, n_batch, seq, ctx_len):
    ctx0 = (n_batch * seq) // ctx_len
    kcol = GLA_K_WIDTH // GLA_DK
    vcol = (2 * GLA_K_WIDTH) // GLA_DV
    rcol = (2 * GLA_K_WIDTH + GLA_V_WIDTH) // GLA_DV
    acol = (2 * GLA_K_WIDTH + 2 * GLA_V_WIDTH) // LANES
    return pl.pallas_call(
        functools.partial(_gla_kernel, seq=seq, ctx_len=ctx_len),
        grid=(n_batch, GLA_HEADS),
        in_specs=[pl.BlockSpec((seq, GLA_DK), lambda b, h: (b, h)),
                  pl.BlockSpec((seq, GLA_DK), lambda b, h: (b, kcol + h)),
                  pl.BlockSpec((seq, GLA_DV), lambda b, h: (b, vcol + h)),
                  pl.BlockSpec((seq, GLA_DV), lambda b, h: (b, rcol + h)),
                  pl.BlockSpec((seq, LANES), lambda b, h: (b, acol)),
                  pl.BlockSpec((ctx_len, GLA_DK), lambda b, h: (ctx0 + b, kcol + h)),
                  pl.BlockSpec((ctx_len, GLA_DV), lambda b, h: (ctx0 + b, vcol + h)),
                  pl.BlockSpec((ctx_len, LANES), lambda b, h: (ctx0 + b, acol)),
                  pl.BlockSpec((None, 2, LANES, 2 * GLA_DK), lambda b, h: (h, 0, 0, 0)),
                  pl.BlockSpec((None, 2, GLA_DK), lambda b, h: (h, 0, 0)),
                  pl.BlockSpec((1, GLA_DV), lambda b, h: (0, 0))],
        out_specs=pl.BlockSpec((seq, GLA_DV), lambda b, h: (b, h)),
        out_shape=jax.ShapeDtypeStruct((n_batch * seq, GLA_V_WIDTH), ACT_DTYPE),
        scratch_shapes=[pltpu.VMEM((seq, GLA_DV), F32),
                        pltpu.VMEM((seq, GLA_DV), F32),
                        pltpu.VMEM((seq, GLA_DK), BF16),
                        pltpu.VMEM((seq, GLA_DK), BF16),
                        pltpu.VMEM((seq, GLA_DK), BF16),
                        pltpu.VMEM((seq, GLA_DK), F32),
                        pltpu.VMEM((seq // GLA_CHUNK, GLA_DK), F32),
                        pltpu.VMEM((seq // GLA_CHUNK, GLA_DV, GLA_DK), F32),
                        pltpu.VMEM((GLA_DV, GLA_DK), F32)],
        compiler_params=_params("arbitrary", "arbitrary"),
        name="gla",
    )(proj, proj, proj, proj, proj, proj, proj, proj, gate_w, gate_b, norm_g)


def _route(rows, rows_bf16, rw_ref, rb_ref):
    n_rows = rows.shape[0]
    rows_lo = (rows - rows_bf16.astype(F32)).astype(BF16)
    both = jnp.dot(rows_bf16, rw_ref[...], preferred_element_type=F32)
    logits_rows = (both[:, 0:LANES] + both[:, LANES:2 * LANES]
                   + jnp.dot(rows_lo, rw_ref[:, 0:LANES], preferred_element_type=F32))
    logits = logits_rows.T[0:N_EXPERTS, :]
    scores = jax.nn.sigmoid(logits)
    sel = scores + rb_ref[...]
    eid = lax.broadcasted_iota(jnp.int32, (N_EXPERTS, n_rows), 0)
    gid = eid // EXPERTS_PER_GROUP
    neg = -jnp.inf
    big = N_EXPERTS

    def top2(masked):
        m1 = jnp.max(masked, axis=0, keepdims=True)
        i1 = jnp.min(jnp.where(masked == m1, eid, big), axis=0, keepdims=True)
        rest = jnp.where(eid == i1, neg, masked)
        m2 = jnp.max(rest, axis=0, keepdims=True)
        i2 = jnp.min(jnp.where(rest == m2, eid, big), axis=0, keepdims=True)
        return m1, i1, m2, i2

    best = None
    best_g = None
    for g in range(N_GROUPS):
        m1, _, m2, _ = top2(jnp.where(gid == g, sel, neg))
        gs = m1 + m2
        if best is None:
            best, best_g = gs, jnp.zeros((1, n_rows), jnp.int32)
        else:
            better = gs > best
            best_g = jnp.where(better, g, best_g)
            best = jnp.where(better, gs, best)
    _, i1, _, i2 = top2(jnp.where(gid == best_g, sel, neg))
    s1 = jnp.sum(jnp.where(eid == i1, scores, 0.0), axis=0, keepdims=True)
    s2 = jnp.sum(jnp.where(eid == i2, scores, 0.0), axis=0, keepdims=True)
    total = s1 + s2
    swap = i2 < i1
    lo = jnp.where(swap, i2, i1)
    hi = jnp.where(swap, i1, i2)
    w_lo = jnp.where(swap, s2, s1) / total
    w_hi = jnp.where(swap, s1, s2) / total
    pad = jnp.zeros((4, n_rows), F32)
    return jnp.concatenate([lo.astype(F32), hi.astype(F32), w_lo, w_hi, pad], axis=0)


def _finish_rows(y, h, mod_ref, g_ref, rw_ref, rb_ref, hn_ref, rows_ref, rt_ref, rc_ref):
    hn = h + mod_ref[2:3, :] * y
    hn_ref[...] = hn
    rows = _modulated_norm(hn, g_ref[...], mod_ref[3:4, :], mod_ref[4:5, :])
    rows_bf16 = rows.astype(BF16)
    rows_ref[...] = rows_bf16
    route = _route(rows, rows_bf16, rw_ref, rb_ref)
    rt_ref[...] = route
    padded = jnp.concatenate([route, jnp.zeros((LANES - SUBLANES, route.shape[1]), F32)], axis=0)
    rc_ref[...] = padded.T


def _outproj_even_kernel(al_ref, ac_ref, bl_ref, bc_ref, w_ref, hl_ref, hc_ref, mod_ref, g_ref, rw_ref, rb_ref,
                         hn_ref, rows_ref, rt_ref, rc_ref, *, n_lat_tiles):
    a = _two_source(al_ref, ac_ref, n_lat_tiles).astype(BF16)
    b = _two_source(bl_ref, bc_ref, n_lat_tiles).astype(BF16)
    y = (jnp.dot(a, w_ref[0:LRU_WIDTH, :], preferred_element_type=F32)
         + jnp.dot(b, w_ref[LRU_WIDTH:LRU_WIDTH + ATT_Q_WIDTH, :], preferred_element_type=F32))
    h = _two_source(hl_ref, hc_ref, n_lat_tiles)
    _finish_rows(y, h, mod_ref, g_ref, rw_ref, rb_ref, hn_ref, rows_ref, rt_ref, rc_ref)


def _outproj_odd_kernel(o_ref, w_ref, h_ref, mod_ref, g_ref, rw_ref, rb_ref, hn_ref, rows_ref, rt_ref, rc_ref):
    y = jnp.dot(o_ref[...].astype(BF16), w_ref[...], preferred_element_type=F32)
    _finish_rows(y, h_ref[...], mod_ref, g_ref, rw_ref, rb_ref, hn_ref, rows_ref, rt_ref, rc_ref)


def _outproj_specs(n_rows, tiles_per_seq, n_batch):
    d = D_MODEL
    row = pl.BlockSpec((ROW_TILE, d), lambda i: (i, 0))
    tail_in = [_mod_spec(tiles_per_seq, n_batch),
               pl.BlockSpec((1, d), lambda i: (0, 0)),
               pl.BlockSpec((d, 2 * LANES), lambda i: (0, 0)),
               pl.BlockSpec((N_EXPERTS, 1), lambda i: (0, 0))]
    out_specs = [row, row, pl.BlockSpec((SUBLANES, ROW_TILE), lambda i: (0, i)),
                 pl.BlockSpec((ROW_TILE, LANES), lambda i: (i, 0))]
    out_shape = [jax.ShapeDtypeStruct((n_rows, d), F32), jax.ShapeDtypeStruct((n_rows, d), ACT_DTYPE),
                 jax.ShapeDtypeStruct((SUBLANES, n_rows), F32), jax.ShapeDtypeStruct((n_rows, LANES), F32)]
    return tail_in, out_specs, out_shape


def _outproj_even_call(a_lat, a_ctx, b_lat, b_ctx, w, h_lat, h_ctx, mod, g, rw_t, rb, tiles_per_seq, n_batch):
    n_rows = h_lat.shape[0] + h_ctx.shape[0]
    n_lat_tiles = a_lat.shape[0] // ROW_TILE
    tail_in, out_specs, out_shape = _outproj_specs(n_rows, tiles_per_seq, n_batch)
    lat, ctx = _two_source_specs(LRU_WIDTH, n_lat_tiles)
    h_specs = list(_two_source_specs(D_MODEL, n_lat_tiles))
    return pl.pallas_call(
        functools.partial(_outproj_even_kernel, n_lat_tiles=n_lat_tiles),
        grid=(n_rows // ROW_TILE,),
        in_specs=[lat, ctx, lat, ctx, pl.BlockSpec(w.shape, lambda i: (0, 0))] + h_specs + tail_in,
        out_specs=out_specs,
        out_shape=out_shape,
        compiler_params=_params("arbitrary"),
        name="out_proj_even",
    )(a_lat, a_ctx, b_lat, b_ctx, w, h_lat, h_ctx, mod, g, rw_t, rb)


def _outproj_odd_call(o, w, h, mod, g, rw_t, rb, tiles_per_seq, n_batch):
    n_rows = o.shape[0]
    tail_in, out_specs, out_shape = _outproj_specs(n_rows, tiles_per_seq, n_batch)
    return pl.pallas_call(
        _outproj_odd_kernel,
        grid=(n_rows // ROW_TILE,),
        in_specs=[pl.BlockSpec((ROW_TILE, o.shape[1]), lambda i: (i, 0)),
                  pl.BlockSpec(w.shape, lambda i: (0, 0)),
                  pl.BlockSpec((ROW_TILE, D_MODEL), lambda i: (i, 0))] + tail_in,
        out_specs=out_specs,
        out_shape=out_shape,
        compiler_params=_params("arbitrary"),
        name="out_proj_odd",
    )(o, w, h, mod, g, rw_t, rb)


X_SLAB = D_MODEL // LANES
Y_SLAB = 2 * X_SLAB


def _for_rows(n_rows, fn):
    for g in range(n_rows // SUBLANES):
        for u in range(SUBLANES):
            fn(g, u)


def _dispatch_kernel(tail_ref, slot_ref, rows_ref, xs_ref, slab_ref, zero_ref, sem, zero_sem, *, steps, n_tiles):
    i = pl.program_id(0)
    k = i % 2
    rt = MOE_ROWS
    tile_rows = MOE_TILE * X_SLAB

    def slab_copy(buf, g, u, slot):
        return pltpu.make_async_copy(
            slab_ref.at[buf, g, :, pl.ds(u, 1), :],
            xs_ref.at[pl.ds(pl.multiple_of(slot * X_SLAB, X_SLAB), X_SLAB)], sem.at[buf])

    def wait_all(buf):
        pltpu.make_async_copy(slab_ref.at[buf], slab_ref.at[buf], sem.at[buf]).wait()

    @pl.when(i == 0)
    def _():
        zero_ref[...] = jnp.zeros(zero_ref.shape, F32)

        def fill(t):
            start = pl.multiple_of(t * tile_rows, tile_rows)
            return pltpu.make_async_copy(zero_ref, xs_ref.at[pl.ds(start, tile_rows)], zero_sem)

        n_used = tail_ref[1, N_CLASSES - 1]
        for phase in ("start", "wait"):
            for c in range(N_CLASSES):
                for cond, t in ((tail_ref[0, c] > 0, tail_ref[1, c] - 1), (n_used + c < n_tiles, n_used + c)):
                    @pl.when(cond)
                    def _():
                        getattr(fill(t), phase)()

    @pl.when(i >= 2)
    def _():
        wait_all(k)

    for j in range(X_SLAB):
        lanes = rows_ref[:, j * LANES:(j + 1) * LANES].astype(F32)
        slab_ref[k, :, j, :, :] = lanes.reshape(rt // SUBLANES, SUBLANES, LANES)

    _for_rows(rt, lambda g, u: slab_copy(k, g, u, slot_ref[0, 0, g * SUBLANES + u]).start(priority=u % 2))

    @pl.when(i == steps - 1)
    def _():
        wait_all(k)
        if steps >= 2:
            wait_all(1 - k)


def _dispatch_call(rows, slots, tails, n_slots):
    n, d = rows.shape
    steps = n // MOE_ROWS
    grid_spec = pltpu.PrefetchScalarGridSpec(
        num_scalar_prefetch=1,
        grid=(steps,),
        in_specs=[pl.BlockSpec((1, 1, MOE_ROWS), lambda i, tails: (i, 0, 0), memory_space=pltpu.SMEM),
                  pl.BlockSpec((MOE_ROWS, d), lambda i, tails: (i, 0))],
        out_specs=pl.BlockSpec(memory_space=pl.ANY),
        scratch_shapes=[pltpu.VMEM((2, MOE_ROWS // SUBLANES, X_SLAB, SUBLANES, LANES), F32),
                        pltpu.VMEM((MOE_TILE * X_SLAB, 1, LANES), F32),
                        pltpu.SemaphoreType.DMA((2,)), pltpu.SemaphoreType.DMA(())],
    )
    xs = pl.pallas_call(
        functools.partial(_dispatch_kernel, steps=steps, n_tiles=n_slots // MOE_TILE),
        grid_spec=grid_spec,
        out_shape=jax.ShapeDtypeStruct((n_slots * X_SLAB, 1, LANES), F32),
        compiler_params=_params("arbitrary"),
        name="moe_dispatch",
    )(tails, slots.reshape(steps, 1, MOE_ROWS), rows)
    return xs.reshape(n_slots * X_SLAB, LANES)


def _experts_kernel(elo_ref, ehi_ref, used_ref, x_ref, wg0_ref, wu0_ref, wd0_ref,
                    wg1_ref, wu1_ref, wd1_ref, y_ref, cg0_ref, cu0_ref, cd0_ref, cg1_ref, cu1_ref, cd1_ref):
    t = pl.program_id(0)
    tm = MOE_TILE
    prev = jnp.maximum(t - 1, 0)
    sides = ((elo_ref, (wg0_ref, wu0_ref, wd0_ref), (cg0_ref, cu0_ref, cd0_ref)),
             (ehi_ref, (wg1_ref, wu1_ref, wd1_ref), (cg1_ref, cu1_ref, cd1_ref)))

    @pl.when(t < used_ref[0])
    def _():
        for ids, weights, cached in sides:
            @pl.when((t == 0) | (ids[t] != ids[prev]))
            def _():
                for w_ref, c_ref in zip(weights, cached):
                    c_ref[...] = w_ref[...].astype(BF16)

        x = jnp.concatenate([x_ref[pl.ds(j, tm, stride=X_SLAB), :] for j in range(X_SLAB)],
                            axis=1).astype(BF16)
        for e, (_, _, (cg_ref, cu_ref, cd_ref)) in enumerate(sides):
            gate = jnp.dot(x, cg_ref[...], preferred_element_type=F32)
            up = jnp.dot(x, cu_ref[...], preferred_element_type=F32)
            act = ((gate * jax.nn.sigmoid(gate)) * up).astype(BF16)
            f = jnp.dot(act, cd_ref[...], preferred_element_type=F32)
            for j in range(X_SLAB):
                y_ref[pl.ds(e * X_SLAB + j, tm, stride=Y_SLAB), :] = f[:, j * LANES:(j + 1) * LANES]

    @pl.when(t >= used_ref[0])
    def _():
        y_ref[...] = jnp.zeros_like(y_ref)


def _experts_call(xs, e_lo, e_hi, n_used, w_gate, w_up, w_down, layer):
    d = D_MODEL
    tiles = xs.shape[0] // (MOE_TILE * X_SLAB)

    def xmap(t, elo, ehi, used):
        return (jnp.maximum(jnp.minimum(t, used[0] - 1), 0), 0)

    def lo(t, elo, ehi, used):
        return (layer, elo[t], 0, 0)

    def hi(t, elo, ehi, used):
        return (layer, ehi[t], 0, 0)

    up_spec = lambda m: pl.BlockSpec((None, None, d, D_EXPERT), m)
    down_spec = lambda m: pl.BlockSpec((None, None, D_EXPERT, d), m)
    cached = [pltpu.VMEM((d, D_EXPERT), BF16), pltpu.VMEM((d, D_EXPERT), BF16), pltpu.VMEM((D_EXPERT, d), BF16)]
    grid_spec = pltpu.PrefetchScalarGridSpec(
        num_scalar_prefetch=3,
        grid=(tiles,),
        in_specs=[pl.BlockSpec((MOE_TILE * X_SLAB, LANES), xmap),
                  up_spec(lo), up_spec(lo), down_spec(lo),
                  up_spec(hi), up_spec(hi), down_spec(hi)],
        out_specs=pl.BlockSpec((MOE_TILE * Y_SLAB, LANES), lambda t, elo, ehi, used: (t, 0)),
        scratch_shapes=cached + cached,
    )
    return pl.pallas_call(
        _experts_kernel,
        grid_spec=grid_spec,
        out_shape=jax.ShapeDtypeStruct((tiles * MOE_TILE * Y_SLAB, LANES), F32),
        compiler_params=_params("arbitrary"),
        name="experts",
    )(e_lo, e_hi, n_used, xs, w_gate, w_up, w_down, w_gate, w_up, w_down)


def _collect_kernel(slot_ref, next_slot_ref, ys_ref, rc_ref, *rest, steps, residual):
    if residual:
        h_ref, mod_ref, o_ref, buf_ref, sem = rest
    else:
        o_ref, buf_ref, sem = rest
    i = pl.program_id(0)
    k = i % 2
    rt = MOE_ROWS

    def slab_copy(buf, g, u, slot):
        return pltpu.make_async_copy(
            ys_ref.at[pl.ds(pl.multiple_of(slot * Y_SLAB, Y_SLAB), Y_SLAB)],
            buf_ref.at[buf, g, :, pl.ds(u, 1), :], sem.at[buf])

    def start_all(slots, buf):
        _for_rows(rt, lambda g, u: slab_copy(buf, g, u, slots[0, 0, g * SUBLANES + u]).start(priority=u % 2))

    @pl.when(i == 0)
    def _():
        start_all(slot_ref, 0)

    @pl.when(i + 1 < steps)
    def _():
        start_all(next_slot_ref, 1 - k)

    pltpu.make_async_copy(buf_ref.at[k], buf_ref.at[k], sem.at[k]).wait()

    rc = rc_ref[...]
    w_lo = rc[:, 2:3]
    w_hi = rc[:, 3:4]
    for j in range(X_SLAB):
        cols = slice(j * LANES, (j + 1) * LANES)
        f = (w_lo * buf_ref[k, :, j, :, :].reshape(rt, LANES)
             + w_hi * buf_ref[k, :, X_SLAB + j, :, :].reshape(rt, LANES))
        if residual:
            o_ref[:, cols] = h_ref[:, cols] + mod_ref[5:6, cols] * f
        else:
            o_ref[:, cols] = f


def _collect_call(ys, slots, route_cols, residual=None):
    n = slots.shape[0]
    d = D_MODEL
    steps = n // MOE_ROWS
    slots3 = slots.reshape(steps, 1, MOE_ROWS)
    row = pl.BlockSpec((MOE_ROWS, d), lambda i: (i, 0))
    in_specs = [pl.BlockSpec((1, 1, MOE_ROWS), lambda i: (i, 0, 0), memory_space=pltpu.SMEM),
                pl.BlockSpec((1, 1, MOE_ROWS), lambda i: (jnp.minimum(i + 1, steps - 1), 0, 0),
                             memory_space=pltpu.SMEM),
                pl.BlockSpec(memory_space=pl.ANY),
                pl.BlockSpec((MOE_ROWS, LANES), lambda i: (i, 0))]
    args = [slots3, slots3, ys.reshape(ys.shape[0], 1, LANES), route_cols]
    if residual is not None:
        h, mod, tiles_per_seq, n_batch = residual
        in_specs += [row, _mod_spec(tiles_per_seq, n_batch)]
        args += [h, mod]
    return pl.pallas_call(
        functools.partial(_collect_kernel, steps=steps, residual=residual is not None),
        grid=(steps,),
        in_specs=in_specs,
        out_specs=row,
        out_shape=jax.ShapeDtypeStruct((n, d), F32),
        scratch_shapes=[pltpu.VMEM((2, MOE_ROWS // SUBLANES, Y_SLAB, SUBLANES, LANES), F32),
                        pltpu.SemaphoreType.DMA((2,))],
        compiler_params=_params("arbitrary"),
        name="moe_collect",
    )(*args)


def _moe(rows, route, route_cols, w_gate, w_up, w_down, layer, residual=None):
    n = rows.shape[0]
    tm = MOE_TILE
    assert n % MOE_ROWS == 0
    max_tiles = -(-n // tm) + N_CLASSES
    p = max_tiles * tm
    lo = route[0].astype(jnp.int32)
    hi = route[1].astype(jnp.int32)
    lo_in = lo % EXPERTS_PER_GROUP
    hi_in = hi % EXPERTS_PER_GROUP
    pair = lo_in * 3 - (lo_in * (lo_in - 1)) // 2 + (hi_in - lo_in - 1)
    cls = (lo // EXPERTS_PER_GROUP) * PAIRS_PER_GROUP + pair
    onehot = (cls[:, None] == jnp.arange(N_CLASSES, dtype=jnp.int32)[None, :]).astype(jnp.int32)
    counts = jnp.sum(onehot, axis=0)
    rank = jnp.sum(jnp.cumsum(onehot, axis=0) * onehot, axis=1) - 1
    tiles_c = (counts + tm - 1) // tm
    tile_end = jnp.cumsum(tiles_c)
    tile_start = tile_end - tiles_c
    n_used = tile_end[-1]
    slot_of_token = (tile_start[cls] * tm + rank).astype(jnp.int32)
    tile_ids = jnp.arange(max_tiles, dtype=jnp.int32)
    tile_cls = jnp.minimum(jnp.sum((tile_end[None, :] <= tile_ids[:, None]).astype(jnp.int32), axis=1),
                           N_CLASSES - 1)
    group0 = (tile_cls // PAIRS_PER_GROUP) * EXPERTS_PER_GROUP
    e_lo = group0 + jnp.asarray(PAIR_LO, jnp.int32)[tile_cls % PAIRS_PER_GROUP]
    e_hi = group0 + jnp.asarray(PAIR_HI, jnp.int32)[tile_cls % PAIRS_PER_GROUP]
    tails = jnp.stack([tiles_c, tile_end]).astype(jnp.int32)
    xs = _dispatch_call(rows, slot_of_token, tails, p)
    ys = _experts_call(xs, e_lo.astype(jnp.int32), e_hi.astype(jnp.int32),
                       n_used.reshape(1).astype(jnp.int32), w_gate, w_up, w_down, layer)
    return _collect_call(ys, slot_of_token, route_cols, residual)


def _rope_tables(seq):
    rows = seq // GRID_W
    row = jnp.repeat(jnp.arange(rows), GRID_W).astype(F32)
    col = jnp.tile(jnp.arange(GRID_W), rows).astype(F32)
    n_freq = HEAD_DIM // 4
    inv_freq = ROPE_BASE ** (-jnp.arange(n_freq, dtype=F32) / n_freq)
    ang_r = row[:, None] * inv_freq
    ang_c = col[:, None] * inv_freq
    cos = jnp.concatenate([jnp.cos(ang_r), jnp.cos(ang_r), jnp.cos(ang_c), jnp.cos(ang_c)], axis=1)
    sin = jnp.concatenate([-jnp.sin(ang_r), jnp.sin(ang_r), -jnp.sin(ang_c), jnp.sin(ang_c)], axis=1)
    cos = jnp.concatenate([jnp.tile(cos, (1, 2)), jnp.ones((ROW_TILE, LANES), F32)], axis=0)
    sin = jnp.concatenate([jnp.tile(sin, (1, 2)), jnp.zeros((ROW_TILE, LANES), F32)], axis=0)
    return cos, sin


def _block_diag_halves(w):
    per_half = LRU_HALF // LRU_BLOCK
    halves = LRU_WIDTH // LRU_HALF
    out = jnp.zeros((halves, 2, LRU_HALF, LRU_HALF), F32)
    for c in range(halves):
        for j in range(per_half):
            s = slice(j * LRU_BLOCK, (j + 1) * LRU_BLOCK)
            out = out.at[c, :, s, s].set(w[:, c * per_half + j])
    return out.astype(BF16)


def _gla_gate_weights(gate_up, gate_b):
    w = jnp.zeros((GLA_HEADS, 2, LANES, GLA_DK), F32)
    for d in range(2):
        wd = gate_up[d].reshape(GLA_LOWRANK, GLA_HEADS, GLA_DK).transpose(1, 0, 2)
        w = w.at[:, d, d * GLA_LOWRANK:(d + 1) * GLA_LOWRANK, :].set(wd)
    b = gate_b.reshape(2, GLA_HEADS, GLA_DK).transpose(1, 0, 2)
    w_hi = w.astype(BF16)
    w_lo = (w - w_hi.astype(F32)).astype(BF16)
    return jnp.concatenate([w_hi, w_lo], axis=-1), b


def kernel(x, c, ctx, c_ctx, router_w, router_bias, ada_w, ada_b, norm_mix_g, norm_ffn_g, moe_w_gate, moe_w_up, moe_w_down, ev_w_in, ev_w_out, ev_conv_w, ev_conv_b, ev_rg_w, ev_rg_b, ev_ig_w, ev_ig_b, ev_lambda, ev_q_norm_g, ev_k_norm_g, ev_sink, od_w_in, od_w_out, od_gate_up, od_gate_b, od_out_norm_g):
    n_batch, seq, d = x.shape
    ctx_len = ctx.shape[1]
    n_lat = n_batch * seq
    n_ctx = n_batch * ctx_len
    assert ada_w.shape[0] == 2 and d == D_MODEL
    assert seq % ROW_TILE == 0 and n_ctx % ROW_TILE == 0 and seq % ctx_len == 0
    assert ctx_len % LRU_CHUNK == 0 and seq % GLA_BLOCK == 0 and ctx_len % GLA_BLOCK == 0
    tiles_per_seq = seq // ROW_TILE
    n_lat_tiles = n_lat // ROW_TILE

    x_lat = x.reshape(n_lat, d)
    x_ctx = ctx.reshape(n_ctx, d)
    mod_rows = -(-(n_batch + 1) // SUBLANES) * SUBLANES
    cvec = jnp.zeros((mod_rows, d), F32).at[:n_batch].set(c).at[n_batch].set(c_ctx)
    mods = _ada_call(cvec, ada_w, ada_b).reshape(2, mod_rows, 6, d)
    mod_mix = [mods[l][:n_batch + 1, 0:6] for l in range(2)]
    rw_pad = jnp.pad(router_w, ((0, 0), (0, LANES - N_EXPERTS)))
    rw_hi = rw_pad.astype(BF16)
    rw_t = jnp.concatenate([rw_hi, (rw_pad - rw_hi.astype(F32)).astype(BF16)], axis=1)
    rb = router_bias.reshape(N_EXPERTS, 1)

    w_in0 = ev_w_in[0].astype(BF16)
    w_out0 = ev_w_out[0].astype(BF16)
    proj0 = _proj_call(x_lat, x_ctx, mod_mix[0], norm_mix_g[0:1], w_in0, tiles_per_seq, n_batch)
    cos_t, sin_t = _rope_tables(seq)
    seg = np.kron(np.eye(LANES // HEAD_DIM, dtype=np.float32), np.full((HEAD_DIM, HEAD_DIM), 1.0 / HEAD_DIM, np.float32))
    seg = jnp.asarray(np.concatenate([seg, seg], axis=0)).astype(BF16)
    qrot, kvrot = _qkprep_call(proj0, cos_t, sin_t, jnp.tile(ev_q_norm_g[0], 2)[None, :],
                               jnp.tile(ev_k_norm_g[0], 2)[None, :], seg,
                               tiles_per_seq, n_lat_tiles)
    a_lat, a_ctx = _rglru_call(proj0, ev_conv_w[0], ev_conv_b[0:1], _block_diag_halves(ev_rg_w[0]),
                               _block_diag_halves(ev_ig_w[0]), ev_rg_b[0], ev_ig_b[0], ev_lambda[0],
                               n_batch, seq, ctx_len)
    b_lat = _attn_call(qrot, kvrot, ev_sink[0], n_batch, seq, ctx_len)
    b_ctx = _ctx_attn_call(qrot, kvrot, ev_sink[0], n_batch, seq, ctx_len)
    h, rows, route, route_cols = _outproj_even_call(a_lat, a_ctx, b_lat, b_ctx, w_out0, x_lat, x_ctx, mod_mix[0],
                                                    norm_ffn_g[0:1], rw_t, rb, tiles_per_seq, n_batch)
    ffn = _moe(rows, route, route_cols, moe_w_gate, moe_w_up, moe_w_down, 0)

    w_in1 = jnp.pad(od_w_in[0], ((0, 0), (0, ODD_IN_PAD - ODD_IN))).astype(BF16)
    h, proj1 = _proj_res_call(h, ffn, mod_mix[0], mod_mix[1], norm_mix_g[1:2], w_in1, tiles_per_seq, n_batch)
    gate_w, gate_b = _gla_gate_weights(od_gate_up[0], od_gate_b[0])
    o = _gla_call(proj1, gate_w, gate_b, od_out_norm_g[0:1], n_batch, seq, ctx_len)
    h_lat, rows, route, route_cols = _outproj_odd_call(o, od_w_out[0].astype(BF16), h, mod_mix[1],
                                                       norm_ffn_g[1:2], rw_t, rb, tiles_per_seq, n_batch)
    out = _moe(rows, route, route_cols, moe_w_gate, moe_w_up, moe_w_down, 1,
               residual=(h_lat, mod_mix[1], seq // MOE_ROWS, n_batch))
    return out.reshape(n_batch, seq, d)
```

```python
import functools

import jax
import jax.numpy as jnp
import numpy as np
from jax import lax
from jax.experimental import pallas as pl
from jax.experimental.pallas import tpu as pltpu

F32 = jnp.float32
BF16 = jnp.bfloat16
HIGHEST = lax.Precision.HIGHEST
ACT_DTYPE = BF16

D_MODEL = 1024
GRID_W = 64
EPS = 1e-6
LRU_WIDTH = 512
LRU_BLOCK = 64
CONV_W = 4
LRU_C = 8.0
ATT_HEADS = 8
ATT_KV_HEADS = 2
HEAD_DIM = 64
WINDOW = 128
ATT_BLOCK = 128
ROPE_BASE = 10000.0
LOG2E = 1.4426950408889634
ATT_Q_WIDTH = ATT_HEADS * HEAD_DIM
ATT_KV_WIDTH = ATT_KV_HEADS * HEAD_DIM
EVEN_IN = 2 * LRU_WIDTH + ATT_Q_WIDTH + 2 * ATT_KV_WIDTH
GLA_HEADS = 4
GLA_DK = 128
GLA_DV = 256
GLA_LOWRANK = 16
GLA_TAU = 16.0
GLA_CHUNK = 64
GLA_BLOCK = 256
GLA_K_WIDTH = GLA_HEADS * GLA_DK
GLA_V_WIDTH = GLA_HEADS * GLA_DV
ODD_IN = 2 * GLA_K_WIDTH + 2 * GLA_V_WIDTH + 2 * GLA_LOWRANK
N_EXPERTS = 16
N_GROUPS = 4
EXPERTS_PER_GROUP = 4
D_EXPERT = 512
PAIRS_PER_GROUP = 6
N_CLASSES = N_GROUPS * PAIRS_PER_GROUP
PAIR_LO = (0, 0, 0, 1, 1, 2)
PAIR_HI = (1, 2, 3, 2, 3, 3)

LANES = 128
SUBLANES = 8
VMEM_LIMIT_BYTES = 56 * 1024 * 1024

ROW_TILE = 512
MOE_TILE = 256
MOE_ROWS = 512
LRU_CHUNK = 128
LRU_HALF = 256
ODD_IN_PAD = ((ODD_IN + LANES - 1) // LANES) * LANES


def _params(*sem):
    return pltpu.CompilerParams(dimension_semantics=sem, vmem_limit_bytes=VMEM_LIMIT_BYTES)


def _modulated_norm(x, g, shift, scale):
    y = x * lax.rsqrt(jnp.mean(x * x, axis=-1, keepdims=True) + EPS)
    return (y * g) * (1.0 + scale) + shift


def _ada_kernel(c_ref, w_ref, b_ref, o_ref):
    c = c_ref[...]
    s = c * jax.nn.sigmoid(c)
    o_ref[...] = jnp.dot(s, w_ref[...], preferred_element_type=F32) + b_ref[...]


def _ada_call(cvec, ada_w, ada_b):
    depth, d, n6 = ada_w.shape
    r = cvec.shape[0]
    tn = n6 // 4
    return pl.pallas_call(
        _ada_kernel,
        grid=(depth, n6 // tn),
        in_specs=[pl.BlockSpec((r, d), lambda l, j: (0, 0)),
                  pl.BlockSpec((None, d, tn), lambda l, j: (l, 0, j)),
                  pl.BlockSpec((None, 1, tn), lambda l, j: (l, 0, j))],
        out_specs=pl.BlockSpec((None, r, tn), lambda l, j: (l, 0, j)),
        out_shape=jax.ShapeDtypeStruct((depth, r, n6), F32),
        compiler_params=_params("arbitrary", "arbitrary"),
        name="adaln",
    )(cvec, ada_w, ada_b.reshape(depth, 1, n6))


def _two_source_specs(width, n_lat_tiles):
    lat = pl.BlockSpec((ROW_TILE, width), lambda i: (jnp.minimum(i, n_lat_tiles - 1), 0))
    ctx = pl.BlockSpec((ROW_TILE, width), lambda i: (jnp.maximum(i - n_lat_tiles, 0), 0))
    return lat, ctx


def _two_source(lat_ref, ctx_ref, n_lat_tiles):
    return jnp.where(pl.program_id(0) >= n_lat_tiles, ctx_ref[...], lat_ref[...])


def _proj_even_kernel(xl_ref, xc_ref, mod_ref, g_ref, w_ref, c_ref, s_ref, qg_ref, kg_ref, m_ref,
                      o_ref, qo_ref, kvo_ref, *, n_lat_tiles):
    x = _two_source(xl_ref, xc_ref, n_lat_tiles)
    n = _modulated_norm(x, g_ref[...], mod_ref[0:1, :], mod_ref[1:2, :])
    proj = jnp.dot(n.astype(BF16), w_ref[...], preferred_element_type=F32)
    lru = 2 * LRU_WIDTH
    o_ref[...] = proj[:, 0:lru].astype(o_ref.dtype)
    k0 = lru + ATT_Q_WIDTH
    _qk_prepare(proj[:, lru:k0], proj[:, k0:k0 + ATT_KV_WIDTH], proj[:, k0 + ATT_KV_WIDTH:k0 + 2 * ATT_KV_WIDTH],
                c_ref[...], s_ref[...], qg_ref[...], kg_ref[...], m_ref[...], qo_ref, kvo_ref)


def _proj_res_kernel(h_ref, f_ref, pmod_ref, mod_ref, g_ref, w_ref, x_ref, o_ref):
    x = h_ref[...] + pmod_ref[5:6, :] * f_ref[...]
    x_ref[...] = x
    n = _modulated_norm(x, g_ref[...], mod_ref[0:1, :], mod_ref[1:2, :])
    o_ref[...] = jnp.dot(n.astype(BF16), w_ref[...], preferred_element_type=F32).astype(o_ref.dtype)


def _mod_spec(tiles_per_seq, n_batch):
    return pl.BlockSpec((None, 6, D_MODEL), lambda i: (jnp.minimum(i // tiles_per_seq, n_batch), 0, 0))


def _proj_even_call(x_lat, x_ctx, mod, g, w, cos_t, sin_t, qg, kg, seg_mean, tiles_per_seq, n_batch):
    d = x_lat.shape[1]
    n = x_lat.shape[0] + x_ctx.shape[0]
    n_out = w.shape[1]
    n_lat_tiles = x_lat.shape[0] // ROW_TILE
    lat, ctx = _two_source_specs(d, n_lat_tiles)

    def pos(i):
        return (jnp.where(i < n_lat_tiles, i % tiles_per_seq, tiles_per_seq), 0)

    const = lambda shape: pl.BlockSpec(shape, lambda i: (0, 0))
    row = lambda width: pl.BlockSpec((ROW_TILE, width), lambda i: (i, 0))
    return pl.pallas_call(
        functools.partial(_proj_even_kernel, n_lat_tiles=n_lat_tiles),
        grid=(n // ROW_TILE,),
        in_specs=[lat, ctx,
                  _mod_spec(tiles_per_seq, n_batch),
                  const((1, d)), const((d, n_out)),
                  pl.BlockSpec((ROW_TILE, LANES), pos), pl.BlockSpec((ROW_TILE, LANES), pos),
                  const((1, LANES)), const((1, LANES)), const((2 * LANES, LANES))],
        out_specs=[row(2 * LRU_WIDTH), row(ATT_Q_WIDTH), row(4 * LANES)],
        out_shape=[jax.ShapeDtypeStruct((n, 2 * LRU_WIDTH), ACT_DTYPE),
                   jax.ShapeDtypeStruct((n, ATT_Q_WIDTH), BF16),
                   jax.ShapeDtypeStruct((n, 4 * LANES), BF16)],
        compiler_params=_params("arbitrary"),
        name="in_proj",
    )(x_lat, x_ctx, mod, g, w, cos_t, sin_t, qg, kg, seg_mean)


def _proj_res_call(h, ffn, pmod, mod, g, w, tiles_per_seq, n_batch):
    n, d = h.shape
    n_out = w.shape[1]
    row = pl.BlockSpec((ROW_TILE, d), lambda i: (i, 0))
    return pl.pallas_call(
        _proj_res_kernel,
        grid=(n // ROW_TILE,),
        in_specs=[row, row, _mod_spec(tiles_per_seq, n_batch), _mod_spec(tiles_per_seq, n_batch),
                  pl.BlockSpec((1, d), lambda i: (0, 0)),
                  pl.BlockSpec((d, n_out), lambda i: (0, 0))],
        out_specs=[row, pl.BlockSpec((ROW_TILE, n_out), lambda i: (i, 0))],
        out_shape=[jax.ShapeDtypeStruct((n, d), F32), jax.ShapeDtypeStruct((n, n_out), ACT_DTYPE)],
        compiler_params=_params("arbitrary"),
        name="in_proj_res",
    )(h, ffn, pmod, mod, g, w)


def _qk_prepare(q, k, v, cos, sin, qg, kg, seg_mean, qo_ref, kvo_ref):
    lane = lax.broadcasted_iota(jnp.int32, cos.shape, 1)
    first = (lane % 32) < 16
    low = lane < HEAD_DIM

    def norm_rope(x, g):
        ms = jnp.dot(jnp.concatenate(_split2(x * x), axis=1), seg_mean, preferred_element_type=F32)
        xn = (x * lax.rsqrt(ms + EPS)) * g
        partner = jnp.where(first, pltpu.roll(xn, LANES - 16, 1), pltpu.roll(xn, 16, 1))
        return xn * cos + partner * sin

    for j in range(ATT_Q_WIDTH // LANES):
        qj = norm_rope(q[:, j * LANES:(j + 1) * LANES], qg)
        qo_ref[:, j * LANES:(j + 1) * LANES] = (qj * (HEAD_DIM ** -0.5 * LOG2E)).astype(BF16)
    kr = norm_rope(k, kg)
    ks = pltpu.roll(kr, HEAD_DIM, 1)
    kvo_ref[:, 0:LANES] = jnp.where(low, kr, ks).astype(BF16)
    kvo_ref[:, LANES:2 * LANES] = jnp.where(low, ks, kr).astype(BF16)
    kvo_ref[:, 2 * LANES:3 * LANES] = v.astype(BF16)
    kvo_ref[:, 3 * LANES:4 * LANES] = jnp.ones((cos.shape[0], LANES), BF16)


def _attend(q, kv, sink_ref, bias):
    rows = q.shape[0]
    lane = lax.broadcasted_iota(jnp.int32, (rows, LANES), 1)
    low = lane < HEAD_DIM
    zero = jnp.zeros((rows, LANES), BF16)
    vv = kv[:, 2 * LANES:4 * LANES]
    groups = []
    heads_per_kv = ATT_HEADS // ATT_KV_HEADS
    for kh in range(ATT_KV_HEADS):
        kk = kv[:, kh * LANES:(kh + 1) * LANES]
        parts = []
        sinks = []
        for j in range(heads_per_kv):
            h = kh * heads_per_kv + j
            qg = q[:, (h // 2) * LANES:(h // 2 + 1) * LANES]
            parts.append(jnp.where(low, qg, zero) if h % 2 == 0 else jnp.where(low, zero, qg))
            sinks.append(jnp.full((rows, 1), sink_ref[h] * LOG2E, F32))
        lhs = jnp.concatenate(parts, axis=0)
        sink = jnp.concatenate(sinks, axis=0)
        s = lax.dot_general(lhs, kk, (((1,), (1,)), ((), ())), preferred_element_type=F32)
        if bias is not None:
            s = s + bias
        m = jnp.maximum(jnp.max(s, axis=-1, keepdims=True), sink)
        e = jnp.exp2(s - m)
        ov = jnp.dot(e.astype(BF16), vv, preferred_element_type=F32)
        o = ov[:, 0:LANES] / (ov[:, LANES:2 * LANES] + jnp.exp2(sink - m))
        r = [o[j * rows:(j + 1) * rows] for j in range(heads_per_kv)]
        for p in range(heads_per_kv // 2):
            even, odd = r[2 * p], r[2 * p + 1]
            if kh == 0:
                groups.append(jnp.where(low, even, pltpu.roll(odd, HEAD_DIM, 1)))
            else:
                groups.append(jnp.where(low, pltpu.roll(even, HEAD_DIM, 1), odd))
    return jnp.concatenate(groups, axis=1)


def _attn_kernel(sink_ref, q_ref, kvp_ref, kvo_ref, kvn_ref, kvc_ref, bias0_ref, bias1_ref, o_ref):
    blk = ATT_BLOCK
    own = kvo_ref[...]
    ctx = kvc_ref[...]
    kv0 = jnp.concatenate([kvp_ref[...], own, ctx], axis=0)
    kv1 = jnp.concatenate([own, kvn_ref[...], ctx], axis=0)
    o_ref[0:blk, :] = _attend(q_ref[0:blk, :], kv0, sink_ref, bias0_ref[...]).astype(o_ref.dtype)
    o_ref[blk:2 * blk, :] = _attend(q_ref[blk:2 * blk, :], kv1, sink_ref, bias1_ref[...]).astype(o_ref.dtype)


def _window_bias(ctx_len):
    blk = ATT_BLOCK
    stacked = (ATT_HEADS // ATT_KV_HEADS) * blk
    qi = lax.broadcasted_iota(jnp.int32, (4, stacked, 3 * blk + ctx_len), 1) % blk
    ks = lax.broadcasted_iota(jnp.int32, (4, stacked, 3 * blk + ctx_len), 2)
    var = lax.broadcasted_iota(jnp.int32, (4, stacked, 3 * blk + ctx_len), 0)
    in_win = jnp.abs(ks - blk - qi) <= WINDOW
    in_seq = ((ks >= blk) | (var % 2 == 1)) & ((ks < 2 * blk) | (var >= 2))
    keep = (in_win & in_seq) | (ks >= 3 * blk)
    return jnp.where(keep, 0.0, -1e30).astype(F32)


def _ctx_attn_kernel(sink_ref, q_ref, kvc_ref, o_ref):
    o_ref[...] = _attend(q_ref[...], kvc_ref[...], sink_ref, None).astype(o_ref.dtype)


def _attn_call(qrot, kvrot, sink, n_batch, seq, ctx_len):
    nb = seq // ATT_BLOCK
    ctx0 = (n_batch * seq) // ctx_len
    kvw = kvrot.shape[1]
    bias = _window_bias(ctx_len)

    assert nb % 2 == 0
    pairs = nb // 2

    def variant0(b, m):
        return ((m > 0).astype(jnp.int32) + 2, 0, 0)

    def variant1(b, m):
        return (1 + 2 * (m < pairs - 1).astype(jnp.int32), 0, 0)

    blk = ATT_BLOCK
    return pl.pallas_call(
        _attn_kernel,
        grid=(n_batch, pairs),
        in_specs=[pl.BlockSpec(memory_space=pltpu.SMEM),
                  pl.BlockSpec((2 * blk, ATT_Q_WIDTH), lambda b, m: (b * pairs + m, 0)),
                  pl.BlockSpec((blk, kvw), lambda b, m: (b * nb + jnp.maximum(2 * m - 1, 0), 0)),
                  pl.BlockSpec((2 * blk, kvw), lambda b, m: (b * pairs + m, 0)),
                  pl.BlockSpec((blk, kvw), lambda b, m: (b * nb + jnp.minimum(2 * m + 2, nb - 1), 0)),
                  pl.BlockSpec((ctx_len, kvw), lambda b, m: (ctx0 + b, 0)),
                  pl.BlockSpec((None,) + bias.shape[1:], variant0),
                  pl.BlockSpec((None,) + bias.shape[1:], variant1)],
        out_specs=pl.BlockSpec((2 * blk, ATT_Q_WIDTH), lambda b, m: (b * pairs + m, 0)),
        out_shape=jax.ShapeDtypeStruct((n_batch * seq, ATT_Q_WIDTH), ACT_DTYPE),
        compiler_params=_params("arbitrary", "arbitrary"),
        name="window_attn",
    )(sink, qrot, kvrot, kvrot, kvrot, kvrot, bias, bias)


def _ctx_attn_call(qrot, kvrot, sink, n_batch, seq, ctx_len):
    ctx0 = (n_batch * seq) // ctx_len
    kvw = kvrot.shape[1]
    return pl.pallas_call(
        _ctx_attn_kernel,
        grid=(n_batch,),
        in_specs=[pl.BlockSpec(memory_space=pltpu.SMEM),
                  pl.BlockSpec((ctx_len, ATT_Q_WIDTH), lambda b: (ctx0 + b, 0)),
                  pl.BlockSpec((ctx_len, kvw), lambda b: (ctx0 + b, 0))],
        out_specs=pl.BlockSpec((ctx_len, ATT_Q_WIDTH), lambda b: (b, 0)),
        out_shape=jax.ShapeDtypeStruct((n_batch * ctx_len, ATT_Q_WIDTH), ACT_DTYPE),
        compiler_params=_params("arbitrary"),
        name="ctx_attn",
    )(sink, qrot, kvrot)


def _scan_rows(a, x, carry, reverse):
    rows, width = a.shape
    n_groups = rows // SUBLANES
    a = a.reshape(n_groups, SUBLANES, width)
    x = x.reshape(n_groups, SUBLANES, width)
    sub = lax.broadcasted_iota(jnp.int32, a.shape, 1)
    for s in (1, 2, 4):
        shift = SUBLANES - s if reverse else s
        a_sh = pltpu.roll(a, shift, 1)
        x_sh = pltpu.roll(x, shift, 1)
        keep = (sub <= SUBLANES - 1 - s) if reverse else (sub >= s)
        x = x + a * jnp.where(keep, x_sh, 0.0)
        a = a * jnp.where(keep, a_sh, 1.0)
    a = a.reshape(rows, width)
    x = x.reshape(rows, width)
    out = [None] * n_groups
    order = range(n_groups - 1, -1, -1) if reverse else range(n_groups)
    for g in order:
        hg = x[g * SUBLANES:(g + 1) * SUBLANES] + a[g * SUBLANES:(g + 1) * SUBLANES] * carry
        carry = hg[0:1] if reverse else hg[SUBLANES - 1:SUBLANES]
        out[g] = hg
    return jnp.concatenate(out, axis=0), carry


def _rglru_kernel(ul_ref, gl_ref, uc_ref, gc_ref, cw_ref, cb_ref, wr_ref, wi_ref, br_ref, bi_ref, lam_ref,
                  yl_ref, yc_ref, upad_ref, conv_ref, *, seq, ctx_len):
    width = ul_ref.shape[1]
    tc = LRU_CHUNK
    neg_lam = -lam_ref[...]
    softplus = jnp.maximum(neg_lam, 0.0) + jnp.log1p(jnp.exp(-jnp.abs(neg_lam)))
    rate = (-LRU_C * LOG2E) * softplus
    zeros = jnp.zeros((SUBLANES, width), F32)

    def gates(u, d):
        ub = u.astype(BF16)
        r = jax.nn.sigmoid(jnp.dot(ub, wr_ref[d], preferred_element_type=F32) + br_ref[d:d + 1, :])
        i = jax.nn.sigmoid(jnp.dot(ub, wi_ref[d], preferred_element_type=F32) + bi_ref[d:d + 1, :])
        a = jnp.exp2(r * rate[d:d + 1, :])
        y = 1.0 - a * a
        root = jnp.where(y > 0.0, y * lax.rsqrt(y), 0.0)
        return a, root * (i * u)

    def run(u_ref, g_ref, y_ref, n, h_fwd, h_bwd):
        n_chunks = n // tc
        upad_ref[0:SUBLANES, :] = zeros
        upad_ref[pl.ds(SUBLANES, n), :] = u_ref[...].astype(F32)
        upad_ref[pl.ds(SUBLANES + n, SUBLANES), :] = zeros

        def conv_body(c, carry):
            t0 = pl.multiple_of(c * tc, tc)
            win = upad_ref[pl.ds(t0, tc + 2 * SUBLANES), :]
            acc = cb_ref[...]
            for k in range(CONV_W):
                acc = acc + cw_ref[k:k + 1, :] * win[SUBLANES - 1 + k:SUBLANES - 1 + k + tc]
            conv_ref[pl.ds(t0, tc), :] = acc
            return carry

        lax.fori_loop(0, n_chunks, conv_body, 0)

        def fwd_body(c, carry):
            t0 = pl.multiple_of(c * tc, tc)
            a, x = gates(conv_ref[pl.ds(t0, tc), :], 0)
            h, carry = _scan_rows(a, x, carry, False)
            upad_ref[pl.ds(pl.multiple_of(t0 + SUBLANES, SUBLANES), tc), :] = h
            return carry

        h_fwd = lax.fori_loop(0, n_chunks, fwd_body, h_fwd, unroll=min(4, n_chunks))

        def bwd_body(c, carry):
            t0 = pl.multiple_of((n_chunks - 1 - c) * tc, tc)
            a, x = gates(conv_ref[pl.ds(t0, tc), :], 1)
            h, carry = _scan_rows(a, x, carry, True)
            h_both = upad_ref[pl.ds(pl.multiple_of(t0 + SUBLANES, SUBLANES), tc), :] + h
            y_ref[pl.ds(t0, tc), :] = (h_both * jax.nn.gelu(g_ref[pl.ds(t0, tc), :].astype(F32))).astype(y_ref.dtype)
            return carry

        h_bwd = lax.fori_loop(0, n_chunks, bwd_body, h_bwd, unroll=min(4, n_chunks))
        return h_fwd, h_bwd

    h0 = jnp.zeros((1, width), F32)
    h_fwd, h_bwd = run(uc_ref, gc_ref, yc_ref, ctx_len, h0, h0)
    run(ul_ref, gl_ref, yl_ref, seq, h_fwd, h_bwd)


def _rglru_call(proj, conv_w, conv_b, wr, wi, br, bi, lam, n_batch, seq, ctx_len):
    ctx0 = (n_batch * seq) // ctx_len
    halves = LRU_WIDTH // LRU_HALF
    w = LRU_HALF
    vec2 = pl.BlockSpec((2, w), lambda b, c: (0, c))
    return pl.pallas_call(
        functools.partial(_rglru_kernel, seq=seq, ctx_len=ctx_len),
        grid=(n_batch, halves),
        in_specs=[pl.BlockSpec((seq, w), lambda b, c: (b, c)),
                  pl.BlockSpec((seq, w), lambda b, c: (b, halves + c)),
                  pl.BlockSpec((ctx_len, w), lambda b, c: (ctx0 + b, c)),
                  pl.BlockSpec((ctx_len, w), lambda b, c: (ctx0 + b, halves + c)),
                  pl.BlockSpec((CONV_W, w), lambda b, c: (0, c)),
                  pl.BlockSpec((1, w), lambda b, c: (0, c)),
                  pl.BlockSpec((None, 2, w, w), lambda b, c: (c, 0, 0, 0)),
                  pl.BlockSpec((None, 2, w, w), lambda b, c: (c, 0, 0, 0)),
                  vec2, vec2, vec2],
        out_specs=[pl.BlockSpec((seq, w), lambda b, c: (b, c)),
                   pl.BlockSpec((ctx_len, w), lambda b, c: (b, c))],
        out_shape=[jax.ShapeDtypeStruct((n_batch * seq, LRU_WIDTH), ACT_DTYPE),
                   jax.ShapeDtypeStruct((n_batch * ctx_len, LRU_WIDTH), ACT_DTYPE)],
        scratch_shapes=[pltpu.VMEM((seq + 2 * SUBLANES, w), F32), pltpu.VMEM((seq, w), F32)],
        compiler_params=_params("arbitrary", "arbitrary"),
        name="rglru",
    )(proj, proj, proj, proj, conv_w, conv_b, wr, wi, br, bi, lam)


def _split2(x):
    hi = x.astype(BF16)
    return hi, (x - hi.astype(F32)).astype(BF16)


def _gla_kernel(q_ref, k_ref, v_ref, r_ref, a_ref, kc_ref, vc_ref, ac_ref, gw_ref, gb_ref, ng_ref,
                o_ref, of_ref, ob_ref, qd_ref, ki_ref, ke_ref, la_ref, eg_ref, kv_ref, st_ref, *, seq, ctx_len):
    ch = GLA_CHUNK
    blk = GLA_BLOCK
    per_blk = blk // ch
    chunk_of_row = lax.broadcasted_iota(jnp.int32, (blk, GLA_DK), 0) // ch
    row = lax.broadcasted_iota(jnp.int32, (blk, blk), 0)
    col = lax.broadcasted_iota(jnp.int32, (blk, blk), 1)
    same = (row // ch) == (col // ch)
    keep = (same & (row >= col), same & (row <= col))
    tri = (keep[0].astype(BF16),)
    scale = GLA_DK ** -0.5
    nt = (((1,), (1,)), ((), ()))
    tn = (((0,), (0,)), ((), ()))

    def mm(a, b):
        return jnp.dot(a, b, preferred_element_type=F32)

    def prepare(a_src, k_src, v_src, q_src, dst, n_rows, d):
        n_blk = n_rows // blk
        unroll = min(4, n_blk)

        def decay_body(i, carry):
            t0 = pl.multiple_of(i * blk, blk)
            both = mm(a_src[pl.ds(t0, blk), :], gw_ref[d])
            pre = (both[:, 0:GLA_DK] + both[:, GLA_DK:2 * GLA_DK]) + gb_ref[d:d + 1, :]
            la_ref[pl.ds(t0, blk), :] = (jnp.minimum(pre, 0.0) - jnp.log(1.0 + jnp.exp(-jnp.abs(pre)))) / GLA_TAU
            return carry

        lax.fori_loop(0, n_blk, decay_body, 0, unroll=min(8, n_blk))

        def scale_body(i, carry):
            t0 = pl.multiple_of(i * blk, blk)
            log_a = la_ref[pl.ds(t0, blk), :]
            sums = mm(tri[0], jnp.concatenate(_split2(log_a), axis=1))
            prefix = sums[:, 0:GLA_DK] + sums[:, GLA_DK:2 * GLA_DK]
            total = jnp.concatenate(
                [jnp.broadcast_to(prefix[(j + 1) * ch - 1:(j + 1) * ch], (ch, GLA_DK)) for j in range(per_blk)],
                axis=0)
            suffix = total - prefix + log_a
            b = prefix if d == 0 else suffix
            to_end = (suffix if d == 0 else prefix) - log_a
            for j in range(per_blk):
                eg_ref[pl.ds(i * per_blk + j, 1), :] = jnp.exp(total[j * ch:j * ch + 1])
            k_blk = k_src[pl.ds(t0, blk), :].astype(F32)
            ke_ref[pl.ds(t0, blk), :] = (k_blk * jnp.exp(to_end)).astype(BF16)
            if q_src is not None:
                q_blk = q_src[pl.ds(t0, blk), :].astype(F32)
                qd_ref[pl.ds(t0, blk), :] = ((q_blk * scale) * jnp.exp(b)).astype(BF16)
                ki_ref[pl.ds(t0, blk), :] = (k_blk * jnp.exp(-b)).astype(BF16)
            return carry

        lax.fori_loop(0, n_blk, scale_body, 0, unroll=unroll)

        def intra_body(i, carry):
            t0 = pl.multiple_of(i * blk, blk)
            v_blk = v_src[pl.ds(t0, blk), :].astype(BF16)
            k_end = ke_ref[pl.ds(t0, blk), :]
            spread = jnp.concatenate([jnp.where(chunk_of_row == j, k_end, jnp.zeros_like(k_end))
                                      for j in range(per_blk)], axis=1)
            kv = lax.dot_general(v_blk, spread, tn, preferred_element_type=F32)
            for j in range(per_blk):
                kv_ref[i * per_blk + j] = kv[:, j * GLA_DK:(j + 1) * GLA_DK]
            if q_src is not None:
                s = lax.dot_general(qd_ref[pl.ds(t0, blk), :], ki_ref[pl.ds(t0, blk), :], nt,
                                    preferred_element_type=F32)
                p = jnp.where(keep[d], s, 0.0).astype(BF16)
                dst[pl.ds(t0, blk), :] = mm(p, v_blk)
            return carry

        lax.fori_loop(0, n_blk, intra_body, 0, unroll=min(8, n_blk))

    def recur(dst, n_rows, d):
        n_chunks = n_rows // ch

        def body(c, carry):
            n = c if d == 0 else n_chunks - 1 - c
            t0 = pl.multiple_of(n * ch, ch)
            st = st_ref[...]
            if dst is not None:
                dst[pl.ds(t0, ch), :] = dst[pl.ds(t0, ch), :] + lax.dot_general(
                    qd_ref[pl.ds(t0, ch), :], st.astype(BF16), nt, preferred_element_type=F32)
            st_ref[...] = st * eg_ref[pl.ds(n, 1), :] + kv_ref[n]
            return carry

        lax.fori_loop(0, n_chunks, body, 0, unroll=min(16, n_chunks))

    for d, dst in ((0, of_ref), (1, ob_ref)):
        st_ref[...] = jnp.zeros(st_ref.shape, F32)
        prepare(ac_ref, kc_ref, vc_ref, None, None, ctx_len, d)
        recur(None, ctx_len, d)
        prepare(a_ref, k_ref, v_ref, q_ref, dst, seq, d)
        recur(dst, seq, d)

    def finish_body(c, carry):
        t0 = pl.multiple_of(c * blk, blk)
        o = of_ref[pl.ds(t0, blk), :] + ob_ref[pl.ds(t0, blk), :]
        on = (o * lax.rsqrt(jnp.mean(o * o, axis=-1, keepdims=True) + EPS)) * ng_ref[...]
        r = r_ref[pl.ds(t0, blk), :].astype(F32)
        o_ref[pl.ds(t0, blk), :] = (on * (r * jax.nn.sigmoid(r))).astype(o_ref.dtype)
        return carry

    lax.fori_loop(0, seq // blk, finish_body, 0)


def _gla_call(proj, gate_w, gate_b, norm_g, n_batch, seq, ctx_len):
    ctx0 = (n_batch * seq) // ctx_len
    kcol = GLA_K_WIDTH // GLA_DK
    vcol = (2 * GLA_K_WIDTH) // GLA_DV
    rcol = (2 * GLA_K_WIDTH + GLA_V_WIDTH) // GLA_DV
    acol = (2 * GLA_K_WIDTH + 2 * GLA_V_WIDTH) // LANES
    return pl.pallas_call(
        functools.partial(_gla_kernel, seq=seq, ctx_len=ctx_len),
        grid=(n_batch, GLA_HEADS),
        in_specs=[pl.BlockSpec((seq, GLA_DK), lambda b, h: (b, h)),
                  pl.BlockSpec((seq, GLA_DK), lambda b, h: (b, kcol + h)),
                  pl.BlockSpec((seq, GLA_DV), lambda b, h: (b, vcol + h)),
                  pl.BlockSpec((seq, GLA_DV), lambda b, h: (b, rcol + h)),
                  pl.BlockSpec((seq, LANES), lambda b, h: (b, acol)),
                  pl.BlockSpec((ctx_len, GLA_DK), lambda b, h: (ctx0 + b, kcol + h)),
                  pl.BlockSpec((ctx_len, GLA_DV), lambda b, h: (ctx0 + b, vcol + h)),
                  pl.BlockSpec((ctx_len, LANES), lambda b, h: (ctx0 + b, acol)),
                  pl.BlockSpec((None, 2, LANES, 2 * GLA_DK), lambda b, h: (h, 0, 0, 0)),
                  pl.BlockSpec((None, 2, GLA_DK), lambda b, h: (h, 0, 0)),
                  pl.BlockSpec((1, GLA_DV), lambda b, h: (0, 0))],
        out_specs=pl.BlockSpec((seq, GLA_DV), lambda b, h: (b, h)),
        out_shape=jax.ShapeDtypeStruct((n_batch * seq, GLA_V_WIDTH), ACT_DTYPE),
        scratch_shapes=[pltpu.VMEM((seq, GLA_DV), F32),
                        pltpu.VMEM((seq, GLA_DV), F32),
                        pltpu.VMEM((seq, GLA_DK), BF16),
                        pltpu.VMEM((seq, GLA_DK), BF16),
                        pltpu.VMEM((seq, GLA_DK), BF16),
                        pltpu.VMEM((seq, GLA_DK), F32),
                        pltpu.VMEM((seq // GLA_CHUNK, GLA_DK), F32),
                        pltpu.VMEM((seq // GLA_CHUNK, GLA_DV, GLA_DK), F32),
                        pltpu.VMEM((GLA_DV, GLA_DK), F32)],
        compiler_params=_params("arbitrary", "arbitrary"),
        name="gla",
    )(proj, proj, proj, proj, proj, proj, proj, proj, gate_w, gate_b, norm_g)


def _route(rows, rows_bf16, rw_ref, rb_ref):
    n_rows = rows.shape[0]
    rows_lo = (rows - rows_bf16.astype(F32)).astype(BF16)
    both = jnp.dot(rows_bf16, rw_ref[...], preferred_element_type=F32)
    logits_rows = (both[:, 0:LANES] + both[:, LANES:2 * LANES]
                   + jnp.dot(rows_lo, rw_ref[:, 0:LANES], preferred_element_type=F32))
    logits = logits_rows.T[0:N_EXPERTS, :]
    scores = jax.nn.sigmoid(logits)
    sel = scores + rb_ref[...]
    eid = lax.broadcasted_iota(jnp.int32, (N_EXPERTS, n_rows), 0)
    gid = eid // EXPERTS_PER_GROUP
    neg = -jnp.inf
    big = N_EXPERTS

    def top2(masked):
        m1 = jnp.max(masked, axis=0, keepdims=True)
        i1 = jnp.min(jnp.where(masked == m1, eid, big), axis=0, keepdims=True)
        rest = jnp.where(eid == i1, neg, masked)
        m2 = jnp.max(rest, axis=0, keepdims=True)
        i2 = jnp.min(jnp.where(rest == m2, eid, big), axis=0, keepdims=True)
        return m1, i1, m2, i2

    best = None
    best_g = None
    for g in range(N_GROUPS):
        m1, _, m2, _ = top2(jnp.where(gid == g, sel, neg))
        gs = m1 + m2
        if best is None:
            best, best_g = gs, jnp.zeros((1, n_rows), jnp.int32)
        else:
            better = gs > best
            best_g = jnp.where(better, g, best_g)
            best = jnp.where(better, gs, best)
    _, i1, _, i2 = top2(jnp.where(gid == best_g, sel, neg))
    s1 = jnp.sum(jnp.where(eid == i1, scores, 0.0), axis=0, keepdims=True)
    s2 = jnp.sum(jnp.where(eid == i2, scores, 0.0), axis=0, keepdims=True)
    total = s1 + s2
    swap = i2 < i1
    lo = jnp.where(swap, i2, i1)
    hi = jnp.where(swap, i1, i2)
    w_lo = jnp.where(swap, s2, s1) / total
    w_hi = jnp.where(swap, s1, s2) / total
    pad = jnp.zeros((4, n_rows), F32)
    return jnp.concatenate([lo.astype(F32), hi.astype(F32), w_lo, w_hi, pad], axis=0)


def _finish_rows(y, h, mod_ref, g_ref, rw_ref, rb_ref, hn_ref, rows_ref, rt_ref, rc_ref):
    hn = h + mod_ref[2:3, :] * y
    hn_ref[...] = hn
    rows = _modulated_norm(hn, g_ref[...], mod_ref[3:4, :], mod_ref[4:5, :])
    rows_bf16 = rows.astype(BF16)
    rows_ref[...] = rows_bf16
    route = _route(rows, rows_bf16, rw_ref, rb_ref)
    rt_ref[...] = route
    padded = jnp.concatenate([route, jnp.zeros((LANES - SUBLANES, route.shape[1]), F32)], axis=0)
    rc_ref[...] = padded.T


def _outproj_even_kernel(al_ref, ac_ref, bl_ref, bc_ref, w_ref, hl_ref, hc_ref, mod_ref, g_ref, rw_ref, rb_ref,
                         hn_ref, rows_ref, rt_ref, rc_ref, *, n_lat_tiles):
    a = _two_source(al_ref, ac_ref, n_lat_tiles).astype(BF16)
    b = _two_source(bl_ref, bc_ref, n_lat_tiles).astype(BF16)
    y = (jnp.dot(a, w_ref[0:LRU_WIDTH, :], preferred_element_type=F32)
         + jnp.dot(b, w_ref[LRU_WIDTH:LRU_WIDTH + ATT_Q_WIDTH, :], preferred_element_type=F32))
    h = _two_source(hl_ref, hc_ref, n_lat_tiles)
    _finish_rows(y, h, mod_ref, g_ref, rw_ref, rb_ref, hn_ref, rows_ref, rt_ref, rc_ref)


def _outproj_odd_kernel(o_ref, w_ref, h_ref, mod_ref, g_ref, rw_ref, rb_ref, hn_ref, rows_ref, rt_ref, rc_ref):
    y = jnp.dot(o_ref[...].astype(BF16), w_ref[...], preferred_element_type=F32)
    _finish_rows(y, h_ref[...], mod_ref, g_ref, rw_ref, rb_ref, hn_ref, rows_ref, rt_ref, rc_ref)


def _outproj_specs(n_rows, tiles_per_seq, n_batch):
    d = D_MODEL
    row = pl.BlockSpec((ROW_TILE, d), lambda i: (i, 0))
    tail_in = [_mod_spec(tiles_per_seq, n_batch),
               pl.BlockSpec((1, d), lambda i: (0, 0)),
               pl.BlockSpec((d, 2 * LANES), lambda i: (0, 0)),
               pl.BlockSpec((N_EXPERTS, 1), lambda i: (0, 0))]
    out_specs = [row, row, pl.BlockSpec((SUBLANES, ROW_TILE), lambda i: (0, i)),
                 pl.BlockSpec((ROW_TILE, LANES), lambda i: (i, 0))]
    out_shape = [jax.ShapeDtypeStruct((n_rows, d), F32), jax.ShapeDtypeStruct((n_rows, d), ACT_DTYPE),
                 jax.ShapeDtypeStruct((SUBLANES, n_rows), F32), jax.ShapeDtypeStruct((n_rows, LANES), F32)]
    return tail_in, out_specs, out_shape


def _outproj_even_call(a_lat, a_ctx, b_lat, b_ctx, w, h_lat, h_ctx, mod, g, rw_t, rb, tiles_per_seq, n_batch):
    n_rows = h_lat.shape[0] + h_ctx.shape[0]
    n_lat_tiles = a_lat.shape[0] // ROW_TILE
    tail_in, out_specs, out_shape = _outproj_specs(n_rows, tiles_per_seq, n_batch)
    lat, ctx = _two_source_specs(LRU_WIDTH, n_lat_tiles)
    h_specs = list(_two_source_specs(D_MODEL, n_lat_tiles))
    return pl.pallas_call(
        functools.partial(_outproj_even_kernel, n_lat_tiles=n_lat_tiles),
        grid=(n_rows // ROW_TILE,),
        in_specs=[lat, ctx, lat, ctx, pl.BlockSpec(w.shape, lambda i: (0, 0))] + h_specs + tail_in,
        out_specs=out_specs,
        out_shape=out_shape,
        compiler_params=_params("arbitrary"),
        name="out_proj_even",
    )(a_lat, a_ctx, b_lat, b_ctx, w, h_lat, h_ctx, mod, g, rw_t, rb)


def _outproj_odd_call(o, w, h, mod, g, rw_t, rb, tiles_per_seq, n_batch):
    n_rows = o.shape[0]
    tail_in, out_specs, out_shape = _outproj_specs(n_rows, tiles_per_seq, n_batch)
    return pl.pallas_call(
        _outproj_odd_kernel,
        grid=(n_rows // ROW_TILE,),
        in_specs=[pl.BlockSpec((ROW_TILE, o.shape[1]), lambda i: (i, 0)),
                  pl.BlockSpec(w.shape, lambda i: (0, 0)),
                  pl.BlockSpec((ROW_TILE, D_MODEL), lambda i: (i, 0))] + tail_in,
        out_specs=out_specs,
        out_shape=out_shape,
        compiler_params=_params("arbitrary"),
        name="out_proj_odd",
    )(o, w, h, mod, g, rw_t, rb)


X_SLAB = D_MODEL // LANES
Y_SLAB = 2 * X_SLAB


def _for_rows(n_rows, fn):
    for g in range(n_rows // SUBLANES):
        for u in range(SUBLANES):
            fn(g, u)


def _dispatch_kernel(tail_ref, slot_ref, rows_ref, xs_ref, slab_ref, zero_ref, sem, zero_sem, *, steps, n_tiles):
    i = pl.program_id(0)
    k = i % 2
    rt = MOE_ROWS
    tile_rows = MOE_TILE * X_SLAB

    def slab_copy(buf, g, u, slot):
        return pltpu.make_async_copy(
            slab_ref.at[buf, g, :, pl.ds(u, 1), :],
            xs_ref.at[pl.ds(pl.multiple_of(slot * X_SLAB, X_SLAB), X_SLAB)], sem.at[buf])

    def wait_all(buf):
        pltpu.make_async_copy(slab_ref.at[buf], slab_ref.at[buf], sem.at[buf]).wait()

    @pl.when(i == 0)
    def _():
        zero_ref[...] = jnp.zeros(zero_ref.shape, F32)

        def fill(t):
            start = pl.multiple_of(t * tile_rows, tile_rows)
            return pltpu.make_async_copy(zero_ref, xs_ref.at[pl.ds(start, tile_rows)], zero_sem)

        n_used = tail_ref[1, N_CLASSES - 1]
        for phase in ("start", "wait"):
            for c in range(N_CLASSES):
                for cond, t in ((tail_ref[0, c] > 0, tail_ref[1, c] - 1), (n_used + c < n_tiles, n_used + c)):
                    @pl.when(cond)
                    def _():
                        getattr(fill(t), phase)()

    @pl.when(i >= 2)
    def _():
        wait_all(k)

    for j in range(X_SLAB):
        lanes = rows_ref[:, j * LANES:(j + 1) * LANES].astype(F32)
        slab_ref[k, :, j, :, :] = lanes.reshape(rt // SUBLANES, SUBLANES, LANES)

    _for_rows(rt, lambda g, u: slab_copy(k, g, u, slot_ref[0, 0, g * SUBLANES + u]).start(priority=u % 2))

    @pl.when(i == steps - 1)
    def _():
        wait_all(k)
        if steps >= 2:
            wait_all(1 - k)


def _dispatch_call(rows, slots, tails, n_slots):
    n, d = rows.shape
    steps = n // MOE_ROWS
    grid_spec = pltpu.PrefetchScalarGridSpec(
        num_scalar_prefetch=1,
        grid=(steps,),
        in_specs=[pl.BlockSpec((1, 1, MOE_ROWS), lambda i, tails: (i, 0, 0), memory_space=pltpu.SMEM),
                  pl.BlockSpec((MOE_ROWS, d), lambda i, tails: (i, 0))],
        out_specs=pl.BlockSpec(memory_space=pl.ANY),
        scratch_shapes=[pltpu.VMEM((2, MOE_ROWS // SUBLANES, X_SLAB, SUBLANES, LANES), F32),
                        pltpu.VMEM((MOE_TILE * X_SLAB, 1, LANES), F32),
                        pltpu.SemaphoreType.DMA((2,)), pltpu.SemaphoreType.DMA(())],
    )
    xs = pl.pallas_call(
        functools.partial(_dispatch_kernel, steps=steps, n_tiles=n_slots // MOE_TILE),
        grid_spec=grid_spec,
        out_shape=jax.ShapeDtypeStruct((n_slots * X_SLAB, 1, LANES), F32),
        compiler_params=_params("arbitrary"),
        name="moe_dispatch",
    )(tails, slots.reshape(steps, 1, MOE_ROWS), rows)
    return xs.reshape(n_slots * X_SLAB, LANES)


def _experts_kernel(elo_ref, ehi_ref, used_ref, x_ref, wg0_ref, wu0_ref, wd0_ref,
                    wg1_ref, wu1_ref, wd1_ref, y_ref, cg0_ref, cu0_ref, cd0_ref, cg1_ref, cu1_ref, cd1_ref):
    t = pl.program_id(0)
    tm = MOE_TILE
    prev = jnp.maximum(t - 1, 0)
    sides = ((elo_ref, (wg0_ref, wu0_ref, wd0_ref), (cg0_ref, cu0_ref, cd0_ref)),
             (ehi_ref, (wg1_ref, wu1_ref, wd1_ref), (cg1_ref, cu1_ref, cd1_ref)))

    @pl.when(t < used_ref[0])
    def _():
        for ids, weights, cached in sides:
            @pl.when((t == 0) | (ids[t] != ids[prev]))
            def _():
                for w_ref, c_ref in zip(weights, cached):
                    c_ref[...] = w_ref[...].astype(BF16)

        x = jnp.concatenate([x_ref[pl.ds(j, tm, stride=X_SLAB), :] for j in range(X_SLAB)],
                            axis=1).astype(BF16)
        for e, (_, _, (cg_ref, cu_ref, cd_ref)) in enumerate(sides):
            gate = jnp.dot(x, cg_ref[...], preferred_element_type=F32)
            up = jnp.dot(x, cu_ref[...], preferred_element_type=F32)
            act = ((gate * jax.nn.sigmoid(gate)) * up).astype(BF16)
            f = jnp.dot(act, cd_ref[...], preferred_element_type=F32)
            for j in range(X_SLAB):
                y_ref[pl.ds(e * X_SLAB + j, tm, stride=Y_SLAB), :] = f[:, j * LANES:(j + 1) * LANES]

    @pl.when(t >= used_ref[0])
    def _():
        y_ref[...] = jnp.zeros_like(y_ref)


def _experts_call(xs, e_lo, e_hi, n_used, w_gate, w_up, w_down, layer):
    d = D_MODEL
    tiles = xs.shape[0] // (MOE_TILE * X_SLAB)

    def xmap(t, elo, ehi, used):
        return (jnp.maximum(jnp.minimum(t, used[0] - 1), 0), 0)

    def lo(t, elo, ehi, used):
        return (layer, elo[t], 0, 0)

    def hi(t, elo, ehi, used):
        return (layer, ehi[t], 0, 0)

    up_spec = lambda m: pl.BlockSpec((None, None, d, D_EXPERT), m)
    down_spec = lambda m: pl.BlockSpec((None, None, D_EXPERT, d), m)
    cached = [pltpu.VMEM((d, D_EXPERT), BF16), pltpu.VMEM((d, D_EXPERT), BF16), pltpu.VMEM((D_EXPERT, d), BF16)]
    grid_spec = pltpu.PrefetchScalarGridSpec(
        num_scalar_prefetch=3,
        grid=(tiles,),
        in_specs=[pl.BlockSpec((MOE_TILE * X_SLAB, LANES), xmap),
                  up_spec(lo), up_spec(lo), down_spec(lo),
                  up_spec(hi), up_spec(hi), down_spec(hi)],
        out_specs=pl.BlockSpec((MOE_TILE * Y_SLAB, LANES), lambda t, elo, ehi, used: (t, 0)),
        scratch_shapes=cached + cached,
    )
    return pl.pallas_call(
        _experts_kernel,
        grid_spec=grid_spec,
        out_shape=jax.ShapeDtypeStruct((tiles * MOE_TILE * Y_SLAB, LANES), F32),
        compiler_params=_params("arbitrary"),
        name="experts",
    )(e_lo, e_hi, n_used, xs, w_gate, w_up, w_down, w_gate, w_up, w_down)


def _collect_kernel(slot_ref, next_slot_ref, ys_ref, rc_ref, *rest, steps, residual):
    if residual:
        h_ref, mod_ref, o_ref, buf_ref, sem = rest
    else:
        o_ref, buf_ref, sem = rest
    i = pl.program_id(0)
    k = i % 2
    rt = MOE_ROWS

    def slab_copy(buf, g, u, slot):
        return pltpu.make_async_copy(
            ys_ref.at[pl.ds(pl.multiple_of(slot * Y_SLAB, Y_SLAB), Y_SLAB)],
            buf_ref.at[buf, g, :, pl.ds(u, 1), :], sem.at[buf])

    def start_all(slots, buf):
        _for_rows(rt, lambda g, u: slab_copy(buf, g, u, slots[0, 0, g * SUBLANES + u]).start(priority=u % 2))

    @pl.when(i == 0)
    def _():
        start_all(slot_ref, 0)

    @pl.when(i + 1 < steps)
    def _():
        start_all(next_slot_ref, 1 - k)

    pltpu.make_async_copy(buf_ref.at[k], buf_ref.at[k], sem.at[k]).wait()

    rc = rc_ref[...]
    w_lo = rc[:, 2:3]
    w_hi = rc[:, 3:4]
    for j in range(X_SLAB):
        cols = slice(j * LANES, (j + 1) * LANES)
        f = (w_lo * buf_ref[k, :, j, :, :].reshape(rt, LANES)
             + w_hi * buf_ref[k, :, X_SLAB + j, :, :].reshape(rt, LANES))
        if residual:
            o_ref[:, cols] = h_ref[:, cols] + mod_ref[5:6, cols] * f
        else:
            o_ref[:, cols] = f


def _collect_call(ys, slots, route_cols, residual=None):
    n = slots.shape[0]
    d = D_MODEL
    steps = n // MOE_ROWS
    slots3 = slots.reshape(steps, 1, MOE_ROWS)
    row = pl.BlockSpec((MOE_ROWS, d), lambda i: (i, 0))
    in_specs = [pl.BlockSpec((1, 1, MOE_ROWS), lambda i: (i, 0, 0), memory_space=pltpu.SMEM),
                pl.BlockSpec((1, 1, MOE_ROWS), lambda i: (jnp.minimum(i + 1, steps - 1), 0, 0),
                             memory_space=pltpu.SMEM),
                pl.BlockSpec(memory_space=pl.ANY),
                pl.BlockSpec((MOE_ROWS, LANES), lambda i: (i, 0))]
    args = [slots3, slots3, ys.reshape(ys.shape[0], 1, LANES), route_cols]
    if residual is not None:
        h, mod, tiles_per_seq, n_batch = residual
        in_specs += [row, _mod_spec(tiles_per_seq, n_batch)]
        args += [h, mod]
    return pl.pallas_call(
        functools.partial(_collect_kernel, steps=steps, residual=residual is not None),
        grid=(steps,),
        in_specs=in_specs,
        out_specs=row,
        out_shape=jax.ShapeDtypeStruct((n, d), F32),
        scratch_shapes=[pltpu.VMEM((2, MOE_ROWS // SUBLANES, Y_SLAB, SUBLANES, LANES), F32),
                        pltpu.SemaphoreType.DMA((2,))],
        compiler_params=_params("arbitrary"),
        name="moe_collect",
    )(*args)


def _moe(rows, route, route_cols, w_gate, w_up, w_down, layer, residual=None):
    n = rows.shape[0]
    tm = MOE_TILE
    assert n % MOE_ROWS == 0
    max_tiles = -(-n // tm) + N_CLASSES
    p = max_tiles * tm
    lo = route[0].astype(jnp.int32)
    hi = route[1].astype(jnp.int32)
    lo_in = lo % EXPERTS_PER_GROUP
    hi_in = hi % EXPERTS_PER_GROUP
    pair = lo_in * 3 - (lo_in * (lo_in - 1)) // 2 + (hi_in - lo_in - 1)
    cls = (lo // EXPERTS_PER_GROUP) * PAIRS_PER_GROUP + pair
    onehot = (cls[:, None] == jnp.arange(N_CLASSES, dtype=jnp.int32)[None, :]).astype(jnp.int32)
    counts = jnp.sum(onehot, axis=0)
    rank = jnp.sum(jnp.cumsum(onehot, axis=0) * onehot, axis=1) - 1
    tiles_c = (counts + tm - 1) // tm
    tile_end = jnp.cumsum(tiles_c)
    tile_start = tile_end - tiles_c
    n_used = tile_end[-1]
    slot_of_token = (tile_start[cls] * tm + rank).astype(jnp.int32)
    tile_ids = jnp.arange(max_tiles, dtype=jnp.int32)
    tile_cls = jnp.minimum(jnp.sum((tile_end[None, :] <= tile_ids[:, None]).astype(jnp.int32), axis=1),
                           N_CLASSES - 1)
    group0 = (tile_cls // PAIRS_PER_GROUP) * EXPERTS_PER_GROUP
    e_lo = group0 + jnp.asarray(PAIR_LO, jnp.int32)[tile_cls % PAIRS_PER_GROUP]
    e_hi = group0 + jnp.asarray(PAIR_HI, jnp.int32)[tile_cls % PAIRS_PER_GROUP]
    tails = jnp.stack([tiles_c, tile_end]).astype(jnp.int32)
    xs = _dispatch_call(rows, slot_of_token, tails, p)
    ys = _experts_call(xs, e_lo.astype(jnp.int32), e_hi.astype(jnp.int32),
                       n_used.reshape(1).astype(jnp.int32), w_gate, w_up, w_down, layer)
    return _collect_call(ys, slot_of_token, route_cols, residual)


def _rope_tables(seq):
    rows = seq // GRID_W
    row = jnp.repeat(jnp.arange(rows), GRID_W).astype(F32)
    col = jnp.tile(jnp.arange(GRID_W), rows).astype(F32)
    n_freq = HEAD_DIM // 4
    inv_freq = ROPE_BASE ** (-jnp.arange(n_freq, dtype=F32) / n_freq)
    ang_r = row[:, None] * inv_freq
    ang_c = col[:, None] * inv_freq
    cos = jnp.concatenate([jnp.cos(ang_r), jnp.cos(ang_r), jnp.cos(ang_c), jnp.cos(ang_c)], axis=1)
    sin = jnp.concatenate([-jnp.sin(ang_r), jnp.sin(ang_r), -jnp.sin(ang_c), jnp.sin(ang_c)], axis=1)
    cos = jnp.concatenate([jnp.tile(cos, (1, 2)), jnp.ones((ROW_TILE, LANES), F32)], axis=0)
    sin = jnp.concatenate([jnp.tile(sin, (1, 2)), jnp.zeros((ROW_TILE, LANES), F32)], axis=0)
    return cos, sin


def _block_diag_halves(w):
    per_half = LRU_HALF // LRU_BLOCK
    halves = LRU_WIDTH // LRU_HALF
    blocks = w.reshape(2, halves, per_half, LRU_BLOCK, LRU_BLOCK).transpose(1, 0, 2, 3, 4)
    eye = jnp.eye(per_half, dtype=w.dtype)
    out = blocks[:, :, :, :, None, :] * eye[None, None, :, None, :, None]
    return out.reshape(halves, 2, LRU_HALF, LRU_HALF).astype(BF16)


def _gla_gate_weights(gate_up, gate_b):
    w = jnp.zeros((GLA_HEADS, 2, LANES, GLA_DK), F32)
    for d in range(2):
        wd = gate_up[d].reshape(GLA_LOWRANK, GLA_HEADS, GLA_DK).transpose(1, 0, 2)
        w = w.at[:, d, d * GLA_LOWRANK:(d + 1) * GLA_LOWRANK, :].set(wd)
    b = gate_b.reshape(2, GLA_HEADS, GLA_DK).transpose(1, 0, 2)
    w_hi = w.astype(BF16)
    w_lo = (w - w_hi.astype(F32)).astype(BF16)
    return jnp.concatenate([w_hi, w_lo], axis=-1), b


def kernel(x, c, ctx, c_ctx, router_w, router_bias, ada_w, ada_b, norm_mix_g, norm_ffn_g, moe_w_gate, moe_w_up, moe_w_down, ev_w_in, ev_w_out, ev_conv_w, ev_conv_b, ev_rg_w, ev_rg_b, ev_ig_w, ev_ig_b, ev_lambda, ev_q_norm_g, ev_k_norm_g, ev_sink, od_w_in, od_w_out, od_gate_up, od_gate_b, od_out_norm_g):
    n_batch, seq, d = x.shape
    ctx_len = ctx.shape[1]
    n_lat = n_batch * seq
    n_ctx = n_batch * ctx_len
    assert ada_w.shape[0] == 2 and d == D_MODEL
    assert seq % ROW_TILE == 0 and n_ctx % ROW_TILE == 0 and seq % ctx_len == 0
    assert ctx_len % LRU_CHUNK == 0 and seq % GLA_BLOCK == 0 and ctx_len % GLA_BLOCK == 0
    tiles_per_seq = seq // ROW_TILE
    n_lat_tiles = n_lat // ROW_TILE

    x_lat = x.reshape(n_lat, d)
    x_ctx = ctx.reshape(n_ctx, d)
    mod_rows = -(-(n_batch + 1) // SUBLANES) * SUBLANES
    cvec = jnp.zeros((mod_rows, d), F32).at[:n_batch].set(c).at[n_batch].set(c_ctx)
    mods = _ada_call(cvec, ada_w, ada_b).reshape(2, mod_rows, 6, d)
    mod_mix = [mods[l][:n_batch + 1, 0:6] for l in range(2)]
    rw_pad = jnp.pad(router_w, ((0, 0), (0, LANES - N_EXPERTS)))
    rw_hi = rw_pad.astype(BF16)
    rw_t = jnp.concatenate([rw_hi, (rw_pad - rw_hi.astype(F32)).astype(BF16)], axis=1)
    rb = router_bias.reshape(N_EXPERTS, 1)

    w_in0 = ev_w_in[0].astype(BF16)
    w_out0 = ev_w_out[0].astype(BF16)
    cos_t, sin_t = _rope_tables(seq)
    seg = np.kron(np.eye(LANES // HEAD_DIM, dtype=np.float32), np.full((HEAD_DIM, HEAD_DIM), 1.0 / HEAD_DIM, np.float32))
    seg = jnp.asarray(np.concatenate([seg, seg], axis=0)).astype(BF16)
    proj0, qrot, kvrot = _proj_even_call(x_lat, x_ctx, mod_mix[0], norm_mix_g[0:1], w_in0, cos_t, sin_t,
                                         jnp.tile(ev_q_norm_g[0], 2)[None, :],
                                         jnp.tile(ev_k_norm_g[0], 2)[None, :], seg, tiles_per_seq, n_batch)
    a_lat, a_ctx = _rglru_call(proj0, ev_conv_w[0], ev_conv_b[0:1], _block_diag_halves(ev_rg_w[0]),
                               _block_diag_halves(ev_ig_w[0]), ev_rg_b[0], ev_ig_b[0], ev_lambda[0],
                               n_batch, seq, ctx_len)
    b_lat = _attn_call(qrot, kvrot, ev_sink[0], n_batch, seq, ctx_len)
    b_ctx = _ctx_attn_call(qrot, kvrot, ev_sink[0], n_batch, seq, ctx_len)
    h, rows, route, route_cols = _outproj_even_call(a_lat, a_ctx, b_lat, b_ctx, w_out0, x_lat, x_ctx, mod_mix[0],
                                                    norm_ffn_g[0:1], rw_t, rb, tiles_per_seq, n_batch)
    ffn = _moe(rows, route, route_cols, moe_w_gate, moe_w_up, moe_w_down, 0)

    w_in1 = jnp.pad(od_w_in[0], ((0, 0), (0, ODD_IN_PAD - ODD_IN))).astype(BF16)
    h, proj1 = _proj_res_call(h, ffn, mod_mix[0], mod_mix[1], norm_mix_g[1:2], w_in1, tiles_per_seq, n_batch)
    gate_w, gate_b = _gla_gate_weights(od_gate_up[0], od_gate_b[0])
    o = _gla_call(proj1, gate_w, gate_b, od_out_norm_g[0:1], n_batch, seq, ctx_len)
    h_lat, rows, route, route_cols = _outproj_odd_call(o, od_w_out[0].astype(BF16), h, mod_mix[1],
                                                       norm_ffn_g[1:2], rw_t, rb, tiles_per_seq, n_batch)
    out = _moe(rows, route, route_cols, moe_w_gate, moe_w_up, moe_w_down, 1,
               residual=(h_lat, mod_mix[1], seq // MOE_ROWS, n_batch))
    return out.reshape(n_batch, seq, d)
```

```python
import functools

import jax
import jax.numpy as jnp
import numpy as np
from jax import lax
from jax.experimental import pallas as pl
from jax.experimental.pallas import tpu as pltpu

F32 = jnp.float32
BF16 = jnp.bfloat16
HIGHEST = lax.Precision.HIGHEST
ACT_DTYPE = BF16

D_MODEL = 1024
GRID_W = 64
EPS = 1e-6
LRU_WIDTH = 512
LRU_BLOCK = 64
CONV_W = 4
LRU_C = 8.0
ATT_HEADS = 8
ATT_KV_HEADS = 2
HEAD_DIM = 64
WINDOW = 128
ATT_BLOCK = 128
ROPE_BASE = 10000.0
LOG2E = 1.4426950408889634
ATT_Q_WIDTH = ATT_HEADS * HEAD_DIM
ATT_KV_WIDTH = ATT_KV_HEADS * HEAD_DIM
EVEN_IN = 2 * LRU_WIDTH + ATT_Q_WIDTH + 2 * ATT_KV_WIDTH
GLA_HEADS = 4
GLA_DK = 128
GLA_DV = 256
GLA_LOWRANK = 16
GLA_TAU = 16.0
GLA_CHUNK = 64
GLA_BLOCK = 256
GLA_K_WIDTH = GLA_HEADS * GLA_DK
GLA_V_WIDTH = GLA_HEADS * GLA_DV
ODD_IN = 2 * GLA_K_WIDTH + 2 * GLA_V_WIDTH + 2 * GLA_LOWRANK
N_EXPERTS = 16
N_GROUPS = 4
EXPERTS_PER_GROUP = 4
D_EXPERT = 512
PAIRS_PER_GROUP = 6
N_CLASSES = N_GROUPS * PAIRS_PER_GROUP
PAIR_LO = (0, 0, 0, 1, 1, 2)
PAIR_HI = (1, 2, 3, 2, 3, 3)

LANES = 128
SUBLANES = 8
VMEM_LIMIT_BYTES = 56 * 1024 * 1024

ROW_TILE = 512
MOE_TILE = 256
MOE_ROWS = 512
LRU_CHUNK = 128
LRU_HALF = 256
ODD_IN_PAD = ((ODD_IN + LANES - 1) // LANES) * LANES


def _params(*sem):
    return pltpu.CompilerParams(dimension_semantics=sem, vmem_limit_bytes=VMEM_LIMIT_BYTES)


def _modulated_norm(x, g, shift, scale):
    y = x * lax.rsqrt(jnp.mean(x * x, axis=-1, keepdims=True) + EPS)
    return (y * g) * (1.0 + scale) + shift


def _ada_kernel(c_ref, w_ref, b_ref, o_ref):
    c = c_ref[...]
    s = c * jax.nn.sigmoid(c)
    o_ref[...] = jnp.dot(s, w_ref[...], preferred_element_type=F32) + b_ref[...]


def _ada_call(cvec, ada_w, ada_b):
    depth, d, n6 = ada_w.shape
    r = cvec.shape[0]
    tn = n6 // 4
    return pl.pallas_call(
        _ada_kernel,
        grid=(depth, n6 // tn),
        in_specs=[pl.BlockSpec((r, d), lambda l, j: (0, 0)),
                  pl.BlockSpec((None, d, tn), lambda l, j: (l, 0, j)),
                  pl.BlockSpec((None, 1, tn), lambda l, j: (l, 0, j))],
        out_specs=pl.BlockSpec((None, r, tn), lambda l, j: (l, 0, j)),
        out_shape=jax.ShapeDtypeStruct((depth, r, n6), F32),
        compiler_params=_params("arbitrary", "arbitrary"),
        name="adaln",
    )(cvec, ada_w, ada_b.reshape(depth, 1, n6))


def _two_source_specs(width, n_lat_tiles):
    lat = pl.BlockSpec((ROW_TILE, width), lambda i: (jnp.minimum(i, n_lat_tiles - 1), 0))
    ctx = pl.BlockSpec((ROW_TILE, width), lambda i: (jnp.maximum(i - n_lat_tiles, 0), 0))
    return lat, ctx


def _two_source(lat_ref, ctx_ref, n_lat_tiles):
    return jnp.where(pl.program_id(0) >= n_lat_tiles, ctx_ref[...], lat_ref[...])


def _proj_even_kernel(xl_ref, xc_ref, mod_ref, g_ref, w_ref, c_ref, s_ref, qg_ref, kg_ref, m_ref,
                      o_ref, qo_ref, kvo_ref, *, n_lat_tiles):
    x = _two_source(xl_ref, xc_ref, n_lat_tiles)
    n = _modulated_norm(x, g_ref[...], mod_ref[0:1, :], mod_ref[1:2, :])
    proj = jnp.dot(n.astype(BF16), w_ref[...], preferred_element_type=F32)
    lru = 2 * LRU_WIDTH
    o_ref[...] = proj[:, 0:lru].astype(o_ref.dtype)
    k0 = lru + ATT_Q_WIDTH
    _qk_prepare(proj[:, lru:k0], proj[:, k0:k0 + ATT_KV_WIDTH], proj[:, k0 + ATT_KV_WIDTH:k0 + 2 * ATT_KV_WIDTH],
                c_ref[...], s_ref[...], qg_ref[...], kg_ref[...], m_ref[...], qo_ref, kvo_ref)


def _proj_res_kernel(h_ref, f_ref, pmod_ref, mod_ref, g_ref, w_ref, x_ref, o_ref):
    x = h_ref[...] + pmod_ref[5:6, :] * f_ref[...]
    x_ref[...] = x
    n = _modulated_norm(x, g_ref[...], mod_ref[0:1, :], mod_ref[1:2, :])
    o_ref[...] = jnp.dot(n.astype(BF16), w_ref[...], preferred_element_type=F32).astype(o_ref.dtype)


def _mod_spec(tiles_per_seq, n_batch):
    return pl.BlockSpec((None, 6, D_MODEL), lambda i: (jnp.minimum(i // tiles_per_seq, n_batch), 0, 0))


def _proj_even_call(x_lat, x_ctx, mod, g, w, cos_t, sin_t, qg, kg, seg_mean, tiles_per_seq, n_batch):
    d = x_lat.shape[1]
    n = x_lat.shape[0] + x_ctx.shape[0]
    n_out = w.shape[1]
    n_lat_tiles = x_lat.shape[0] // ROW_TILE
    lat, ctx = _two_source_specs(d, n_lat_tiles)

    def pos(i):
        return (jnp.where(i < n_lat_tiles, i % tiles_per_seq, tiles_per_seq), 0)

    const = lambda shape: pl.BlockSpec(shape, lambda i: (0, 0))
    row = lambda width: pl.BlockSpec((ROW_TILE, width), lambda i: (i, 0))
    return pl.pallas_call(
        functools.partial(_proj_even_kernel, n_lat_tiles=n_lat_tiles),
        grid=(n // ROW_TILE,),
        in_specs=[lat, ctx,
                  _mod_spec(tiles_per_seq, n_batch),
                  const((1, d)), const((d, n_out)),
                  pl.BlockSpec((ROW_TILE, LANES), pos), pl.BlockSpec((ROW_TILE, LANES), pos),
                  const((1, LANES)), const((1, LANES)), const((2 * LANES, LANES))],
        out_specs=[row(2 * LRU_WIDTH), row(ATT_Q_WIDTH), row(4 * LANES)],
        out_shape=[jax.ShapeDtypeStruct((n, 2 * LRU_WIDTH), ACT_DTYPE),
                   jax.ShapeDtypeStruct((n, ATT_Q_WIDTH), BF16),
                   jax.ShapeDtypeStruct((n, 4 * LANES), BF16)],
        compiler_params=_params("arbitrary"),
        name="in_proj",
    )(x_lat, x_ctx, mod, g, w, cos_t, sin_t, qg, kg, seg_mean)


def _proj_res_call(h, ffn, pmod, mod, g, w, tiles_per_seq, n_batch):
    n, d = h.shape
    n_out = w.shape[1]
    row = pl.BlockSpec((ROW_TILE, d), lambda i: (i, 0))
    return pl.pallas_call(
        _proj_res_kernel,
        grid=(n // ROW_TILE,),
        in_specs=[row, row, _mod_spec(tiles_per_seq, n_batch), _mod_spec(tiles_per_seq, n_batch),
                  pl.BlockSpec((1, d), lambda i: (0, 0)),
                  pl.BlockSpec((d, n_out), lambda i: (0, 0))],
        out_specs=[row, pl.BlockSpec((ROW_TILE, n_out), lambda i: (i, 0))],
        out_shape=[jax.ShapeDtypeStruct((n, d), F32), jax.ShapeDtypeStruct((n, n_out), ACT_DTYPE)],
        compiler_params=_params("arbitrary"),
        name="in_proj_res",
    )(h, ffn, pmod, mod, g, w)


def _qk_prepare(q, k, v, cos, sin, qg, kg, seg_mean, qo_ref, kvo_ref):
    lane = lax.broadcasted_iota(jnp.int32, cos.shape, 1)
    first = (lane % 32) < 16
    low = lane < HEAD_DIM

    def norm_rope(x, g):
        ms = jnp.dot(jnp.concatenate(_split2(x * x), axis=1), seg_mean, preferred_element_type=F32)
        xn = (x * lax.rsqrt(ms + EPS)) * g
        partner = jnp.where(first, pltpu.roll(xn, LANES - 16, 1), pltpu.roll(xn, 16, 1))
        return xn * cos + partner * sin

    for j in range(ATT_Q_WIDTH // LANES):
        qj = norm_rope(q[:, j * LANES:(j + 1) * LANES], qg)
        qo_ref[:, j * LANES:(j + 1) * LANES] = (qj * (HEAD_DIM ** -0.5 * LOG2E)).astype(BF16)
    kr = norm_rope(k, kg)
    ks = pltpu.roll(kr, HEAD_DIM, 1)
    kvo_ref[:, 0:LANES] = jnp.where(low, kr, ks).astype(BF16)
    kvo_ref[:, LANES:2 * LANES] = jnp.where(low, ks, kr).astype(BF16)
    kvo_ref[:, 2 * LANES:3 * LANES] = v.astype(BF16)
    kvo_ref[:, 3 * LANES:4 * LANES] = jnp.ones((cos.shape[0], LANES), BF16)


def _attend(q, kv, sink_ref, bias):
    rows = q.shape[0]
    lane = lax.broadcasted_iota(jnp.int32, (rows, LANES), 1)
    low = lane < HEAD_DIM
    zero = jnp.zeros((rows, LANES), BF16)
    vv = kv[:, 2 * LANES:4 * LANES]
    groups = []
    heads_per_kv = ATT_HEADS // ATT_KV_HEADS
    for kh in range(ATT_KV_HEADS):
        kk = kv[:, kh * LANES:(kh + 1) * LANES]
        parts = []
        sinks = []
        for j in range(heads_per_kv):
            h = kh * heads_per_kv + j
            qg = q[:, (h // 2) * LANES:(h // 2 + 1) * LANES]
            parts.append(jnp.where(low, qg, zero) if h % 2 == 0 else jnp.where(low, zero, qg))
            sinks.append(jnp.full((rows, 1), sink_ref[h] * LOG2E, F32))
        lhs = jnp.concatenate(parts, axis=0)
        sink = jnp.concatenate(sinks, axis=0)
        s = lax.dot_general(lhs, kk, (((1,), (1,)), ((), ())), preferred_element_type=F32)
        if bias is not None:
            s = s + bias
        m = jnp.maximum(jnp.max(s, axis=-1, keepdims=True), sink)
        e = jnp.exp2(s - m)
        ov = jnp.dot(e.astype(BF16), vv, preferred_element_type=F32)
        o = ov[:, 0:LANES] / (ov[:, LANES:2 * LANES] + jnp.exp2(sink - m))
        r = [o[j * rows:(j + 1) * rows] for j in range(heads_per_kv)]
        for p in range(heads_per_kv // 2):
            even, odd = r[2 * p], r[2 * p + 1]
            if kh == 0:
                groups.append(jnp.where(low, even, pltpu.roll(odd, HEAD_DIM, 1)))
            else:
                groups.append(jnp.where(low, pltpu.roll(even, HEAD_DIM, 1), odd))
    return jnp.concatenate(groups, axis=1)


def _attn_kernel(sink_ref, q_ref, kvp_ref, kvo_ref, kvn_ref, kvc_ref, bias0_ref, bias1_ref, o_ref):
    blk = ATT_BLOCK
    own = kvo_ref[...]
    ctx = kvc_ref[...]
    kv0 = jnp.concatenate([kvp_ref[...], own, ctx], axis=0)
    kv1 = jnp.concatenate([own, kvn_ref[...], ctx], axis=0)
    o_ref[0:blk, :] = _attend(q_ref[0:blk, :], kv0, sink_ref, bias0_ref[...]).astype(o_ref.dtype)
    o_ref[blk:2 * blk, :] = _attend(q_ref[blk:2 * blk, :], kv1, sink_ref, bias1_ref[...]).astype(o_ref.dtype)


def _window_bias(ctx_len):
    blk = ATT_BLOCK
    stacked = (ATT_HEADS // ATT_KV_HEADS) * blk
    qi = lax.broadcasted_iota(jnp.int32, (4, stacked, 3 * blk + ctx_len), 1) % blk
    ks = lax.broadcasted_iota(jnp.int32, (4, stacked, 3 * blk + ctx_len), 2)
    var = lax.broadcasted_iota(jnp.int32, (4, stacked, 3 * blk + ctx_len), 0)
    in_win = jnp.abs(ks - blk - qi) <= WINDOW
    in_seq = ((ks >= blk) | (var % 2 == 1)) & ((ks < 2 * blk) | (var >= 2))
    keep = (in_win & in_seq) | (ks >= 3 * blk)
    return jnp.where(keep, 0.0, -1e30).astype(F32)


def _ctx_attn_kernel(sink_ref, q_ref, kvc_ref, o_ref):
    o_ref[...] = _attend(q_ref[...], kvc_ref[...], sink_ref, None).astype(o_ref.dtype)


def _attn_call(qrot, kvrot, sink, n_batch, seq, ctx_len):
    nb = seq // ATT_BLOCK
    ctx0 = (n_batch * seq) // ctx_len
    kvw = kvrot.shape[1]
    bias = _window_bias(ctx_len)

    assert nb % 2 == 0
    pairs = nb // 2

    def variant0(b, m):
        return ((m > 0).astype(jnp.int32) + 2, 0, 0)

    def variant1(b, m):
        return (1 + 2 * (m < pairs - 1).astype(jnp.int32), 0, 0)

    blk = ATT_BLOCK
    return pl.pallas_call(
        _attn_kernel,
        grid=(n_batch, pairs),
        in_specs=[pl.BlockSpec(memory_space=pltpu.SMEM),
                  pl.BlockSpec((2 * blk, ATT_Q_WIDTH), lambda b, m: (b * pairs + m, 0)),
                  pl.BlockSpec((blk, kvw), lambda b, m: (b * nb + jnp.maximum(2 * m - 1, 0), 0)),
                  pl.BlockSpec((2 * blk, kvw), lambda b, m: (b * pairs + m, 0)),
                  pl.BlockSpec((blk, kvw), lambda b, m: (b * nb + jnp.minimum(2 * m + 2, nb - 1), 0)),
                  pl.BlockSpec((ctx_len, kvw), lambda b, m: (ctx0 + b, 0)),
                  pl.BlockSpec((None,) + bias.shape[1:], variant0),
                  pl.BlockSpec((None,) + bias.shape[1:], variant1)],
        out_specs=pl.BlockSpec((2 * blk, ATT_Q_WIDTH), lambda b, m: (b * pairs + m, 0)),
        out_shape=jax.ShapeDtypeStruct((n_batch * seq, ATT_Q_WIDTH), ACT_DTYPE),
        compiler_params=_params("arbitrary", "arbitrary"),
        name="window_attn",
    )(sink, qrot, kvrot, kvrot, kvrot, kvrot, bias, bias)


def _ctx_attn_call(qrot, kvrot, sink, n_batch, seq, ctx_len):
    ctx0 = (n_batch * seq) // ctx_len
    kvw = kvrot.shape[1]
    return pl.pallas_call(
        _ctx_attn_kernel,
        grid=(n_batch,),
        in_specs=[pl.BlockSpec(memory_space=pltpu.SMEM),
                  pl.BlockSpec((ctx_len, ATT_Q_WIDTH), lambda b: (ctx0 + b, 0)),
                  pl.BlockSpec((ctx_len, kvw), lambda b: (ctx0 + b, 0))],
        out_specs=pl.BlockSpec((ctx_len, ATT_Q_WIDTH), lambda b: (b, 0)),
        out_shape=jax.ShapeDtypeStruct((n_batch * ctx_len, ATT_Q_WIDTH), ACT_DTYPE),
        compiler_params=_params("arbitrary"),
        name="ctx_attn",
    )(sink, qrot, kvrot)


def _scan_rows(a, x, carry, reverse):
    rows, width = a.shape
    n_groups = rows // SUBLANES
    a = a.reshape(n_groups, SUBLANES, width)
    x = x.reshape(n_groups, SUBLANES, width)
    sub = lax.broadcasted_iota(jnp.int32, a.shape, 1)
    for s in (1, 2, 4):
        shift = SUBLANES - s if reverse else s
        a_sh = pltpu.roll(a, shift, 1)
        x_sh = pltpu.roll(x, shift, 1)
        keep = (sub <= SUBLANES - 1 - s) if reverse else (sub >= s)
        x = x + a * jnp.where(keep, x_sh, 0.0)
        a = a * jnp.where(keep, a_sh, 1.0)
    a = a.reshape(rows, width)
    x = x.reshape(rows, width)
    out = [None] * n_groups
    order = range(n_groups - 1, -1, -1) if reverse else range(n_groups)
    for g in order:
        hg = x[g * SUBLANES:(g + 1) * SUBLANES] + a[g * SUBLANES:(g + 1) * SUBLANES] * carry
        carry = hg[0:1] if reverse else hg[SUBLANES - 1:SUBLANES]
        out[g] = hg
    return jnp.concatenate(out, axis=0), carry


def _rglru_kernel(ul_ref, gl_ref, uc_ref, gc_ref, cw_ref, cb_ref, wr_ref, wi_ref, br_ref, bi_ref, lam_ref,
                  yl_ref, yc_ref, upad_ref, conv_ref, *, seq, ctx_len):
    width = ul_ref.shape[1]
    tc = LRU_CHUNK
    neg_lam = -lam_ref[...]
    softplus = jnp.maximum(neg_lam, 0.0) + jnp.log1p(jnp.exp(-jnp.abs(neg_lam)))
    rate = (-LRU_C * LOG2E) * softplus
    zeros = jnp.zeros((SUBLANES, width), F32)

    def gates(u, d):
        ub = u.astype(BF16)
        r = jax.nn.sigmoid(jnp.dot(ub, wr_ref[d], preferred_element_type=F32) + br_ref[d:d + 1, :])
        i = jax.nn.sigmoid(jnp.dot(ub, wi_ref[d], preferred_element_type=F32) + bi_ref[d:d + 1, :])
        a = jnp.exp2(r * rate[d:d + 1, :])
        y = 1.0 - a * a
        root = jnp.where(y > 0.0, y * lax.rsqrt(y), 0.0)
        return a, root * (i * u)

    def run(u_ref, g_ref, y_ref, n, h_fwd, h_bwd):
        n_chunks = n // tc
        upad_ref[0:SUBLANES, :] = zeros
        upad_ref[pl.ds(SUBLANES, n), :] = u_ref[...].astype(F32)
        upad_ref[pl.ds(SUBLANES + n, SUBLANES), :] = zeros

        def conv_body(c, carry):
            t0 = pl.multiple_of(c * tc, tc)
            win = upad_ref[pl.ds(t0, tc + 2 * SUBLANES), :]
            acc = cb_ref[...]
            for k in range(CONV_W):
                acc = acc + cw_ref[k:k + 1, :] * win[SUBLANES - 1 + k:SUBLANES - 1 + k + tc]
            conv_ref[pl.ds(t0, tc), :] = acc
            return carry

        lax.fori_loop(0, n_chunks, conv_body, 0)

        def fwd_body(c, carry):
            t0 = pl.multiple_of(c * tc, tc)
            a, x = gates(conv_ref[pl.ds(t0, tc), :], 0)
            h, carry = _scan_rows(a, x, carry, False)
            upad_ref[pl.ds(pl.multiple_of(t0 + SUBLANES, SUBLANES), tc), :] = h
            return carry

        h_fwd = lax.fori_loop(0, n_chunks, fwd_body, h_fwd, unroll=min(4, n_chunks))

        def bwd_body(c, carry):
            t0 = pl.multiple_of((n_chunks - 1 - c) * tc, tc)
            a, x = gates(conv_ref[pl.ds(t0, tc), :], 1)
            h, carry = _scan_rows(a, x, carry, True)
            h_both = upad_ref[pl.ds(pl.multiple_of(t0 + SUBLANES, SUBLANES), tc), :] + h
            y_ref[pl.ds(t0, tc), :] = (h_both * jax.nn.gelu(g_ref[pl.ds(t0, tc), :].astype(F32))).astype(y_ref.dtype)
            return carry

        h_bwd = lax.fori_loop(0, n_chunks, bwd_body, h_bwd, unroll=min(4, n_chunks))
        return h_fwd, h_bwd

    h0 = jnp.zeros((1, width), F32)
    h_fwd, h_bwd = run(uc_ref, gc_ref, yc_ref, ctx_len, h0, h0)
    run(ul_ref, gl_ref, yl_ref, seq, h_fwd, h_bwd)


def _rglru_call(proj, conv_w, conv_b, wr, wi, br, bi, lam, n_batch, seq, ctx_len):
    ctx0 = (n_batch * seq) // ctx_len
    halves = LRU_WIDTH // LRU_HALF
    w = LRU_HALF
    vec2 = pl.BlockSpec((2, w), lambda b, c: (0, c))
    return pl.pallas_call(
        functools.partial(_rglru_kernel, seq=seq, ctx_len=ctx_len),
        grid=(n_batch, halves),
        in_specs=[pl.BlockSpec((seq, w), lambda b, c: (b, c)),
                  pl.BlockSpec((seq, w), lambda b, c: (b, halves + c)),
                  pl.BlockSpec((ctx_len, w), lambda b, c: (ctx0 + b, c)),
                  pl.BlockSpec((ctx_len, w), lambda b, c: (ctx0 + b, halves + c)),
                  pl.BlockSpec((CONV_W, w), lambda b, c: (0, c)),
                  pl.BlockSpec((1, w), lambda b, c: (0, c)),
                  pl.BlockSpec((None, 2, w, w), lambda b, c: (c, 0, 0, 0)),
                  pl.BlockSpec((None, 2, w, w), lambda b, c: (c, 0, 0, 0)),
                  vec2, vec2, vec2],
        out_specs=[pl.BlockSpec((seq, w), lambda b, c: (b, c)),
                   pl.BlockSpec((ctx_len, w), lambda b, c: (b, c))],
        out_shape=[jax.ShapeDtypeStruct((n_batch * seq, LRU_WIDTH), ACT_DTYPE),
                   jax.ShapeDtypeStruct((n_batch * ctx_len, LRU_WIDTH), ACT_DTYPE)],
        scratch_shapes=[pltpu.VMEM((seq + 2 * SUBLANES, w), F32), pltpu.VMEM((seq, w), F32)],
        compiler_params=_params("arbitrary", "arbitrary"),
        name="rglru",
    )(proj, proj, proj, proj, conv_w, conv_b, wr, wi, br, bi, lam)


def _split2(x):
    hi = x.astype(BF16)
    return hi, (x - hi.astype(F32)).astype(BF16)


def _gla_kernel(q_ref, k_ref, v_ref, r_ref, a_ref, kc_ref, vc_ref, ac_ref, gw_ref, gb_ref, ng_ref,
                o_ref, of_ref, ob_ref, qd_ref, ki_ref, ke_ref, la_ref, eg_ref, kv_ref, st_ref, *, seq, ctx_len):
    ch = GLA_CHUNK
    blk = GLA_BLOCK
    per_blk = blk // ch
    chunk_of_row = lax.broadcasted_iota(jnp.int32, (blk, GLA_DK), 0) // ch
    row = lax.broadcasted_iota(jnp.int32, (blk, blk), 0)
    col = lax.broadcasted_iota(jnp.int32, (blk, blk), 1)
    same = (row // ch) == (col // ch)
    keep = (same & (row >= col), same & (row <= col))
    tri = (keep[0].astype(BF16),)
    scale = GLA_DK ** -0.5
    nt = (((1,), (1,)), ((), ()))
    tn = (((0,), (0,)), ((), ()))

    def mm(a, b):
        return jnp.dot(a, b, preferred_element_type=F32)

    def prepare(a_src, k_src, v_src, q_src, dst, n_rows, d):
        n_blk = n_rows // blk

        def decay_body(i, carry):
            t0 = pl.multiple_of(i * blk, blk)
            both = mm(a_src[pl.ds(t0, blk), :], gw_ref[d])
            pre = (both[:, 0:GLA_DK] + both[:, GLA_DK:2 * GLA_DK]) + gb_ref[d:d + 1, :]
            la_ref[pl.ds(t0, blk), :] = (jnp.minimum(pre, 0.0) - jnp.log(1.0 + jnp.exp(-jnp.abs(pre)))) / GLA_TAU
            return carry

        lax.fori_loop(0, n_blk, decay_body, 0, unroll=min(8, n_blk))

        def scale_block(i):
            t0 = pl.multiple_of(i * blk, blk)
            log_a = la_ref[pl.ds(t0, blk), :]
            sums = mm(tri[0], jnp.concatenate(_split2(log_a), axis=1))
            prefix = sums[:, 0:GLA_DK] + sums[:, GLA_DK:2 * GLA_DK]
            total = jnp.concatenate(
                [jnp.broadcast_to(prefix[(j + 1) * ch - 1:(j + 1) * ch], (ch, GLA_DK)) for j in range(per_blk)],
                axis=0)
            suffix = total - prefix + log_a
            b = prefix if d == 0 else suffix
            to_end = (suffix if d == 0 else prefix) - log_a
            for j in range(per_blk):
                eg_ref[pl.ds(i * per_blk + j, 1), :] = jnp.exp(total[j * ch:j * ch + 1])
            k_blk = k_src[pl.ds(t0, blk), :].astype(F32)
            ke_ref[pl.ds(t0, blk), :] = (k_blk * jnp.exp(to_end)).astype(BF16)
            if q_src is not None:
                q_blk = q_src[pl.ds(t0, blk), :].astype(F32)
                qd_ref[pl.ds(t0, blk), :] = ((q_blk * scale) * jnp.exp(b)).astype(BF16)
                ki_ref[pl.ds(t0, blk), :] = (k_blk * jnp.exp(-b)).astype(BF16)

        def intra_block(i):
            t0 = pl.multiple_of(i * blk, blk)
            v_blk = v_src[pl.ds(t0, blk), :].astype(BF16)
            k_end = ke_ref[pl.ds(t0, blk), :]
            spread = jnp.concatenate([jnp.where(chunk_of_row == j, k_end, jnp.zeros_like(k_end))
                                      for j in range(per_blk)], axis=1)
            kv = lax.dot_general(v_blk, spread, tn, preferred_element_type=F32)
            for j in range(per_blk):
                kv_ref[i * per_blk + j] = kv[:, j * GLA_DK:(j + 1) * GLA_DK]
            if q_src is not None:
                s = lax.dot_general(qd_ref[pl.ds(t0, blk), :], ki_ref[pl.ds(t0, blk), :], nt,
                                    preferred_element_type=F32)
                p = jnp.where(keep[d], s, 0.0).astype(BF16)
                dst[pl.ds(t0, blk), :] = mm(p, v_blk)

        def run(block_fn, unroll):
            def body(i, carry):
                block_fn(i)
                return carry

            lax.fori_loop(0, n_blk, body, 0, unroll=min(unroll, n_blk))

        run(scale_block, 4)
        run(intra_block, 8)

    def finish(t0):
        o = of_ref[pl.ds(t0, blk), :] + ob_ref[pl.ds(t0, blk), :]
        on = (o * lax.rsqrt(jnp.mean(o * o, axis=-1, keepdims=True) + EPS)) * ng_ref[...]
        r = r_ref[pl.ds(t0, blk), :].astype(F32)
        o_ref[pl.ds(t0, blk), :] = (on * (r * jax.nn.sigmoid(r))).astype(o_ref.dtype)

    def recur(dst, n_rows, d):
        n_blk = n_rows // blk

        def body(i, carry):
            blk_i = i if d == 0 else n_blk - 1 - i
            for jj in range(per_blk):
                j = jj if d == 0 else per_blk - 1 - jj
                n = blk_i * per_blk + j
                t0 = pl.multiple_of(n * ch, ch)
                st = st_ref[...]
                if dst is not None:
                    dst[pl.ds(t0, ch), :] = dst[pl.ds(t0, ch), :] + lax.dot_general(
                        qd_ref[pl.ds(t0, ch), :], st.astype(BF16), nt, preferred_element_type=F32)
                st_ref[...] = st * eg_ref[pl.ds(n, 1), :] + kv_ref[n]
            if dst is not None and d == 1:
                finish(pl.multiple_of(blk_i * blk, blk))
            return carry

        lax.fori_loop(0, n_blk, body, 0, unroll=min(4, n_blk))

    for d, dst in ((0, of_ref), (1, ob_ref)):
        st_ref[...] = jnp.zeros(st_ref.shape, F32)
        prepare(ac_ref, kc_ref, vc_ref, None, None, ctx_len, d)
        recur(None, ctx_len, d)
        prepare(a_ref, k_ref, v_ref, q_ref, dst, seq, d)
        recur(dst, seq, d)


def _gla_call(proj, gate_w, gate_b, norm_g, n_batch, seq, ctx_len):
    ctx0 = (n_batch * seq) // ctx_len
    kcol = GLA_K_WIDTH // GLA_DK
    vcol = (2 * GLA_K_WIDTH) // GLA_DV
    rcol = (2 * GLA_K_WIDTH + GLA_V_WIDTH) // GLA_DV
    acol = (2 * GLA_K_WIDTH + 2 * GLA_V_WIDTH) // LANES
    return pl.pallas_call(
        functools.partial(_gla_kernel, seq=seq, ctx_len=ctx_len),
        grid=(n_batch, GLA_HEADS),
        in_specs=[pl.BlockSpec((seq, GLA_DK), lambda b, h: (b, h)),
                  pl.BlockSpec((seq, GLA_DK), lambda b, h: (b, kcol + h)),
                  pl.BlockSpec((seq, GLA_DV), lambda b, h: (b, vcol + h)),
                  pl.BlockSpec((seq, GLA_DV), lambda b, h: (b, rcol + h)),
                  pl.BlockSpec((seq, LANES), lambda b, h: (b, acol)),
                  pl.BlockSpec((ctx_len, GLA_DK), lambda b, h: (ctx0 + b, kcol + h)),
                  pl.BlockSpec((ctx_len, GLA_DV), lambda b, h: (ctx0 + b, vcol + h)),
                  pl.BlockSpec((ctx_len, LANES), lambda b, h: (ctx0 + b, acol)),
                  pl.BlockSpec((None, 2, LANES, 2 * GLA_DK), lambda b, h: (h, 0, 0, 0)),
                  pl.BlockSpec((None, 2, GLA_DK), lambda b, h: (h, 0, 0)),
                  pl.BlockSpec((1, GLA_DV), lambda b, h: (0, 0))],
        out_specs=pl.BlockSpec((seq, GLA_DV), lambda b, h: (b, h)),
        out_shape=jax.ShapeDtypeStruct((n_batch * seq, GLA_V_WIDTH), ACT_DTYPE),
        scratch_shapes=[pltpu.VMEM((seq, GLA_DV), F32),
                        pltpu.VMEM((seq, GLA_DV), F32),
                        pltpu.VMEM((seq, GLA_DK), BF16),
                        pltpu.VMEM((seq, GLA_DK), BF16),
                        pltpu.VMEM((seq, GLA_DK), BF16),
                        pltpu.VMEM((seq, GLA_DK), F32),
                        pltpu.VMEM((seq // GLA_CHUNK, GLA_DK), F32),
                        pltpu.VMEM((seq // GLA_CHUNK, GLA_DV, GLA_DK), F32),
                        pltpu.VMEM((GLA_DV, GLA_DK), F32)],
        compiler_params=_params("arbitrary", "arbitrary"),
        name="gla",
    )(proj, proj, proj, proj, proj, proj, proj, proj, gate_w, gate_b, norm_g)


def _route(rows, rows_bf16, rw_ref, rb_ref):
    n_rows = rows.shape[0]
    rows_lo = (rows - rows_bf16.astype(F32)).astype(BF16)
    both = jnp.dot(rows_bf16, rw_ref[...], preferred_element_type=F32)
    logits_rows = (both[:, 0:LANES] + both[:, LANES:2 * LANES]
                   + jnp.dot(rows_lo, rw_ref[:, 0:LANES], preferred_element_type=F32))
    logits = logits_rows.T[0:N_EXPERTS, :]
    scores = jax.nn.sigmoid(logits)
    sel = scores + rb_ref[...]
    eid = lax.broadcasted_iota(jnp.int32, (N_EXPERTS, n_rows), 0)
    gid = eid // EXPERTS_PER_GROUP
    neg = -jnp.inf
    big = N_EXPERTS

    def top2(masked):
        m1 = jnp.max(masked, axis=0, keepdims=True)
        i1 = jnp.min(jnp.where(masked == m1, eid, big), axis=0, keepdims=True)
        rest = jnp.where(eid == i1, neg, masked)
        m2 = jnp.max(rest, axis=0, keepdims=True)
        i2 = jnp.min(jnp.where(rest == m2, eid, big), axis=0, keepdims=True)
        return m1, i1, m2, i2

    best = None
    best_g = None
    for g in range(N_GROUPS):
        m1, _, m2, _ = top2(jnp.where(gid == g, sel, neg))
        gs = m1 + m2
        if best is None:
            best, best_g = gs, jnp.zeros((1, n_rows), jnp.int32)
        else:
            better = gs > best
            best_g = jnp.where(better, g, best_g)
            best = jnp.where(better, gs, best)
    _, i1, _, i2 = top2(jnp.where(gid == best_g, sel, neg))
    s1 = jnp.sum(jnp.where(eid == i1, scores, 0.0), axis=0, keepdims=True)
    s2 = jnp.sum(jnp.where(eid == i2, scores, 0.0), axis=0, keepdims=True)
    total = s1 + s2
    swap = i2 < i1
    lo = jnp.where(swap, i2, i1)
    hi = jnp.where(swap, i1, i2)
    w_lo = jnp.where(swap, s2, s1) / total
    w_hi = jnp.where(swap, s1, s2) / total
    pad = jnp.zeros((4, n_rows), F32)
    return jnp.concatenate([lo.astype(F32), hi.astype(F32), w_lo, w_hi, pad], axis=0)


def _finish_rows(y, h, mod_ref, g_ref, rw_ref, rb_ref, hn_ref, rows_ref, rt_ref, rc_ref):
    hn = h + mod_ref[2:3, :] * y
    hn_ref[...] = hn
    rows = _modulated_norm(hn, g_ref[...], mod_ref[3:4, :], mod_ref[4:5, :])
    rows_bf16 = rows.astype(BF16)
    rows_ref[...] = rows_bf16
    route = _route(rows, rows_bf16, rw_ref, rb_ref)
    rt_ref[...] = route
    padded = jnp.concatenate([route, jnp.zeros((LANES - SUBLANES, route.shape[1]), F32)], axis=0)
    rc_ref[...] = padded.T


def _outproj_even_kernel(al_ref, ac_ref, bl_ref, bc_ref, w_ref, hl_ref, hc_ref, mod_ref, g_ref, rw_ref, rb_ref,
                         hn_ref, rows_ref, rt_ref, rc_ref, *, n_lat_tiles):
    a = _two_source(al_ref, ac_ref, n_lat_tiles).astype(BF16)
    b = _two_source(bl_ref, bc_ref, n_lat_tiles).astype(BF16)
    y = (jnp.dot(a, w_ref[0:LRU_WIDTH, :], preferred_element_type=F32)
         + jnp.dot(b, w_ref[LRU_WIDTH:LRU_WIDTH + ATT_Q_WIDTH, :], preferred_element_type=F32))
    h = _two_source(hl_ref, hc_ref, n_lat_tiles)
    _finish_rows(y, h, mod_ref, g_ref, rw_ref, rb_ref, hn_ref, rows_ref, rt_ref, rc_ref)


def _outproj_odd_kernel(o_ref, w_ref, h_ref, mod_ref, g_ref, rw_ref, rb_ref, hn_ref, rows_ref, rt_ref, rc_ref):
    y = jnp.dot(o_ref[...].astype(BF16), w_ref[...], preferred_element_type=F32)
    _finish_rows(y, h_ref[...], mod_ref, g_ref, rw_ref, rb_ref, hn_ref, rows_ref, rt_ref, rc_ref)


def _outproj_specs(n_rows, tiles_per_seq, n_batch):
    d = D_MODEL
    row = pl.BlockSpec((ROW_TILE, d), lambda i: (i, 0))
    tail_in = [_mod_spec(tiles_per_seq, n_batch),
               pl.BlockSpec((1, d), lambda i: (0, 0)),
               pl.BlockSpec((d, 2 * LANES), lambda i: (0, 0)),
               pl.BlockSpec((N_EXPERTS, 1), lambda i: (0, 0))]
    out_specs = [row, row, pl.BlockSpec((SUBLANES, ROW_TILE), lambda i: (0, i)),
                 pl.BlockSpec((ROW_TILE, LANES), lambda i: (i, 0))]
    out_shape = [jax.ShapeDtypeStruct((n_rows, d), F32), jax.ShapeDtypeStruct((n_rows, d), ACT_DTYPE),
                 jax.ShapeDtypeStruct((SUBLANES, n_rows), F32), jax.ShapeDtypeStruct((n_rows, LANES), F32)]
    return tail_in, out_specs, out_shape


def _outproj_even_call(a_lat, a_ctx, b_lat, b_ctx, w, h_lat, h_ctx, mod, g, rw_t, rb, tiles_per_seq, n_batch):
    n_rows = h_lat.shape[0] + h_ctx.shape[0]
    n_lat_tiles = a_lat.shape[0] // ROW_TILE
    tail_in, out_specs, out_shape = _outproj_specs(n_rows, tiles_per_seq, n_batch)
    lat, ctx = _two_source_specs(LRU_WIDTH, n_lat_tiles)
    h_specs = list(_two_source_specs(D_MODEL, n_lat_tiles))
    return pl.pallas_call(
        functools.partial(_outproj_even_kernel, n_lat_tiles=n_lat_tiles),
        grid=(n_rows // ROW_TILE,),
        in_specs=[lat, ctx, lat, ctx, pl.BlockSpec(w.shape, lambda i: (0, 0))] + h_specs + tail_in,
        out_specs=out_specs,
        out_shape=out_shape,
        compiler_params=_params("arbitrary"),
        name="out_proj_even",
    )(a_lat, a_ctx, b_lat, b_ctx, w, h_lat, h_ctx, mod, g, rw_t, rb)


def _outproj_odd_call(o, w, h, mod, g, rw_t, rb, tiles_per_seq, n_batch):
    n_rows = o.shape[0]
    tail_in, out_specs, out_shape = _outproj_specs(n_rows, tiles_per_seq, n_batch)
    return pl.pallas_call(
        _outproj_odd_kernel,
        grid=(n_rows // ROW_TILE,),
        in_specs=[pl.BlockSpec((ROW_TILE, o.shape[1]), lambda i: (i, 0)),
                  pl.BlockSpec(w.shape, lambda i: (0, 0)),
                  pl.BlockSpec((ROW_TILE, D_MODEL), lambda i: (i, 0))] + tail_in,
        out_specs=out_specs,
        out_shape=out_shape,
        compiler_params=_params("arbitrary"),
        name="out_proj_odd",
    )(o, w, h, mod, g, rw_t, rb)


X_SLAB = D_MODEL // LANES
Y_SLAB = 2 * X_SLAB


def _for_rows(n_rows, fn):
    for g in range(n_rows // SUBLANES):
        for u in range(SUBLANES):
            fn(g, u)


def _dispatch_kernel(tail_ref, slot_ref, rows_ref, xs_ref, slab_ref, zero_ref, sem, zero_sem, *, steps, n_tiles):
    i = pl.program_id(0)
    k = i % 2
    rt = MOE_ROWS
    tile_rows = MOE_TILE * X_SLAB

    def slab_copy(buf, g, u, slot):
        return pltpu.make_async_copy(
            slab_ref.at[buf, g, :, pl.ds(u, 1), :],
            xs_ref.at[pl.ds(pl.multiple_of(slot * X_SLAB, X_SLAB), X_SLAB)], sem.at[buf])

    def wait_all(buf):
        pltpu.make_async_copy(slab_ref.at[buf], slab_ref.at[buf], sem.at[buf]).wait()

    @pl.when(i == 0)
    def _():
        zero_ref[...] = jnp.zeros(zero_ref.shape, F32)

        def fill(t):
            start = pl.multiple_of(t * tile_rows, tile_rows)
            return pltpu.make_async_copy(zero_ref, xs_ref.at[pl.ds(start, tile_rows)], zero_sem)

        n_used = tail_ref[1, N_CLASSES - 1]
        for phase in ("start", "wait"):
            for c in range(N_CLASSES):
                for cond, t in ((tail_ref[0, c] > 0, tail_ref[1, c] - 1), (n_used + c < n_tiles, n_used + c)):
                    @pl.when(cond)
                    def _():
                        getattr(fill(t), phase)()

    @pl.when(i >= 2)
    def _():
        wait_all(k)

    for j in range(X_SLAB):
        lanes = rows_ref[:, j * LANES:(j + 1) * LANES].astype(F32)
        slab_ref[k, :, j, :, :] = lanes.reshape(rt // SUBLANES, SUBLANES, LANES)

    _for_rows(rt, lambda g, u: slab_copy(k, g, u, slot_ref[0, 0, g * SUBLANES + u]).start(priority=u % 2))

    @pl.when(i == steps - 1)
    def _():
        wait_all(k)
        if steps >= 2:
            wait_all(1 - k)


def _dispatch_call(rows, slots, tails, n_slots):
    n, d = rows.shape
    steps = n // MOE_ROWS
    grid_spec = pltpu.PrefetchScalarGridSpec(
        num_scalar_prefetch=1,
        grid=(steps,),
        in_specs=[pl.BlockSpec((1, 1, MOE_ROWS), lambda i, tails: (i, 0, 0), memory_space=pltpu.SMEM),
                  pl.BlockSpec((MOE_ROWS, d), lambda i, tails: (i, 0))],
        out_specs=pl.BlockSpec(memory_space=pl.ANY),
        scratch_shapes=[pltpu.VMEM((2, MOE_ROWS // SUBLANES, X_SLAB, SUBLANES, LANES), F32),
                        pltpu.VMEM((MOE_TILE * X_SLAB, 1, LANES), F32),
                        pltpu.SemaphoreType.DMA((2,)), pltpu.SemaphoreType.DMA(())],
    )
    xs = pl.pallas_call(
        functools.partial(_dispatch_kernel, steps=steps, n_tiles=n_slots // MOE_TILE),
        grid_spec=grid_spec,
        out_shape=jax.ShapeDtypeStruct((n_slots * X_SLAB, 1, LANES), F32),
        compiler_params=_params("arbitrary"),
        name="moe_dispatch",
    )(tails, slots.reshape(steps, 1, MOE_ROWS), rows)
    return xs.reshape(n_slots * X_SLAB, LANES)


def _experts_kernel(elo_ref, ehi_ref, used_ref, x_ref, wg0_ref, wu0_ref, wd0_ref,
                    wg1_ref, wu1_ref, wd1_ref, y_ref, cg0_ref, cu0_ref, cd0_ref, cg1_ref, cu1_ref, cd1_ref):
    t = pl.program_id(0)
    tm = MOE_TILE
    prev = jnp.maximum(t - 1, 0)
    sides = ((elo_ref, (wg0_ref, wu0_ref, wd0_ref), (cg0_ref, cu0_ref, cd0_ref)),
             (ehi_ref, (wg1_ref, wu1_ref, wd1_ref), (cg1_ref, cu1_ref, cd1_ref)))

    @pl.when(t < used_ref[0])
    def _():
        for ids, weights, cached in sides:
            @pl.when((t == 0) | (ids[t] != ids[prev]))
            def _():
                for w_ref, c_ref in zip(weights, cached):
                    c_ref[...] = w_ref[...].astype(BF16)

        x = jnp.concatenate([x_ref[pl.ds(j, tm, stride=X_SLAB), :] for j in range(X_SLAB)],
                            axis=1).astype(BF16)
        for e, (_, _, (cg_ref, cu_ref, cd_ref)) in enumerate(sides):
            gate = jnp.dot(x, cg_ref[...], preferred_element_type=F32)
            up = jnp.dot(x, cu_ref[...], preferred_element_type=F32)
            act = ((gate * jax.nn.sigmoid(gate)) * up).astype(BF16)
            f = jnp.dot(act, cd_ref[...], preferred_element_type=F32)
            for j in range(X_SLAB):
                y_ref[pl.ds(e * X_SLAB + j, tm, stride=Y_SLAB), :] = f[:, j * LANES:(j + 1) * LANES]

    @pl.when(t >= used_ref[0])
    def _():
        y_ref[...] = jnp.zeros_like(y_ref)


def _experts_call(xs, e_lo, e_hi, n_used, w_gate, w_up, w_down, layer):
    d = D_MODEL
    tiles = xs.shape[0] // (MOE_TILE * X_SLAB)

    def xmap(t, elo, ehi, used):
        return (jnp.maximum(jnp.minimum(t, used[0] - 1), 0), 0)

    def lo(t, elo, ehi, used):
        return (layer, elo[t], 0, 0)

    def hi(t, elo, ehi, used):
        return (layer, ehi[t], 0, 0)

    up_spec = lambda m: pl.BlockSpec((None, None, d, D_EXPERT), m)
    down_spec = lambda m: pl.BlockSpec((None, None, D_EXPERT, d), m)
    cached = [pltpu.VMEM((d, D_EXPERT), BF16), pltpu.VMEM((d, D_EXPERT), BF16), pltpu.VMEM((D_EXPERT, d), BF16)]
    grid_spec = pltpu.PrefetchScalarGridSpec(
        num_scalar_prefetch=3,
        grid=(tiles,),
        in_specs=[pl.BlockSpec((MOE_TILE * X_SLAB, LANES), xmap),
                  up_spec(lo), up_spec(lo), down_spec(lo),
                  up_spec(hi), up_spec(hi), down_spec(hi)],
        out_specs=pl.BlockSpec((MOE_TILE * Y_SLAB, LANES), lambda t, elo, ehi, used: (t, 0)),
        scratch_shapes=cached + cached,
    )
    return pl.pallas_call(
        _experts_kernel,
        grid_spec=grid_spec,
        out_shape=jax.ShapeDtypeStruct((tiles * MOE_TILE * Y_SLAB, LANES), F32),
        compiler_params=_params("arbitrary"),
        name="experts",
    )(e_lo, e_hi, n_used, xs, w_gate, w_up, w_down, w_gate, w_up, w_down)


def _collect_kernel(slot_ref, next_slot_ref, ys_ref, rc_ref, *rest, steps, residual):
    if residual:
        h_ref, mod_ref, o_ref, buf_ref, sem = rest
    else:
        o_ref, buf_ref, sem = rest
    i = pl.program_id(0)
    k = i % 2
    rt = MOE_ROWS

    def slab_copy(buf, g, u, slot):
        return pltpu.make_async_copy(
            ys_ref.at[pl.ds(pl.multiple_of(slot * Y_SLAB, Y_SLAB), Y_SLAB)],
            buf_ref.at[buf, g, :, pl.ds(u, 1), :], sem.at[buf])

    def start_all(slots, buf):
        _for_rows(rt, lambda g, u: slab_copy(buf, g, u, slots[0, 0, g * SUBLANES + u]).start(priority=u % 2))

    @pl.when(i == 0)
    def _():
        start_all(slot_ref, 0)

    @pl.when(i + 1 < steps)
    def _():
        start_all(next_slot_ref, 1 - k)

    pltpu.make_async_copy(buf_ref.at[k], buf_ref.at[k], sem.at[k]).wait()

    rc = rc_ref[...]
    w_lo = rc[:, 2:3]
    w_hi = rc[:, 3:4]
    for j in range(X_SLAB):
        cols = slice(j * LANES, (j + 1) * LANES)
        f = (w_lo * buf_ref[k, :, j, :, :].reshape(rt, LANES)
             + w_hi * buf_ref[k, :, X_SLAB + j, :, :].reshape(rt, LANES))
        if residual:
            o_ref[:, cols] = h_ref[:, cols] + mod_ref[5:6, cols] * f
        else:
            o_ref[:, cols] = f


def _collect_call(ys, slots, route_cols, residual=None):
    n = slots.shape[0]
    d = D_MODEL
    steps = n // MOE_ROWS
    slots3 = slots.reshape(steps, 1, MOE_ROWS)
    row = pl.BlockSpec((MOE_ROWS, d), lambda i: (i, 0))
    in_specs = [pl.BlockSpec((1, 1, MOE_ROWS), lambda i: (i, 0, 0), memory_space=pltpu.SMEM),
                pl.BlockSpec((1, 1, MOE_ROWS), lambda i: (jnp.minimum(i + 1, steps - 1), 0, 0),
                             memory_space=pltpu.SMEM),
                pl.BlockSpec(memory_space=pl.ANY),
                pl.BlockSpec((MOE_ROWS, LANES), lambda i: (i, 0))]
    args = [slots3, slots3, ys.reshape(ys.shape[0], 1, LANES), route_cols]
    if residual is not None:
        h, mod, tiles_per_seq, n_batch = residual
        in_specs += [row, _mod_spec(tiles_per_seq, n_batch)]
        args += [h, mod]
    return pl.pallas_call(
        functools.partial(_collect_kernel, steps=steps, residual=residual is not None),
        grid=(steps,),
        in_specs=in_specs,
        out_specs=row,
        out_shape=jax.ShapeDtypeStruct((n, d), F32),
        scratch_shapes=[pltpu.VMEM((2, MOE_ROWS // SUBLANES, Y_SLAB, SUBLANES, LANES), F32),
                        pltpu.SemaphoreType.DMA((2,))],
        compiler_params=_params("arbitrary"),
        name="moe_collect",
    )(*args)


def _moe(rows, route, route_cols, w_gate, w_up, w_down, layer, residual=None):
    n = rows.shape[0]
    tm = MOE_TILE
    assert n % MOE_ROWS == 0
    max_tiles = -(-n // tm) + N_CLASSES
    p = max_tiles * tm
    lo = route[0].astype(jnp.int32)
    hi = route[1].astype(jnp.int32)
    lo_in = lo % EXPERTS_PER_GROUP
    hi_in = hi % EXPERTS_PER_GROUP
    pair = lo_in * 3 - (lo_in * (lo_in - 1)) // 2 + (hi_in - lo_in - 1)
    cls = (lo // EXPERTS_PER_GROUP) * PAIRS_PER_GROUP + pair
    onehot = (cls[:, None] == jnp.arange(N_CLASSES, dtype=jnp.int32)[None, :]).astype(jnp.int32)
    counts = jnp.sum(onehot, axis=0)
    rank = jnp.sum(jnp.cumsum(onehot, axis=0) * onehot, axis=1) - 1
    tiles_c = (counts + tm - 1) // tm
    tile_end = jnp.cumsum(tiles_c)
    tile_start = tile_end - tiles_c
    n_used = tile_end[-1]
    slot_of_token = (tile_start[cls] * tm + rank).astype(jnp.int32)
    tile_ids = jnp.arange(max_tiles, dtype=jnp.int32)
    tile_cls = jnp.minimum(jnp.sum((tile_end[None, :] <= tile_ids[:, None]).astype(jnp.int32), axis=1),
                           N_CLASSES - 1)
    group0 = (tile_cls // PAIRS_PER_GROUP) * EXPERTS_PER_GROUP
    e_lo = group0 + jnp.asarray(PAIR_LO, jnp.int32)[tile_cls % PAIRS_PER_GROUP]
    e_hi = group0 + jnp.asarray(PAIR_HI, jnp.int32)[tile_cls % PAIRS_PER_GROUP]
    tails = jnp.stack([tiles_c, tile_end]).astype(jnp.int32)
    xs = _dispatch_call(rows, slot_of_token, tails, p)
    ys = _experts_call(xs, e_lo.astype(jnp.int32), e_hi.astype(jnp.int32),
                       n_used.reshape(1).astype(jnp.int32), w_gate, w_up, w_down, layer)
    return _collect_call(ys, slot_of_token, route_cols, residual)


def _rope_tables(seq):
    rows = seq // GRID_W
    row = jnp.repeat(jnp.arange(rows), GRID_W).astype(F32)
    col = jnp.tile(jnp.arange(GRID_W), rows).astype(F32)
    n_freq = HEAD_DIM // 4
    inv_freq = ROPE_BASE ** (-jnp.arange(n_freq, dtype=F32) / n_freq)
    ang_r = row[:, None] * inv_freq
    ang_c = col[:, None] * inv_freq
    cos = jnp.concatenate([jnp.cos(ang_r), jnp.cos(ang_r), jnp.cos(ang_c), jnp.cos(ang_c)], axis=1)
    sin = jnp.concatenate([-jnp.sin(ang_r), jnp.sin(ang_r), -jnp.sin(ang_c), jnp.sin(ang_c)], axis=1)
    cos = jnp.concatenate([jnp.tile(cos, (1, 2)), jnp.ones((ROW_TILE, LANES), F32)], axis=0)
    sin = jnp.concatenate([jnp.tile(sin, (1, 2)), jnp.zeros((ROW_TILE, LANES), F32)], axis=0)
    return cos, sin


def _block_diag_halves(w):
    per_half = LRU_HALF // LRU_BLOCK
    halves = LRU_WIDTH // LRU_HALF
    blocks = w.reshape(2, halves, per_half, LRU_BLOCK, LRU_BLOCK).transpose(1, 0, 2, 3, 4)
    eye = jnp.eye(per_half, dtype=w.dtype)
    out = blocks[:, :, :, :, None, :] * eye[None, None, :, None, :, None]
    return out.reshape(halves, 2, LRU_HALF, LRU_HALF).astype(BF16)


def _gla_gate_weights(gate_up, gate_b):
    w = jnp.zeros((GLA_HEADS, 2, LANES, GLA_DK), F32)
    for d in range(2):
        wd = gate_up[d].reshape(GLA_LOWRANK, GLA_HEADS, GLA_DK).transpose(1, 0, 2)
        w = w.at[:, d, d * GLA_LOWRANK:(d + 1) * GLA_LOWRANK, :].set(wd)
    b = gate_b.reshape(2, GLA_HEADS, GLA_DK).transpose(1, 0, 2)
    w_hi = w.astype(BF16)
    w_lo = (w - w_hi.astype(F32)).astype(BF16)
    return jnp.concatenate([w_hi, w_lo], axis=-1), b


def kernel(x, c, ctx, c_ctx, router_w, router_bias, ada_w, ada_b, norm_mix_g, norm_ffn_g, moe_w_gate, moe_w_up, moe_w_down, ev_w_in, ev_w_out, ev_conv_w, ev_conv_b, ev_rg_w, ev_rg_b, ev_ig_w, ev_ig_b, ev_lambda, ev_q_norm_g, ev_k_norm_g, ev_sink, od_w_in, od_w_out, od_gate_up, od_gate_b, od_out_norm_g):
    n_batch, seq, d = x.shape
    ctx_len = ctx.shape[1]
    n_lat = n_batch * seq
    n_ctx = n_batch * ctx_len
    assert ada_w.shape[0] == 2 and d == D_MODEL
    assert seq % ROW_TILE == 0 and n_ctx % ROW_TILE == 0 and seq % ctx_len == 0
    assert ctx_len % LRU_CHUNK == 0 and seq % GLA_BLOCK == 0 and ctx_len % GLA_BLOCK == 0
    tiles_per_seq = seq // ROW_TILE
    n_lat_tiles = n_lat // ROW_TILE

    x_lat = x.reshape(n_lat, d)
    x_ctx = ctx.reshape(n_ctx, d)
    mod_rows = -(-(n_batch + 1) // SUBLANES) * SUBLANES
    cvec = jnp.zeros((mod_rows, d), F32).at[:n_batch].set(c).at[n_batch].set(c_ctx)
    mods = _ada_call(cvec, ada_w, ada_b).reshape(2, mod_rows, 6, d)
    mod_mix = [mods[l][:n_batch + 1, 0:6] for l in range(2)]
    rw_pad = jnp.pad(router_w, ((0, 0), (0, LANES - N_EXPERTS)))
    rw_hi = rw_pad.astype(BF16)
    rw_t = jnp.concatenate([rw_hi, (rw_pad - rw_hi.astype(F32)).astype(BF16)], axis=1)
    rb = router_bias.reshape(N_EXPERTS, 1)

    w_in0 = ev_w_in[0].astype(BF16)
    w_out0 = ev_w_out[0].astype(BF16)
    cos_t, sin_t = _rope_tables(seq)
    seg = np.kron(np.eye(LANES // HEAD_DIM, dtype=np.float32), np.full((HEAD_DIM, HEAD_DIM), 1.0 / HEAD_DIM, np.float32))
    seg = jnp.asarray(np.concatenate([seg, seg], axis=0)).astype(BF16)
    proj0, qrot, kvrot = _proj_even_call(x_lat, x_ctx, mod_mix[0], norm_mix_g[0:1], w_in0, cos_t, sin_t,
                                         jnp.tile(ev_q_norm_g[0], 2)[None, :],
                                         jnp.tile(ev_k_norm_g[0], 2)[None, :], seg, tiles_per_seq, n_batch)
    a_lat, a_ctx = _rglru_call(proj0, ev_conv_w[0], ev_conv_b[0:1], _block_diag_halves(ev_rg_w[0]),
                               _block_diag_halves(ev_ig_w[0]), ev_rg_b[0], ev_ig_b[0], ev_lambda[0],
                               n_batch, seq, ctx_len)
    b_lat = _attn_call(qrot, kvrot, ev_sink[0], n_batch, seq, ctx_len)
    b_ctx = _ctx_attn_call(qrot, kvrot, ev_sink[0], n_batch, seq, ctx_len)
    h, rows, route, route_cols = _outproj_even_call(a_lat, a_ctx, b_lat, b_ctx, w_out0, x_lat, x_ctx, mod_mix[0],
                                                    norm_ffn_g[0:1], rw_t, rb, tiles_per_seq, n_batch)
    ffn = _moe(rows, route, route_cols, moe_w_gate, moe_w_up, moe_w_down, 0)

    w_in1 = jnp.pad(od_w_in[0], ((0, 0), (0, ODD_IN_PAD - ODD_IN))).astype(BF16)
    h, proj1 = _proj_res_call(h, ffn, mod_mix[0], mod_mix[1], norm_mix_g[1:2], w_in1, tiles_per_seq, n_batch)
    gate_w, gate_b = _gla_gate_weights(od_gate_up[0], od_gate_b[0])
    o = _gla_call(proj1, gate_w, gate_b, od_out_norm_g[0:1], n_batch, seq, ctx_len)
    h_lat, rows, route, route_cols = _outproj_odd_call(o, od_w_out[0].astype(BF16), h, mod_mix[1],
                                                       norm_ffn_g[1:2], rw_t, rb, tiles_per_seq, n_batch)
    out = _moe(rows, route, route_cols, moe_w_gate, moe_w_up, moe_w_down, 1,
               residual=(h_lat, mod_mix[1], seq // MOE_ROWS, n_batch))
    return out.reshape(n_batch, seq, d)
```

```python
import functools

import jax
import jax.numpy as jnp
import numpy as np
from jax import lax
from jax.experimental import pallas as pl
from jax.experimental.pallas import tpu as pltpu

F32 = jnp.float32
BF16 = jnp.bfloat16
HIGHEST = lax.Precision.HIGHEST
ACT_DTYPE = BF16

D_MODEL = 1024
GRID_W = 64
EPS = 1e-6
LRU_WIDTH = 512
LRU_BLOCK = 64
CONV_W = 4
LRU_C = 8.0
ATT_HEADS = 8
ATT_KV_HEADS = 2
HEAD_DIM = 64
WINDOW = 128
ATT_BLOCK = 128
ROPE_BASE = 10000.0
LOG2E = 1.4426950408889634
ATT_Q_WIDTH = ATT_HEADS * HEAD_DIM
ATT_KV_WIDTH = ATT_KV_HEADS * HEAD_DIM
EVEN_IN = 2 * LRU_WIDTH + ATT_Q_WIDTH + 2 * ATT_KV_WIDTH
GLA_HEADS = 4
GLA_DK = 128
GLA_DV = 256
GLA_LOWRANK = 16
GLA_TAU = 16.0
GLA_CHUNK = 64
GLA_BLOCK = 256
GLA_K_WIDTH = GLA_HEADS * GLA_DK
GLA_V_WIDTH = GLA_HEADS * GLA_DV
ODD_IN = 2 * GLA_K_WIDTH + 2 * GLA_V_WIDTH + 2 * GLA_LOWRANK
N_EXPERTS = 16
N_GROUPS = 4
EXPERTS_PER_GROUP = 4
D_EXPERT = 512
PAIRS_PER_GROUP = 6
N_CLASSES = N_GROUPS * PAIRS_PER_GROUP
PAIR_LO = (0, 0, 0, 1, 1, 2)
PAIR_HI = (1, 2, 3, 2, 3, 3)

LANES = 128
SUBLANES = 8
VMEM_LIMIT_BYTES = 56 * 1024 * 1024

ROW_TILE = 512
MOE_TILE = 256
MOE_ROWS = 1024
LRU_CHUNK = 128
LRU_HALF = 256
ODD_IN_PAD = ((ODD_IN + LANES - 1) // LANES) * LANES


def _params(*sem):
    return pltpu.CompilerParams(dimension_semantics=sem, vmem_limit_bytes=VMEM_LIMIT_BYTES)


def _modulated_norm(x, g, shift, scale):
    y = x * lax.rsqrt(jnp.mean(x * x, axis=-1, keepdims=True) + EPS)
    return (y * g) * (1.0 + scale) + shift


def _ada_kernel(c_ref, w_ref, b_ref, o_ref):
    c = c_ref[...]
    s = c * jax.nn.sigmoid(c)
    o_ref[...] = jnp.dot(s, w_ref[...], preferred_element_type=F32) + b_ref[...]


def _ada_call(cvec, ada_w, ada_b):
    depth, d, n6 = ada_w.shape
    r = cvec.shape[0]
    tn = n6 // 4
    return pl.pallas_call(
        _ada_kernel,
        grid=(depth, n6 // tn),
        in_specs=[pl.BlockSpec((r, d), lambda l, j: (0, 0)),
                  pl.BlockSpec((None, d, tn), lambda l, j: (l, 0, j)),
                  pl.BlockSpec((None, 1, tn), lambda l, j: (l, 0, j))],
        out_specs=pl.BlockSpec((None, r, tn), lambda l, j: (l, 0, j)),
        out_shape=jax.ShapeDtypeStruct((depth, r, n6), F32),
        compiler_params=_params("arbitrary", "arbitrary"),
        name="adaln",
    )(cvec, ada_w, ada_b.reshape(depth, 1, n6))


def _two_source_specs(width, n_lat_tiles):
    lat = pl.BlockSpec((ROW_TILE, width), lambda i: (jnp.minimum(i, n_lat_tiles - 1), 0))
    ctx = pl.BlockSpec((ROW_TILE, width), lambda i: (jnp.maximum(i - n_lat_tiles, 0), 0))
    return lat, ctx


def _two_source(lat_ref, ctx_ref, n_lat_tiles):
    return jnp.where(pl.program_id(0) >= n_lat_tiles, ctx_ref[...], lat_ref[...])


def _proj_even_kernel(xl_ref, xc_ref, mod_ref, g_ref, w_ref, c_ref, s_ref, qg_ref, kg_ref, m_ref,
                      o_ref, qo_ref, kvo_ref, *, n_lat_tiles):
    x = _two_source(xl_ref, xc_ref, n_lat_tiles)
    n = _modulated_norm(x, g_ref[...], mod_ref[0:1, :], mod_ref[1:2, :])
    proj = jnp.dot(n.astype(BF16), w_ref[...], preferred_element_type=F32)
    lru = 2 * LRU_WIDTH
    o_ref[...] = proj[:, 0:lru].astype(o_ref.dtype)
    k0 = lru + ATT_Q_WIDTH
    _qk_prepare(proj[:, lru:k0], proj[:, k0:k0 + ATT_KV_WIDTH], proj[:, k0 + ATT_KV_WIDTH:k0 + 2 * ATT_KV_WIDTH],
                c_ref[...], s_ref[...], qg_ref[...], kg_ref[...], m_ref[...], qo_ref, kvo_ref)


def _proj_res_kernel(h_ref, f_ref, pmod_ref, mod_ref, g_ref, w_ref, x_ref, o_ref):
    x = h_ref[...] + pmod_ref[5:6, :] * f_ref[...]
    x_ref[...] = x
    n = _modulated_norm(x, g_ref[...], mod_ref[0:1, :], mod_ref[1:2, :])
    o_ref[...] = jnp.dot(n.astype(BF16), w_ref[...], preferred_element_type=F32).astype(o_ref.dtype)


def _mod_spec(tiles_per_seq, n_batch):
    return pl.BlockSpec((None, 6, D_MODEL), lambda i: (jnp.minimum(i // tiles_per_seq, n_batch), 0, 0))


def _proj_even_call(x_lat, x_ctx, mod, g, w, cos_t, sin_t, qg, kg, seg_mean, tiles_per_seq, n_batch):
    d = x_lat.shape[1]
    n = x_lat.shape[0] + x_ctx.shape[0]
    n_out = w.shape[1]
    n_lat_tiles = x_lat.shape[0] // ROW_TILE
    lat, ctx = _two_source_specs(d, n_lat_tiles)

    def pos(i):
        return (jnp.where(i < n_lat_tiles, i % tiles_per_seq, tiles_per_seq), 0)

    const = lambda shape: pl.BlockSpec(shape, lambda i: (0, 0))
    row = lambda width: pl.BlockSpec((ROW_TILE, width), lambda i: (i, 0))
    return pl.pallas_call(
        functools.partial(_proj_even_kernel, n_lat_tiles=n_lat_tiles),
        grid=(n // ROW_TILE,),
        in_specs=[lat, ctx,
                  _mod_spec(tiles_per_seq, n_batch),
                  const((1, d)), const((d, n_out)),
                  pl.BlockSpec((ROW_TILE, LANES), pos), pl.BlockSpec((ROW_TILE, LANES), pos),
                  const((1, LANES)), const((1, LANES)), const((2 * LANES, LANES))],
        out_specs=[row(2 * LRU_WIDTH), row(ATT_Q_WIDTH), row(4 * LANES)],
        out_shape=[jax.ShapeDtypeStruct((n, 2 * LRU_WIDTH), ACT_DTYPE),
                   jax.ShapeDtypeStruct((n, ATT_Q_WIDTH), BF16),
                   jax.ShapeDtypeStruct((n, 4 * LANES), BF16)],
        compiler_params=_params("arbitrary"),
        name="in_proj",
    )(x_lat, x_ctx, mod, g, w, cos_t, sin_t, qg, kg, seg_mean)


def _proj_res_call(h, ffn, pmod, mod, g, w, tiles_per_seq, n_batch):
    n, d = h.shape
    n_out = w.shape[1]
    row = pl.BlockSpec((ROW_TILE, d), lambda i: (i, 0))
    return pl.pallas_call(
        _proj_res_kernel,
        grid=(n // ROW_TILE,),
        in_specs=[row, row, _mod_spec(tiles_per_seq, n_batch), _mod_spec(tiles_per_seq, n_batch),
                  pl.BlockSpec((1, d), lambda i: (0, 0)),
                  pl.BlockSpec((d, n_out), lambda i: (0, 0))],
        out_specs=[row, pl.BlockSpec((ROW_TILE, n_out), lambda i: (i, 0))],
        out_shape=[jax.ShapeDtypeStruct((n, d), F32), jax.ShapeDtypeStruct((n, n_out), ACT_DTYPE)],
        compiler_params=_params("arbitrary"),
        name="in_proj_res",
    )(h, ffn, pmod, mod, g, w)


def _qk_prepare(q, k, v, cos, sin, qg, kg, seg_mean, qo_ref, kvo_ref):
    lane = lax.broadcasted_iota(jnp.int32, cos.shape, 1)
    first = (lane % 32) < 16
    low = lane < HEAD_DIM

    def norm_rope(x, g):
        ms = jnp.dot(jnp.concatenate(_split2(x * x), axis=1), seg_mean, preferred_element_type=F32)
        xn = (x * lax.rsqrt(ms + EPS)) * g
        partner = jnp.where(first, pltpu.roll(xn, LANES - 16, 1), pltpu.roll(xn, 16, 1))
        return xn * cos + partner * sin

    for j in range(ATT_Q_WIDTH // LANES):
        qj = norm_rope(q[:, j * LANES:(j + 1) * LANES], qg)
        qo_ref[:, j * LANES:(j + 1) * LANES] = (qj * (HEAD_DIM ** -0.5 * LOG2E)).astype(BF16)
    kr = norm_rope(k, kg)
    ks = pltpu.roll(kr, HEAD_DIM, 1)
    kvo_ref[:, 0:LANES] = jnp.where(low, kr, ks).astype(BF16)
    kvo_ref[:, LANES:2 * LANES] = jnp.where(low, ks, kr).astype(BF16)
    kvo_ref[:, 2 * LANES:3 * LANES] = v.astype(BF16)
    kvo_ref[:, 3 * LANES:4 * LANES] = jnp.ones((cos.shape[0], LANES), BF16)


def _attend(q, kv, sink_ref, bias):
    rows = q.shape[0]
    lane = lax.broadcasted_iota(jnp.int32, (rows, LANES), 1)
    low = lane < HEAD_DIM
    zero = jnp.zeros((rows, LANES), BF16)
    vv = kv[:, 2 * LANES:4 * LANES]
    groups = []
    heads_per_kv = ATT_HEADS // ATT_KV_HEADS
    for kh in range(ATT_KV_HEADS):
        kk = kv[:, kh * LANES:(kh + 1) * LANES]
        parts = []
        sinks = []
        for j in range(heads_per_kv):
            h = kh * heads_per_kv + j
            qg = q[:, (h // 2) * LANES:(h // 2 + 1) * LANES]
            parts.append(jnp.where(low, qg, zero) if h % 2 == 0 else jnp.where(low, zero, qg))
            sinks.append(jnp.full((rows, 1), sink_ref[h] * LOG2E, F32))
        lhs = jnp.concatenate(parts, axis=0)
        sink = jnp.concatenate(sinks, axis=0)
        s = lax.dot_general(lhs, kk, (((1,), (1,)), ((), ())), preferred_element_type=F32)
        if bias is not None:
            s = s + bias
        m = jnp.maximum(jnp.max(s, axis=-1, keepdims=True), sink)
        e = jnp.exp2(s - m)
        ov = jnp.dot(e.astype(BF16), vv, preferred_element_type=F32)
        o = ov[:, 0:LANES] / (ov[:, LANES:2 * LANES] + jnp.exp2(sink - m))
        r = [o[j * rows:(j + 1) * rows] for j in range(heads_per_kv)]
        for p in range(heads_per_kv // 2):
            even, odd = r[2 * p], r[2 * p + 1]
            if kh == 0:
                groups.append(jnp.where(low, even, pltpu.roll(odd, HEAD_DIM, 1)))
            else:
                groups.append(jnp.where(low, pltpu.roll(even, HEAD_DIM, 1), odd))
    return jnp.concatenate(groups, axis=1)


def _attn_kernel(sink_ref, q_ref, kvp_ref, kvo_ref, kvn_ref, kvc_ref, bias0_ref, bias1_ref, o_ref):
    blk = ATT_BLOCK
    own = kvo_ref[...]
    ctx = kvc_ref[...]
    kv0 = jnp.concatenate([kvp_ref[...], own, ctx], axis=0)
    kv1 = jnp.concatenate([own, kvn_ref[...], ctx], axis=0)
    o_ref[0:blk, :] = _attend(q_ref[0:blk, :], kv0, sink_ref, bias0_ref[...]).astype(o_ref.dtype)
    o_ref[blk:2 * blk, :] = _attend(q_ref[blk:2 * blk, :], kv1, sink_ref, bias1_ref[...]).astype(o_ref.dtype)


def _window_bias(ctx_len):
    blk = ATT_BLOCK
    stacked = (ATT_HEADS // ATT_KV_HEADS) * blk
    qi = lax.broadcasted_iota(jnp.int32, (4, stacked, 3 * blk + ctx_len), 1) % blk
    ks = lax.broadcasted_iota(jnp.int32, (4, stacked, 3 * blk + ctx_len), 2)
    var = lax.broadcasted_iota(jnp.int32, (4, stacked, 3 * blk + ctx_len), 0)
    in_win = jnp.abs(ks - blk - qi) <= WINDOW
    in_seq = ((ks >= blk) | (var % 2 == 1)) & ((ks < 2 * blk) | (var >= 2))
    keep = (in_win & in_seq) | (ks >= 3 * blk)
    return jnp.where(keep, 0.0, -1e30).astype(F32)


def _ctx_attn_kernel(sink_ref, q_ref, kvc_ref, o_ref):
    o_ref[...] = _attend(q_ref[...], kvc_ref[...], sink_ref, None).astype(o_ref.dtype)


def _attn_call(qrot, kvrot, sink, n_batch, seq, ctx_len):
    nb = seq // ATT_BLOCK
    ctx0 = (n_batch * seq) // ctx_len
    kvw = kvrot.shape[1]
    bias = _window_bias(ctx_len)

    assert nb % 2 == 0
    pairs = nb // 2

    def variant0(b, m):
        return ((m > 0).astype(jnp.int32) + 2, 0, 0)

    def variant1(b, m):
        return (1 + 2 * (m < pairs - 1).astype(jnp.int32), 0, 0)

    blk = ATT_BLOCK
    return pl.pallas_call(
        _attn_kernel,
        grid=(n_batch, pairs),
        in_specs=[pl.BlockSpec(memory_space=pltpu.SMEM),
                  pl.BlockSpec((2 * blk, ATT_Q_WIDTH), lambda b, m: (b * pairs + m, 0)),
                  pl.BlockSpec((blk, kvw), lambda b, m: (b * nb + jnp.maximum(2 * m - 1, 0), 0)),
                  pl.BlockSpec((2 * blk, kvw), lambda b, m: (b * pairs + m, 0)),
                  pl.BlockSpec((blk, kvw), lambda b, m: (b * nb + jnp.minimum(2 * m + 2, nb - 1), 0)),
                  pl.BlockSpec((ctx_len, kvw), lambda b, m: (ctx0 + b, 0)),
                  pl.BlockSpec((None,) + bias.shape[1:], variant0),
                  pl.BlockSpec((None,) + bias.shape[1:], variant1)],
        out_specs=pl.BlockSpec((2 * blk, ATT_Q_WIDTH), lambda b, m: (b * pairs + m, 0)),
        out_shape=jax.ShapeDtypeStruct((n_batch * seq, ATT_Q_WIDTH), ACT_DTYPE),
        compiler_params=_params("arbitrary", "arbitrary"),
        name="window_attn",
    )(sink, qrot, kvrot, kvrot, kvrot, kvrot, bias, bias)


def _ctx_attn_call(qrot, kvrot, sink, n_batch, seq, ctx_len):
    ctx0 = (n_batch * seq) // ctx_len
    kvw = kvrot.shape[1]
    return pl.pallas_call(
        _ctx_attn_kernel,
        grid=(n_batch,),
        in_specs=[pl.BlockSpec(memory_space=pltpu.SMEM),
                  pl.BlockSpec((ctx_len, ATT_Q_WIDTH), lambda b: (ctx0 + b, 0)),
                  pl.BlockSpec((ctx_len, kvw), lambda b: (ctx0 + b, 0))],
        out_specs=pl.BlockSpec((ctx_len, ATT_Q_WIDTH), lambda b: (b, 0)),
        out_shape=jax.ShapeDtypeStruct((n_batch * ctx_len, ATT_Q_WIDTH), ACT_DTYPE),
        compiler_params=_params("arbitrary"),
        name="ctx_attn",
    )(sink, qrot, kvrot)


def _scan_rows(a, x, carry, reverse):
    rows, width = a.shape
    n_groups = rows // SUBLANES
    a = a.reshape(n_groups, SUBLANES, width)
    x = x.reshape(n_groups, SUBLANES, width)
    sub = lax.broadcasted_iota(jnp.int32, a.shape, 1)
    for s in (1, 2, 4):
        shift = SUBLANES - s if reverse else s
        a_sh = pltpu.roll(a, shift, 1)
        x_sh = pltpu.roll(x, shift, 1)
        keep = (sub <= SUBLANES - 1 - s) if reverse else (sub >= s)
        x = x + a * jnp.where(keep, x_sh, 0.0)
        a = a * jnp.where(keep, a_sh, 1.0)
    a = a.reshape(rows, width)
    x = x.reshape(rows, width)
    out = [None] * n_groups
    order = range(n_groups - 1, -1, -1) if reverse else range(n_groups)
    for g in order:
        hg = x[g * SUBLANES:(g + 1) * SUBLANES] + a[g * SUBLANES:(g + 1) * SUBLANES] * carry
        carry = hg[0:1] if reverse else hg[SUBLANES - 1:SUBLANES]
        out[g] = hg
    return jnp.concatenate(out, axis=0), carry


def _rglru_kernel(ul_ref, gl_ref, uc_ref, gc_ref, cw_ref, cb_ref, wr_ref, wi_ref, br_ref, bi_ref, lam_ref,
                  yl_ref, yc_ref, upad_ref, conv_ref, *, seq, ctx_len):
    width = ul_ref.shape[1]
    tc = LRU_CHUNK
    neg_lam = -lam_ref[...]
    softplus = jnp.maximum(neg_lam, 0.0) + jnp.log1p(jnp.exp(-jnp.abs(neg_lam)))
    rate = (-LRU_C * LOG2E) * softplus
    zeros = jnp.zeros((SUBLANES, width), F32)

    def gates(u, d):
        ub = u.astype(BF16)
        r = jax.nn.sigmoid(jnp.dot(ub, wr_ref[d], preferred_element_type=F32) + br_ref[d:d + 1, :])
        i = jax.nn.sigmoid(jnp.dot(ub, wi_ref[d], preferred_element_type=F32) + bi_ref[d:d + 1, :])
        a = jnp.exp2(r * rate[d:d + 1, :])
        y = 1.0 - a * a
        root = jnp.where(y > 0.0, y * lax.rsqrt(y), 0.0)
        return a, root * (i * u)

    def run(u_ref, g_ref, y_ref, n, h_fwd, h_bwd):
        n_chunks = n // tc
        upad_ref[0:SUBLANES, :] = zeros
        upad_ref[pl.ds(SUBLANES, n), :] = u_ref[...].astype(F32)
        upad_ref[pl.ds(SUBLANES + n, SUBLANES), :] = zeros

        def conv_body(c, carry):
            t0 = pl.multiple_of(c * tc, tc)
            win = upad_ref[pl.ds(t0, tc + 2 * SUBLANES), :]
            acc = cb_ref[...]
            for k in range(CONV_W):
                acc = acc + cw_ref[k:k + 1, :] * win[SUBLANES - 1 + k:SUBLANES - 1 + k + tc]
            conv_ref[pl.ds(t0, tc), :] = acc
            return carry

        lax.fori_loop(0, n_chunks, conv_body, 0)

        def fwd_body(c, carry):
            t0 = pl.multiple_of(c * tc, tc)
            a, x = gates(conv_ref[pl.ds(t0, tc), :], 0)
            h, carry = _scan_rows(a, x, carry, False)
            upad_ref[pl.ds(pl.multiple_of(t0 + SUBLANES, SUBLANES), tc), :] = h
            return carry

        h_fwd = lax.fori_loop(0, n_chunks, fwd_body, h_fwd, unroll=min(4, n_chunks))

        def bwd_body(c, carry):
            t0 = pl.multiple_of((n_chunks - 1 - c) * tc, tc)
            a, x = gates(conv_ref[pl.ds(t0, tc), :], 1)
            h, carry = _scan_rows(a, x, carry, True)
            h_both = upad_ref[pl.ds(pl.multiple_of(t0 + SUBLANES, SUBLANES), tc), :] + h
            y_ref[pl.ds(t0, tc), :] = (h_both * jax.nn.gelu(g_ref[pl.ds(t0, tc), :].astype(F32))).astype(y_ref.dtype)
            return carry

        h_bwd = lax.fori_loop(0, n_chunks, bwd_body, h_bwd, unroll=min(4, n_chunks))
        return h_fwd, h_bwd

    h0 = jnp.zeros((1, width), F32)
    h_fwd, h_bwd = run(uc_ref, gc_ref, yc_ref, ctx_len, h0, h0)
    run(ul_ref, gl_ref, yl_ref, seq, h_fwd, h_bwd)


def _rglru_call(proj, conv_w, conv_b, wr, wi, br, bi, lam, n_batch, seq, ctx_len):
    ctx0 = (n_batch * seq) // ctx_len
    halves = LRU_WIDTH // LRU_HALF
    w = LRU_HALF
    vec2 = pl.BlockSpec((2, w), lambda b, c: (0, c))
    return pl.pallas_call(
        functools.partial(_rglru_kernel, seq=seq, ctx_len=ctx_len),
        grid=(n_batch, halves),
        in_specs=[pl.BlockSpec((seq, w), lambda b, c: (b, c)),
                  pl.BlockSpec((seq, w), lambda b, c: (b, halves + c)),
                  pl.BlockSpec((ctx_len, w), lambda b, c: (ctx0 + b, c)),
                  pl.BlockSpec((ctx_len, w), lambda b, c: (ctx0 + b, halves + c)),
                  pl.BlockSpec((CONV_W, w), lambda b, c: (0, c)),
                  pl.BlockSpec((1, w), lambda b, c: (0, c)),
                  pl.BlockSpec((None, 2, w, w), lambda b, c: (c, 0, 0, 0)),
                  pl.BlockSpec((None, 2, w, w), lambda b, c: (c, 0, 0, 0)),
                  vec2, vec2, vec2],
        out_specs=[pl.BlockSpec((seq, w), lambda b, c: (b, c)),
                   pl.BlockSpec((ctx_len, w), lambda b, c: (b, c))],
        out_shape=[jax.ShapeDtypeStruct((n_batch * seq, LRU_WIDTH), ACT_DTYPE),
                   jax.ShapeDtypeStruct((n_batch * ctx_len, LRU_WIDTH), ACT_DTYPE)],
        scratch_shapes=[pltpu.VMEM((seq + 2 * SUBLANES, w), F32), pltpu.VMEM((seq, w), F32)],
        compiler_params=_params("arbitrary", "arbitrary"),
        name="rglru",
    )(proj, proj, proj, proj, conv_w, conv_b, wr, wi, br, bi, lam)


def _split2(x):
    hi = x.astype(BF16)
    return hi, (x - hi.astype(F32)).astype(BF16)


def _gla_kernel(q_ref, k_ref, v_ref, r_ref, a_ref, kc_ref, vc_ref, ac_ref, gw_ref, gb_ref, ng_ref,
                o_ref, of_ref, ob_ref, qd_ref, ki_ref, ke_ref, la_ref, eg_ref, kv_ref, st_ref, *, seq, ctx_len):
    ch = GLA_CHUNK
    blk = GLA_BLOCK
    per_blk = blk // ch
    chunk_of_row = lax.broadcasted_iota(jnp.int32, (blk, GLA_DK), 0) // ch
    row = lax.broadcasted_iota(jnp.int32, (blk, blk), 0)
    col = lax.broadcasted_iota(jnp.int32, (blk, blk), 1)
    same = (row // ch) == (col // ch)
    keep = (same & (row >= col), same & (row <= col))
    tri = (keep[0].astype(BF16),)
    scale = GLA_DK ** -0.5
    nt = (((1,), (1,)), ((), ()))
    tn = (((0,), (0,)), ((), ()))

    def mm(a, b):
        return jnp.dot(a, b, preferred_element_type=F32)

    def prepare(a_src, k_src, v_src, q_src, dst, n_rows, d):
        n_blk = n_rows // blk

        def decay_body(i, carry):
            t0 = pl.multiple_of(i * blk, blk)
            both = mm(a_src[pl.ds(t0, blk), :], gw_ref[d])
            pre = (both[:, 0:GLA_DK] + both[:, GLA_DK:2 * GLA_DK]) + gb_ref[d:d + 1, :]
            la_ref[pl.ds(t0, blk), :] = (jnp.minimum(pre, 0.0) - jnp.log(1.0 + jnp.exp(-jnp.abs(pre)))) / GLA_TAU
            return carry

        lax.fori_loop(0, n_blk, decay_body, 0, unroll=min(8, n_blk))

        def scale_block(i):
            t0 = pl.multiple_of(i * blk, blk)
            log_a = la_ref[pl.ds(t0, blk), :]
            sums = mm(tri[0], jnp.concatenate(_split2(log_a), axis=1))
            prefix = sums[:, 0:GLA_DK] + sums[:, GLA_DK:2 * GLA_DK]
            total = jnp.concatenate(
                [jnp.broadcast_to(prefix[(j + 1) * ch - 1:(j + 1) * ch], (ch, GLA_DK)) for j in range(per_blk)],
                axis=0)
            suffix = total - prefix + log_a
            b = prefix if d == 0 else suffix
            to_end = (suffix if d == 0 else prefix) - log_a
            for j in range(per_blk):
                eg_ref[pl.ds(i * per_blk + j, 1), :] = jnp.exp(total[j * ch:j * ch + 1])
            k_blk = k_src[pl.ds(t0, blk), :].astype(F32)
            ke_ref[pl.ds(t0, blk), :] = (k_blk * jnp.exp(to_end)).astype(BF16)
            if q_src is not None:
                q_blk = q_src[pl.ds(t0, blk), :].astype(F32)
                qd_ref[pl.ds(t0, blk), :] = ((q_blk * scale) * jnp.exp(b)).astype(BF16)
                ki_ref[pl.ds(t0, blk), :] = (k_blk * jnp.exp(-b)).astype(BF16)

        def intra_block(i):
            t0 = pl.multiple_of(i * blk, blk)
            v_blk = v_src[pl.ds(t0, blk), :].astype(BF16)
            k_end = ke_ref[pl.ds(t0, blk), :]
            spread = jnp.concatenate([jnp.where(chunk_of_row == j, k_end, jnp.zeros_like(k_end))
                                      for j in range(per_blk)], axis=1)
            kv = lax.dot_general(v_blk, spread, tn, preferred_element_type=F32)
            for j in range(per_blk):
                kv_ref[i * per_blk + j] = kv[:, j * GLA_DK:(j + 1) * GLA_DK]
            if q_src is not None:
                s = lax.dot_general(qd_ref[pl.ds(t0, blk), :], ki_ref[pl.ds(t0, blk), :], nt,
                                    preferred_element_type=F32)
                p = jnp.where(keep[d], s, 0.0).astype(BF16)
                dst[pl.ds(t0, blk), :] = mm(p, v_blk)

        def run(block_fn, unroll):
            def body(i, carry):
                block_fn(i)
                return carry

            lax.fori_loop(0, n_blk, body, 0, unroll=min(unroll, n_blk))

        run(scale_block, 4)
        run(intra_block, 8)

    def finish(t0):
        o = of_ref[pl.ds(t0, blk), :] + ob_ref[pl.ds(t0, blk), :]
        on = (o * lax.rsqrt(jnp.mean(o * o, axis=-1, keepdims=True) + EPS)) * ng_ref[...]
        r = r_ref[pl.ds(t0, blk), :].astype(F32)
        o_ref[pl.ds(t0, blk), :] = (on * (r * jax.nn.sigmoid(r))).astype(o_ref.dtype)

    def recur(dst, n_rows, d):
        n_blk = n_rows // blk

        def body(i, carry):
            blk_i = i if d == 0 else n_blk - 1 - i
            for jj in range(per_blk):
                j = jj if d == 0 else per_blk - 1 - jj
                n = blk_i * per_blk + j
                t0 = pl.multiple_of(n * ch, ch)
                st = st_ref[...]
                if dst is not None:
                    dst[pl.ds(t0, ch), :] = dst[pl.ds(t0, ch), :] + lax.dot_general(
                        qd_ref[pl.ds(t0, ch), :], st.astype(BF16), nt, preferred_element_type=F32)
                st_ref[...] = st * eg_ref[pl.ds(n, 1), :] + kv_ref[n]
            if dst is not None and d == 1:
                finish(pl.multiple_of(blk_i * blk, blk))
            return carry

        lax.fori_loop(0, n_blk, body, 0, unroll=min(4, n_blk))

    for d, dst in ((0, of_ref), (1, ob_ref)):
        st_ref[...] = jnp.zeros(st_ref.shape, F32)
        prepare(ac_ref, kc_ref, vc_ref, None, None, ctx_len, d)
        recur(None, ctx_len, d)
        prepare(a_ref, k_ref, v_ref, q_ref, dst, seq, d)
        recur(dst, seq, d)


def _gla_call(proj, gate_w, gate_b, norm_g, n_batch, seq, ctx_len):
    ctx0 = (n_batch * seq) // ctx_len
    kcol = GLA_K_WIDTH // GLA_DK
    vcol = (2 * GLA_K_WIDTH) // GLA_DV
    rcol = (2 * GLA_K_WIDTH + GLA_V_WIDTH) // GLA_DV
    acol = (2 * GLA_K_WIDTH + 2 * GLA_V_WIDTH) // LANES
    return pl.pallas_call(
        functools.partial(_gla_kernel, seq=seq, ctx_len=ctx_len),
        grid=(n_batch, GLA_HEADS),
        in_specs=[pl.BlockSpec((seq, GLA_DK), lambda b, h: (b, h)),
                  pl.BlockSpec((seq, GLA_DK), lambda b, h: (b, kcol + h)),
                  pl.BlockSpec((seq, GLA_DV), lambda b, h: (b, vcol + h)),
                  pl.BlockSpec((seq, GLA_DV), lambda b, h: (b, rcol + h)),
                  pl.BlockSpec((seq, LANES), lambda b, h: (b, acol)),
                  pl.BlockSpec((ctx_len, GLA_DK), lambda b, h: (ctx0 + b, kcol + h)),
                  pl.BlockSpec((ctx_len, GLA_DV), lambda b, h: (ctx0 + b, vcol + h)),
                  pl.BlockSpec((ctx_len, LANES), lambda b, h: (ctx0 + b, acol)),
                  pl.BlockSpec((None, 2, LANES, 2 * GLA_DK), lambda b, h: (h, 0, 0, 0)),
                  pl.BlockSpec((None, 2, GLA_DK), lambda b, h: (h, 0, 0)),
                  pl.BlockSpec((1, GLA_DV), lambda b, h: (0, 0))],
        out_specs=pl.BlockSpec((seq, GLA_DV), lambda b, h: (b, h)),
        out_shape=jax.ShapeDtypeStruct((n_batch * seq, GLA_V_WIDTH), ACT_DTYPE),
        scratch_shapes=[pltpu.VMEM((seq, GLA_DV), F32),
                        pltpu.VMEM((seq, GLA_DV), F32),
                        pltpu.VMEM((seq, GLA_DK), BF16),
                        pltpu.VMEM((seq, GLA_DK), BF16),
                        pltpu.VMEM((seq, GLA_DK), BF16),
                        pltpu.VMEM((seq, GLA_DK), F32),
                        pltpu.VMEM((seq // GLA_CHUNK, GLA_DK), F32),
                        pltpu.VMEM((seq // GLA_CHUNK, GLA_DV, GLA_DK), F32),
                        pltpu.VMEM((GLA_DV, GLA_DK), F32)],
        compiler_params=_params("arbitrary", "arbitrary"),
        name="gla",
    )(proj, proj, proj, proj, proj, proj, proj, proj, gate_w, gate_b, norm_g)


def _route(rows, rows_bf16, rw_ref, rb_ref):
    n_rows = rows.shape[0]
    rows_lo = (rows - rows_bf16.astype(F32)).astype(BF16)
    both = jnp.dot(rows_bf16, rw_ref[...], preferred_element_type=F32)
    logits_rows = (both[:, 0:LANES] + both[:, LANES:2 * LANES]
                   + jnp.dot(rows_lo, rw_ref[:, 0:LANES], preferred_element_type=F32))
    logits = logits_rows.T[0:N_EXPERTS, :]
    scores = jax.nn.sigmoid(logits)
    sel = scores + rb_ref[...]
    eid = lax.broadcasted_iota(jnp.int32, (N_EXPERTS, n_rows), 0)
    gid = eid // EXPERTS_PER_GROUP
    neg = -jnp.inf
    big = N_EXPERTS

    def top2(masked):
        m1 = jnp.max(masked, axis=0, keepdims=True)
        i1 = jnp.min(jnp.where(masked == m1, eid, big), axis=0, keepdims=True)
        rest = jnp.where(eid == i1, neg, masked)
        m2 = jnp.max(rest, axis=0, keepdims=True)
        i2 = jnp.min(jnp.where(rest == m2, eid, big), axis=0, keepdims=True)
        return m1, i1, m2, i2

    best = None
    best_g = None
    for g in range(N_GROUPS):
        m1, _, m2, _ = top2(jnp.where(gid == g, sel, neg))
        gs = m1 + m2
        if best is None:
            best, best_g = gs, jnp.zeros((1, n_rows), jnp.int32)
        else:
            better = gs > best
            best_g = jnp.where(better, g, best_g)
            best = jnp.where(better, gs, best)
    _, i1, _, i2 = top2(jnp.where(gid == best_g, sel, neg))
    s1 = jnp.sum(jnp.where(eid == i1, scores, 0.0), axis=0, keepdims=True)
    s2 = jnp.sum(jnp.where(eid == i2, scores, 0.0), axis=0, keepdims=True)
    total = s1 + s2
    swap = i2 < i1
    lo = jnp.where(swap, i2, i1)
    hi = jnp.where(swap, i1, i2)
    w_lo = jnp.where(swap, s2, s1) / total
    w_hi = jnp.where(swap, s1, s2) / total
    pad = jnp.zeros((4, n_rows), F32)
    return jnp.concatenate([lo.astype(F32), hi.astype(F32), w_lo, w_hi, pad], axis=0)


def _finish_rows(y, h, mod_ref, g_ref, rw_ref, rb_ref, hn_ref, rows_ref, rt_ref, rc_ref):
    hn = h + mod_ref[2:3, :] * y
    hn_ref[...] = hn
    rows = _modulated_norm(hn, g_ref[...], mod_ref[3:4, :], mod_ref[4:5, :])
    rows_bf16 = rows.astype(BF16)
    rows_ref[...] = rows_bf16
    route = _route(rows, rows_bf16, rw_ref, rb_ref)
    rt_ref[...] = route
    padded = jnp.concatenate([route, jnp.zeros((LANES - SUBLANES, route.shape[1]), F32)], axis=0)
    rc_ref[...] = padded.T


def _outproj_even_kernel(al_ref, ac_ref, bl_ref, bc_ref, w_ref, hl_ref, hc_ref, mod_ref, g_ref, rw_ref, rb_ref,
                         hn_ref, rows_ref, rt_ref, rc_ref, *, n_lat_tiles):
    a = _two_source(al_ref, ac_ref, n_lat_tiles).astype(BF16)
    b = _two_source(bl_ref, bc_ref, n_lat_tiles).astype(BF16)
    y = (jnp.dot(a, w_ref[0:LRU_WIDTH, :], preferred_element_type=F32)
         + jnp.dot(b, w_ref[LRU_WIDTH:LRU_WIDTH + ATT_Q_WIDTH, :], preferred_element_type=F32))
    h = _two_source(hl_ref, hc_ref, n_lat_tiles)
    _finish_rows(y, h, mod_ref, g_ref, rw_ref, rb_ref, hn_ref, rows_ref, rt_ref, rc_ref)


def _outproj_odd_kernel(o_ref, w_ref, h_ref, mod_ref, g_ref, rw_ref, rb_ref, hn_ref, rows_ref, rt_ref, rc_ref):
    y = jnp.dot(o_ref[...].astype(BF16), w_ref[...], preferred_element_type=F32)
    _finish_rows(y, h_ref[...], mod_ref, g_ref, rw_ref, rb_ref, hn_ref, rows_ref, rt_ref, rc_ref)


def _outproj_specs(n_rows, tiles_per_seq, n_batch):
    d = D_MODEL
    row = pl.BlockSpec((ROW_TILE, d), lambda i: (i, 0))
    tail_in = [_mod_spec(tiles_per_seq, n_batch),
               pl.BlockSpec((1, d), lambda i: (0, 0)),
               pl.BlockSpec((d, 2 * LANES), lambda i: (0, 0)),
               pl.BlockSpec((N_EXPERTS, 1), lambda i: (0, 0))]
    out_specs = [row, row, pl.BlockSpec((SUBLANES, ROW_TILE), lambda i: (0, i)),
                 pl.BlockSpec((ROW_TILE, LANES), lambda i: (i, 0))]
    out_shape = [jax.ShapeDtypeStruct((n_rows, d), F32), jax.ShapeDtypeStruct((n_rows, d), ACT_DTYPE),
                 jax.ShapeDtypeStruct((SUBLANES, n_rows), F32), jax.ShapeDtypeStruct((n_rows, LANES), F32)]
    return tail_in, out_specs, out_shape


def _outproj_even_call(a_lat, a_ctx, b_lat, b_ctx, w, h_lat, h_ctx, mod, g, rw_t, rb, tiles_per_seq, n_batch):
    n_rows = h_lat.shape[0] + h_ctx.shape[0]
    n_lat_tiles = a_lat.shape[0] // ROW_TILE
    tail_in, out_specs, out_shape = _outproj_specs(n_rows, tiles_per_seq, n_batch)
    lat, ctx = _two_source_specs(LRU_WIDTH, n_lat_tiles)
    h_specs = list(_two_source_specs(D_MODEL, n_lat_tiles))
    return pl.pallas_call(
        functools.partial(_outproj_even_kernel, n_lat_tiles=n_lat_tiles),
        grid=(n_rows // ROW_TILE,),
        in_specs=[lat, ctx, lat, ctx, pl.BlockSpec(w.shape, lambda i: (0, 0))] + h_specs + tail_in,
        out_specs=out_specs,
        out_shape=out_shape,
        compiler_params=_params("arbitrary"),
        name="out_proj_even",
    )(a_lat, a_ctx, b_lat, b_ctx, w, h_lat, h_ctx, mod, g, rw_t, rb)


def _outproj_odd_call(o, w, h, mod, g, rw_t, rb, tiles_per_seq, n_batch):
    n_rows = o.shape[0]
    tail_in, out_specs, out_shape = _outproj_specs(n_rows, tiles_per_seq, n_batch)
    return pl.pallas_call(
        _outproj_odd_kernel,
        grid=(n_rows // ROW_TILE,),
        in_specs=[pl.BlockSpec((ROW_TILE, o.shape[1]), lambda i: (i, 0)),
                  pl.BlockSpec(w.shape, lambda i: (0, 0)),
                  pl.BlockSpec((ROW_TILE, D_MODEL), lambda i: (i, 0))] + tail_in,
        out_specs=out_specs,
        out_shape=out_shape,
        compiler_params=_params("arbitrary"),
        name="out_proj_odd",
    )(o, w, h, mod, g, rw_t, rb)


X_SLAB = D_MODEL // LANES
Y_SLAB = 2 * X_SLAB


def _for_rows(n_rows, fn):
    for g in range(n_rows // SUBLANES):
        for u in range(SUBLANES):
            fn(g, u)


def _dispatch_kernel(tail_ref, slot_ref, rows_ref, xs_ref, slab_ref, zero_ref, sem, zero_sem, *, steps, n_tiles):
    i = pl.program_id(0)
    k = i % 2
    rt = MOE_ROWS
    tile_rows = MOE_TILE * X_SLAB

    def slab_copy(buf, g, u, slot):
        return pltpu.make_async_copy(
            slab_ref.at[buf, g, :, pl.ds(u, 1), :],
            xs_ref.at[pl.ds(pl.multiple_of(slot * X_SLAB, X_SLAB), X_SLAB)], sem.at[buf])

    def wait_all(buf):
        pltpu.make_async_copy(slab_ref.at[buf], slab_ref.at[buf], sem.at[buf]).wait()

    @pl.when(i == 0)
    def _():
        zero_ref[...] = jnp.zeros(zero_ref.shape, F32)

        def fill(t):
            start = pl.multiple_of(t * tile_rows, tile_rows)
            return pltpu.make_async_copy(zero_ref, xs_ref.at[pl.ds(start, tile_rows)], zero_sem)

        n_used = tail_ref[1, N_CLASSES - 1]
        for phase in ("start", "wait"):
            for c in range(N_CLASSES):
                for cond, t in ((tail_ref[0, c] > 0, tail_ref[1, c] - 1), (n_used + c < n_tiles, n_used + c)):
                    @pl.when(cond)
                    def _():
                        getattr(fill(t), phase)()

    @pl.when(i >= 2)
    def _():
        wait_all(k)

    for j in range(X_SLAB):
        lanes = rows_ref[:, j * LANES:(j + 1) * LANES].astype(F32)
        slab_ref[k, :, j, :, :] = lanes.reshape(rt // SUBLANES, SUBLANES, LANES)

    _for_rows(rt, lambda g, u: slab_copy(k, g, u, slot_ref[0, 0, g * SUBLANES + u]).start(priority=u % 2))

    @pl.when(i == steps - 1)
    def _():
        wait_all(k)
        if steps >= 2:
            wait_all(1 - k)


def _dispatch_call(rows, slots, tails, n_slots):
    n, d = rows.shape
    steps = n // MOE_ROWS
    grid_spec = pltpu.PrefetchScalarGridSpec(
        num_scalar_prefetch=1,
        grid=(steps,),
        in_specs=[pl.BlockSpec((1, 1, MOE_ROWS), lambda i, tails: (i, 0, 0), memory_space=pltpu.SMEM),
                  pl.BlockSpec((MOE_ROWS, d), lambda i, tails: (i, 0))],
        out_specs=pl.BlockSpec(memory_space=pl.ANY),
        scratch_shapes=[pltpu.VMEM((2, MOE_ROWS // SUBLANES, X_SLAB, SUBLANES, LANES), F32),
                        pltpu.VMEM((MOE_TILE * X_SLAB, 1, LANES), F32),
                        pltpu.SemaphoreType.DMA((2,)), pltpu.SemaphoreType.DMA(())],
    )
    xs = pl.pallas_call(
        functools.partial(_dispatch_kernel, steps=steps, n_tiles=n_slots // MOE_TILE),
        grid_spec=grid_spec,
        out_shape=jax.ShapeDtypeStruct((n_slots * X_SLAB, 1, LANES), F32),
        compiler_params=_params("arbitrary"),
        name="moe_dispatch",
    )(tails, slots.reshape(steps, 1, MOE_ROWS), rows)
    return xs.reshape(n_slots * X_SLAB, LANES)


def _experts_kernel(elo_ref, ehi_ref, used_ref, x_ref, wg0_ref, wu0_ref, wd0_ref,
                    wg1_ref, wu1_ref, wd1_ref, y_ref, cg0_ref, cu0_ref, cd0_ref, cg1_ref, cu1_ref, cd1_ref):
    t = pl.program_id(0)
    tm = MOE_TILE
    prev = jnp.maximum(t - 1, 0)
    sides = ((elo_ref, (wg0_ref, wu0_ref, wd0_ref), (cg0_ref, cu0_ref, cd0_ref)),
             (ehi_ref, (wg1_ref, wu1_ref, wd1_ref), (cg1_ref, cu1_ref, cd1_ref)))

    @pl.when(t < used_ref[0])
    def _():
        for ids, weights, cached in sides:
            @pl.when((t == 0) | (ids[t] != ids[prev]))
            def _():
                for w_ref, c_ref in zip(weights, cached):
                    c_ref[...] = w_ref[...].astype(BF16)

        x = jnp.concatenate([x_ref[pl.ds(j, tm, stride=X_SLAB), :] for j in range(X_SLAB)],
                            axis=1).astype(BF16)
        for e, (_, _, (cg_ref, cu_ref, cd_ref)) in enumerate(sides):
            gate = jnp.dot(x, cg_ref[...], preferred_element_type=F32)
            up = jnp.dot(x, cu_ref[...], preferred_element_type=F32)
            act = ((gate * jax.nn.sigmoid(gate)) * up).astype(BF16)
            f = jnp.dot(act, cd_ref[...], preferred_element_type=F32)
            for j in range(X_SLAB):
                y_ref[pl.ds(e * X_SLAB + j, tm, stride=Y_SLAB), :] = f[:, j * LANES:(j + 1) * LANES]

    @pl.when(t >= used_ref[0])
    def _():
        y_ref[...] = jnp.zeros_like(y_ref)


def _experts_call(xs, e_lo, e_hi, n_used, w_gate, w_up, w_down, layer):
    d = D_MODEL
    tiles = xs.shape[0] // (MOE_TILE * X_SLAB)

    def xmap(t, elo, ehi, used):
        return (jnp.maximum(jnp.minimum(t, used[0] - 1), 0), 0)

    def lo(t, elo, ehi, used):
        return (layer, elo[t], 0, 0)

    def hi(t, elo, ehi, used):
        return (layer, ehi[t], 0, 0)

    up_spec = lambda m: pl.BlockSpec((None, None, d, D_EXPERT), m)
    down_spec = lambda m: pl.BlockSpec((None, None, D_EXPERT, d), m)
    cached = [pltpu.VMEM((d, D_EXPERT), BF16), pltpu.VMEM((d, D_EXPERT), BF16), pltpu.VMEM((D_EXPERT, d), BF16)]
    grid_spec = pltpu.PrefetchScalarGridSpec(
        num_scalar_prefetch=3,
        grid=(tiles,),
        in_specs=[pl.BlockSpec((MOE_TILE * X_SLAB, LANES), xmap),
                  up_spec(lo), up_spec(lo), down_spec(lo),
                  up_spec(hi), up_spec(hi), down_spec(hi)],
        out_specs=pl.BlockSpec((MOE_TILE * Y_SLAB, LANES), lambda t, elo, ehi, used: (t, 0)),
        scratch_shapes=cached + cached,
    )
    return pl.pallas_call(
        _experts_kernel,
        grid_spec=grid_spec,
        out_shape=jax.ShapeDtypeStruct((tiles * MOE_TILE * Y_SLAB, LANES), F32),
        compiler_params=_params("arbitrary"),
        name="experts",
    )(e_lo, e_hi, n_used, xs, w_gate, w_up, w_down, w_gate, w_up, w_down)


def _collect_kernel(slot_ref, next_slot_ref, ys_ref, rc_ref, *rest, steps, residual):
    if residual:
        h_ref, mod_ref, o_ref, buf_ref, sem = rest
    else:
        o_ref, buf_ref, sem = rest
    i = pl.program_id(0)
    k = i % 2
    rt = MOE_ROWS

    def slab_copy(buf, g, u, slot):
        return pltpu.make_async_copy(
            ys_ref.at[pl.ds(pl.multiple_of(slot * Y_SLAB, Y_SLAB), Y_SLAB)],
            buf_ref.at[buf, g, :, pl.ds(u, 1), :], sem.at[buf])

    def start_all(slots, buf):
        _for_rows(rt, lambda g, u: slab_copy(buf, g, u, slots[0, 0, g * SUBLANES + u]).start(priority=u % 2))

    @pl.when(i == 0)
    def _():
        start_all(slot_ref, 0)

    @pl.when(i + 1 < steps)
    def _():
        start_all(next_slot_ref, 1 - k)

    pltpu.make_async_copy(buf_ref.at[k], buf_ref.at[k], sem.at[k]).wait()

    rc = rc_ref[...]
    w_lo = rc[:, 2:3]
    w_hi = rc[:, 3:4]
    for j in range(X_SLAB):
        cols = slice(j * LANES, (j + 1) * LANES)
        f = (w_lo * buf_ref[k, :, j, :, :].reshape(rt, LANES)
             + w_hi * buf_ref[k, :, X_SLAB + j, :, :].reshape(rt, LANES))
        if residual:
            o_ref[:, cols] = h_ref[:, cols] + mod_ref[5:6, cols] * f
        else:
            o_ref[:, cols] = f


def _collect_call(ys, slots, route_cols, residual=None):
    n = slots.shape[0]
    d = D_MODEL
    steps = n // MOE_ROWS
    slots3 = slots.reshape(steps, 1, MOE_ROWS)
    row = pl.BlockSpec((MOE_ROWS, d), lambda i: (i, 0))
    in_specs = [pl.BlockSpec((1, 1, MOE_ROWS), lambda i: (i, 0, 0), memory_space=pltpu.SMEM),
                pl.BlockSpec((1, 1, MOE_ROWS), lambda i: (jnp.minimum(i + 1, steps - 1), 0, 0),
                             memory_space=pltpu.SMEM),
                pl.BlockSpec(memory_space=pl.ANY),
                pl.BlockSpec((MOE_ROWS, LANES), lambda i: (i, 0))]
    args = [slots3, slots3, ys.reshape(ys.shape[0], 1, LANES), route_cols]
    if residual is not None:
        h, mod, tiles_per_seq, n_batch = residual
        in_specs += [row, _mod_spec(tiles_per_seq, n_batch)]
        args += [h, mod]
    return pl.pallas_call(
        functools.partial(_collect_kernel, steps=steps, residual=residual is not None),
        grid=(steps,),
        in_specs=in_specs,
        out_specs=row,
        out_shape=jax.ShapeDtypeStruct((n, d), F32),
        scratch_shapes=[pltpu.VMEM((2, MOE_ROWS // SUBLANES, Y_SLAB, SUBLANES, LANES), F32),
                        pltpu.SemaphoreType.DMA((2,))],
        compiler_params=_params("arbitrary"),
        name="moe_collect",
    )(*args)


def _moe(rows, route, route_cols, w_gate, w_up, w_down, layer, residual=None):
    n = rows.shape[0]
    tm = MOE_TILE
    assert n % MOE_ROWS == 0
    max_tiles = -(-n // tm) + N_CLASSES
    p = max_tiles * tm
    lo = route[0].astype(jnp.int32)
    hi = route[1].astype(jnp.int32)
    lo_in = lo % EXPERTS_PER_GROUP
    hi_in = hi % EXPERTS_PER_GROUP
    pair = lo_in * 3 - (lo_in * (lo_in - 1)) // 2 + (hi_in - lo_in - 1)
    cls = (lo // EXPERTS_PER_GROUP) * PAIRS_PER_GROUP + pair
    onehot = (cls[:, None] == jnp.arange(N_CLASSES, dtype=jnp.int32)[None, :]).astype(jnp.int32)
    counts = jnp.sum(onehot, axis=0)
    rank = jnp.sum(jnp.cumsum(onehot, axis=0) * onehot, axis=1) - 1
    tiles_c = (counts + tm - 1) // tm
    tile_end = jnp.cumsum(tiles_c)
    tile_start = tile_end - tiles_c
    n_used = tile_end[-1]
    slot_of_token = (tile_start[cls] * tm + rank).astype(jnp.int32)
    tile_ids = jnp.arange(max_tiles, dtype=jnp.int32)
    tile_cls = jnp.minimum(jnp.sum((tile_end[None, :] <= tile_ids[:, None]).astype(jnp.int32), axis=1),
                           N_CLASSES - 1)
    group0 = (tile_cls // PAIRS_PER_GROUP) * EXPERTS_PER_GROUP
    e_lo = group0 + jnp.asarray(PAIR_LO, jnp.int32)[tile_cls % PAIRS_PER_GROUP]
    e_hi = group0 + jnp.asarray(PAIR_HI, jnp.int32)[tile_cls % PAIRS_PER_GROUP]
    tails = jnp.stack([tiles_c, tile_end]).astype(jnp.int32)
    xs = _dispatch_call(rows, slot_of_token, tails, p)
    ys = _experts_call(xs, e_lo.astype(jnp.int32), e_hi.astype(jnp.int32),
                       n_used.reshape(1).astype(jnp.int32), w_gate, w_up, w_down, layer)
    return _collect_call(ys, slot_of_token, route_cols, residual)


def _rope_tables(seq):
    rows = seq // GRID_W
    row = jnp.repeat(jnp.arange(rows), GRID_W).astype(F32)
    col = jnp.tile(jnp.arange(GRID_W), rows).astype(F32)
    n_freq = HEAD_DIM // 4
    inv_freq = ROPE_BASE ** (-jnp.arange(n_freq, dtype=F32) / n_freq)
    ang_r = row[:, None] * inv_freq
    ang_c = col[:, None] * inv_freq
    cos = jnp.concatenate([jnp.cos(ang_r), jnp.cos(ang_r), jnp.cos(ang_c), jnp.cos(ang_c)], axis=1)
    sin = jnp.concatenate([-jnp.sin(ang_r), jnp.sin(ang_r), -jnp.sin(ang_c), jnp.sin(ang_c)], axis=1)
    cos = jnp.concatenate([jnp.tile(cos, (1, 2)), jnp.ones((ROW_TILE, LANES), F32)], axis=0)
    sin = jnp.concatenate([jnp.tile(sin, (1, 2)), jnp.zeros((ROW_TILE, LANES), F32)], axis=0)
    return cos, sin


def _block_diag_halves(w):
    per_half = LRU_HALF // LRU_BLOCK
    halves = LRU_WIDTH // LRU_HALF
    blocks = w.reshape(2, halves, per_half, LRU_BLOCK, LRU_BLOCK).transpose(1, 0, 2, 3, 4)
    eye = jnp.eye(per_half, dtype=w.dtype)
    out = blocks[:, :, :, :, None, :] * eye[None, None, :, None, :, None]
    return out.reshape(halves, 2, LRU_HALF, LRU_HALF).astype(BF16)


def _gla_gate_weights(gate_up, gate_b):
    w = jnp.zeros((GLA_HEADS, 2, LANES, GLA_DK), F32)
    for d in range(2):
        wd = gate_up[d].reshape(GLA_LOWRANK, GLA_HEADS, GLA_DK).transpose(1, 0, 2)
        w = w.at[:, d, d * GLA_LOWRANK:(d + 1) * GLA_LOWRANK, :].set(wd)
    b = gate_b.reshape(2, GLA_HEADS, GLA_DK).transpose(1, 0, 2)
    w_hi = w.astype(BF16)
    w_lo = (w - w_hi.astype(F32)).astype(BF16)
    return jnp.concatenate([w_hi, w_lo], axis=-1), b


def kernel(x, c, ctx, c_ctx, router_w, router_bias, ada_w, ada_b, norm_mix_g, norm_ffn_g, moe_w_gate, moe_w_up, moe_w_down, ev_w_in, ev_w_out, ev_conv_w, ev_conv_b, ev_rg_w, ev_rg_b, ev_ig_w, ev_ig_b, ev_lambda, ev_q_norm_g, ev_k_norm_g, ev_sink, od_w_in, od_w_out, od_gate_up, od_gate_b, od_out_norm_g):
    n_batch, seq, d = x.shape
    ctx_len = ctx.shape[1]
    n_lat = n_batch * seq
    n_ctx = n_batch * ctx_len
    assert ada_w.shape[0] == 2 and d == D_MODEL
    assert seq % ROW_TILE == 0 and n_ctx % ROW_TILE == 0 and seq % ctx_len == 0 and seq % MOE_ROWS == 0
    assert ctx_len % LRU_CHUNK == 0 and seq % GLA_BLOCK == 0 and ctx_len % GLA_BLOCK == 0
    tiles_per_seq = seq // ROW_TILE
    n_lat_tiles = n_lat // ROW_TILE

    x_lat = x.reshape(n_lat, d)
    x_ctx = ctx.reshape(n_ctx, d)
    mod_rows = -(-(n_batch + 1) // SUBLANES) * SUBLANES
    cvec = jnp.zeros((mod_rows, d), F32).at[:n_batch].set(c).at[n_batch].set(c_ctx)
    mods = _ada_call(cvec, ada_w, ada_b).reshape(2, mod_rows, 6, d)
    mod_mix = [mods[l][:n_batch + 1, 0:6] for l in range(2)]
    rw_pad = jnp.pad(router_w, ((0, 0), (0, LANES - N_EXPERTS)))
    rw_hi = rw_pad.astype(BF16)
    rw_t = jnp.concatenate([rw_hi, (rw_pad - rw_hi.astype(F32)).astype(BF16)], axis=1)
    rb = router_bias.reshape(N_EXPERTS, 1)

    w_in0 = ev_w_in[0].astype(BF16)
    w_out0 = ev_w_out[0].astype(BF16)
    cos_t, sin_t = _rope_tables(seq)
    seg = np.kron(np.eye(LANES // HEAD_DIM, dtype=np.float32), np.full((HEAD_DIM, HEAD_DIM), 1.0 / HEAD_DIM, np.float32))
    seg = jnp.asarray(np.concatenate([seg, seg], axis=0)).astype(BF16)
    proj0, qrot, kvrot = _proj_even_call(x_lat, x_ctx, mod_mix[0], norm_mix_g[0:1], w_in0, cos_t, sin_t,
                                         jnp.tile(ev_q_norm_g[0], 2)[None, :],
                                         jnp.tile(ev_k_norm_g[0], 2)[None, :], seg, tiles_per_seq, n_batch)
    a_lat, a_ctx = _rglru_call(proj0, ev_conv_w[0], ev_conv_b[0:1], _block_diag_halves(ev_rg_w[0]),
                               _block_diag_halves(ev_ig_w[0]), ev_rg_b[0], ev_ig_b[0], ev_lambda[0],
                               n_batch, seq, ctx_len)
    b_lat = _attn_call(qrot, kvrot, ev_sink[0], n_batch, seq, ctx_len)
    b_ctx = _ctx_attn_call(qrot, kvrot, ev_sink[0], n_batch, seq, ctx_len)
    h, rows, route, route_cols = _outproj_even_call(a_lat, a_ctx, b_lat, b_ctx, w_out0, x_lat, x_ctx, mod_mix[0],
                                                    norm_ffn_g[0:1], rw_t, rb, tiles_per_seq, n_batch)
    ffn = _moe(rows, route, route_cols, moe_w_gate, moe_w_up, moe_w_down, 0)

    w_in1 = jnp.pad(od_w_in[0], ((0, 0), (0, ODD_IN_PAD - ODD_IN))).astype(BF16)
    h, proj1 = _proj_res_call(h, ffn, mod_mix[0], mod_mix[1], norm_mix_g[1:2], w_in1, tiles_per_seq, n_batch)
    gate_w, gate_b = _gla_gate_weights(od_gate_up[0], od_gate_b[0])
    o = _gla_call(proj1, gate_w, gate_b, od_out_norm_g[0:1], n_batch, seq, ctx_len)
    h_lat, rows, route, route_cols = _outproj_odd_call(o, od_w_out[0].astype(BF16), h, mod_mix[1],
                                                       norm_ffn_g[1:2], rw_t, rb, tiles_per_seq, n_batch)
    out = _moe(rows, route, route_cols, moe_w_gate, moe_w_up, moe_w_down, 1,
               residual=(h_lat, mod_mix[1], seq // MOE_ROWS, n_batch))
    return out.reshape(n_batch, seq, d)
```
